```python
import math
import jax
import jax.numpy as jnp
from jax import lax
import numpy as np

D_MODEL = 2048
BATCH = 2
SEQ = 16384
DEPTH = 2

GRID_W = 64
CTX_LEN = 256
N_BRANCH = 4
BRANCH_WIDTH = 512
FNET_GROUPS = 4
FNET_GROUP_DIM = 128
GQA_HEADS = 4
GQA_KV_HEADS = 2
GQA_HEAD_DIM = 128
HYENA_WIDTH = 512
HYENA_ORDER = 2
HYENA_EMB_DIM = 33
HYENA_FILTER_DIM = 64
HYENA_DECAY_TARGET = 1e-2
HYENA_SHORT_DECAY_PCT = 0.3
HYENA_LONG_DECAY_PCT = 1.5
SWA_HEADS = 8
SWA_KV_HEADS = 2
SWA_HEAD_DIM = 64
WINDOW = 128
Q_BLOCK = 128
ROPE_THETA = 10000.0
N_EXPERTS = 32
TOP_K = 4
D_EXPERT = 1536
SWIGLU_LIMIT = 7.0
SWIGLU_ALPHA = 1.702
MOE_BLOCK = 128
LN_EPS = 1e-5
RMS_EPS = 1e-6
DEEPNORM_ALPHA = (2 * DEPTH) ** 0.25
DEEPNORM_BETA = (8 * DEPTH) ** -0.25
SPLIT_SIZES = (FNET_GROUPS * FNET_GROUP_DIM,
               GQA_HEADS * GQA_HEAD_DIM, GQA_KV_HEADS * GQA_HEAD_DIM, GQA_KV_HEADS * GQA_HEAD_DIM,
               3 * HYENA_WIDTH,
               SWA_HEADS * SWA_HEAD_DIM, SWA_KV_HEADS * SWA_HEAD_DIM, SWA_KV_HEADS * SWA_HEAD_DIM)
IN_WIDTH = sum(SPLIT_SIZES)
F32 = jnp.float32

kernel_name = "hybrid_fourier_gqa_hyena_swa_moe_diffusion_block"


def layer_norm(x):
    xf = x.astype(F32)
    xc = xf - jnp.mean(xf, -1, keepdims=True)
    return xc * lax.rsqrt(jnp.mean(xc * xc, -1, keepdims=True) + LN_EPS)


def modulate(x, shift, scale):
    return (layer_norm(x) * (1.0 + scale.astype(F32)) + shift.astype(F32)).astype(x.dtype)


def post_norm(x, update, gain, bias):
    y = layer_norm(DEEPNORM_ALPHA * x.astype(F32) + update.astype(F32))
    return (y * gain.astype(F32) + bias.astype(F32)).astype(x.dtype)


def rms_norm(x, gain):
    xf = x.astype(F32)
    y = xf * lax.rsqrt(jnp.mean(xf * xf, -1, keepdims=True) + RMS_EPS) * gain.astype(F32)
    return y.astype(x.dtype)


def split_projection(z):
    points = np.cumsum(SPLIT_SIZES)[:-1].tolist()
    return jnp.split(z, points, axis=-1)


def heads(z, n_heads):
    b, n, w = z.shape
    return z.reshape(b, n, n_heads, w // n_heads)


def group_heads(q, n_kv):
    b, n, h, hd = q.shape
    return q.reshape(b, n, n_kv, h // n_kv, hd)


def flat_heads(o):
    return o.reshape(o.shape[0], o.shape[1], -1)


def axial_rope(rows, head_dim):
    n_freq = head_dim // 4
    inv_freq = jnp.power(ROPE_THETA, -jnp.arange(n_freq, dtype=F32) / n_freq)
    t = jnp.arange(rows * GRID_W)
    row = (t // GRID_W).astype(F32)
    col = (t % GRID_W).astype(F32)
    ang = jnp.stack([row[:, None] * inv_freq, col[:, None] * inv_freq], axis=1)
    return jnp.cos(ang), jnp.sin(ang)


def apply_rope(x, rope):
    cos, sin = rope
    b, n, h, hd = x.shape
    xr = x.astype(F32).reshape(b, n, h, 2, 2, hd // 4)
    x1, x2 = xr[..., 0, :], xr[..., 1, :]
    c = cos[None, :, None]
    s = sin[None, :, None]
    out = jnp.stack([x1 * c - x2 * s, x2 * c + x1 * s], axis=-2)
    return out.reshape(x.shape).astype(x.dtype)


def softmax_attend(q, k, v, sinks=None):
    scale = q.shape[-1] ** -0.5
    s = jnp.einsum("bqkgd,bskd->bkgqs", q, k, preferred_element_type=F32) * scale
    if sinks is None:
        p = jax.nn.softmax(s, axis=-1)
    else:
        sink_col = jnp.broadcast_to(sinks.astype(F32)[None, :, :, None, None], s.shape[:-1] + (1,))
        p = jax.nn.softmax(jnp.concatenate([sink_col, s], axis=-1), axis=-1)[..., 1:]
    return jnp.einsum("bkgqs,bskd->bqkgd", p.astype(v.dtype), v)


def global_attention_latent(q, k, v, k_ctx, v_ctx):
    b, n = q.shape[:2]
    k_all = jnp.concatenate([k_ctx, k], axis=1)
    v_all = jnp.concatenate([v_ctx, v], axis=1)
    qb = q.reshape((b, n // Q_BLOCK, Q_BLOCK) + q.shape[2:]).swapaxes(0, 1)
    out = lax.map(lambda q_blk: softmax_attend(q_blk, k_all, v_all), qb)
    return out.swapaxes(0, 1).reshape(b, n, -1)


def window_attention_latent(q, k, v, k_ctx, v_ctx, sinks):
    b, n = q.shape[:2]
    n_blk = n // Q_BLOCK
    n_c = k_ctx.shape[1]
    scale = q.shape[-1] ** -0.5
    pad = ((0, 0), (Q_BLOCK, Q_BLOCK), (0, 0), (0, 0))
    kp = jnp.pad(k, pad)
    vp = jnp.pad(v, pad)
    qb = q.reshape((b, n_blk, Q_BLOCK) + q.shape[2:]).swapaxes(0, 1)
    qi = jnp.arange(Q_BLOCK)[:, None]
    kj = jnp.arange(3 * Q_BLOCK)[None, :]
    in_window = jnp.abs(kj - Q_BLOCK - qi) <= WINDOW
    sink_val = sinks.astype(F32)[None, :, :, None, None]

    def block(args):
        q_blk, i = args
        kb = lax.dynamic_slice_in_dim(kp, i * Q_BLOCK, 3 * Q_BLOCK, axis=1)
        vb = lax.dynamic_slice_in_dim(vp, i * Q_BLOCK, 3 * Q_BLOCK, axis=1)
        kpos = i * Q_BLOCK - Q_BLOCK + kj
        valid = in_window & (kpos >= 0) & (kpos < n)
        s_ctx = jnp.einsum("bqkgd,bskd->bkgqs", q_blk, k_ctx, preferred_element_type=F32) * scale
        s_band = jnp.einsum("bqkgd,bskd->bkgqs", q_blk, kb, preferred_element_type=F32) * scale
        s_band = jnp.where(valid, s_band, -jnp.inf)
        sink_col = jnp.broadcast_to(sink_val, s_ctx.shape[:-1] + (1,))
        p = jax.nn.softmax(jnp.concatenate([sink_col, s_ctx, s_band], axis=-1), axis=-1).astype(v.dtype)
        return (jnp.einsum("bkgqs,bskd->bqkgd", p[..., 1:1 + n_c], v_ctx)
                + jnp.einsum("bkgqs,bskd->bqkgd", p[..., 1 + n_c:], vb))

    out = lax.map(block, (qb, jnp.arange(n_blk)))
    return out.swapaxes(0, 1).reshape(b, n, -1)


def fourier_mix(u):
    b, n, _ = u.shape
    ug = u.astype(F32).reshape(b, n, FNET_GROUPS, FNET_GROUP_DIM)
    y = jnp.fft.fft2(ug, axes=(1, 3), norm="ortho").real
    return y.reshape(b, n, FNET_GROUPS * FNET_GROUP_DIM).astype(u.dtype)


def short_conv(u, w, bias):
    n = u.shape[1]
    up = jnp.pad(u, ((0, 0), (1, 1), (0, 0)))
    return up[:, :n] * w[0] + up[:, 1:n + 1] * w[1] + up[:, 2:] * w[2] + bias


def hyena_filter_spectrum(n, w1, b1, fr1, w2, b2, fr2, w3):
    t = jnp.linspace(0.0, 1.0, n, dtype=F32)[:, None]
    bands = (HYENA_EMB_DIM - 1) // 2
    omega = 2.0 * math.pi * jnp.arange(n, dtype=F32)[:, None] / n
    freqs = jnp.linspace(1e-4, bands - 1, bands, dtype=F32)[None, :]
    z = jnp.concatenate([t, jnp.cos(freqs * omega), -jnp.sin(freqs * omega)], axis=-1)
    h = jnp.sin(fr1.astype(F32) * (z @ w1.astype(F32) + b1.astype(F32)))
    h = jnp.sin(fr2.astype(F32) * (h @ w2.astype(F32) + b2.astype(F32)))
    h = (h @ w3.astype(F32)).reshape(n, HYENA_ORDER, 2, HYENA_WIDTH)
    max_decay = math.log(HYENA_DECAY_TARGET) / HYENA_SHORT_DECAY_PCT
    min_decay = math.log(HYENA_DECAY_TARGET) / HYENA_LONG_DECAY_PCT
    deltas = jnp.abs(jnp.linspace(min_decay, max_decay, HYENA_WIDTH, dtype=F32))
    h = h * jnp.exp(-t * deltas)[:, None, None, :]
    taps = jnp.concatenate([h[:, :, 0],
                            jnp.zeros((1, HYENA_ORDER, HYENA_WIDTH), F32),
                            h[:0:-1, :, 1]], axis=0)
    taps = taps * lax.rsqrt(jnp.sum(taps * taps, axis=0, keepdims=True) + 1e-6)
    return jnp.fft.rfft(taps, axis=0)


def hyena_mix(zc, conv_w, conv_b, filt, bias):
    n = zc.shape[1]
    u = short_conv(zc, conv_w, conv_b).astype(F32)
    x1, x2, v = jnp.split(u, 3, axis=-1)
    spec = hyena_filter_spectrum(n, *filt)

    def long_conv(z, order):
        zf = jnp.fft.rfft(z, n=2 * n, axis=1)
        y = jnp.fft.irfft(zf * spec[None, :, order], n=2 * n, axis=1)[:, :n]
        return y + z * bias[order].astype(F32)

    y = x1 * long_conv(v, 0)
    y = x2 * long_conv(y, 1)
    return y.astype(zc.dtype)


def merge_branches(h, branches, w_gate_l, w_branch_l, w_out_l):
    merged = sum(jax.nn.sigmoid(h @ w_gate_l[i]) * (br @ w_branch_l[i]) for i, br in enumerate(branches))
    return merged @ w_out_l


def moe_ffn(tokens, router_w, router_b, w_gu, b_gu, w_dn, b_dn):
    n_tok, d = tokens.shape
    logits = (tokens @ router_w + router_b).astype(F32)
    top_v, top_i = lax.top_k(logits, TOP_K)
    weights = jax.nn.softmax(top_v, axis=-1)
    flat_e = top_i.reshape(-1)
    order = jnp.argsort(flat_e)
    sorted_e = flat_e[order]
    token_of = order // TOP_K
    counts = jnp.bincount(flat_e, length=N_EXPERTS)
    padded = (counts + MOE_BLOCK - 1) // MOE_BLOCK * MOE_BLOCK
    start = jnp.cumsum(counts) - counts
    padded_end = jnp.cumsum(padded)
    padded_start = padded_end - padded
    dest = padded_start[sorted_e] + jnp.arange(n_tok * TOP_K) - start[sorted_e]
    n_rows = n_tok * TOP_K + N_EXPERTS * MOE_BLOCK
    n_blk = n_rows // MOE_BLOCK
    buf = jnp.zeros((n_rows, d), tokens.dtype).at[dest].set(tokens[token_of])
    blk_e = jnp.minimum(jnp.searchsorted(padded_end, jnp.arange(n_blk) * MOE_BLOCK, side="right"),
                        N_EXPERTS - 1)

    def expert_block(args):
        xb, e = args
        gu = xb @ w_gu[e] + b_gu[e]
        gate, up = gu[:, ::2], gu[:, 1::2]
        gate = jnp.minimum(gate, SWIGLU_LIMIT)
        up = jnp.clip(up, -SWIGLU_LIMIT, SWIGLU_LIMIT)
        glu = gate * jax.nn.sigmoid(gate * SWIGLU_ALPHA)
        return ((up + 1.0) * glu) @ w_dn[e] + b_dn[e]

    y = lax.map(expert_block, (buf.reshape(n_blk, MOE_BLOCK, d), blk_e)).reshape(n_rows, d)
    contrib = (y[dest].astype(F32) * weights.reshape(-1)[order][:, None]).astype(tokens.dtype)
    return jnp.zeros((n_tok, d), tokens.dtype).at[token_of].add(contrib)


def setup_inputs(seed: int = 0) -> dict:
    key = jax.random.key(seed)
    keys = iter(jax.random.split(key, 40))

    def normal(shape, std):
        return jax.random.normal(next(keys), shape, F32) * std

    D = D_MODEL
    return {
        "x": normal((BATCH, SEQ, D), 1.0),
        "c": normal((BATCH, D), 1.0),
        "ctx": normal((BATCH, CTX_LEN, D), 1.0),
        "c_ctx": normal((D,), 1.0),
        "mod_w": normal((DEPTH, D, 6 * D), 0.5 * D ** -0.5),
        "mod_b": normal((DEPTH, 6 * D), 0.01),
        "w_in": normal((DEPTH, D, IN_WIDTH), D ** -0.5),
        "gqa_q_gain": 1.0 + normal((DEPTH, GQA_HEAD_DIM), 0.02),
        "gqa_k_gain": 1.0 + normal((DEPTH, GQA_HEAD_DIM), 0.02),
        "conv_w": normal((DEPTH, 3, 3 * HYENA_WIDTH), 3 ** -0.5),
        "conv_b": normal((DEPTH, 3 * HYENA_WIDTH), 0.01),
        "filt_w1": normal((DEPTH, HYENA_EMB_DIM, HYENA_FILTER_DIM), HYENA_EMB_DIM ** -0.5),
        "filt_b1": normal((DEPTH, HYENA_FILTER_DIM), 0.1),
        "filt_freq1": 1.0 + normal((DEPTH, HYENA_FILTER_DIM), 0.1),
        "filt_w2": normal((DEPTH, HYENA_FILTER_DIM, HYENA_FILTER_DIM), HYENA_FILTER_DIM ** -0.5),
        "filt_b2": normal((DEPTH, HYENA_FILTER_DIM), 0.1),
        "filt_freq2": 1.0 + normal((DEPTH, HYENA_FILTER_DIM), 0.1),
        "filt_w3": normal((DEPTH, HYENA_FILTER_DIM, HYENA_ORDER * 2 * HYENA_WIDTH), HYENA_FILTER_DIM ** -0.5),
        "hyena_bias": normal((DEPTH, HYENA_ORDER, HYENA_WIDTH), 0.1),
        "swa_sinks": normal((DEPTH, SWA_HEADS), 0.5),
        "w_branch": normal((DEPTH, N_BRANCH, BRANCH_WIDTH, D), DEEPNORM_BETA * BRANCH_WIDTH ** -0.5),
        "w_gate": normal((DEPTH, N_BRANCH, D, D), D ** -0.5),
        "w_out": normal((DEPTH, D, D), DEEPNORM_BETA * D ** -0.5),
        "ln1_g": 1.0 + normal((DEPTH, D), 0.02),
        "ln1_b": normal((DEPTH, D), 0.01),
        "router_w": normal((DEPTH, D, N_EXPERTS), D ** -0.5),
        "router_b": normal((DEPTH, N_EXPERTS), 0.01),
        "exp_w_gate_up": normal((DEPTH, N_EXPERTS, D, 2 * D_EXPERT), D ** -0.5),
        "exp_b_gate_up": normal((DEPTH, N_EXPERTS, 2 * D_EXPERT), 0.01),
        "exp_w_down": normal((DEPTH, N_EXPERTS, D_EXPERT, D), DEEPNORM_BETA * D_EXPERT ** -0.5),
        "exp_b_down": normal((DEPTH, N_EXPERTS, D), 0.01),
        "ln2_g": 1.0 + normal((DEPTH, D), 0.02),
        "ln2_b": normal((DEPTH, D), 0.01),
    }


def reference(x, c, ctx, c_ctx, mod_w, mod_b, w_in, gqa_q_gain, gqa_k_gain, conv_w, conv_b,
              filt_w1, filt_b1, filt_freq1, filt_w2, filt_b2, filt_freq2, filt_w3, hyena_bias,
              swa_sinks, w_branch, w_gate, w_out, ln1_g, ln1_b, router_w, router_b,
              exp_w_gate_up, exp_b_gate_up, exp_w_down, exp_b_down, ln2_g, ln2_b):
    batch, n_lat, d = x.shape
    n_ctx = ctx.shape[1]
    rows = n_lat // GRID_W
    rope_gqa = axial_rope(rows, GQA_HEAD_DIM)
    rope_swa = axial_rope(rows, SWA_HEAD_DIM)
    x_lat, x_ctx = x, ctx
    for l in range(DEPTH):
        ctx_continues = l < DEPTH - 1
        mod_lat = jnp.split((jax.nn.silu(c) @ mod_w[l] + mod_b[l])[:, None, :], 6, axis=-1)
        mod_ctx = jnp.split((jax.nn.silu(c_ctx) @ mod_w[l] + mod_b[l])[None, None, :], 6, axis=-1)
        filt = (filt_w1[l], filt_b1[l], filt_freq1[l], filt_w2[l], filt_b2[l], filt_freq2[l], filt_w3[l])
        sinks = swa_sinks[l].reshape(SWA_KV_HEADS, SWA_HEADS // SWA_KV_HEADS)

        h_lat = modulate(x_lat, mod_lat[0], mod_lat[1])
        h_ctx = modulate(x_ctx, mod_ctx[0], mod_ctx[1])
        fa_l, gq_l, gk_l, gv_l, hy_l, sq_l, sk_l, sv_l = split_projection(h_lat @ w_in[l])
        fa_c, gq_c, gk_c, gv_c, hy_c, sq_c, sk_c, sv_c = split_projection(h_ctx @ w_in[l])
        gk_c = rms_norm(heads(gk_c, GQA_KV_HEADS), gqa_k_gain[l])
        gv_c = heads(gv_c, GQA_KV_HEADS)
        sk_c = heads(sk_c, SWA_KV_HEADS)
        sv_c = heads(sv_c, SWA_KV_HEADS)
        gq = apply_rope(rms_norm(heads(gq_l, GQA_HEADS), gqa_q_gain[l]), rope_gqa)
        gk = apply_rope(rms_norm(heads(gk_l, GQA_KV_HEADS), gqa_k_gain[l]), rope_gqa)
        sq = apply_rope(heads(sq_l, SWA_HEADS), rope_swa)
        sk = apply_rope(heads(sk_l, SWA_KV_HEADS), rope_swa)
        lat_branches = (
            fourier_mix(fa_l),
            global_attention_latent(group_heads(gq, GQA_KV_HEADS), gk, heads(gv_l, GQA_KV_HEADS), gk_c, gv_c),
            hyena_mix(hy_l, conv_w[l], conv_b[l], filt, hyena_bias[l]),
            window_attention_latent(group_heads(sq, SWA_KV_HEADS), sk, heads(sv_l, SWA_KV_HEADS),
                                    sk_c, sv_c, sinks),
        )
        mix_lat = merge_branches(h_lat, lat_branches, w_gate[l], w_branch[l], w_out[l])
        if ctx_continues:
            gq_ctx = group_heads(rms_norm(heads(gq_c, GQA_HEADS), gqa_q_gain[l]), GQA_KV_HEADS)
            sq_ctx = group_heads(heads(sq_c, SWA_HEADS), SWA_KV_HEADS)
            ctx_branches = (
                fourier_mix(fa_c),
                flat_heads(softmax_attend(gq_ctx, gk_c, gv_c)),
                hyena_mix(hy_c, conv_w[l], conv_b[l], filt, hyena_bias[l]),
                flat_heads(softmax_attend(sq_ctx, sk_c, sv_c, sinks)),
            )
            mix_ctx = merge_branches(h_ctx, ctx_branches, w_gate[l], w_branch[l], w_out[l])
            x_ctx = post_norm(x_ctx, mod_ctx[2] * mix_ctx, ln1_g[l], ln1_b[l])
        x_lat = post_norm(x_lat, mod_lat[2] * mix_lat, ln1_g[l], ln1_b[l])

        moe_args = (router_w[l], router_b[l], exp_w_gate_up[l], exp_b_gate_up[l], exp_w_down[l], exp_b_down[l])
        h_lat = modulate(x_lat, mod_lat[3], mod_lat[4])
        if ctx_continues:
            h_ctx = modulate(x_ctx, mod_ctx[3], mod_ctx[4])
            tokens = jnp.concatenate([h_lat.reshape(-1, d), h_ctx.reshape(-1, d)], axis=0)
            y = moe_ffn(tokens, *moe_args)
            y_lat = y[:batch * n_lat].reshape(batch, n_lat, d)
            y_ctx = y[batch * n_lat:].reshape(batch, n_ctx, d)
            x_ctx = post_norm(x_ctx, mod_ctx[5] * y_ctx, ln2_g[l], ln2_b[l])
        else:
            y_lat = moe_ffn(h_lat.reshape(-1, d), *moe_args).reshape(batch, n_lat, d)
        x_lat = post_norm(x_lat, mod_lat[5] * y_lat, ln2_g[l], ln2_b[l])
    return x_lat
```

```python
import functools
import math

import jax
import jax.numpy as jnp
import numpy as np
from jax import lax
from jax.experimental import pallas as pl
from jax.experimental.pallas import tpu as pltpu

D_MODEL = 2048
DEPTH = 2
GRID_W = 64
FNET_GROUPS = 4
FNET_GROUP_DIM = 128
GQA_HEADS = 4
GQA_KV_HEADS = 2
GQA_HEAD_DIM = 128
HYENA_WIDTH = 512
HYENA_ORDER = 2
HYENA_EMB_DIM = 33
HYENA_DECAY_TARGET = 1e-2
HYENA_SHORT_DECAY_PCT = 0.3
HYENA_LONG_DECAY_PCT = 1.5
SWA_HEADS = 8
SWA_KV_HEADS = 2
SWA_HEAD_DIM = 64
WINDOW = 128
Q_BLOCK = 128
ROPE_THETA = 10000.0
N_EXPERTS = 32
TOP_K = 4
D_EXPERT = 1536
SWIGLU_LIMIT = 7.0
SWIGLU_ALPHA = 1.702
MOE_BLOCK = 128
LN_EPS = 1e-5
RMS_EPS = 1e-6
DEEPNORM_ALPHA = (2 * DEPTH) ** 0.25
SPLIT_SIZES = (FNET_GROUPS * FNET_GROUP_DIM,
               GQA_HEADS * GQA_HEAD_DIM, GQA_KV_HEADS * GQA_HEAD_DIM, GQA_KV_HEADS * GQA_HEAD_DIM,
               3 * HYENA_WIDTH,
               SWA_HEADS * SWA_HEAD_DIM, SWA_KV_HEADS * SWA_HEAD_DIM, SWA_KV_HEADS * SWA_HEAD_DIM)
F32 = jnp.float32
BF16 = jnp.bfloat16


def _mm_body(a_ref, b_ref, o_ref):
    o_ref[...] = jnp.dot(a_ref[...], b_ref[...], preferred_element_type=F32).astype(o_ref.dtype)


def _pick(n, cands):
    for c in cands:
        if n % c == 0:
            return c
    return n


def matmul(a, b, out_dtype=F32):
    m, k = a.shape
    n = b.shape[1]
    tm = _pick(m, (512, 256, 128))
    tn = _pick(n, (512, 256, 128))
    return pl.pallas_call(
        _mm_body,
        grid=(m // tm, n // tn),
        in_specs=[pl.BlockSpec((tm, k), lambda i, j: (i, 0)),
                  pl.BlockSpec((k, tn), lambda i, j: (0, j))],
        out_specs=pl.BlockSpec((tm, tn), lambda i, j: (i, j)),
        out_shape=jax.ShapeDtypeStruct((m, n), out_dtype),
        compiler_params=pltpu.CompilerParams(dimension_semantics=("parallel", "parallel")),
        name="mm",
    )(a.astype(BF16), b.astype(BF16))


def mm3(x, w):
    b, n, k = x.shape
    return matmul(x.reshape(b * n, k), w).reshape(b, n, w.shape[1])


def layer_norm(x):
    xf = x.astype(F32)
    xc = xf - jnp.mean(xf, -1, keepdims=True)
    return xc * lax.rsqrt(jnp.mean(xc * xc, -1, keepdims=True) + LN_EPS)


def modulate(x, shift, scale):
    return (layer_norm(x) * (1.0 + scale.astype(F32)) + shift.astype(F32)).astype(x.dtype)


def post_norm(x, update, gain, bias):
    y = layer_norm(DEEPNORM_ALPHA * x.astype(F32) + update.astype(F32))
    return (y * gain.astype(F32) + bias.astype(F32)).astype(x.dtype)


def rms_norm(x, gain):
    xf = x.astype(F32)
    y = xf * lax.rsqrt(jnp.mean(xf * xf, -1, keepdims=True) + RMS_EPS) * gain.astype(F32)
    return y.astype(x.dtype)


def split_projection(z):
    points = np.cumsum(SPLIT_SIZES)[:-1].tolist()
    return jnp.split(z, points, axis=-1)


def heads(z, n_heads):
    b, n, w = z.shape
    return z.reshape(b, n, n_heads, w // n_heads)


def group_heads(q, n_kv):
    b, n, h, hd = q.shape
    return q.reshape(b, n, n_kv, h // n_kv, hd)


def flat_heads(o):
    return o.reshape(o.shape[0], o.shape[1], -1)


def axial_rope(rows, head_dim):
    n_freq = head_dim // 4
    inv_freq = jnp.power(ROPE_THETA, -jnp.arange(n_freq, dtype=F32) / n_freq)
    t = jnp.arange(rows * GRID_W)
    row = (t // GRID_W).astype(F32)
    col = (t % GRID_W).astype(F32)
    ang = jnp.stack([row[:, None] * inv_freq, col[:, None] * inv_freq], axis=1)
    return jnp.cos(ang), jnp.sin(ang)


def apply_rope(x, rope):
    cos, sin = rope
    b, n, h, hd = x.shape
    xr = x.astype(F32).reshape(b, n, h, 2, 2, hd // 4)
    x1, x2 = xr[..., 0, :], xr[..., 1, :]
    c = cos[None, :, None]
    s = sin[None, :, None]
    out = jnp.stack([x1 * c - x2 * s, x2 * c + x1 * s], axis=-2)
    return out.reshape(x.shape).astype(x.dtype)


def softmax_attend(q, k, v, sinks=None):
    scale = q.shape[-1] ** -0.5
    s = jnp.einsum("bqkgd,bskd->bkgqs", q, k, preferred_element_type=F32) * scale
    if sinks is None:
        p = jax.nn.softmax(s, axis=-1)
    else:
        sink_col = jnp.broadcast_to(sinks.astype(F32)[None, :, :, None, None], s.shape[:-1] + (1,))
        p = jax.nn.softmax(jnp.concatenate([sink_col, s], axis=-1), axis=-1)[..., 1:]
    return jnp.einsum("bkgqs,bskd->bqkgd", p.astype(v.dtype), v)


def global_attention_latent(q, k, v, k_ctx, v_ctx):
    b, n = q.shape[:2]
    k_all = jnp.concatenate([k_ctx, k], axis=1)
    v_all = jnp.concatenate([v_ctx, v], axis=1)
    qb = q.reshape((b, n // Q_BLOCK, Q_BLOCK) + q.shape[2:]).swapaxes(0, 1)
    out = lax.map(lambda q_blk: softmax_attend(q_blk, k_all, v_all), qb)
    return out.swapaxes(0, 1).reshape(b, n, -1)


def window_attention_latent(q, k, v, k_ctx, v_ctx, sinks):
    b, n = q.shape[:2]
    n_blk = n // Q_BLOCK
    n_c = k_ctx.shape[1]
    scale = q.shape[-1] ** -0.5
    pad = ((0, 0), (Q_BLOCK, Q_BLOCK), (0, 0), (0, 0))
    kp = jnp.pad(k, pad)
    vp = jnp.pad(v, pad)
    qb = q.reshape((b, n_blk, Q_BLOCK) + q.shape[2:]).swapaxes(0, 1)
    qi = jnp.arange(Q_BLOCK)[:, None]
    kj = jnp.arange(3 * Q_BLOCK)[None, :]
    in_window = jnp.abs(kj - Q_BLOCK - qi) <= WINDOW
    sink_val = sinks.astype(F32)[None, :, :, None, None]

    def block(args):
        q_blk, i = args
        kb = lax.dynamic_slice_in_dim(kp, i * Q_BLOCK, 3 * Q_BLOCK, axis=1)
        vb = lax.dynamic_slice_in_dim(vp, i * Q_BLOCK, 3 * Q_BLOCK, axis=1)
        kpos = i * Q_BLOCK - Q_BLOCK + kj
        valid = in_window & (kpos >= 0) & (kpos < n)
        s_ctx = jnp.einsum("bqkgd,bskd->bkgqs", q_blk, k_ctx, preferred_element_type=F32) * scale
        s_band = jnp.einsum("bqkgd,bskd->bkgqs", q_blk, kb, preferred_element_type=F32) * scale
        s_band = jnp.where(valid, s_band, -jnp.inf)
        sink_col = jnp.broadcast_to(sink_val, s_ctx.shape[:-1] + (1,))
        p = jax.nn.softmax(jnp.concatenate([sink_col, s_ctx, s_band], axis=-1), axis=-1).astype(v.dtype)
        return (jnp.einsum("bkgqs,bskd->bqkgd", p[..., 1:1 + n_c], v_ctx)
                + jnp.einsum("bkgqs,bskd->bqkgd", p[..., 1 + n_c:], vb))

    out = lax.map(block, (qb, jnp.arange(n_blk)))
    return out.swapaxes(0, 1).reshape(b, n, -1)


def fourier_mix(u):
    b, n, _ = u.shape
    ug = u.astype(F32).reshape(b, n, FNET_GROUPS, FNET_GROUP_DIM)
    y = jnp.fft.fft2(ug, axes=(1, 3), norm="ortho").real
    return y.reshape(b, n, FNET_GROUPS * FNET_GROUP_DIM).astype(u.dtype)


def short_conv(u, w, bias):
    n = u.shape[1]
    up = jnp.pad(u, ((0, 0), (1, 1), (0, 0)))
    return up[:, :n] * w[0] + up[:, 1:n + 1] * w[1] + up[:, 2:] * w[2] + bias


def hyena_filter_spectrum(n, w1, b1, fr1, w2, b2, fr2, w3):
    t = jnp.linspace(0.0, 1.0, n, dtype=F32)[:, None]
    bands = (HYENA_EMB_DIM - 1) // 2
    omega = 2.0 * math.pi * jnp.arange(n, dtype=F32)[:, None] / n
    freqs = jnp.linspace(1e-4, bands - 1, bands, dtype=F32)[None, :]
    z = jnp.concatenate([t, jnp.cos(freqs * omega), -jnp.sin(freqs * omega)], axis=-1)
    h = jnp.sin(fr1.astype(F32) * (z @ w1.astype(F32) + b1.astype(F32)))
    h = jnp.sin(fr2.astype(F32) * (h @ w2.astype(F32) + b2.astype(F32)))
    h = (h @ w3.astype(F32)).reshape(n, HYENA_ORDER, 2, HYENA_WIDTH)
    max_decay = math.log(HYENA_DECAY_TARGET) / HYENA_SHORT_DECAY_PCT
    min_decay = math.log(HYENA_DECAY_TARGET) / HYENA_LONG_DECAY_PCT
    deltas = jnp.abs(jnp.linspace(min_decay, max_decay, HYENA_WIDTH, dtype=F32))
    h = h * jnp.exp(-t * deltas)[:, None, None, :]
    taps = jnp.concatenate([h[:, :, 0],
                            jnp.zeros((1, HYENA_ORDER, HYENA_WIDTH), F32),
                            h[:0:-1, :, 1]], axis=0)
    taps = taps * lax.rsqrt(jnp.sum(taps * taps, axis=0, keepdims=True) + 1e-6)
    return jnp.fft.rfft(taps, axis=0)


def hyena_mix(zc, conv_w, conv_b, filt, bias):
    n = zc.shape[1]
    u = short_conv(zc, conv_w, conv_b).astype(F32)
    x1, x2, v = jnp.split(u, 3, axis=-1)
    spec = hyena_filter_spectrum(n, *filt)

    def long_conv(z, order):
        zf = jnp.fft.rfft(z, n=2 * n, axis=1)
        y = jnp.fft.irfft(zf * spec[None, :, order], n=2 * n, axis=1)[:, :n]
        return y + z * bias[order].astype(F32)

    y = x1 * long_conv(v, 0)
    y = x2 * long_conv(y, 1)
    return y.astype(zc.dtype)


def merge_branches(h, branches, w_gate_l, w_branch_l, w_out_l):
    merged = sum(jax.nn.sigmoid(mm3(h, w_gate_l[i])) * mm3(br, w_branch_l[i]) for i, br in enumerate(branches))
    return mm3(merged, w_out_l)


def moe_ffn(tokens, router_w, router_b, w_gu, b_gu, w_dn, b_dn):
    n_tok, d = tokens.shape
    logits = (tokens @ router_w + router_b).astype(F32)
    top_v, top_i = lax.top_k(logits, TOP_K)
    weights = jax.nn.softmax(top_v, axis=-1)
    flat_e = top_i.reshape(-1)
    order = jnp.argsort(flat_e)
    sorted_e = flat_e[order]
    token_of = order // TOP_K
    counts = jnp.bincount(flat_e, length=N_EXPERTS)
    padded = (counts + MOE_BLOCK - 1) // MOE_BLOCK * MOE_BLOCK
    start = jnp.cumsum(counts) - counts
    padded_end = jnp.cumsum(padded)
    padded_start = padded_end - padded
    dest = padded_start[sorted_e] + jnp.arange(n_tok * TOP_K) - start[sorted_e]
    n_rows = n_tok * TOP_K + N_EXPERTS * MOE_BLOCK
    n_blk = n_rows // MOE_BLOCK
    buf = jnp.zeros((n_rows, d), tokens.dtype).at[dest].set(tokens[token_of])
    blk_e = jnp.minimum(jnp.searchsorted(padded_end, jnp.arange(n_blk) * MOE_BLOCK, side="right"),
                        N_EXPERTS - 1)

    def expert_block(args):
        xb, e = args
        gu = xb @ w_gu[e] + b_gu[e]
        gate, up = gu[:, ::2], gu[:, 1::2]
        gate = jnp.minimum(gate, SWIGLU_LIMIT)
        up = jnp.clip(up, -SWIGLU_LIMIT, SWIGLU_LIMIT)
        glu = gate * jax.nn.sigmoid(gate * SWIGLU_ALPHA)
        return ((up + 1.0) * glu) @ w_dn[e] + b_dn[e]

    y = lax.map(expert_block, (buf.reshape(n_blk, MOE_BLOCK, d), blk_e)).reshape(n_rows, d)
    contrib = (y[dest].astype(F32) * weights.reshape(-1)[order][:, None]).astype(tokens.dtype)
    return jnp.zeros((n_tok, d), tokens.dtype).at[token_of].add(contrib)


def kernel(x, c, ctx, c_ctx, mod_w, mod_b, w_in, gqa_q_gain, gqa_k_gain, conv_w, conv_b,
           filt_w1, filt_b1, filt_freq1, filt_w2, filt_b2, filt_freq2, filt_w3, hyena_bias,
           swa_sinks, w_branch, w_gate, w_out, ln1_g, ln1_b, router_w, router_b,
           exp_w_gate_up, exp_b_gate_up, exp_w_down, exp_b_down, ln2_g, ln2_b):
    batch, n_lat, d = x.shape
    n_ctx = ctx.shape[1]
    rows = n_lat // GRID_W
    rope_gqa = axial_rope(rows, GQA_HEAD_DIM)
    rope_swa = axial_rope(rows, SWA_HEAD_DIM)
    x_lat, x_ctx = x, ctx
    for l in range(DEPTH):
        ctx_continues = l < DEPTH - 1
        mod_lat = jnp.split((jax.nn.silu(c) @ mod_w[l] + mod_b[l])[:, None, :], 6, axis=-1)
        mod_ctx = jnp.split((jax.nn.silu(c_ctx) @ mod_w[l] + mod_b[l])[None, None, :], 6, axis=-1)
        filt = (filt_w1[l], filt_b1[l], filt_freq1[l], filt_w2[l], filt_b2[l], filt_freq2[l], filt_w3[l])
        sinks = swa_sinks[l].reshape(SWA_KV_HEADS, SWA_HEADS // SWA_KV_HEADS)

        h_lat = modulate(x_lat, mod_lat[0], mod_lat[1])
        h_ctx = modulate(x_ctx, mod_ctx[0], mod_ctx[1])
        fa_l, gq_l, gk_l, gv_l, hy_l, sq_l, sk_l, sv_l = split_projection(mm3(h_lat, w_in[l]))
        fa_c, gq_c, gk_c, gv_c, hy_c, sq_c, sk_c, sv_c = split_projection(mm3(h_ctx, w_in[l]))
        gk_c = rms_norm(heads(gk_c, GQA_KV_HEADS), gqa_k_gain[l])
        gv_c = heads(gv_c, GQA_KV_HEADS)
        sk_c = heads(sk_c, SWA_KV_HEADS)
        sv_c = heads(sv_c, SWA_KV_HEADS)
        gq = apply_rope(rms_norm(heads(gq_l, GQA_HEADS), gqa_q_gain[l]), rope_gqa)
        gk = apply_rope(rms_norm(heads(gk_l, GQA_KV_HEADS), gqa_k_gain[l]), rope_gqa)
        sq = apply_rope(heads(sq_l, SWA_HEADS), rope_swa)
        sk = apply_rope(heads(sk_l, SWA_KV_HEADS), rope_swa)
        lat_branches = (
            fourier_mix(fa_l),
            global_attention_latent(group_heads(gq, GQA_KV_HEADS), gk, heads(gv_l, GQA_KV_HEADS), gk_c, gv_c),
            hyena_mix(hy_l, conv_w[l], conv_b[l], filt, hyena_bias[l]),
            window_attention_latent(group_heads(sq, SWA_KV_HEADS), sk, heads(sv_l, SWA_KV_HEADS),
                                    sk_c, sv_c, sinks),
        )
        mix_lat = merge_branches(h_lat, lat_branches, w_gate[l], w_branch[l], w_out[l])
        if ctx_continues:
            gq_ctx = group_heads(rms_norm(heads(gq_c, GQA_HEADS), gqa_q_gain[l]), GQA_KV_HEADS)
            sq_ctx = group_heads(heads(sq_c, SWA_HEADS), SWA_KV_HEADS)
            ctx_branches = (
                fourier_mix(fa_c),
                flat_heads(softmax_attend(gq_ctx, gk_c, gv_c)),
                hyena_mix(hy_c, conv_w[l], conv_b[l], filt, hyena_bias[l]),
                flat_heads(softmax_attend(sq_ctx, sk_c, sv_c, sinks)),
            )
            mix_ctx = merge_branches(h_ctx, ctx_branches, w_gate[l], w_branch[l], w_out[l])
            x_ctx = post_norm(x_ctx, mod_ctx[2] * mix_ctx, ln1_g[l], ln1_b[l])
        x_lat = post_norm(x_lat, mod_lat[2] * mix_lat, ln1_g[l], ln1_b[l])

        moe_args = (router_w[l], router_b[l], exp_w_gate_up[l], exp_b_gate_up[l], exp_w_down[l], exp_b_down[l])
        h_lat = modulate(x_lat, mod_lat[3], mod_lat[4])
        if ctx_continues:
            h_ctx = modulate(x_ctx, mod_ctx[3], mod_ctx[4])
            tokens = jnp.concatenate([h_lat.reshape(-1, d), h_ctx.reshape(-1, d)], axis=0)
            y = moe_ffn(tokens, *moe_args)
            y_lat = y[:batch * n_lat].reshape(batch, n_lat, d)
            y_ctx = y[batch * n_lat:].reshape(batch, n_ctx, d)
            x_ctx = post_norm(x_ctx, mod_ctx[5] * y_ctx, ln2_g[l], ln2_b[l])
        else:
            y_lat = moe_ffn(h_lat.reshape(-1, d), *moe_args).reshape(batch, n_lat, d)
        x_lat = post_norm(x_lat, mod_lat[5] * y_lat, ln2_g[l], ln2_b[l])
    return x_lat
```

```python
import functools
import math

import jax
import jax.numpy as jnp
import numpy as np
from jax import lax
from jax.experimental import pallas as pl
from jax.experimental.pallas import tpu as pltpu

D_MODEL = 2048
DEPTH = 2
GRID_W = 64
FNET_GROUPS = 4
FNET_GROUP_DIM = 128
GQA_HEADS = 4
GQA_KV_HEADS = 2
GQA_HEAD_DIM = 128
HYENA_WIDTH = 512
HYENA_ORDER = 2
HYENA_EMB_DIM = 33
HYENA_DECAY_TARGET = 1e-2
HYENA_SHORT_DECAY_PCT = 0.3
HYENA_LONG_DECAY_PCT = 1.5
SWA_HEADS = 8
SWA_KV_HEADS = 2
SWA_HEAD_DIM = 64
WINDOW = 128
Q_BLOCK = 128
ROPE_THETA = 10000.0
N_EXPERTS = 32
TOP_K = 4
D_EXPERT = 1536
SWIGLU_LIMIT = 7.0
SWIGLU_ALPHA = 1.702
MOE_BLOCK = 128
LN_EPS = 1e-5
RMS_EPS = 1e-6
DEEPNORM_ALPHA = (2 * DEPTH) ** 0.25
SPLIT_SIZES = (FNET_GROUPS * FNET_GROUP_DIM,
               GQA_HEADS * GQA_HEAD_DIM, GQA_KV_HEADS * GQA_HEAD_DIM, GQA_KV_HEADS * GQA_HEAD_DIM,
               3 * HYENA_WIDTH,
               SWA_HEADS * SWA_HEAD_DIM, SWA_KV_HEADS * SWA_HEAD_DIM, SWA_KV_HEADS * SWA_HEAD_DIM)
F32 = jnp.float32
BF16 = jnp.bfloat16


def _mm_body(a_ref, b_ref, o_ref):
    o_ref[...] = jnp.dot(a_ref[...], b_ref[...], preferred_element_type=F32).astype(o_ref.dtype)


def _pick(n, cands):
    for c in cands:
        if n % c == 0:
            return c
    return n


def matmul(a, b, out_dtype=F32):
    m, k = a.shape
    n = b.shape[1]
    tm = _pick(m, (512, 256, 128))
    tn = _pick(n, (512, 256, 128))
    return pl.pallas_call(
        _mm_body,
        grid=(m // tm, n // tn),
        in_specs=[pl.BlockSpec((tm, k), lambda i, j: (i, 0)),
                  pl.BlockSpec((k, tn), lambda i, j: (0, j))],
        out_specs=pl.BlockSpec((tm, tn), lambda i, j: (i, j)),
        out_shape=jax.ShapeDtypeStruct((m, n), out_dtype),
        compiler_params=pltpu.CompilerParams(dimension_semantics=("parallel", "parallel")),
        name="mm",
    )(a.astype(BF16), b.astype(BF16))


def mm3(x, w):
    b, n, k = x.shape
    return matmul(x.reshape(b * n, k), w).reshape(b, n, w.shape[1])


GATTN_TQ = 512
GATTN_TK = 256
GATTN_UNROLL = 4
VMEM_LIMIT = 56 * 1024 * 1024
LANES = 128
LOG2E = math.log2(math.e)


def _gattn_body(q_ref, kc_ref, vc_ref, k_ref, v_ref, o_ref, m_ref, acc_ref, *, c2):
    tq = q_ref.shape[0]
    hd = k_ref.shape[1]
    q2 = jnp.concatenate([q_ref[:, :hd], q_ref[:, hd:]], axis=0)
    m_ref[...] = jnp.full(m_ref.shape, -jnp.inf, F32)
    acc_ref[...] = jnp.zeros(acc_ref.shape, F32)

    def step(k, v):
        s = lax.dot_general(q2, k, (((1,), (1,)), ((), ())), preferred_element_type=F32) * c2
        m_old = m_ref[...]
        m_new = jnp.maximum(m_old, jnp.max(s, axis=-1, keepdims=True))
        alpha = jnp.exp2(m_old - m_new)
        p = jnp.concatenate([jnp.exp2(s[:, j * LANES:(j + 1) * LANES] - m_new)
                             for j in range(k.shape[0] // LANES)], axis=1)
        acc_ref[...] = (jnp.concatenate([alpha, alpha], axis=1) * acc_ref[...]
                        + jnp.dot(p.astype(BF16), v, preferred_element_type=F32))
        m_ref[...] = m_new

    step(kc_ref[...], vc_ref[...])

    def loop(c, carry):
        off = pl.multiple_of(c * GATTN_TK, GATTN_TK)
        step(k_ref[pl.ds(off, GATTN_TK), :], v_ref[pl.ds(off, GATTN_TK), :])
        return carry

    lax.fori_loop(0, k_ref.shape[0] // GATTN_TK, loop, 0, unroll=GATTN_UNROLL)
    acc = acc_ref[...]
    out = acc[:, :hd] / acc[:, hd:]
    o_ref[...] = jnp.concatenate([out[:tq], out[tq:]], axis=1).astype(o_ref.dtype)


def _with_ones(v, n_heads):
    r = v.shape[0]
    v3 = v.reshape(r, n_heads, -1)
    return jnp.concatenate([v3, jnp.ones_like(v3)], axis=-1).reshape(r, -1)


def global_attention(q, k, v, k_ctx, v_ctx, batch):
    hd = GQA_HEAD_DIM
    n = q.shape[0] // batch
    n_c = k_ctx.shape[0] // batch
    g = GQA_HEADS // GQA_KV_HEADS
    tq = min(GATTN_TQ, n)
    assert g == 2 and hd == LANES and n % tq == 0 and n % (GATTN_TK * GATTN_UNROLL) == 0 and n_c % LANES == 0
    nq = n // tq
    return pl.pallas_call(
        functools.partial(_gattn_body, c2=hd ** -0.5 * LOG2E),
        grid=(batch, GQA_KV_HEADS, nq),
        in_specs=[pl.BlockSpec((tq, g * hd), lambda b, h, i: (b * nq + i, h)),
                  pl.BlockSpec((n_c, hd), lambda b, h, i: (b, h)),
                  pl.BlockSpec((n_c, 2 * hd), lambda b, h, i: (b, h)),
                  pl.BlockSpec((n, hd), lambda b, h, i: (b, h)),
                  pl.BlockSpec((n, 2 * hd), lambda b, h, i: (b, h))],
        out_specs=pl.BlockSpec((tq, g * hd), lambda b, h, i: (b * nq + i, h)),
        out_shape=jax.ShapeDtypeStruct(q.shape, BF16),
        scratch_shapes=[pltpu.VMEM((g * tq, hd), F32), pltpu.VMEM((g * tq, 2 * hd), F32)],
        compiler_params=pltpu.CompilerParams(dimension_semantics=("parallel", "parallel", "parallel"),
                                             vmem_limit_bytes=VMEM_LIMIT),
        name="gattn",
    )(q, k_ctx, _with_ones(v_ctx, GQA_KV_HEADS), k, _with_ones(v, GQA_KV_HEADS))


WATTN_TQ = 256


def _wattn_body(sink_ref, q_ref, kc_ref, vc_ref, kp_ref, kx_ref, kn_ref, vp_ref, vx_ref, vn_ref, o_ref,
                *, scale, n, nq):
    tq = q_ref.shape[0]
    hd = SWA_HEAD_DIM
    g = SWA_HEADS // SWA_KV_HEADS
    h = pl.program_id(1)
    i = pl.program_id(2)
    k_band = jnp.concatenate([kp_ref[...], kx_ref[...], kn_ref[...]], axis=0)
    v_band = jnp.concatenate([vp_ref[...], vx_ref[...], vn_ref[...]], axis=0)
    qpos = i * tq + lax.broadcasted_iota(jnp.int32, (tq, tq + 2 * WINDOW), 0)
    kpos = i * tq - WINDOW + lax.broadcasted_iota(jnp.int32, (tq, tq + 2 * WINDOW), 1)
    valid = (jnp.abs(kpos - qpos) <= WINDOW) & (kpos >= 0) & (kpos < n)
    outs = []
    for j in range(g):
        q = q_ref[:, j * hd:(j + 1) * hd]
        s_c = lax.dot_general(q, kc_ref[...], (((1,), (1,)), ((), ())), preferred_element_type=F32) * scale
        s_b = lax.dot_general(q, k_band, (((1,), (1,)), ((), ())), preferred_element_type=F32) * scale
        s_b = jnp.where(valid, s_b, -jnp.inf)
        sink = sink_ref[h * g + j]
        m = jnp.maximum(jnp.maximum(jnp.max(s_c, -1, keepdims=True), jnp.max(s_b, -1, keepdims=True)), sink)
        p_c = jnp.exp(s_c - m)
        p_b = jnp.exp(s_b - m)
        l = jnp.sum(p_c, -1, keepdims=True) + jnp.sum(p_b, -1, keepdims=True) + jnp.exp(sink - m)
        o = (jnp.dot(p_c.astype(BF16), vc_ref[...], preferred_element_type=F32)
             + jnp.dot(p_b.astype(BF16), v_band, preferred_element_type=F32))
        outs.append(o / l)
    o_ref[...] = jnp.concatenate(outs, axis=1).astype(o_ref.dtype)


def window_attention(q, k, v, k_ctx, v_ctx, sinks, batch):
    hd = SWA_HEAD_DIM
    n = q.shape[0] // batch
    n_c = k_ctx.shape[0] // batch
    g = SWA_HEADS // SWA_KV_HEADS
    tq = WATTN_TQ
    assert n % tq == 0 and tq % WINDOW == 0
    nq = n // tq
    r = tq // WINDOW
    nw = n // WINDOW
    k_h = k.reshape(batch * n, SWA_KV_HEADS, hd).transpose(1, 0, 2)
    v_h = v.reshape(batch * n, SWA_KV_HEADS, hd).transpose(1, 0, 2)
    kc_h = k_ctx.reshape(batch * n_c, SWA_KV_HEADS, hd).transpose(1, 0, 2)
    vc_h = v_ctx.reshape(batch * n_c, SWA_KV_HEADS, hd).transpose(1, 0, 2)
    prev_map = lambda b, h, i, s: (h, b * nw + jnp.maximum(i * r - 1, 0), 0)
    cur_map = lambda b, h, i, s: (h, b * nq + i, 0)
    next_map = lambda b, h, i, s: (h, b * nw + jnp.minimum((i + 1) * r, nw - 1), 0)
    ctx_map = lambda b, h, i, s: (h, b, 0)
    grid_spec = pltpu.PrefetchScalarGridSpec(
        num_scalar_prefetch=1,
        grid=(batch, SWA_KV_HEADS, nq),
        in_specs=[pl.BlockSpec((tq, g * hd), lambda b, h, i, s: (b * nq + i, h)),
                  pl.BlockSpec((None, n_c, hd), ctx_map),
                  pl.BlockSpec((None, n_c, hd), ctx_map),
                  pl.BlockSpec((None, WINDOW, hd), prev_map),
                  pl.BlockSpec((None, tq, hd), cur_map),
                  pl.BlockSpec((None, WINDOW, hd), next_map),
                  pl.BlockSpec((None, WINDOW, hd), prev_map),
                  pl.BlockSpec((None, tq, hd), cur_map),
                  pl.BlockSpec((None, WINDOW, hd), next_map)],
        out_specs=pl.BlockSpec((tq, g * hd), lambda b, h, i, s: (b * nq + i, h)),
    )
    return pl.pallas_call(
        functools.partial(_wattn_body, scale=hd ** -0.5, n=n, nq=nq),
        grid_spec=grid_spec,
        out_shape=jax.ShapeDtypeStruct(q.shape, BF16),
        compiler_params=pltpu.CompilerParams(dimension_semantics=("parallel", "parallel", "parallel"),
                                             vmem_limit_bytes=VMEM_LIMIT),
        name="wattn",
    )(sinks.astype(F32), q, kc_h, vc_h, k_h, k_h, k_h, v_h, v_h, v_h)


MOE_TB = 256
MOE_FC = 512


def _ffn_body(blk_e_ref, nused_ref, x_ref, wg_ref, wu_ref, bg_ref, bu_ref, wd_ref, bd_ref, o_ref):
    i = pl.program_id(0)

    @pl.when(i < nused_ref[0])
    def _():
        x = x_ref[...]
        acc = jnp.zeros(o_ref.shape, F32) + bd_ref[...]
        for c in range(D_EXPERT // MOE_FC):
            sl = slice(c * MOE_FC, (c + 1) * MOE_FC)
            gate = jnp.dot(x, wg_ref[:, sl], preferred_element_type=F32) + bg_ref[:, sl]
            up = jnp.dot(x, wu_ref[:, sl], preferred_element_type=F32) + bu_ref[:, sl]
            gate = jnp.minimum(gate, SWIGLU_LIMIT)
            up = jnp.clip(up, -SWIGLU_LIMIT, SWIGLU_LIMIT)
            glu = gate * jax.nn.sigmoid(gate * SWIGLU_ALPHA)
            act = ((up + 1.0) * glu).astype(BF16)
            acc = acc + jnp.dot(act, wd_ref[sl, :], preferred_element_type=F32)
        o_ref[...] = acc

    @pl.when(i >= nused_ref[0])
    def _():
        o_ref[...] = jnp.zeros(o_ref.shape, F32)


def expert_ffn(buf, blk_e, n_used, w_gate, w_up, b_gate, b_up, w_dn, b_dn):
    n_rows, d = buf.shape
    n_blk = n_rows // MOE_TB
    f = w_gate.shape[2]
    wmap = lambda i, e, u: (e[i], 0, 0)
    grid_spec = pltpu.PrefetchScalarGridSpec(
        num_scalar_prefetch=2,
        grid=(n_blk,),
        in_specs=[pl.BlockSpec((MOE_TB, d), lambda i, e, u: (i, 0)),
                  pl.BlockSpec((None, d, f), wmap),
                  pl.BlockSpec((None, d, f), wmap),
                  pl.BlockSpec((None, 1, f), wmap),
                  pl.BlockSpec((None, 1, f), wmap),
                  pl.BlockSpec((None, f, d), wmap),
                  pl.BlockSpec((None, 1, d), wmap)],
        out_specs=pl.BlockSpec((MOE_TB, d), lambda i, e, u: (i, 0)),
    )
    return pl.pallas_call(
        _ffn_body,
        grid_spec=grid_spec,
        out_shape=jax.ShapeDtypeStruct((n_rows, d), F32),
        compiler_params=pltpu.CompilerParams(dimension_semantics=("arbitrary",),
                                             vmem_limit_bytes=VMEM_LIMIT),
        name="expert_ffn",
    )(blk_e, n_used, buf, w_gate, w_up, b_gate, b_up, w_dn, b_dn)


def layer_norm(x):
    xf = x.astype(F32)
    xc = xf - jnp.mean(xf, -1, keepdims=True)
    return xc * lax.rsqrt(jnp.mean(xc * xc, -1, keepdims=True) + LN_EPS)


def modulate(x, shift, scale):
    return (layer_norm(x) * (1.0 + scale.astype(F32)) + shift.astype(F32)).astype(x.dtype)


def post_norm(x, update, gain, bias):
    y = layer_norm(DEEPNORM_ALPHA * x.astype(F32) + update.astype(F32))
    return (y * gain.astype(F32) + bias.astype(F32)).astype(x.dtype)


def rms_norm(x, gain):
    xf = x.astype(F32)
    y = xf * lax.rsqrt(jnp.mean(xf * xf, -1, keepdims=True) + RMS_EPS) * gain.astype(F32)
    return y.astype(x.dtype)


def split_projection(z):
    points = np.cumsum(SPLIT_SIZES)[:-1].tolist()
    return jnp.split(z, points, axis=-1)


def heads(z, n_heads):
    b, n, w = z.shape
    return z.reshape(b, n, n_heads, w // n_heads)


def group_heads(q, n_kv):
    b, n, h, hd = q.shape
    return q.reshape(b, n, n_kv, h // n_kv, hd)


def flat_heads(o):
    return o.reshape(o.shape[0], o.shape[1], -1)


def axial_rope(rows, head_dim):
    n_freq = head_dim // 4
    inv_freq = jnp.power(ROPE_THETA, -jnp.arange(n_freq, dtype=F32) / n_freq)
    t = jnp.arange(rows * GRID_W)
    row = (t // GRID_W).astype(F32)
    col = (t % GRID_W).astype(F32)
    ang = jnp.stack([row[:, None] * inv_freq, col[:, None] * inv_freq], axis=1)
    return jnp.cos(ang), jnp.sin(ang)


def apply_rope(x, rope):
    cos, sin = rope
    b, n, h, hd = x.shape
    xr = x.astype(F32).reshape(b, n, h, 2, 2, hd // 4)
    x1, x2 = xr[..., 0, :], xr[..., 1, :]
    c = cos[None, :, None]
    s = sin[None, :, None]
    out = jnp.stack([x1 * c - x2 * s, x2 * c + x1 * s], axis=-2)
    return out.reshape(x.shape).astype(x.dtype)


def softmax_attend(q, k, v, sinks=None):
    scale = q.shape[-1] ** -0.5
    s = jnp.einsum("bqkgd,bskd->bkgqs", q, k, preferred_element_type=F32) * scale
    if sinks is None:
        p = jax.nn.softmax(s, axis=-1)
    else:
        sink_col = jnp.broadcast_to(sinks.astype(F32)[None, :, :, None, None], s.shape[:-1] + (1,))
        p = jax.nn.softmax(jnp.concatenate([sink_col, s], axis=-1), axis=-1)[..., 1:]
    return jnp.einsum("bkgqs,bskd->bqkgd", p.astype(v.dtype), v)


def global_attention_latent(q, k, v, k_ctx, v_ctx):
    b, n = q.shape[:2]
    k_all = jnp.concatenate([k_ctx, k], axis=1)
    v_all = jnp.concatenate([v_ctx, v], axis=1)
    qb = q.reshape((b, n // Q_BLOCK, Q_BLOCK) + q.shape[2:]).swapaxes(0, 1)
    out = lax.map(lambda q_blk: softmax_attend(q_blk, k_all, v_all), qb)
    return out.swapaxes(0, 1).reshape(b, n, -1)


def window_attention_latent(q, k, v, k_ctx, v_ctx, sinks):
    b, n = q.shape[:2]
    n_blk = n // Q_BLOCK
    n_c = k_ctx.shape[1]
    scale = q.shape[-1] ** -0.5
    pad = ((0, 0), (Q_BLOCK, Q_BLOCK), (0, 0), (0, 0))
    kp = jnp.pad(k, pad)
    vp = jnp.pad(v, pad)
    qb = q.reshape((b, n_blk, Q_BLOCK) + q.shape[2:]).swapaxes(0, 1)
    qi = jnp.arange(Q_BLOCK)[:, None]
    kj = jnp.arange(3 * Q_BLOCK)[None, :]
    in_window = jnp.abs(kj - Q_BLOCK - qi) <= WINDOW
    sink_val = sinks.astype(F32)[None, :, :, None, None]

    def block(args):
        q_blk, i = args
        kb = lax.dynamic_slice_in_dim(kp, i * Q_BLOCK, 3 * Q_BLOCK, axis=1)
        vb = lax.dynamic_slice_in_dim(vp, i * Q_BLOCK, 3 * Q_BLOCK, axis=1)
        kpos = i * Q_BLOCK - Q_BLOCK + kj
        valid = in_window & (kpos >= 0) & (kpos < n)
        s_ctx = jnp.einsum("bqkgd,bskd->bkgqs", q_blk, k_ctx, preferred_element_type=F32) * scale
        s_band = jnp.einsum("bqkgd,bskd->bkgqs", q_blk, kb, preferred_element_type=F32) * scale
        s_band = jnp.where(valid, s_band, -jnp.inf)
        sink_col = jnp.broadcast_to(sink_val, s_ctx.shape[:-1] + (1,))
        p = jax.nn.softmax(jnp.concatenate([sink_col, s_ctx, s_band], axis=-1), axis=-1).astype(v.dtype)
        return (jnp.einsum("bkgqs,bskd->bqkgd", p[..., 1:1 + n_c], v_ctx)
                + jnp.einsum("bkgqs,bskd->bqkgd", p[..., 1 + n_c:], vb))

    out = lax.map(block, (qb, jnp.arange(n_blk)))
    return out.swapaxes(0, 1).reshape(b, n, -1)


def fourier_mix(u):
    b, n, _ = u.shape
    ug = u.astype(F32).reshape(b, n, FNET_GROUPS, FNET_GROUP_DIM)
    y = jnp.fft.fft2(ug, axes=(1, 3), norm="ortho").real
    return y.reshape(b, n, FNET_GROUPS * FNET_GROUP_DIM).astype(u.dtype)


def short_conv(u, w, bias):
    n = u.shape[1]
    up = jnp.pad(u, ((0, 0), (1, 1), (0, 0)))
    return up[:, :n] * w[0] + up[:, 1:n + 1] * w[1] + up[:, 2:] * w[2] + bias


def hyena_filter_spectrum(n, w1, b1, fr1, w2, b2, fr2, w3):
    t = jnp.linspace(0.0, 1.0, n, dtype=F32)[:, None]
    bands = (HYENA_EMB_DIM - 1) // 2
    omega = 2.0 * math.pi * jnp.arange(n, dtype=F32)[:, None] / n
    freqs = jnp.linspace(1e-4, bands - 1, bands, dtype=F32)[None, :]
    z = jnp.concatenate([t, jnp.cos(freqs * omega), -jnp.sin(freqs * omega)], axis=-1)
    h = jnp.sin(fr1.astype(F32) * (z @ w1.astype(F32) + b1.astype(F32)))
    h = jnp.sin(fr2.astype(F32) * (h @ w2.astype(F32) + b2.astype(F32)))
    h = (h @ w3.astype(F32)).reshape(n, HYENA_ORDER, 2, HYENA_WIDTH)
    max_decay = math.log(HYENA_DECAY_TARGET) / HYENA_SHORT_DECAY_PCT
    min_decay = math.log(HYENA_DECAY_TARGET) / HYENA_LONG_DECAY_PCT
    deltas = jnp.abs(jnp.linspace(min_decay, max_decay, HYENA_WIDTH, dtype=F32))
    h = h * jnp.exp(-t * deltas)[:, None, None, :]
    taps = jnp.concatenate([h[:, :, 0],
                            jnp.zeros((1, HYENA_ORDER, HYENA_WIDTH), F32),
                            h[:0:-1, :, 1]], axis=0)
    taps = taps * lax.rsqrt(jnp.sum(taps * taps, axis=0, keepdims=True) + 1e-6)
    return jnp.fft.rfft(taps, axis=0)


def hyena_mix(zc, conv_w, conv_b, filt, bias):
    n = zc.shape[1]
    u = short_conv(zc, conv_w, conv_b).astype(F32)
    x1, x2, v = jnp.split(u, 3, axis=-1)
    spec = hyena_filter_spectrum(n, *filt)

    def long_conv(z, order):
        zf = jnp.fft.rfft(z, n=2 * n, axis=1)
        y = jnp.fft.irfft(zf * spec[None, :, order], n=2 * n, axis=1)[:, :n]
        return y + z * bias[order].astype(F32)

    y = x1 * long_conv(v, 0)
    y = x2 * long_conv(y, 1)
    return y.astype(zc.dtype)


def merge_branches(h, branches, w_gate_l, w_branch_l, w_out_l):
    merged = sum(jax.nn.sigmoid(mm3(h, w_gate_l[i])) * mm3(br, w_branch_l[i]) for i, br in enumerate(branches))
    return mm3(merged, w_out_l)


def moe_ffn(tokens, router_w, router_b, w_gu, b_gu, w_dn, b_dn):
    n_tok, d = tokens.shape
    logits = (tokens @ router_w + router_b).astype(F32)
    top_v, top_i = lax.top_k(logits, TOP_K)
    weights = jax.nn.softmax(top_v, axis=-1)
    flat_e = top_i.reshape(-1)
    order = jnp.argsort(flat_e)
    sorted_e = flat_e[order]
    token_of = order // TOP_K
    counts = jnp.bincount(flat_e, length=N_EXPERTS)
    padded = (counts + MOE_TB - 1) // MOE_TB * MOE_TB
    start = jnp.cumsum(counts) - counts
    padded_end = jnp.cumsum(padded)
    padded_start = padded_end - padded
    n_asg = n_tok * TOP_K
    dest = (padded_start[sorted_e] + jnp.arange(n_asg) - start[sorted_e]).astype(jnp.int32)
    n_rows = n_asg + N_EXPERTS * MOE_TB
    n_blk = n_rows // MOE_TB
    src = jnp.zeros((n_rows,), jnp.int32).at[dest].set(token_of.astype(jnp.int32))
    buf = tokens.astype(BF16)[src]
    blk_e = jnp.minimum(jnp.searchsorted(padded_end, jnp.arange(n_blk) * MOE_TB, side="right"),
                        N_EXPERTS - 1).astype(jnp.int32)
    n_used = (padded_end[-1:] // MOE_TB).astype(jnp.int32)
    w_gate, w_up, b_gate, b_up, w_dn_b, b_dn_r = w_gu
    y = expert_ffn(buf, blk_e, n_used, w_gate, w_up, b_gate, b_up, w_dn_b, b_dn_r)
    inv = jnp.zeros((n_asg,), jnp.int32).at[order].set(dest)
    yk = y[inv].reshape(n_tok, TOP_K, d)
    return jnp.sum(yk * weights[:, :, None], axis=1)


def prep_expert_weights(w_gu, b_gu, w_dn, b_dn):
    return ((w_gu[:, :, 0::2].astype(BF16), w_gu[:, :, 1::2].astype(BF16),
             b_gu[:, None, 0::2].astype(F32), b_gu[:, None, 1::2].astype(F32),
             w_dn.astype(BF16), b_dn[:, None, :].astype(F32)), None, None, None)


def kernel(x, c, ctx, c_ctx, mod_w, mod_b, w_in, gqa_q_gain, gqa_k_gain, conv_w, conv_b,
           filt_w1, filt_b1, filt_freq1, filt_w2, filt_b2, filt_freq2, filt_w3, hyena_bias,
           swa_sinks, w_branch, w_gate, w_out, ln1_g, ln1_b, router_w, router_b,
           exp_w_gate_up, exp_b_gate_up, exp_w_down, exp_b_down, ln2_g, ln2_b):
    batch, n_lat, d = x.shape
    n_ctx = ctx.shape[1]
    rows = n_lat // GRID_W
    rope_gqa = axial_rope(rows, GQA_HEAD_DIM)
    rope_swa = axial_rope(rows, SWA_HEAD_DIM)
    x_lat, x_ctx = x, ctx
    for l in range(DEPTH):
        ctx_continues = l < DEPTH - 1
        mod_lat = jnp.split((jax.nn.silu(c) @ mod_w[l] + mod_b[l])[:, None, :], 6, axis=-1)
        mod_ctx = jnp.split((jax.nn.silu(c_ctx) @ mod_w[l] + mod_b[l])[None, None, :], 6, axis=-1)
        filt = (filt_w1[l], filt_b1[l], filt_freq1[l], filt_w2[l], filt_b2[l], filt_freq2[l], filt_w3[l])
        sinks = swa_sinks[l].reshape(SWA_KV_HEADS, SWA_HEADS // SWA_KV_HEADS)

        h_lat = modulate(x_lat, mod_lat[0], mod_lat[1])
        h_ctx = modulate(x_ctx, mod_ctx[0], mod_ctx[1])
        fa_l, gq_l, gk_l, gv_l, hy_l, sq_l, sk_l, sv_l = split_projection(mm3(h_lat, w_in[l]))
        fa_c, gq_c, gk_c, gv_c, hy_c, sq_c, sk_c, sv_c = split_projection(mm3(h_ctx, w_in[l]))
        gk_c = rms_norm(heads(gk_c, GQA_KV_HEADS), gqa_k_gain[l])
        gv_c = heads(gv_c, GQA_KV_HEADS)
        sk_c = heads(sk_c, SWA_KV_HEADS)
        sv_c = heads(sv_c, SWA_KV_HEADS)
        gq = apply_rope(rms_norm(heads(gq_l, GQA_HEADS), gqa_q_gain[l]), rope_gqa)
        gk = apply_rope(rms_norm(heads(gk_l, GQA_KV_HEADS), gqa_k_gain[l]), rope_gqa)
        sq = apply_rope(heads(sq_l, SWA_HEADS), rope_swa)
        sk = apply_rope(heads(sk_l, SWA_KV_HEADS), rope_swa)
        flat2 = lambda t: t.reshape(t.shape[0] * t.shape[1], -1).astype(BF16)
        ga = global_attention(flat2(gq), flat2(gk), flat2(gv_l), flat2(gk_c), flat2(gv_c), batch)
        wa = window_attention(flat2(sq), flat2(sk), flat2(sv_l), flat2(sk_c), flat2(sv_c), swa_sinks[l], batch)
        lat_branches = (
            fourier_mix(fa_l),
            ga.reshape(batch, n_lat, -1),
            hyena_mix(hy_l, conv_w[l], conv_b[l], filt, hyena_bias[l]),
            wa.reshape(batch, n_lat, -1),
        )
        mix_lat = merge_branches(h_lat, lat_branches, w_gate[l], w_branch[l], w_out[l])
        if ctx_continues:
            gq_ctx = group_heads(rms_norm(heads(gq_c, GQA_HEADS), gqa_q_gain[l]), GQA_KV_HEADS)
            sq_ctx = group_heads(heads(sq_c, SWA_HEADS), SWA_KV_HEADS)
            ctx_branches = (
                fourier_mix(fa_c),
                flat_heads(softmax_attend(gq_ctx, gk_c, gv_c)),
                hyena_mix(hy_c, conv_w[l], conv_b[l], filt, hyena_bias[l]),
                flat_heads(softmax_attend(sq_ctx, sk_c, sv_c, sinks)),
            )
            mix_ctx = merge_branches(h_ctx, ctx_branches, w_gate[l], w_branch[l], w_out[l])
            x_ctx = post_norm(x_ctx, mod_ctx[2] * mix_ctx, ln1_g[l], ln1_b[l])
        x_lat = post_norm(x_lat, mod_lat[2] * mix_lat, ln1_g[l], ln1_b[l])

        moe_args = (router_w[l], router_b[l]) + prep_expert_weights(
            exp_w_gate_up[l], exp_b_gate_up[l], exp_w_down[l], exp_b_down[l])
        h_lat = modulate(x_lat, mod_lat[3], mod_lat[4])
        if ctx_continues:
            h_ctx = modulate(x_ctx, mod_ctx[3], mod_ctx[4])
            tokens = jnp.concatenate([h_lat.reshape(-1, d), h_ctx.reshape(-1, d)], axis=0)
            y = moe_ffn(tokens, *moe_args)
            y_lat = y[:batch * n_lat].reshape(batch, n_lat, d)
            y_ctx = y[batch * n_lat:].reshape(batch, n_ctx, d)
            x_ctx = post_norm(x_ctx, mod_ctx[5] * y_ctx, ln2_g[l], ln2_b[l])
        else:
            y_lat = moe_ffn(h_lat.reshape(-1, d), *moe_args).reshape(batch, n_lat, d)
        x_lat = post_norm(x_lat, mod_lat[5] * y_lat, ln2_g[l], ln2_b[l])
    return x_lat
```

```python
import functools
import math

import jax
import jax.numpy as jnp
import numpy as np
from jax import lax
from jax.experimental import pallas as pl
from jax.experimental.pallas import tpu as pltpu

D_MODEL = 2048
DEPTH = 2
GRID_W = 64
FNET_GROUPS = 4
FNET_GROUP_DIM = 128
GQA_HEADS = 4
GQA_KV_HEADS = 2
GQA_HEAD_DIM = 128
HYENA_WIDTH = 512
HYENA_ORDER = 2
HYENA_EMB_DIM = 33
HYENA_DECAY_TARGET = 1e-2
HYENA_SHORT_DECAY_PCT = 0.3
HYENA_LONG_DECAY_PCT = 1.5
SWA_HEADS = 8
SWA_KV_HEADS = 2
SWA_HEAD_DIM = 64
WINDOW = 128
Q_BLOCK = 128
ROPE_THETA = 10000.0
N_EXPERTS = 32
TOP_K = 4
D_EXPERT = 1536
SWIGLU_LIMIT = 7.0
SWIGLU_ALPHA = 1.702
MOE_BLOCK = 128
LN_EPS = 1e-5
RMS_EPS = 1e-6
DEEPNORM_ALPHA = (2 * DEPTH) ** 0.25
SPLIT_SIZES = (FNET_GROUPS * FNET_GROUP_DIM,
               GQA_HEADS * GQA_HEAD_DIM, GQA_KV_HEADS * GQA_HEAD_DIM, GQA_KV_HEADS * GQA_HEAD_DIM,
               3 * HYENA_WIDTH,
               SWA_HEADS * SWA_HEAD_DIM, SWA_KV_HEADS * SWA_HEAD_DIM, SWA_KV_HEADS * SWA_HEAD_DIM)
F32 = jnp.float32
BF16 = jnp.bfloat16


def _mm_body(a_ref, b_ref, o_ref):
    o_ref[...] = jnp.dot(a_ref[...], b_ref[...], preferred_element_type=F32).astype(o_ref.dtype)


def _pick(n, cands):
    for c in cands:
        if n % c == 0:
            return c
    return n


def matmul(a, b, out_dtype=F32):
    m, k = a.shape
    n = b.shape[1]
    tm = _pick(m, (512, 256, 128))
    tn = _pick(n, (2048, 1280, 1024, 768, 512, 256, 128))
    return pl.pallas_call(
        _mm_body,
        grid=(n // tn, m // tm),
        in_specs=[pl.BlockSpec((tm, k), lambda j, i: (i, 0)),
                  pl.BlockSpec((k, tn), lambda j, i: (0, j))],
        out_specs=pl.BlockSpec((tm, tn), lambda j, i: (i, j)),
        out_shape=jax.ShapeDtypeStruct((m, n), out_dtype),
        compiler_params=pltpu.CompilerParams(dimension_semantics=("parallel", "parallel"),
                                             vmem_limit_bytes=VMEM_LIMIT),
        name="mm",
    )(a.astype(BF16), b.astype(BF16))


def mm3(x, w):
    b, n, k = x.shape
    return matmul(x.reshape(b * n, k), w).reshape(b, n, w.shape[1])


GATTN_TQ = 512
GATTN_TK = 256
GATTN_UNROLL = 4
VMEM_LIMIT = 56 * 1024 * 1024
LANES = 128
LOG2E = math.log2(math.e)


def _gattn_body(q_ref, kc_ref, vc_ref, k_ref, v_ref, o_ref, m_ref, acc_ref, *, c2):
    tq = q_ref.shape[0]
    hd = k_ref.shape[1]
    q2 = jnp.concatenate([q_ref[:, :hd], q_ref[:, hd:]], axis=0)
    m_ref[...] = jnp.full(m_ref.shape, -jnp.inf, F32)
    acc_ref[...] = jnp.zeros(acc_ref.shape, F32)

    def step(k, v):
        s = lax.dot_general(q2, k, (((1,), (1,)), ((), ())), preferred_element_type=F32) * c2
        m_old = m_ref[...]
        m_new = jnp.maximum(m_old, jnp.max(s, axis=-1, keepdims=True))
        alpha = jnp.exp2(m_old - m_new)
        p = jnp.concatenate([jnp.exp2(s[:, j * LANES:(j + 1) * LANES] - m_new)
                             for j in range(k.shape[0] // LANES)], axis=1)
        acc_ref[...] = (jnp.concatenate([alpha, alpha], axis=1) * acc_ref[...]
                        + jnp.dot(p.astype(BF16), v, preferred_element_type=F32))
        m_ref[...] = m_new

    step(kc_ref[...], vc_ref[...])

    def loop(c, carry):
        off = pl.multiple_of(c * GATTN_TK, GATTN_TK)
        step(k_ref[pl.ds(off, GATTN_TK), :], v_ref[pl.ds(off, GATTN_TK), :])
        return carry

    lax.fori_loop(0, k_ref.shape[0] // GATTN_TK, loop, 0, unroll=GATTN_UNROLL)
    acc = acc_ref[...]
    out = acc[:, :hd] / acc[:, hd:]
    o_ref[...] = jnp.concatenate([out[:tq], out[tq:]], axis=1).astype(o_ref.dtype)


def _with_ones(v, n_heads):
    r = v.shape[0]
    v3 = v.reshape(r, n_heads, -1)
    return jnp.concatenate([v3, jnp.ones_like(v3)], axis=-1).reshape(r, -1)


def global_attention(q, k, v, k_ctx, v_ctx, batch):
    hd = GQA_HEAD_DIM
    n = q.shape[0] // batch
    n_c = k_ctx.shape[0] // batch
    g = GQA_HEADS // GQA_KV_HEADS
    tq = min(GATTN_TQ, n)
    assert g == 2 and hd == LANES and n % tq == 0 and n % (GATTN_TK * GATTN_UNROLL) == 0 and n_c % LANES == 0
    nq = n // tq
    return pl.pallas_call(
        functools.partial(_gattn_body, c2=hd ** -0.5 * LOG2E),
        grid=(batch, GQA_KV_HEADS, nq),
        in_specs=[pl.BlockSpec((tq, g * hd), lambda b, h, i: (b * nq + i, h)),
                  pl.BlockSpec((n_c, hd), lambda b, h, i: (b, h)),
                  pl.BlockSpec((n_c, 2 * hd), lambda b, h, i: (b, h)),
                  pl.BlockSpec((n, hd), lambda b, h, i: (b, h)),
                  pl.BlockSpec((n, 2 * hd), lambda b, h, i: (b, h))],
        out_specs=pl.BlockSpec((tq, g * hd), lambda b, h, i: (b * nq + i, h)),
        out_shape=jax.ShapeDtypeStruct(q.shape, BF16),
        scratch_shapes=[pltpu.VMEM((g * tq, hd), F32), pltpu.VMEM((g * tq, 2 * hd), F32)],
        compiler_params=pltpu.CompilerParams(dimension_semantics=("parallel", "parallel", "parallel"),
                                             vmem_limit_bytes=VMEM_LIMIT),
        name="gattn",
    )(q, k_ctx, _with_ones(v_ctx, GQA_KV_HEADS), k, _with_ones(v, GQA_KV_HEADS))


WATTN_TQ = 256


def _wattn_body(sink_ref, q_ref, kc_ref, vc_ref, kp_ref, kx_ref, kn_ref, vp_ref, vx_ref, vn_ref, o_ref,
                *, scale, n, nq):
    tq = q_ref.shape[0]
    hd = SWA_HEAD_DIM
    g = SWA_HEADS // SWA_KV_HEADS
    h = pl.program_id(1)
    i = pl.program_id(2)
    k_band = jnp.concatenate([kp_ref[...], kx_ref[...], kn_ref[...]], axis=0)
    v_band = jnp.concatenate([vp_ref[...], vx_ref[...], vn_ref[...]], axis=0)
    qpos = i * tq + lax.broadcasted_iota(jnp.int32, (tq, tq + 2 * WINDOW), 0)
    kpos = i * tq - WINDOW + lax.broadcasted_iota(jnp.int32, (tq, tq + 2 * WINDOW), 1)
    valid = (jnp.abs(kpos - qpos) <= WINDOW) & (kpos >= 0) & (kpos < n)
    outs = []
    for j in range(g):
        q = q_ref[:, j * hd:(j + 1) * hd]
        s_c = lax.dot_general(q, kc_ref[...], (((1,), (1,)), ((), ())), preferred_element_type=F32) * scale
        s_b = lax.dot_general(q, k_band, (((1,), (1,)), ((), ())), preferred_element_type=F32) * scale
        s_b = jnp.where(valid, s_b, -jnp.inf)
        sink = sink_ref[h * g + j]
        m = jnp.maximum(jnp.maximum(jnp.max(s_c, -1, keepdims=True), jnp.max(s_b, -1, keepdims=True)), sink)
        p_c = jnp.exp(s_c - m)
        p_b = jnp.exp(s_b - m)
        l = jnp.sum(p_c, -1, keepdims=True) + jnp.sum(p_b, -1, keepdims=True) + jnp.exp(sink - m)
        o = (jnp.dot(p_c.astype(BF16), vc_ref[...], preferred_element_type=F32)
             + jnp.dot(p_b.astype(BF16), v_band, preferred_element_type=F32))
        outs.append(o / l)
    o_ref[...] = jnp.concatenate(outs, axis=1).astype(o_ref.dtype)


def window_attention(q, k, v, k_ctx, v_ctx, sinks, batch):
    hd = SWA_HEAD_DIM
    n = q.shape[0] // batch
    n_c = k_ctx.shape[0] // batch
    g = SWA_HEADS // SWA_KV_HEADS
    tq = WATTN_TQ
    assert n % tq == 0 and tq % WINDOW == 0
    nq = n // tq
    r = tq // WINDOW
    nw = n // WINDOW
    k_h = k.reshape(batch * n, SWA_KV_HEADS, hd).transpose(1, 0, 2)
    v_h = v.reshape(batch * n, SWA_KV_HEADS, hd).transpose(1, 0, 2)
    kc_h = k_ctx.reshape(batch * n_c, SWA_KV_HEADS, hd).transpose(1, 0, 2)
    vc_h = v_ctx.reshape(batch * n_c, SWA_KV_HEADS, hd).transpose(1, 0, 2)
    prev_map = lambda b, h, i, s: (h, b * nw + jnp.maximum(i * r - 1, 0), 0)
    cur_map = lambda b, h, i, s: (h, b * nq + i, 0)
    next_map = lambda b, h, i, s: (h, b * nw + jnp.minimum((i + 1) * r, nw - 1), 0)
    ctx_map = lambda b, h, i, s: (h, b, 0)
    grid_spec = pltpu.PrefetchScalarGridSpec(
        num_scalar_prefetch=1,
        grid=(batch, SWA_KV_HEADS, nq),
        in_specs=[pl.BlockSpec((tq, g * hd), lambda b, h, i, s: (b * nq + i, h)),
                  pl.BlockSpec((None, n_c, hd), ctx_map),
                  pl.BlockSpec((None, n_c, hd), ctx_map),
                  pl.BlockSpec((None, WINDOW, hd), prev_map),
                  pl.BlockSpec((None, tq, hd), cur_map),
                  pl.BlockSpec((None, WINDOW, hd), next_map),
                  pl.BlockSpec((None, WINDOW, hd), prev_map),
                  pl.BlockSpec((None, tq, hd), cur_map),
                  pl.BlockSpec((None, WINDOW, hd), next_map)],
        out_specs=pl.BlockSpec((tq, g * hd), lambda b, h, i, s: (b * nq + i, h)),
    )
    return pl.pallas_call(
        functools.partial(_wattn_body, scale=hd ** -0.5, n=n, nq=nq),
        grid_spec=grid_spec,
        out_shape=jax.ShapeDtypeStruct(q.shape, BF16),
        compiler_params=pltpu.CompilerParams(dimension_semantics=("parallel", "parallel", "parallel"),
                                             vmem_limit_bytes=VMEM_LIMIT),
        name="wattn",
    )(sinks.astype(F32), q, kc_h, vc_h, k_h, k_h, k_h, v_h, v_h, v_h)


MOE_TB = 256
MOE_FC = 512


def _ffn_body(blk_e_ref, nused_ref, x_ref, wg_ref, wu_ref, bg_ref, bu_ref, wd_ref, bd_ref, o_ref):
    i = pl.program_id(0)

    @pl.when(i < nused_ref[0])
    def _():
        x = x_ref[...]
        acc = jnp.zeros(o_ref.shape, F32) + bd_ref[...]
        for c in range(D_EXPERT // MOE_FC):
            sl = slice(c * MOE_FC, (c + 1) * MOE_FC)
            gate = jnp.dot(x, wg_ref[:, sl], preferred_element_type=F32) + bg_ref[:, sl]
            up = jnp.dot(x, wu_ref[:, sl], preferred_element_type=F32) + bu_ref[:, sl]
            gate = jnp.minimum(gate, SWIGLU_LIMIT)
            up = jnp.clip(up, -SWIGLU_LIMIT, SWIGLU_LIMIT)
            glu = gate * jax.nn.sigmoid(gate * SWIGLU_ALPHA)
            act = ((up + 1.0) * glu).astype(BF16)
            acc = acc + jnp.dot(act, wd_ref[sl, :], preferred_element_type=F32)
        o_ref[...] = acc

    @pl.when(i >= nused_ref[0])
    def _():
        o_ref[...] = jnp.zeros(o_ref.shape, F32)


def expert_ffn(buf, blk_e, n_used, w_gate, w_up, b_gate, b_up, w_dn, b_dn):
    n_rows, d = buf.shape
    n_blk = n_rows // MOE_TB
    f = w_gate.shape[2]
    wmap = lambda i, e, u: (e[i], 0, 0)
    grid_spec = pltpu.PrefetchScalarGridSpec(
        num_scalar_prefetch=2,
        grid=(n_blk,),
        in_specs=[pl.BlockSpec((MOE_TB, d), lambda i, e, u: (i, 0)),
                  pl.BlockSpec((None, d, f), wmap),
                  pl.BlockSpec((None, d, f), wmap),
                  pl.BlockSpec((None, 1, f), wmap),
                  pl.BlockSpec((None, 1, f), wmap),
                  pl.BlockSpec((None, f, d), wmap),
                  pl.BlockSpec((None, 1, d), wmap)],
        out_specs=pl.BlockSpec((MOE_TB, d), lambda i, e, u: (i, 0)),
    )
    return pl.pallas_call(
        _ffn_body,
        grid_spec=grid_spec,
        out_shape=jax.ShapeDtypeStruct((n_rows, d), F32),
        compiler_params=pltpu.CompilerParams(dimension_semantics=("arbitrary",),
                                             vmem_limit_bytes=VMEM_LIMIT),
        name="expert_ffn",
    )(blk_e, n_used, buf, w_gate, w_up, b_gate, b_up, w_dn, b_dn)


def _deint_body(w_ref, p_ref, g_ref, u_ref):
    y = jnp.dot(w_ref[...].astype(BF16), p_ref[...], preferred_element_type=F32)
    half = g_ref.shape[1]
    g_ref[...] = y[:, :half].astype(BF16)
    u_ref[...] = y[:, half:].astype(BF16)


def split_gate_up(w_gu):
    e, d, f2 = w_gu.shape
    blk = 2 * LANES
    src = jnp.arange(blk)[:, None]
    dst = jnp.arange(blk)[None, :]
    perm = (src == jnp.where(dst < LANES, 2 * dst, 2 * (dst - LANES) + 1)).astype(BF16)
    out = jax.ShapeDtypeStruct((e, d, f2 // 2), BF16)
    return pl.pallas_call(
        _deint_body,
        grid=(e, f2 // blk),
        in_specs=[pl.BlockSpec((None, d, blk), lambda i, j: (i, 0, j)),
                  pl.BlockSpec((blk, blk), lambda i, j: (0, 0))],
        out_specs=[pl.BlockSpec((None, d, LANES), lambda i, j: (i, 0, j)),
                   pl.BlockSpec((None, d, LANES), lambda i, j: (i, 0, j))],
        out_shape=[out, out],
        compiler_params=pltpu.CompilerParams(dimension_semantics=("parallel", "parallel")),
        name="split_gate_up",
    )(w_gu, perm)


FFT_N2 = 128
FFT_SPLIT_MIN = 1024


def _angles(num, den):
    return (2.0 * math.pi / den) * (num % den).astype(F32)


def dft_stage_a(n1, k_in, real_input):
    ang = _angles(jnp.arange(n1)[:, None] * jnp.arange(k_in)[None, :], n1)
    c, s = jnp.cos(ang), jnp.sin(ang)
    if real_input:
        return jnp.concatenate([c, -s], axis=0).astype(BF16)
    return jnp.block([[c, s], [-s, c]]).astype(BF16)


def dft_stage_a_inv(n1, k_out):
    ang = _angles(jnp.arange(k_out)[:, None] * jnp.arange(n1)[None, :], n1)
    c, s = jnp.cos(ang), jnp.sin(ang)
    return jnp.block([[c, -s], [s, c]]).astype(BF16)


def dft_stage_b(n1, n2):
    k1 = jnp.arange(n1)[:, None, None]
    k2 = jnp.arange(n2)[None, :, None]
    m = jnp.arange(n2)[None, None, :]
    ang = _angles(m * (k1 + n1 * k2), n1 * n2)
    c, s = jnp.cos(ang), jnp.sin(ang)
    top = jnp.concatenate([c, s], axis=2)
    bot = jnp.concatenate([-s, c], axis=2)
    return jnp.concatenate([top, bot], axis=1).astype(BF16)


def _lmm_body(m_ref, x_ref, o_ref):
    o_ref[...] = jnp.dot(m_ref[...], x_ref[...].astype(BF16), preferred_element_type=F32).astype(o_ref.dtype)


def left_matmul(m, x, tn=2048):
    g, k, cols = x.shape
    r = m.shape[0]
    tn = min(tn, cols)
    assert cols % tn == 0 and m.shape[1] == k
    return pl.pallas_call(
        _lmm_body,
        grid=(g, cols // tn),
        in_specs=[pl.BlockSpec((r, k), lambda i, j: (0, 0)),
                  pl.BlockSpec((None, k, tn), lambda i, j: (i, 0, j))],
        out_specs=pl.BlockSpec((None, r, tn), lambda i, j: (i, 0, j)),
        out_shape=jax.ShapeDtypeStruct((g, r, cols), F32),
        compiler_params=pltpu.CompilerParams(dimension_semantics=("parallel", "parallel"),
                                             vmem_limit_bytes=VMEM_LIMIT),
        name="left_matmul",
    )(m, x)


def _conv_b_body(t_ref, gs_ref, h_ref, u_ref):
    n2 = t_ref.shape[1]
    c = t_ref.shape[2]
    gs = gs_ref[...]
    y = jnp.dot(gs, t_ref[...].reshape(2 * n2, c).astype(BF16), preferred_element_type=F32)
    yr, yi = y[:n2], y[n2:]
    hr, hi = h_ref[0], h_ref[1]
    z = jnp.concatenate([yr * hr - yi * hi, yr * hi + yi * hr], axis=0).astype(BF16)
    u = lax.dot_general(gs, z, (((0,), (0,)), ((), ())), preferred_element_type=F32)
    u_ref[...] = u.reshape(2, n2, c)


def conv_stage_b(t, gs, h):
    _, n1, n2, c = t.shape
    spec = pl.BlockSpec((2, None, n2, c), lambda i: (0, i, 0, 0))
    return pl.pallas_call(
        _conv_b_body,
        grid=(n1,),
        in_specs=[spec, pl.BlockSpec((None, 2 * n2, 2 * n2), lambda i: (i, 0, 0)), spec],
        out_specs=spec,
        out_shape=jax.ShapeDtypeStruct(t.shape, F32),
        compiler_params=pltpu.CompilerParams(dimension_semantics=("parallel",), vmem_limit_bytes=VMEM_LIMIT),
        name="conv_stage_b",
    )(t, gs, h)


def _filt_b_body(t_ref, gs_ref, ss_ref, h_ref):
    n2 = t_ref.shape[1]
    c = t_ref.shape[2]
    w = HYENA_WIDTH
    y = jnp.dot(gs_ref[...], t_ref[...].reshape(2 * n2, c).astype(BF16), preferred_element_type=F32)
    yr, yi = y[:n2], y[n2:]
    ss = ss_ref[...]
    for o in range(HYENA_ORDER):
        f0, b0 = (2 * o) * w, (2 * o + 1) * w
        scale = lax.rsqrt(ss[:, f0:f0 + w] + ss[:, b0:b0 + w] + 1e-6)
        h_ref[o, 0] = (yr[:, f0:f0 + w] + yr[:, b0:b0 + w]) * scale
        h_ref[o, 1] = (yi[:, f0:f0 + w] - yi[:, b0:b0 + w]) * scale


def filter_stage_b(t, gs, sumsq):
    _, n1, n2, c = t.shape
    w = HYENA_WIDTH
    return pl.pallas_call(
        _filt_b_body,
        grid=(n1,),
        in_specs=[pl.BlockSpec((2, None, n2, c), lambda i: (0, i, 0, 0)),
                  pl.BlockSpec((None, 2 * n2, 2 * n2), lambda i: (i, 0, 0)),
                  pl.BlockSpec((1, c), lambda i: (0, 0))],
        out_specs=pl.BlockSpec((HYENA_ORDER, 2, None, n2, w), lambda i: (0, 0, i, 0, 0)),
        out_shape=jax.ShapeDtypeStruct((HYENA_ORDER, 2, n1, n2, w), F32),
        compiler_params=pltpu.CompilerParams(dimension_semantics=("parallel",), vmem_limit_bytes=VMEM_LIMIT),
        name="filter_stage_b",
    )(t, gs, sumsq)


def _fnet_b_body(t_ref, gs_ref, cs_ref, o_ref):
    n2 = t_ref.shape[1]
    c = t_ref.shape[2]
    y = jnp.dot(gs_ref[...], t_ref[...].reshape(2 * n2, c).astype(BF16), preferred_element_type=F32)
    yr, yi = y[:n2].astype(BF16), y[n2:].astype(BF16)
    gd = FNET_GROUP_DIM
    outs = []
    for g in range(c // gd):
        v = jnp.concatenate([yr[:, g * gd:(g + 1) * gd], yi[:, g * gd:(g + 1) * gd]], axis=1)
        outs.append(jnp.dot(v, cs_ref[...], preferred_element_type=F32))
    o_ref[...] = jnp.concatenate(outs, axis=1).astype(o_ref.dtype)


def fnet_stage_b(t, gs, cs, out_dtype):
    b, _, n1, n2, c = t.shape
    return pl.pallas_call(
        _fnet_b_body,
        grid=(b, n1),
        in_specs=[pl.BlockSpec((None, 2, None, n2, c), lambda i, j: (i, 0, j, 0, 0)),
                  pl.BlockSpec((None, 2 * n2, 2 * n2), lambda i, j: (j, 0, 0)),
                  pl.BlockSpec(cs.shape, lambda i, j: (0, 0))],
        out_specs=pl.BlockSpec((None, n2, c), lambda i, j: (i, 0, j)),
        out_shape=jax.ShapeDtypeStruct((b, n2, n1 * c), out_dtype),
        compiler_params=pltpu.CompilerParams(dimension_semantics=("parallel", "parallel"),
                                             vmem_limit_bytes=VMEM_LIMIT),
        name="fnet_stage_b",
    )(t, gs, cs)


def fourier_mix_pallas(u):
    b, n, c = u.shape
    gd = FNET_GROUP_DIM
    n2 = FFT_N2 if n >= FFT_SPLIT_MIN else n
    n1 = n // n2
    gs = dft_stage_b(n1, n2)
    if n1 > 1:
        t = left_matmul(dft_stage_a(n1, n1, True), u.astype(F32).reshape(b, n1, n2 * c))
        t = t.reshape(b, 2, n1, n2, c)
    else:
        t = jnp.stack([u.astype(F32), jnp.zeros(u.shape, F32)], axis=1).reshape(b, 2, 1, n2, c)
    ang = _angles(jnp.arange(gd)[:, None] * jnp.arange(gd)[None, :], gd)
    cs = (jnp.concatenate([jnp.cos(ang), jnp.sin(ang)], axis=0) * (n * gd) ** -0.5).astype(BF16)
    out = fnet_stage_b(t, gs, cs, u.dtype)
    return out.reshape(b, n2, n1, c).reshape(b, n, c)


FILT_TM = 512
FILT_LANES = 128


def _filter_body(ff_ref, w1_ref, b1_ref, fr1_ref, w2_ref, b2_ref, fr2_ref, w3_ref, dl_ref, taps_ref, ss_ref, *, n):
    i = pl.program_id(0)
    tm = taps_ref.shape[0]
    hi = lax.Precision.HIGHEST
    pos = (i * tm + lax.broadcasted_iota(jnp.int32, (tm, FILT_LANES), 0)).astype(F32)
    lane = lax.broadcasted_iota(jnp.int32, (tm, FILT_LANES), 1)
    t = pos / (n - 1.0) if n > 1 else pos * 0.0
    omega = (2.0 * math.pi) * pos / n
    arg = ff_ref[...] * omega
    bands = (HYENA_EMB_DIM - 1) // 2
    z = jnp.where(lane == 0, t, jnp.where(lane <= bands, jnp.cos(arg),
                                          jnp.where(lane <= 2 * bands, -jnp.sin(arg), 0.0)))
    h = jnp.sin(fr1_ref[...] * (jnp.dot(z, w1_ref[...], precision=hi, preferred_element_type=F32) + b1_ref[...]))
    h = jnp.sin(fr2_ref[...] * (jnp.dot(h, w2_ref[...], precision=hi, preferred_element_type=F32) + b2_ref[...]))
    h = jnp.dot(h, w3_ref[...], precision=hi, preferred_element_type=F32)
    width = h.shape[1]
    h = h * jnp.exp(-t[:, :1] * dl_ref[...])
    row = i * tm + lax.broadcasted_iota(jnp.int32, (tm, width), 0)
    col = lax.broadcasted_iota(jnp.int32, (tm, width), 1)
    h = jnp.where((row == 0) & ((col // HYENA_WIDTH) % 2 == 1), 0.0, h)
    taps_ref[...] = h

    @pl.when(i == 0)
    def _():
        ss_ref[...] = jnp.zeros(ss_ref.shape, F32)

    ss_ref[...] += jnp.sum(h * h, axis=0, keepdims=True)


def hyena_taps(n, w1, b1, fr1, w2, b2, fr2, w3):
    bands = (HYENA_EMB_DIM - 1) // 2
    fd = w1.shape[1]
    width = w3.shape[1]
    freqs = jnp.linspace(1e-4, bands - 1, bands, dtype=F32)
    featfreq = jnp.zeros((1, FILT_LANES), F32).at[0, 1:1 + bands].set(freqs).at[0, 1 + bands:1 + 2 * bands].set(freqs)
    w1p = jnp.zeros((FILT_LANES, fd), F32).at[:HYENA_EMB_DIM].set(w1.astype(F32))
    max_decay = math.log(HYENA_DECAY_TARGET) / HYENA_SHORT_DECAY_PCT
    min_decay = math.log(HYENA_DECAY_TARGET) / HYENA_LONG_DECAY_PCT
    deltas = jnp.abs(jnp.linspace(min_decay, max_decay, HYENA_WIDTH, dtype=F32))
    dl = jnp.tile(deltas, width // HYENA_WIDTH)[None, :]
    tm = min(FILT_TM, n)
    full = lambda a: pl.BlockSpec(a.shape, lambda i: (0,) * a.ndim)
    ops = (featfreq, w1p, b1.astype(F32)[None, :], fr1.astype(F32)[None, :], w2.astype(F32),
           b2.astype(F32)[None, :], fr2.astype(F32)[None, :], w3.astype(F32), dl)
    return pl.pallas_call(
        functools.partial(_filter_body, n=n),
        grid=(n // tm,),
        in_specs=[full(a) for a in ops],
        out_specs=[pl.BlockSpec((tm, width), lambda i: (i, 0)), pl.BlockSpec((1, width), lambda i: (0, 0))],
        out_shape=[jax.ShapeDtypeStruct((n, width), F32), jax.ShapeDtypeStruct((1, width), F32)],
        compiler_params=pltpu.CompilerParams(dimension_semantics=("arbitrary",), vmem_limit_bytes=VMEM_LIMIT),
        name="hyena_taps",
    )(*ops)


def _fft_split(n):
    if n >= FFT_SPLIT_MIN:
        return 2 * n // FFT_N2, FFT_N2, n // FFT_N2
    return 1, 2 * n, 1


def hyena_mix_pallas(zc, conv_w, conv_b, filt, bias):
    batch, n, _ = zc.shape
    assert batch == 2
    w = HYENA_WIDTH
    u = short_conv(zc, conv_w, conv_b).astype(F32)
    x1, x2, v = jnp.split(u, 3, axis=-1)
    n1, n2, k1 = _fft_split(n)
    gs = dft_stage_b(n1, n2)
    taps, sumsq = hyena_taps(n, *filt)
    if n1 > 1:
        t = left_matmul(dft_stage_a(n1, k1, True), taps.reshape(1, k1, n2 * taps.shape[1]))
        t = t.reshape(2, n1, n2, taps.shape[1])
    else:
        t = jnp.stack([jnp.pad(taps, ((0, n), (0, 0))), jnp.zeros((2 * n, taps.shape[1]), F32)])[:, None]
    spec = filter_stage_b(t, gs, sumsq)
    fwd = dft_stage_a(n1, k1, False)
    inv = dft_stage_a_inv(n1, k1)

    def long_conv(z, order):
        if n1 > 1:
            t = left_matmul(fwd, z.reshape(1, 2 * k1, n2 * w)).reshape(2, n1, n2, w)
            uu = conv_stage_b(t, gs, spec[order])
            y = left_matmul(inv, uu.reshape(1, 2 * n1, n2 * w)).reshape(2, n, w)
        else:
            t = jnp.pad(z, ((0, 0), (0, n), (0, 0)))[:, None]
            y = conv_stage_b(t, gs, spec[order])[:, 0, :n]
        return y * (1.0 / (2 * n)) + z * bias[order].astype(F32)

    y = x1 * long_conv(v, 0)
    y = x2 * long_conv(y, 1)
    return y.astype(zc.dtype)


def layer_norm(x):
    xf = x.astype(F32)
    xc = xf - jnp.mean(xf, -1, keepdims=True)
    return xc * lax.rsqrt(jnp.mean(xc * xc, -1, keepdims=True) + LN_EPS)


def modulate(x, shift, scale):
    return (layer_norm(x) * (1.0 + scale.astype(F32)) + shift.astype(F32)).astype(x.dtype)


def post_norm(x, update, gain, bias):
    y = layer_norm(DEEPNORM_ALPHA * x.astype(F32) + update.astype(F32))
    return (y * gain.astype(F32) + bias.astype(F32)).astype(x.dtype)


def rms_norm(x, gain):
    xf = x.astype(F32)
    y = xf * lax.rsqrt(jnp.mean(xf * xf, -1, keepdims=True) + RMS_EPS) * gain.astype(F32)
    return y.astype(x.dtype)


def split_projection(z):
    points = np.cumsum(SPLIT_SIZES)[:-1].tolist()
    return jnp.split(z, points, axis=-1)


def heads(z, n_heads):
    b, n, w = z.shape
    return z.reshape(b, n, n_heads, w // n_heads)


def group_heads(q, n_kv):
    b, n, h, hd = q.shape
    return q.reshape(b, n, n_kv, h // n_kv, hd)


def flat_heads(o):
    return o.reshape(o.shape[0], o.shape[1], -1)


def axial_rope(rows, head_dim):
    n_freq = head_dim // 4
    inv_freq = jnp.power(ROPE_THETA, -jnp.arange(n_freq, dtype=F32) / n_freq)
    t = jnp.arange(rows * GRID_W)
    row = (t // GRID_W).astype(F32)
    col = (t % GRID_W).astype(F32)
    ang = jnp.stack([row[:, None] * inv_freq, col[:, None] * inv_freq], axis=1)
    return jnp.cos(ang), jnp.sin(ang)


def apply_rope(x, rope):
    cos, sin = rope
    b, n, h, hd = x.shape
    xr = x.astype(F32).reshape(b, n, h, 2, 2, hd // 4)
    x1, x2 = xr[..., 0, :], xr[..., 1, :]
    c = cos[None, :, None]
    s = sin[None, :, None]
    out = jnp.stack([x1 * c - x2 * s, x2 * c + x1 * s], axis=-2)
    return out.reshape(x.shape).astype(x.dtype)


def softmax_attend(q, k, v, sinks=None):
    scale = q.shape[-1] ** -0.5
    s = jnp.einsum("bqkgd,bskd->bkgqs", q, k, preferred_element_type=F32) * scale
    if sinks is None:
        p = jax.nn.softmax(s, axis=-1)
    else:
        sink_col = jnp.broadcast_to(sinks.astype(F32)[None, :, :, None, None], s.shape[:-1] + (1,))
        p = jax.nn.softmax(jnp.concatenate([sink_col, s], axis=-1), axis=-1)[..., 1:]
    return jnp.einsum("bkgqs,bskd->bqkgd", p.astype(v.dtype), v)


def global_attention_latent(q, k, v, k_ctx, v_ctx):
    b, n = q.shape[:2]
    k_all = jnp.concatenate([k_ctx, k], axis=1)
    v_all = jnp.concatenate([v_ctx, v], axis=1)
    qb = q.reshape((b, n // Q_BLOCK, Q_BLOCK) + q.shape[2:]).swapaxes(0, 1)
    out = lax.map(lambda q_blk: softmax_attend(q_blk, k_all, v_all), qb)
    return out.swapaxes(0, 1).reshape(b, n, -1)


def window_attention_latent(q, k, v, k_ctx, v_ctx, sinks):
    b, n = q.shape[:2]
    n_blk = n // Q_BLOCK
    n_c = k_ctx.shape[1]
    scale = q.shape[-1] ** -0.5
    pad = ((0, 0), (Q_BLOCK, Q_BLOCK), (0, 0), (0, 0))
    kp = jnp.pad(k, pad)
    vp = jnp.pad(v, pad)
    qb = q.reshape((b, n_blk, Q_BLOCK) + q.shape[2:]).swapaxes(0, 1)
    qi = jnp.arange(Q_BLOCK)[:, None]
    kj = jnp.arange(3 * Q_BLOCK)[None, :]
    in_window = jnp.abs(kj - Q_BLOCK - qi) <= WINDOW
    sink_val = sinks.astype(F32)[None, :, :, None, None]

    def block(args):
        q_blk, i = args
        kb = lax.dynamic_slice_in_dim(kp, i * Q_BLOCK, 3 * Q_BLOCK, axis=1)
        vb = lax.dynamic_slice_in_dim(vp, i * Q_BLOCK, 3 * Q_BLOCK, axis=1)
        kpos = i * Q_BLOCK - Q_BLOCK + kj
        valid = in_window & (kpos >= 0) & (kpos < n)
        s_ctx = jnp.einsum("bqkgd,bskd->bkgqs", q_blk, k_ctx, preferred_element_type=F32) * scale
        s_band = jnp.einsum("bqkgd,bskd->bkgqs", q_blk, kb, preferred_element_type=F32) * scale
        s_band = jnp.where(valid, s_band, -jnp.inf)
        sink_col = jnp.broadcast_to(sink_val, s_ctx.shape[:-1] + (1,))
        p = jax.nn.softmax(jnp.concatenate([sink_col, s_ctx, s_band], axis=-1), axis=-1).astype(v.dtype)
        return (jnp.einsum("bkgqs,bskd->bqkgd", p[..., 1:1 + n_c], v_ctx)
                + jnp.einsum("bkgqs,bskd->bqkgd", p[..., 1 + n_c:], vb))

    out = lax.map(block, (qb, jnp.arange(n_blk)))
    return out.swapaxes(0, 1).reshape(b, n, -1)


def fourier_mix(u):
    b, n, _ = u.shape
    ug = u.astype(F32).reshape(b, n, FNET_GROUPS, FNET_GROUP_DIM)
    y = jnp.fft.fft2(ug, axes=(1, 3), norm="ortho").real
    return y.reshape(b, n, FNET_GROUPS * FNET_GROUP_DIM).astype(u.dtype)


def short_conv(u, w, bias):
    n = u.shape[1]
    up = jnp.pad(u, ((0, 0), (1, 1), (0, 0)))
    return up[:, :n] * w[0] + up[:, 1:n + 1] * w[1] + up[:, 2:] * w[2] + bias


def hyena_filter_spectrum(n, w1, b1, fr1, w2, b2, fr2, w3):
    t = jnp.linspace(0.0, 1.0, n, dtype=F32)[:, None]
    bands = (HYENA_EMB_DIM - 1) // 2
    omega = 2.0 * math.pi * jnp.arange(n, dtype=F32)[:, None] / n
    freqs = jnp.linspace(1e-4, bands - 1, bands, dtype=F32)[None, :]
    z = jnp.concatenate([t, jnp.cos(freqs * omega), -jnp.sin(freqs * omega)], axis=-1)
    h = jnp.sin(fr1.astype(F32) * (z @ w1.astype(F32) + b1.astype(F32)))
    h = jnp.sin(fr2.astype(F32) * (h @ w2.astype(F32) + b2.astype(F32)))
    h = (h @ w3.astype(F32)).reshape(n, HYENA_ORDER, 2, HYENA_WIDTH)
    max_decay = math.log(HYENA_DECAY_TARGET) / HYENA_SHORT_DECAY_PCT
    min_decay = math.log(HYENA_DECAY_TARGET) / HYENA_LONG_DECAY_PCT
    deltas = jnp.abs(jnp.linspace(min_decay, max_decay, HYENA_WIDTH, dtype=F32))
    h = h * jnp.exp(-t * deltas)[:, None, None, :]
    taps = jnp.concatenate([h[:, :, 0],
                            jnp.zeros((1, HYENA_ORDER, HYENA_WIDTH), F32),
                            h[:0:-1, :, 1]], axis=0)
    taps = taps * lax.rsqrt(jnp.sum(taps * taps, axis=0, keepdims=True) + 1e-6)
    return jnp.fft.rfft(taps, axis=0)


def hyena_mix(zc, conv_w, conv_b, filt, bias):
    n = zc.shape[1]
    u = short_conv(zc, conv_w, conv_b).astype(F32)
    x1, x2, v = jnp.split(u, 3, axis=-1)
    spec = hyena_filter_spectrum(n, *filt)

    def long_conv(z, order):
        zf = jnp.fft.rfft(z, n=2 * n, axis=1)
        y = jnp.fft.irfft(zf * spec[None, :, order], n=2 * n, axis=1)[:, :n]
        return y + z * bias[order].astype(F32)

    y = x1 * long_conv(v, 0)
    y = x2 * long_conv(y, 1)
    return y.astype(zc.dtype)


def merge_branches(h, branches, w_gate_l, w_branch_l, w_out_l):
    merged = sum(jax.nn.sigmoid(mm3(h, w_gate_l[i])) * mm3(br, w_branch_l[i]) for i, br in enumerate(branches))
    return mm3(merged, w_out_l)


def moe_ffn(tokens, router_w, router_b, w_gu, b_gu, w_dn, b_dn):
    n_tok, d = tokens.shape
    logits = (tokens @ router_w + router_b).astype(F32)
    top_v, top_i = lax.top_k(logits, TOP_K)
    weights = jax.nn.softmax(top_v, axis=-1)
    flat_e = top_i.reshape(-1)
    order = jnp.argsort(flat_e)
    sorted_e = flat_e[order]
    token_of = order // TOP_K
    counts = jnp.bincount(flat_e, length=N_EXPERTS)
    padded = (counts + MOE_TB - 1) // MOE_TB * MOE_TB
    start = jnp.cumsum(counts) - counts
    padded_end = jnp.cumsum(padded)
    padded_start = padded_end - padded
    n_asg = n_tok * TOP_K
    dest = (padded_start[sorted_e] + jnp.arange(n_asg) - start[sorted_e]).astype(jnp.int32)
    n_rows = n_asg + N_EXPERTS * MOE_TB
    n_blk = n_rows // MOE_TB
    src = jnp.zeros((n_rows,), jnp.int32).at[dest].set(token_of.astype(jnp.int32))
    buf = tokens.astype(BF16)[src]
    blk_e = jnp.minimum(jnp.searchsorted(padded_end, jnp.arange(n_blk) * MOE_TB, side="right"),
                        N_EXPERTS - 1).astype(jnp.int32)
    n_used = (padded_end[-1:] // MOE_TB).astype(jnp.int32)
    w_gate, w_up, b_gate, b_up, w_dn_b, b_dn_r = w_gu
    y = expert_ffn(buf, blk_e, n_used, w_gate, w_up, b_gate, b_up, w_dn_b, b_dn_r)
    inv = jnp.zeros((n_asg,), jnp.int32).at[order].set(dest)
    yk = y[inv].reshape(n_tok, TOP_K, d)
    return jnp.sum(yk * weights[:, :, None], axis=1)


def prep_expert_weights(w_gu, b_gu, w_dn, b_dn):
    packed = tuple(split_gate_up(w_gu)) + (b_gu[:, None, 0::2].astype(F32), b_gu[:, None, 1::2].astype(F32),
                                           w_dn.astype(BF16), b_dn[:, None, :].astype(F32))
    return packed, None, None, None


def kernel(x, c, ctx, c_ctx, mod_w, mod_b, w_in, gqa_q_gain, gqa_k_gain, conv_w, conv_b,
           filt_w1, filt_b1, filt_freq1, filt_w2, filt_b2, filt_freq2, filt_w3, hyena_bias,
           swa_sinks, w_branch, w_gate, w_out, ln1_g, ln1_b, router_w, router_b,
           exp_w_gate_up, exp_b_gate_up, exp_w_down, exp_b_down, ln2_g, ln2_b):
    batch, n_lat, d = x.shape
    n_ctx = ctx.shape[1]
    rows = n_lat // GRID_W
    rope_gqa = axial_rope(rows, GQA_HEAD_DIM)
    rope_swa = axial_rope(rows, SWA_HEAD_DIM)
    x_lat, x_ctx = x, ctx
    for l in range(DEPTH):
        ctx_continues = l < DEPTH - 1
        mod_lat = jnp.split((jax.nn.silu(c) @ mod_w[l] + mod_b[l])[:, None, :], 6, axis=-1)
        mod_ctx = jnp.split((jax.nn.silu(c_ctx) @ mod_w[l] + mod_b[l])[None, None, :], 6, axis=-1)
        filt = (filt_w1[l], filt_b1[l], filt_freq1[l], filt_w2[l], filt_b2[l], filt_freq2[l], filt_w3[l])
        sinks = swa_sinks[l].reshape(SWA_KV_HEADS, SWA_HEADS // SWA_KV_HEADS)

        h_lat = modulate(x_lat, mod_lat[0], mod_lat[1])
        h_ctx = modulate(x_ctx, mod_ctx[0], mod_ctx[1])
        fa_l, gq_l, gk_l, gv_l, hy_l, sq_l, sk_l, sv_l = split_projection(mm3(h_lat, w_in[l]))
        fa_c, gq_c, gk_c, gv_c, hy_c, sq_c, sk_c, sv_c = split_projection(mm3(h_ctx, w_in[l]))
        gk_c = rms_norm(heads(gk_c, GQA_KV_HEADS), gqa_k_gain[l])
        gv_c = heads(gv_c, GQA_KV_HEADS)
        sk_c = heads(sk_c, SWA_KV_HEADS)
        sv_c = heads(sv_c, SWA_KV_HEADS)
        gq = apply_rope(rms_norm(heads(gq_l, GQA_HEADS), gqa_q_gain[l]), rope_gqa)
        gk = apply_rope(rms_norm(heads(gk_l, GQA_KV_HEADS), gqa_k_gain[l]), rope_gqa)
        sq = apply_rope(heads(sq_l, SWA_HEADS), rope_swa)
        sk = apply_rope(heads(sk_l, SWA_KV_HEADS), rope_swa)
        flat2 = lambda t: t.reshape(t.shape[0] * t.shape[1], -1).astype(BF16)
        ga = global_attention(flat2(gq), flat2(gk), flat2(gv_l), flat2(gk_c), flat2(gv_c), batch)
        wa = window_attention(flat2(sq), flat2(sk), flat2(sv_l), flat2(sk_c), flat2(sv_c), swa_sinks[l], batch)
        lat_branches = (
            fourier_mix_pallas(fa_l),
            ga.reshape(batch, n_lat, -1),
            hyena_mix_pallas(hy_l, conv_w[l], conv_b[l], filt, hyena_bias[l]),
            wa.reshape(batch, n_lat, -1),
        )
        mix_lat = merge_branches(h_lat, lat_branches, w_gate[l], w_branch[l], w_out[l])
        if ctx_continues:
            gq_ctx = group_heads(rms_norm(heads(gq_c, GQA_HEADS), gqa_q_gain[l]), GQA_KV_HEADS)
            sq_ctx = group_heads(heads(sq_c, SWA_HEADS), SWA_KV_HEADS)
            ctx_branches = (
                fourier_mix_pallas(fa_c),
                flat_heads(softmax_attend(gq_ctx, gk_c, gv_c)),
                hyena_mix_pallas(hy_c, conv_w[l], conv_b[l], filt, hyena_bias[l]),
                flat_heads(softmax_attend(sq_ctx, sk_c, sv_c, sinks)),
            )
            mix_ctx = merge_branches(h_ctx, ctx_branches, w_gate[l], w_branch[l], w_out[l])
            x_ctx = post_norm(x_ctx, mod_ctx[2] * mix_ctx, ln1_g[l], ln1_b[l])
        x_lat = post_norm(x_lat, mod_lat[2] * mix_lat, ln1_g[l], ln1_b[l])

        moe_args = (router_w[l], router_b[l]) + prep_expert_weights(
            exp_w_gate_up[l], exp_b_gate_up[l], exp_w_down[l], exp_b_down[l])
        h_lat = modulate(x_lat, mod_lat[3], mod_lat[4])
        if ctx_continues:
            h_ctx = modulate(x_ctx, mod_ctx[3], mod_ctx[4])
            tokens = jnp.concatenate([h_lat.reshape(-1, d), h_ctx.reshape(-1, d)], axis=0)
            y = moe_ffn(tokens, *moe_args)
            y_lat = y[:batch * n_lat].reshape(batch, n_lat, d)
            y_ctx = y[batch * n_lat:].reshape(batch, n_ctx, d)
            x_ctx = post_norm(x_ctx, mod_ctx[5] * y_ctx, ln2_g[l], ln2_b[l])
        else:
            y_lat = moe_ffn(h_lat.reshape(-1, d), *moe_args).reshape(batch, n_lat, d)
        x_lat = post_norm(x_lat, mod_lat[5] * y_lat, ln2_g[l], ln2_b[l])
    return x_lat
```

```python
import functools
import math

import jax
import jax.numpy as jnp
import numpy as np
from jax import lax
from jax.experimental import pallas as pl
from jax.experimental.pallas import tpu as pltpu

D_MODEL = 2048
DEPTH = 2
GRID_W = 64
FNET_GROUPS = 4
FNET_GROUP_DIM = 128
GQA_HEADS = 4
GQA_KV_HEADS = 2
GQA_HEAD_DIM = 128
HYENA_WIDTH = 512
HYENA_ORDER = 2
HYENA_EMB_DIM = 33
HYENA_DECAY_TARGET = 1e-2
HYENA_SHORT_DECAY_PCT = 0.3
HYENA_LONG_DECAY_PCT = 1.5
SWA_HEADS = 8
SWA_KV_HEADS = 2
SWA_HEAD_DIM = 64
WINDOW = 128
Q_BLOCK = 128
ROPE_THETA = 10000.0
N_EXPERTS = 32
TOP_K = 4
D_EXPERT = 1536
SWIGLU_LIMIT = 7.0
SWIGLU_ALPHA = 1.702
MOE_BLOCK = 128
LN_EPS = 1e-5
RMS_EPS = 1e-6
DEEPNORM_ALPHA = (2 * DEPTH) ** 0.25
SPLIT_SIZES = (FNET_GROUPS * FNET_GROUP_DIM,
               GQA_HEADS * GQA_HEAD_DIM, GQA_KV_HEADS * GQA_HEAD_DIM, GQA_KV_HEADS * GQA_HEAD_DIM,
               3 * HYENA_WIDTH,
               SWA_HEADS * SWA_HEAD_DIM, SWA_KV_HEADS * SWA_HEAD_DIM, SWA_KV_HEADS * SWA_HEAD_DIM)
F32 = jnp.float32
BF16 = jnp.bfloat16


def _mm_body(a_ref, b_ref, o_ref):
    o_ref[...] = jnp.dot(a_ref[...], b_ref[...], preferred_element_type=F32).astype(o_ref.dtype)


def _pick(n, cands):
    for c in cands:
        if n % c == 0:
            return c
    return n


def matmul(a, b, out_dtype=F32):
    m, k = a.shape
    n = b.shape[1]
    tm = _pick(m, (512, 256, 128))
    tn = _pick(n, (2048, 1280, 1024, 768, 512, 256, 128))
    return pl.pallas_call(
        _mm_body,
        grid=(n // tn, m // tm),
        in_specs=[pl.BlockSpec((tm, k), lambda j, i: (i, 0)),
                  pl.BlockSpec((k, tn), lambda j, i: (0, j))],
        out_specs=pl.BlockSpec((tm, tn), lambda j, i: (i, j)),
        out_shape=jax.ShapeDtypeStruct((m, n), out_dtype),
        compiler_params=pltpu.CompilerParams(dimension_semantics=("parallel", "parallel"),
                                             vmem_limit_bytes=VMEM_LIMIT),
        name="mm",
    )(a.astype(BF16), b.astype(BF16))


def mm3(x, w):
    b, n, k = x.shape
    return matmul(x.reshape(b * n, k), w).reshape(b, n, w.shape[1])


ROW_TILE = 512


def _group_map(tiles_per_sample, batch):
    return lambda i, *_: (jnp.minimum(i // tiles_per_sample, batch), 0, 0)


def _ln(x):
    xc = x - jnp.mean(x, axis=-1, keepdims=True)
    return xc * lax.rsqrt(jnp.mean(xc * xc, axis=-1, keepdims=True) + LN_EPS)


def _lnmod_body(x_ref, sh_ref, sc_ref, o_ref):
    o_ref[...] = (_ln(x_ref[...]) * (1.0 + sc_ref[...]) + sh_ref[...]).astype(o_ref.dtype)


def ln_modulate(x, shift, scale, tiles_per_sample, batch):
    t, d = x.shape
    gmap = _group_map(tiles_per_sample, batch)
    return pl.pallas_call(
        _lnmod_body,
        grid=(t // ROW_TILE,),
        in_specs=[pl.BlockSpec((ROW_TILE, d), lambda i: (i, 0)),
                  pl.BlockSpec((None, 1, d), gmap), pl.BlockSpec((None, 1, d), gmap)],
        out_specs=pl.BlockSpec((ROW_TILE, d), lambda i: (i, 0)),
        out_shape=jax.ShapeDtypeStruct((t, d), BF16),
        compiler_params=pltpu.CompilerParams(dimension_semantics=("parallel",)),
        name="ln_modulate",
    )(x, shift, scale)


def matmul_cols(a, w, col0, ncols, out_dtype):
    t, k = a.shape
    tn = _pick(ncols, (512, 256, 128))
    assert col0 % tn == 0 and t % ROW_TILE == 0
    c0 = col0 // tn
    return pl.pallas_call(
        _mm_body,
        grid=(ncols // tn, t // ROW_TILE),
        in_specs=[pl.BlockSpec((ROW_TILE, k), lambda j, i: (i, 0)),
                  pl.BlockSpec((k, tn), lambda j, i: (0, c0 + j))],
        out_specs=pl.BlockSpec((ROW_TILE, tn), lambda j, i: (i, j)),
        out_shape=jax.ShapeDtypeStruct((t, ncols), out_dtype),
        compiler_params=pltpu.CompilerParams(dimension_semantics=("parallel", "parallel"),
                                             vmem_limit_bytes=VMEM_LIMIT),
        name="in_proj",
    )(a, w)


def _merge_body(h_ref, b0_ref, b1_ref, b2_ref, b3_ref, wg_ref, wb_ref, o_ref, acc_ref, br_ref):
    j = pl.program_id(1)
    for jj, b_ref in enumerate((b0_ref, b1_ref, b2_ref, b3_ref)):
        @pl.when(j == jj)
        def _():
            br_ref[...] = b_ref[...].astype(BF16)

    gate = jax.nn.sigmoid(jnp.dot(h_ref[...], wg_ref[...], preferred_element_type=F32))
    term = gate * jnp.dot(br_ref[...], wb_ref[...], preferred_element_type=F32)

    @pl.when(j == 0)
    def _():
        acc_ref[...] = term

    @pl.when(j > 0)
    def _():
        acc_ref[...] += term

    @pl.when(j == pl.num_programs(1) - 1)
    def _():
        o_ref[...] = acc_ref[...].astype(o_ref.dtype)


def merge_gated(h, branches, w_gate, w_branch, rows):
    d = h.shape[1]
    bw = w_branch.shape[1]
    nb = len(branches)
    row = lambda i, j: (i, 0)
    return pl.pallas_call(
        _merge_body,
        grid=(rows // ROW_TILE, nb),
        in_specs=[pl.BlockSpec((ROW_TILE, d), row)] + [pl.BlockSpec((ROW_TILE, bw), row)] * nb
                 + [pl.BlockSpec((None, d, d), lambda i, j: (j, 0, 0)),
                    pl.BlockSpec((None, bw, d), lambda i, j: (j, 0, 0))],
        out_specs=pl.BlockSpec((ROW_TILE, d), row),
        out_shape=jax.ShapeDtypeStruct((rows, d), BF16),
        scratch_shapes=[pltpu.VMEM((ROW_TILE, d), F32), pltpu.VMEM((ROW_TILE, bw), BF16)],
        compiler_params=pltpu.CompilerParams(dimension_semantics=("parallel", "arbitrary"),
                                             vmem_limit_bytes=VMEM_LIMIT),
        name="merge_gated",
    )(h, *branches, w_gate, w_branch)


OUT_TILE = 256
ROUTER_PAD = 128


def _outproj_body(m_ref, w_ref, x_ref, g2_ref, sh_ref, sc_ref, lg_ref, lb_ref, rw_ref, rb_ref,
                  x1_ref, hm_ref, lo_ref):
    mix = jnp.dot(m_ref[...], w_ref[...], preferred_element_type=F32)
    x1 = _ln(DEEPNORM_ALPHA * x_ref[...] + g2_ref[...] * mix) * lg_ref[...] + lb_ref[...]
    x1_ref[...] = x1
    hm = _ln(x1) * (1.0 + sc_ref[...]) + sh_ref[...]
    hm_ref[...] = hm.astype(hm_ref.dtype)
    lo_ref[...] = jnp.dot(hm, rw_ref[...], precision=lax.Precision.HIGHEST,
                          preferred_element_type=F32) + rb_ref[...]


def outproj_postnorm(merged, w_out, x, gate2, shift3, scale4, ln_g, ln_b, router_w, router_b,
                     tiles_per_sample, batch):
    rows, d = merged.shape
    gmap = _group_map(tiles_per_sample * (ROW_TILE // OUT_TILE), batch)
    row = lambda i: (i, 0)
    const = lambda i: (0, 0)
    return pl.pallas_call(
        _outproj_body,
        grid=(rows // OUT_TILE,),
        in_specs=[pl.BlockSpec((OUT_TILE, d), row), pl.BlockSpec((d, d), const), pl.BlockSpec((OUT_TILE, d), row),
                  pl.BlockSpec((None, 1, d), gmap), pl.BlockSpec((None, 1, d), gmap), pl.BlockSpec((None, 1, d), gmap),
                  pl.BlockSpec((1, d), const), pl.BlockSpec((1, d), const),
                  pl.BlockSpec((d, ROUTER_PAD), const), pl.BlockSpec((1, ROUTER_PAD), const)],
        out_specs=[pl.BlockSpec((OUT_TILE, d), row), pl.BlockSpec((OUT_TILE, d), row),
                   pl.BlockSpec((OUT_TILE, ROUTER_PAD), row)],
        out_shape=[jax.ShapeDtypeStruct((rows, d), F32), jax.ShapeDtypeStruct((rows, d), BF16),
                   jax.ShapeDtypeStruct((rows, ROUTER_PAD), F32)],
        compiler_params=pltpu.CompilerParams(dimension_semantics=("parallel",), vmem_limit_bytes=VMEM_LIMIT),
        name="outproj_postnorm",
    )(merged, w_out, x, gate2, shift3, scale4, ln_g, ln_b, router_w, router_b)


GATTN_TQ = 512
GATTN_TK = 256
GATTN_UNROLL = 4
VMEM_LIMIT = 56 * 1024 * 1024
LANES = 128
LOG2E = math.log2(math.e)


def _gattn_body(q_ref, kc_ref, vc_ref, k_ref, v_ref, o_ref, m_ref, acc_ref, *, c2):
    tq = q_ref.shape[0]
    hd = k_ref.shape[1]
    q2 = jnp.concatenate([q_ref[:, :hd], q_ref[:, hd:]], axis=0)
    m_ref[...] = jnp.full(m_ref.shape, -jnp.inf, F32)
    acc_ref[...] = jnp.zeros(acc_ref.shape, F32)

    def step(k, v):
        s = lax.dot_general(q2, k, (((1,), (1,)), ((), ())), preferred_element_type=F32) * c2
        m_old = m_ref[...]
        m_new = jnp.maximum(m_old, jnp.max(s, axis=-1, keepdims=True))
        alpha = jnp.exp2(m_old - m_new)
        p = jnp.concatenate([jnp.exp2(s[:, j * LANES:(j + 1) * LANES] - m_new)
                             for j in range(k.shape[0] // LANES)], axis=1)
        acc_ref[...] = (jnp.concatenate([alpha, alpha], axis=1) * acc_ref[...]
                        + jnp.dot(p.astype(BF16), v, preferred_element_type=F32))
        m_ref[...] = m_new

    step(kc_ref[...], vc_ref[...])

    def loop(c, carry):
        off = pl.multiple_of(c * GATTN_TK, GATTN_TK)
        step(k_ref[pl.ds(off, GATTN_TK), :], v_ref[pl.ds(off, GATTN_TK), :])
        return carry

    lax.fori_loop(0, k_ref.shape[0] // GATTN_TK, loop, 0, unroll=GATTN_UNROLL)
    acc = acc_ref[...]
    out = acc[:, :hd] / acc[:, hd:]
    o_ref[...] = jnp.concatenate([out[:tq], out[tq:]], axis=1).astype(o_ref.dtype)


def _with_ones(v, n_heads):
    r = v.shape[0]
    v3 = v.reshape(r, n_heads, -1)
    return jnp.concatenate([v3, jnp.ones_like(v3)], axis=-1).reshape(r, -1)


def global_attention(q, k, v, k_ctx, v_ctx, batch):
    hd = GQA_HEAD_DIM
    n = q.shape[0] // batch
    n_c = k_ctx.shape[0] // batch
    g = GQA_HEADS // GQA_KV_HEADS
    tq = min(GATTN_TQ, n)
    assert g == 2 and hd == LANES and n % tq == 0 and n % (GATTN_TK * GATTN_UNROLL) == 0 and n_c % LANES == 0
    nq = n // tq
    return pl.pallas_call(
        functools.partial(_gattn_body, c2=hd ** -0.5 * LOG2E),
        grid=(batch, GQA_KV_HEADS, nq),
        in_specs=[pl.BlockSpec((tq, g * hd), lambda b, h, i: (b * nq + i, h)),
                  pl.BlockSpec((n_c, hd), lambda b, h, i: (b, h)),
                  pl.BlockSpec((n_c, 2 * hd), lambda b, h, i: (b, h)),
                  pl.BlockSpec((n, hd), lambda b, h, i: (b, h)),
                  pl.BlockSpec((n, 2 * hd), lambda b, h, i: (b, h))],
        out_specs=pl.BlockSpec((tq, g * hd), lambda b, h, i: (b * nq + i, h)),
        out_shape=jax.ShapeDtypeStruct(q.shape, BF16),
        scratch_shapes=[pltpu.VMEM((g * tq, hd), F32), pltpu.VMEM((g * tq, 2 * hd), F32)],
        compiler_params=pltpu.CompilerParams(dimension_semantics=("parallel", "parallel", "parallel"),
                                             vmem_limit_bytes=VMEM_LIMIT),
        name="gattn",
    )(q, k_ctx, _with_ones(v_ctx, GQA_KV_HEADS), k, _with_ones(v, GQA_KV_HEADS))


WATTN_TQ = 256


def _wattn_body(sink_ref, q_ref, kc_ref, vc_ref, kp_ref, kx_ref, kn_ref, vp_ref, vx_ref, vn_ref, o_ref,
                *, scale, n, nq):
    tq = q_ref.shape[0]
    hd = SWA_HEAD_DIM
    g = SWA_HEADS // SWA_KV_HEADS
    h = pl.program_id(1)
    i = pl.program_id(2)
    k_band = jnp.concatenate([kp_ref[...], kx_ref[...], kn_ref[...]], axis=0)
    v_band = jnp.concatenate([vp_ref[...], vx_ref[...], vn_ref[...]], axis=0)
    qpos = i * tq + lax.broadcasted_iota(jnp.int32, (tq, tq + 2 * WINDOW), 0)
    kpos = i * tq - WINDOW + lax.broadcasted_iota(jnp.int32, (tq, tq + 2 * WINDOW), 1)
    valid = (jnp.abs(kpos - qpos) <= WINDOW) & (kpos >= 0) & (kpos < n)
    outs = []
    for j in range(g):
        q = q_ref[:, j * hd:(j + 1) * hd]
        s_c = lax.dot_general(q, kc_ref[...], (((1,), (1,)), ((), ())), preferred_element_type=F32) * scale
        s_b = lax.dot_general(q, k_band, (((1,), (1,)), ((), ())), preferred_element_type=F32) * scale
        s_b = jnp.where(valid, s_b, -jnp.inf)
        sink = sink_ref[h * g + j]
        m = jnp.maximum(jnp.maximum(jnp.max(s_c, -1, keepdims=True), jnp.max(s_b, -1, keepdims=True)), sink)
        p_c = jnp.exp(s_c - m)
        p_b = jnp.exp(s_b - m)
        l = jnp.sum(p_c, -1, keepdims=True) + jnp.sum(p_b, -1, keepdims=True) + jnp.exp(sink - m)
        o = (jnp.dot(p_c.astype(BF16), vc_ref[...], preferred_element_type=F32)
             + jnp.dot(p_b.astype(BF16), v_band, preferred_element_type=F32))
        outs.append(o / l)
    o_ref[...] = jnp.concatenate(outs, axis=1).astype(o_ref.dtype)


def window_attention(q, k, v, k_ctx, v_ctx, sinks, batch):
    hd = SWA_HEAD_DIM
    n = q.shape[0] // batch
    n_c = k_ctx.shape[0] // batch
    g = SWA_HEADS // SWA_KV_HEADS
    tq = WATTN_TQ
    assert n % tq == 0 and tq % WINDOW == 0
    nq = n // tq
    r = tq // WINDOW
    nw = n // WINDOW
    k_h = k.reshape(batch * n, SWA_KV_HEADS, hd).transpose(1, 0, 2)
    v_h = v.reshape(batch * n, SWA_KV_HEADS, hd).transpose(1, 0, 2)
    kc_h = k_ctx.reshape(batch * n_c, SWA_KV_HEADS, hd).transpose(1, 0, 2)
    vc_h = v_ctx.reshape(batch * n_c, SWA_KV_HEADS, hd).transpose(1, 0, 2)
    prev_map = lambda b, h, i, s: (h, b * nw + jnp.maximum(i * r - 1, 0), 0)
    cur_map = lambda b, h, i, s: (h, b * nq + i, 0)
    next_map = lambda b, h, i, s: (h, b * nw + jnp.minimum((i + 1) * r, nw - 1), 0)
    ctx_map = lambda b, h, i, s: (h, b, 0)
    grid_spec = pltpu.PrefetchScalarGridSpec(
        num_scalar_prefetch=1,
        grid=(batch, SWA_KV_HEADS, nq),
        in_specs=[pl.BlockSpec((tq, g * hd), lambda b, h, i, s: (b * nq + i, h)),
                  pl.BlockSpec((None, n_c, hd), ctx_map),
                  pl.BlockSpec((None, n_c, hd), ctx_map),
                  pl.BlockSpec((None, WINDOW, hd), prev_map),
                  pl.BlockSpec((None, tq, hd), cur_map),
                  pl.BlockSpec((None, WINDOW, hd), next_map),
                  pl.BlockSpec((None, WINDOW, hd), prev_map),
                  pl.BlockSpec((None, tq, hd), cur_map),
                  pl.BlockSpec((None, WINDOW, hd), next_map)],
        out_specs=pl.BlockSpec((tq, g * hd), lambda b, h, i, s: (b * nq + i, h)),
    )
    return pl.pallas_call(
        functools.partial(_wattn_body, scale=hd ** -0.5, n=n, nq=nq),
        grid_spec=grid_spec,
        out_shape=jax.ShapeDtypeStruct(q.shape, BF16),
        compiler_params=pltpu.CompilerParams(dimension_semantics=("parallel", "parallel", "parallel"),
                                             vmem_limit_bytes=VMEM_LIMIT),
        name="wattn",
    )(sinks.astype(F32), q, kc_h, vc_h, k_h, k_h, k_h, v_h, v_h, v_h)


MOE_TB = 256
MOE_FC = 512


def _ffn_body(blk_e_ref, nused_ref, x_ref, wg_ref, wu_ref, bg_ref, bu_ref, wd_ref, bd_ref, o_ref):
    i = pl.program_id(0)

    @pl.when(i < nused_ref[0])
    def _():
        x = x_ref[...]
        acc = jnp.zeros(o_ref.shape, F32) + bd_ref[...]
        for c in range(D_EXPERT // MOE_FC):
            sl = slice(c * MOE_FC, (c + 1) * MOE_FC)
            gate = jnp.dot(x, wg_ref[:, sl], preferred_element_type=F32) + bg_ref[:, sl]
            up = jnp.dot(x, wu_ref[:, sl], preferred_element_type=F32) + bu_ref[:, sl]
            gate = jnp.minimum(gate, SWIGLU_LIMIT)
            up = jnp.clip(up, -SWIGLU_LIMIT, SWIGLU_LIMIT)
            glu = gate * jax.nn.sigmoid(gate * SWIGLU_ALPHA)
            act = ((up + 1.0) * glu).astype(BF16)
            acc = acc + jnp.dot(act, wd_ref[sl, :], preferred_element_type=F32)
        o_ref[...] = acc

    @pl.when(i >= nused_ref[0])
    def _():
        o_ref[...] = jnp.zeros(o_ref.shape, F32)


def expert_ffn(buf, blk_e, n_used, w_gate, w_up, b_gate, b_up, w_dn, b_dn):
    n_rows, d = buf.shape
    n_blk = n_rows // MOE_TB
    f = w_gate.shape[2]
    wmap = lambda i, e, u: (e[i], 0, 0)
    grid_spec = pltpu.PrefetchScalarGridSpec(
        num_scalar_prefetch=2,
        grid=(n_blk,),
        in_specs=[pl.BlockSpec((MOE_TB, d), lambda i, e, u: (i, 0)),
                  pl.BlockSpec((None, d, f), wmap),
                  pl.BlockSpec((None, d, f), wmap),
                  pl.BlockSpec((None, 1, f), wmap),
                  pl.BlockSpec((None, 1, f), wmap),
                  pl.BlockSpec((None, f, d), wmap),
                  pl.BlockSpec((None, 1, d), wmap)],
        out_specs=pl.BlockSpec((MOE_TB, d), lambda i, e, u: (i, 0)),
    )
    return pl.pallas_call(
        _ffn_body,
        grid_spec=grid_spec,
        out_shape=jax.ShapeDtypeStruct((n_rows, d), F32),
        compiler_params=pltpu.CompilerParams(dimension_semantics=("arbitrary",),
                                             vmem_limit_bytes=VMEM_LIMIT),
        name="expert_ffn",
    )(blk_e, n_used, buf, w_gate, w_up, b_gate, b_up, w_dn, b_dn)


def _deint_body(w_ref, p_ref, g_ref, u_ref):
    y = jnp.dot(w_ref[...].astype(BF16), p_ref[...], preferred_element_type=F32)
    half = g_ref.shape[1]
    g_ref[...] = y[:, :half].astype(BF16)
    u_ref[...] = y[:, half:].astype(BF16)


def split_gate_up(w_gu):
    e, d, f2 = w_gu.shape
    blk = 2 * LANES
    src = jnp.arange(blk)[:, None]
    dst = jnp.arange(blk)[None, :]
    perm = (src == jnp.where(dst < LANES, 2 * dst, 2 * (dst - LANES) + 1)).astype(BF16)
    out = jax.ShapeDtypeStruct((e, d, f2 // 2), BF16)
    return pl.pallas_call(
        _deint_body,
        grid=(e, f2 // blk),
        in_specs=[pl.BlockSpec((None, d, blk), lambda i, j: (i, 0, j)),
                  pl.BlockSpec((blk, blk), lambda i, j: (0, 0))],
        out_specs=[pl.BlockSpec((None, d, LANES), lambda i, j: (i, 0, j)),
                   pl.BlockSpec((None, d, LANES), lambda i, j: (i, 0, j))],
        out_shape=[out, out],
        compiler_params=pltpu.CompilerParams(dimension_semantics=("parallel", "parallel")),
        name="split_gate_up",
    )(w_gu, perm)


FFT_N2 = 128
FFT_SPLIT_MIN = 1024


def _angles(num, den):
    return (2.0 * math.pi / den) * (num % den).astype(F32)


def dft_stage_a(n1, k_in, real_input):
    ang = _angles(jnp.arange(n1)[:, None] * jnp.arange(k_in)[None, :], n1)
    c, s = jnp.cos(ang), jnp.sin(ang)
    if real_input:
        return jnp.concatenate([c, -s], axis=0).astype(BF16)
    return jnp.block([[c, s], [-s, c]]).astype(BF16)


def dft_stage_a_inv(n1, k_out):
    ang = _angles(jnp.arange(k_out)[:, None] * jnp.arange(n1)[None, :], n1)
    c, s = jnp.cos(ang), jnp.sin(ang)
    return jnp.block([[c, -s], [s, c]]).astype(BF16)


def dft_stage_b(n1, n2):
    k1 = jnp.arange(n1)[:, None, None]
    k2 = jnp.arange(n2)[None, :, None]
    m = jnp.arange(n2)[None, None, :]
    ang = _angles(m * (k1 + n1 * k2), n1 * n2)
    c, s = jnp.cos(ang), jnp.sin(ang)
    top = jnp.concatenate([c, s], axis=2)
    bot = jnp.concatenate([-s, c], axis=2)
    return jnp.concatenate([top, bot], axis=1).astype(BF16)


SUBLANES = 8


def _lmm_body(m_ref, x_ref, o_ref):
    m = m_ref[...]
    for s in range(x_ref.shape[1]):
        o_ref[:, s, :] = jnp.dot(m, x_ref[:, s, :].astype(BF16), preferred_element_type=F32)


def left_matmul(m, x, n2):
    g, rows, c = x.shape
    k = rows // n2
    r = m.shape[0]
    assert m.shape[1] == k and n2 % SUBLANES == 0 and c % LANES == 0
    x5 = x.reshape(g, k, n2 // SUBLANES, SUBLANES, c)
    out = pl.pallas_call(
        _lmm_body,
        grid=(g, n2 // SUBLANES, c // LANES),
        in_specs=[pl.BlockSpec((r, k), lambda i, j, l: (0, 0)),
                  pl.BlockSpec((None, k, None, SUBLANES, LANES), lambda i, j, l: (i, 0, j, 0, l))],
        out_specs=pl.BlockSpec((None, r, None, SUBLANES, LANES), lambda i, j, l: (i, 0, j, 0, l)),
        out_shape=jax.ShapeDtypeStruct((g, r, n2 // SUBLANES, SUBLANES, c), F32),
        compiler_params=pltpu.CompilerParams(dimension_semantics=("parallel", "parallel", "parallel"),
                                             vmem_limit_bytes=VMEM_LIMIT),
        name="left_matmul",
    )(m, x5)
    return out.reshape(g, r * n2, c)


def _conv_b_body(t_ref, gs_ref, h_ref, u_ref):
    n2 = t_ref.shape[1]
    c = t_ref.shape[2]
    gs = gs_ref[...]
    y = jnp.dot(gs, t_ref[...].reshape(2 * n2, c).astype(BF16), preferred_element_type=F32)
    yr, yi = y[:n2], y[n2:]
    hr, hi = h_ref[0], h_ref[1]
    z = jnp.concatenate([yr * hr - yi * hi, yr * hi + yi * hr], axis=0).astype(BF16)
    u = lax.dot_general(gs, z, (((0,), (0,)), ((), ())), preferred_element_type=F32)
    u_ref[...] = u.reshape(2, n2, c)


def conv_stage_b(t, gs, h):
    _, n1, n2, c = t.shape
    spec = pl.BlockSpec((2, None, n2, c), lambda i: (0, i, 0, 0))
    return pl.pallas_call(
        _conv_b_body,
        grid=(n1,),
        in_specs=[spec, pl.BlockSpec((None, 2 * n2, 2 * n2), lambda i: (i, 0, 0)), spec],
        out_specs=spec,
        out_shape=jax.ShapeDtypeStruct(t.shape, F32),
        compiler_params=pltpu.CompilerParams(dimension_semantics=("parallel",), vmem_limit_bytes=VMEM_LIMIT),
        name="conv_stage_b",
    )(t, gs, h)


def _filt_b_body(t_ref, gs_ref, ss_ref, h_ref):
    n2 = t_ref.shape[1]
    c = t_ref.shape[2]
    w = HYENA_WIDTH
    y = jnp.dot(gs_ref[...], t_ref[...].reshape(2 * n2, c).astype(BF16), preferred_element_type=F32)
    yr, yi = y[:n2], y[n2:]
    ss = ss_ref[...]
    for o in range(HYENA_ORDER):
        f0, b0 = (2 * o) * w, (2 * o + 1) * w
        scale = lax.rsqrt(ss[:, f0:f0 + w] + ss[:, b0:b0 + w] + 1e-6)
        h_ref[o, 0] = (yr[:, f0:f0 + w] + yr[:, b0:b0 + w]) * scale
        h_ref[o, 1] = (yi[:, f0:f0 + w] - yi[:, b0:b0 + w]) * scale


def filter_stage_b(t, gs, sumsq):
    _, n1, n2, c = t.shape
    w = HYENA_WIDTH
    return pl.pallas_call(
        _filt_b_body,
        grid=(n1,),
        in_specs=[pl.BlockSpec((2, None, n2, c), lambda i: (0, i, 0, 0)),
                  pl.BlockSpec((None, 2 * n2, 2 * n2), lambda i: (i, 0, 0)),
                  pl.BlockSpec((1, c), lambda i: (0, 0))],
        out_specs=pl.BlockSpec((HYENA_ORDER, 2, None, n2, w), lambda i: (0, 0, i, 0, 0)),
        out_shape=jax.ShapeDtypeStruct((HYENA_ORDER, 2, n1, n2, w), F32),
        compiler_params=pltpu.CompilerParams(dimension_semantics=("parallel",), vmem_limit_bytes=VMEM_LIMIT),
        name="filter_stage_b",
    )(t, gs, sumsq)


def _fnet_b_body(t_ref, gs_ref, cs_ref, o_ref):
    n2 = t_ref.shape[1]
    c = t_ref.shape[2]
    y = jnp.dot(gs_ref[...], t_ref[...].reshape(2 * n2, c).astype(BF16), preferred_element_type=F32)
    yr, yi = y[:n2].astype(BF16), y[n2:].astype(BF16)
    gd = FNET_GROUP_DIM
    outs = []
    for g in range(c // gd):
        v = jnp.concatenate([yr[:, g * gd:(g + 1) * gd], yi[:, g * gd:(g + 1) * gd]], axis=1)
        outs.append(jnp.dot(v, cs_ref[...], preferred_element_type=F32))
    o_ref[...] = jnp.concatenate(outs, axis=1).astype(o_ref.dtype)


def fnet_stage_b(t, gs, cs, out_dtype):
    b, _, n1, n2, c = t.shape
    return pl.pallas_call(
        _fnet_b_body,
        grid=(b, n1),
        in_specs=[pl.BlockSpec((None, 2, None, n2, c), lambda i, j: (i, 0, j, 0, 0)),
                  pl.BlockSpec((None, 2 * n2, 2 * n2), lambda i, j: (j, 0, 0)),
                  pl.BlockSpec(cs.shape, lambda i, j: (0, 0))],
        out_specs=pl.BlockSpec((None, n2, c), lambda i, j: (i, 0, j)),
        out_shape=jax.ShapeDtypeStruct((b, n2, n1 * c), out_dtype),
        compiler_params=pltpu.CompilerParams(dimension_semantics=("parallel", "parallel"),
                                             vmem_limit_bytes=VMEM_LIMIT),
        name="fnet_stage_b",
    )(t, gs, cs)


def fourier_mix_pallas(u):
    b, n, c = u.shape
    gd = FNET_GROUP_DIM
    n2 = FFT_N2 if n >= FFT_SPLIT_MIN else n
    n1 = n // n2
    gs = dft_stage_b(n1, n2)
    if n1 > 1:
        t = left_matmul(dft_stage_a(n1, n1, True), u.astype(F32), n2).reshape(b, 2, n1, n2, c)
    else:
        t = jnp.stack([u.astype(F32), jnp.zeros(u.shape, F32)], axis=1).reshape(b, 2, 1, n2, c)
    ang = _angles(jnp.arange(gd)[:, None] * jnp.arange(gd)[None, :], gd)
    cs = (jnp.concatenate([jnp.cos(ang), jnp.sin(ang)], axis=0) * (n * gd) ** -0.5).astype(BF16)
    out = fnet_stage_b(t, gs, cs, u.dtype)
    return out.reshape(b, n2, n1, c).reshape(b, n, c)


FILT_TM = 512
FILT_LANES = 128


def _filter_body(ff_ref, w1_ref, b1_ref, fr1_ref, w2_ref, b2_ref, fr2_ref, w3_ref, dl_ref, taps_ref, ss_ref, *, n):
    i = pl.program_id(0)
    tm = taps_ref.shape[0]
    hi = lax.Precision.HIGHEST
    pos = (i * tm + lax.broadcasted_iota(jnp.int32, (tm, FILT_LANES), 0)).astype(F32)
    lane = lax.broadcasted_iota(jnp.int32, (tm, FILT_LANES), 1)
    t = pos / (n - 1.0) if n > 1 else pos * 0.0
    omega = (2.0 * math.pi) * pos / n
    arg = ff_ref[...] * omega
    bands = (HYENA_EMB_DIM - 1) // 2
    z = jnp.where(lane == 0, t, jnp.where(lane <= bands, jnp.cos(arg),
                                          jnp.where(lane <= 2 * bands, -jnp.sin(arg), 0.0)))
    h = jnp.sin(fr1_ref[...] * (jnp.dot(z, w1_ref[...], precision=hi, preferred_element_type=F32) + b1_ref[...]))
    h = jnp.sin(fr2_ref[...] * (jnp.dot(h, w2_ref[...], precision=hi, preferred_element_type=F32) + b2_ref[...]))
    h = jnp.dot(h, w3_ref[...], precision=hi, preferred_element_type=F32)
    width = h.shape[1]
    h = h * jnp.exp(-t[:, :1] * dl_ref[...])
    row = i * tm + lax.broadcasted_iota(jnp.int32, (tm, width), 0)
    col = lax.broadcasted_iota(jnp.int32, (tm, width), 1)
    h = jnp.where((row == 0) & ((col // HYENA_WIDTH) % 2 == 1), 0.0, h)
    taps_ref[...] = h

    @pl.when(i == 0)
    def _():
        ss_ref[...] = jnp.zeros(ss_ref.shape, F32)

    ss_ref[...] += jnp.sum(h * h, axis=0, keepdims=True)


def hyena_taps(n, w1, b1, fr1, w2, b2, fr2, w3):
    bands = (HYENA_EMB_DIM - 1) // 2
    fd = w1.shape[1]
    width = w3.shape[1]
    freqs = jnp.linspace(1e-4, bands - 1, bands, dtype=F32)
    featfreq = jnp.zeros((1, FILT_LANES), F32).at[0, 1:1 + bands].set(freqs).at[0, 1 + bands:1 + 2 * bands].set(freqs)
    w1p = jnp.zeros((FILT_LANES, fd), F32).at[:HYENA_EMB_DIM].set(w1.astype(F32))
    max_decay = math.log(HYENA_DECAY_TARGET) / HYENA_SHORT_DECAY_PCT
    min_decay = math.log(HYENA_DECAY_TARGET) / HYENA_LONG_DECAY_PCT
    deltas = jnp.abs(jnp.linspace(min_decay, max_decay, HYENA_WIDTH, dtype=F32))
    dl = jnp.tile(deltas, width // HYENA_WIDTH)[None, :]
    tm = min(FILT_TM, n)
    full = lambda a: pl.BlockSpec(a.shape, lambda i: (0,) * a.ndim)
    ops = (featfreq, w1p, b1.astype(F32)[None, :], fr1.astype(F32)[None, :], w2.astype(F32),
           b2.astype(F32)[None, :], fr2.astype(F32)[None, :], w3.astype(F32), dl)
    return pl.pallas_call(
        functools.partial(_filter_body, n=n),
        grid=(n // tm,),
        in_specs=[full(a) for a in ops],
        out_specs=[pl.BlockSpec((tm, width), lambda i: (i, 0)), pl.BlockSpec((1, width), lambda i: (0, 0))],
        out_shape=[jax.ShapeDtypeStruct((n, width), F32), jax.ShapeDtypeStruct((1, width), F32)],
        compiler_params=pltpu.CompilerParams(dimension_semantics=("arbitrary",), vmem_limit_bytes=VMEM_LIMIT),
        name="hyena_taps",
    )(*ops)


def _fft_split(n):
    if n >= FFT_SPLIT_MIN:
        return 2 * n // FFT_N2, FFT_N2, n // FFT_N2
    return 1, 2 * n, 1


def hyena_mix_pallas(zc, conv_w, conv_b, filt, bias):
    batch, n, _ = zc.shape
    assert batch == 2
    w = HYENA_WIDTH
    u = short_conv(zc, conv_w, conv_b).astype(F32)
    x1, x2, v = jnp.split(u, 3, axis=-1)
    n1, n2, k1 = _fft_split(n)
    gs = dft_stage_b(n1, n2)
    taps, sumsq = hyena_taps(n, *filt)
    if n1 > 1:
        t = left_matmul(dft_stage_a(n1, k1, True), taps[None], n2).reshape(2, n1, n2, taps.shape[1])
    else:
        t = jnp.stack([jnp.pad(taps, ((0, n), (0, 0))), jnp.zeros((2 * n, taps.shape[1]), F32)])[:, None]
    spec = filter_stage_b(t, gs, sumsq)
    fwd = dft_stage_a(n1, k1, False)
    inv = dft_stage_a_inv(n1, k1)

    def long_conv(z, order):
        if n1 > 1:
            t = left_matmul(fwd, z.reshape(1, 2 * n, w), n2).reshape(2, n1, n2, w)
            uu = conv_stage_b(t, gs, spec[order])
            y = left_matmul(inv, uu.reshape(1, 2 * n1 * n2, w), n2).reshape(2, n, w)
        else:
            t = jnp.pad(z, ((0, 0), (0, n), (0, 0)))[:, None]
            y = conv_stage_b(t, gs, spec[order])[:, 0, :n]
        return y * (1.0 / (2 * n)) + z * bias[order].astype(F32)

    y = x1 * long_conv(v, 0)
    y = x2 * long_conv(y, 1)
    return y.astype(zc.dtype)


def layer_norm(x):
    xf = x.astype(F32)
    xc = xf - jnp.mean(xf, -1, keepdims=True)
    return xc * lax.rsqrt(jnp.mean(xc * xc, -1, keepdims=True) + LN_EPS)


def modulate(x, shift, scale):
    return (layer_norm(x) * (1.0 + scale.astype(F32)) + shift.astype(F32)).astype(x.dtype)


def post_norm(x, update, gain, bias):
    y = layer_norm(DEEPNORM_ALPHA * x.astype(F32) + update.astype(F32))
    return (y * gain.astype(F32) + bias.astype(F32)).astype(x.dtype)


def rms_norm(x, gain):
    xf = x.astype(F32)
    y = xf * lax.rsqrt(jnp.mean(xf * xf, -1, keepdims=True) + RMS_EPS) * gain.astype(F32)
    return y.astype(x.dtype)


def split_projection(z):
    points = np.cumsum(SPLIT_SIZES)[:-1].tolist()
    return jnp.split(z, points, axis=-1)


def heads(z, n_heads):
    b, n, w = z.shape
    return z.reshape(b, n, n_heads, w // n_heads)


def group_heads(q, n_kv):
    b, n, h, hd = q.shape
    return q.reshape(b, n, n_kv, h // n_kv, hd)


def flat_heads(o):
    return o.reshape(o.shape[0], o.shape[1], -1)


def axial_rope(rows, head_dim):
    n_freq = head_dim // 4
    inv_freq = jnp.power(ROPE_THETA, -jnp.arange(n_freq, dtype=F32) / n_freq)
    t = jnp.arange(rows * GRID_W)
    row = (t // GRID_W).astype(F32)
    col = (t % GRID_W).astype(F32)
    ang = jnp.stack([row[:, None] * inv_freq, col[:, None] * inv_freq], axis=1)
    return jnp.cos(ang), jnp.sin(ang)


def apply_rope(x, rope):
    cos, sin = rope
    b, n, h, hd = x.shape
    xr = x.astype(F32).reshape(b, n, h, 2, 2, hd // 4)
    x1, x2 = xr[..., 0, :], xr[..., 1, :]
    c = cos[None, :, None]
    s = sin[None, :, None]
    out = jnp.stack([x1 * c - x2 * s, x2 * c + x1 * s], axis=-2)
    return out.reshape(x.shape).astype(x.dtype)


def softmax_attend(q, k, v, sinks=None):
    scale = q.shape[-1] ** -0.5
    s = jnp.einsum("bqkgd,bskd->bkgqs", q, k, preferred_element_type=F32) * scale
    if sinks is None:
        p = jax.nn.softmax(s, axis=-1)
    else:
        sink_col = jnp.broadcast_to(sinks.astype(F32)[None, :, :, None, None], s.shape[:-1] + (1,))
        p = jax.nn.softmax(jnp.concatenate([sink_col, s], axis=-1), axis=-1)[..., 1:]
    return jnp.einsum("bkgqs,bskd->bqkgd", p.astype(v.dtype), v)


def global_attention_latent(q, k, v, k_ctx, v_ctx):
    b, n = q.shape[:2]
    k_all = jnp.concatenate([k_ctx, k], axis=1)
    v_all = jnp.concatenate([v_ctx, v], axis=1)
    qb = q.reshape((b, n // Q_BLOCK, Q_BLOCK) + q.shape[2:]).swapaxes(0, 1)
    out = lax.map(lambda q_blk: softmax_attend(q_blk, k_all, v_all), qb)
    return out.swapaxes(0, 1).reshape(b, n, -1)


def window_attention_latent(q, k, v, k_ctx, v_ctx, sinks):
    b, n = q.shape[:2]
    n_blk = n // Q_BLOCK
    n_c = k_ctx.shape[1]
    scale = q.shape[-1] ** -0.5
    pad = ((0, 0), (Q_BLOCK, Q_BLOCK), (0, 0), (0, 0))
    kp = jnp.pad(k, pad)
    vp = jnp.pad(v, pad)
    qb = q.reshape((b, n_blk, Q_BLOCK) + q.shape[2:]).swapaxes(0, 1)
    qi = jnp.arange(Q_BLOCK)[:, None]
    kj = jnp.arange(3 * Q_BLOCK)[None, :]
    in_window = jnp.abs(kj - Q_BLOCK - qi) <= WINDOW
    sink_val = sinks.astype(F32)[None, :, :, None, None]

    def block(args):
        q_blk, i = args
        kb = lax.dynamic_slice_in_dim(kp, i * Q_BLOCK, 3 * Q_BLOCK, axis=1)
        vb = lax.dynamic_slice_in_dim(vp, i * Q_BLOCK, 3 * Q_BLOCK, axis=1)
        kpos = i * Q_BLOCK - Q_BLOCK + kj
        valid = in_window & (kpos >= 0) & (kpos < n)
        s_ctx = jnp.einsum("bqkgd,bskd->bkgqs", q_blk, k_ctx, preferred_element_type=F32) * scale
        s_band = jnp.einsum("bqkgd,bskd->bkgqs", q_blk, kb, preferred_element_type=F32) * scale
        s_band = jnp.where(valid, s_band, -jnp.inf)
        sink_col = jnp.broadcast_to(sink_val, s_ctx.shape[:-1] + (1,))
        p = jax.nn.softmax(jnp.concatenate([sink_col, s_ctx, s_band], axis=-1), axis=-1).astype(v.dtype)
        return (jnp.einsum("bkgqs,bskd->bqkgd", p[..., 1:1 + n_c], v_ctx)
                + jnp.einsum("bkgqs,bskd->bqkgd", p[..., 1 + n_c:], vb))

    out = lax.map(block, (qb, jnp.arange(n_blk)))
    return out.swapaxes(0, 1).reshape(b, n, -1)


def fourier_mix(u):
    b, n, _ = u.shape
    ug = u.astype(F32).reshape(b, n, FNET_GROUPS, FNET_GROUP_DIM)
    y = jnp.fft.fft2(ug, axes=(1, 3), norm="ortho").real
    return y.reshape(b, n, FNET_GROUPS * FNET_GROUP_DIM).astype(u.dtype)


def short_conv(u, w, bias):
    n = u.shape[1]
    up = jnp.pad(u, ((0, 0), (1, 1), (0, 0)))
    return up[:, :n] * w[0] + up[:, 1:n + 1] * w[1] + up[:, 2:] * w[2] + bias


def hyena_filter_spectrum(n, w1, b1, fr1, w2, b2, fr2, w3):
    t = jnp.linspace(0.0, 1.0, n, dtype=F32)[:, None]
    bands = (HYENA_EMB_DIM - 1) // 2
    omega = 2.0 * math.pi * jnp.arange(n, dtype=F32)[:, None] / n
    freqs = jnp.linspace(1e-4, bands - 1, bands, dtype=F32)[None, :]
    z = jnp.concatenate([t, jnp.cos(freqs * omega), -jnp.sin(freqs * omega)], axis=-1)
    h = jnp.sin(fr1.astype(F32) * (z @ w1.astype(F32) + b1.astype(F32)))
    h = jnp.sin(fr2.astype(F32) * (h @ w2.astype(F32) + b2.astype(F32)))
    h = (h @ w3.astype(F32)).reshape(n, HYENA_ORDER, 2, HYENA_WIDTH)
    max_decay = math.log(HYENA_DECAY_TARGET) / HYENA_SHORT_DECAY_PCT
    min_decay = math.log(HYENA_DECAY_TARGET) / HYENA_LONG_DECAY_PCT
    deltas = jnp.abs(jnp.linspace(min_decay, max_decay, HYENA_WIDTH, dtype=F32))
    h = h * jnp.exp(-t * deltas)[:, None, None, :]
    taps = jnp.concatenate([h[:, :, 0],
                            jnp.zeros((1, HYENA_ORDER, HYENA_WIDTH), F32),
                            h[:0:-1, :, 1]], axis=0)
    taps = taps * lax.rsqrt(jnp.sum(taps * taps, axis=0, keepdims=True) + 1e-6)
    return jnp.fft.rfft(taps, axis=0)


def hyena_mix(zc, conv_w, conv_b, filt, bias):
    n = zc.shape[1]
    u = short_conv(zc, conv_w, conv_b).astype(F32)
    x1, x2, v = jnp.split(u, 3, axis=-1)
    spec = hyena_filter_spectrum(n, *filt)

    def long_conv(z, order):
        zf = jnp.fft.rfft(z, n=2 * n, axis=1)
        y = jnp.fft.irfft(zf * spec[None, :, order], n=2 * n, axis=1)[:, :n]
        return y + z * bias[order].astype(F32)

    y = x1 * long_conv(v, 0)
    y = x2 * long_conv(y, 1)
    return y.astype(zc.dtype)


def merge_branches(h, branches, w_gate_l, w_branch_l, w_out_l):
    merged = sum(jax.nn.sigmoid(mm3(h, w_gate_l[i])) * mm3(br, w_branch_l[i]) for i, br in enumerate(branches))
    return mm3(merged, w_out_l)


def moe_ffn(tokens, logits, w_pack):
    n_tok, d = tokens.shape
    top_v, top_i = lax.top_k(logits, TOP_K)
    weights = jax.nn.softmax(top_v, axis=-1)
    flat_e = top_i.reshape(-1).astype(jnp.int32)
    n_asg = n_tok * TOP_K
    onehot = (flat_e[:, None] == jnp.arange(N_EXPERTS, dtype=jnp.int32)[None, :]).astype(jnp.int32)
    csum = jnp.cumsum(onehot, axis=0)
    rank = jnp.sum(onehot * csum, axis=1) - 1
    counts = csum[-1]
    padded = (counts + MOE_TB - 1) // MOE_TB * MOE_TB
    start = jnp.cumsum(counts) - counts
    padded_end = jnp.cumsum(padded)
    padded_start = padded_end - padded
    dest = padded_start[flat_e] + rank
    n_rows = n_asg + N_EXPERTS * MOE_TB
    n_blk = n_rows // MOE_TB
    blk_e = jnp.minimum(jnp.searchsorted(padded_end, jnp.arange(n_blk) * MOE_TB, side="right"),
                        N_EXPERTS - 1).astype(jnp.int32)
    n_used = (padded_end[-1:] // MOE_TB).astype(jnp.int32)
    order = jnp.argsort(flat_e)
    e_row = jnp.repeat(blk_e, MOE_TB, total_repeat_length=n_rows)
    off = jnp.arange(n_rows, dtype=jnp.int32) - padded_start[e_row]
    pos = jnp.clip(off + start[e_row], 0, n_asg - 1)
    src = jnp.where(off < counts[e_row], order[pos] // TOP_K, 0)
    buf = tokens[src]
    y = expert_ffn(buf, blk_e, n_used, *w_pack)
    yk = y[dest].reshape(n_tok, TOP_K, d)
    return jnp.sum(yk * weights[:, :, None], axis=1)


def prep_expert_weights(w_gu, b_gu, w_dn, b_dn):
    packed = tuple(split_gate_up(w_gu)) + (b_gu[:, None, 0::2].astype(F32), b_gu[:, None, 1::2].astype(F32),
                                           w_dn.astype(BF16), b_dn[:, None, :].astype(F32))
    return packed


def kernel(x, c, ctx, c_ctx, mod_w, mod_b, w_in, gqa_q_gain, gqa_k_gain, conv_w, conv_b,
           filt_w1, filt_b1, filt_freq1, filt_w2, filt_b2, filt_freq2, filt_w3, hyena_bias,
           swa_sinks, w_branch, w_gate, w_out, ln1_g, ln1_b, router_w, router_b,
           exp_w_gate_up, exp_b_gate_up, exp_w_down, exp_b_down, ln2_g, ln2_b):
    batch, n_lat, d = x.shape
    n_ctx = ctx.shape[1]
    rows = n_lat // GRID_W
    rope_gqa = axial_rope(rows, GQA_HEAD_DIM)
    rope_swa = axial_rope(rows, SWA_HEAD_DIM)
    n_l, n_c = batch * n_lat, batch * n_ctx
    assert n_lat % ROW_TILE == 0 and n_c % ROW_TILE == 0
    tps = n_lat // ROW_TILE
    xs = jnp.concatenate([x.reshape(n_l, d), ctx.reshape(n_c, d)], axis=0)
    cond = jnp.concatenate([c, c_ctx[None, :]], axis=0)
    offs = np.cumsum((0,) + SPLIT_SIZES)
    for l in range(DEPTH):
        ctx_continues = l < DEPTH - 1
        m_rows = n_l + n_c if ctx_continues else n_l
        mods = (jax.nn.silu(cond) @ mod_w[l] + mod_b[l]).reshape(batch + 1, 6, 1, d).swapaxes(0, 1)
        filt = (filt_w1[l], filt_b1[l], filt_freq1[l], filt_w2[l], filt_b2[l], filt_freq2[l], filt_w3[l])
        sinks = swa_sinks[l].reshape(SWA_KV_HEADS, SWA_HEADS // SWA_KV_HEADS)

        h = ln_modulate(xs, mods[0], mods[1], tps, batch)
        w_in_b = w_in[l].astype(BF16)
        seg_dtypes = (F32, F32, F32, BF16, F32, F32, F32, BF16)
        segs = [matmul_cols(h, w_in_b, int(offs[i]), SPLIT_SIZES[i], seg_dtypes[i]) for i in range(len(SPLIT_SIZES))]
        lat3 = lambda t: t[:n_l].reshape(batch, n_lat, -1)
        ctx3 = lambda t: t[n_l:].reshape(batch, n_ctx, -1)
        fa_l, gq_l, gk_l, gv_l, hy_l, sq_l, sk_l, sv_l = [lat3(t) for t in segs]
        fa_c, gq_c, gk_c, gv_c, hy_c, sq_c, sk_c, sv_c = [ctx3(t) for t in segs]
        gk_c = rms_norm(heads(gk_c, GQA_KV_HEADS), gqa_k_gain[l])
        gv_c = heads(gv_c, GQA_KV_HEADS)
        sk_c = heads(sk_c, SWA_KV_HEADS)
        sv_c = heads(sv_c, SWA_KV_HEADS)
        gq = apply_rope(rms_norm(heads(gq_l, GQA_HEADS), gqa_q_gain[l]), rope_gqa)
        gk = apply_rope(rms_norm(heads(gk_l, GQA_KV_HEADS), gqa_k_gain[l]), rope_gqa)
        sq = apply_rope(heads(sq_l, SWA_HEADS), rope_swa)
        sk = apply_rope(heads(sk_l, SWA_KV_HEADS), rope_swa)
        flat2 = lambda t: t.reshape(t.shape[0] * t.shape[1], -1).astype(BF16)
        ga = global_attention(flat2(gq), flat2(gk), flat2(gv_l), flat2(gk_c), flat2(gv_c), batch)
        wa = window_attention(flat2(sq), flat2(sk), flat2(sv_l), flat2(sk_c), flat2(sv_c), swa_sinks[l], batch)
        branches = [
            fourier_mix_pallas(fa_l).reshape(n_l, -1),
            ga,
            hyena_mix_pallas(hy_l, conv_w[l], conv_b[l], filt, hyena_bias[l]).reshape(n_l, -1),
            wa,
        ]
        if ctx_continues:
            gq_ctx = group_heads(rms_norm(heads(gq_c, GQA_HEADS), gqa_q_gain[l]), GQA_KV_HEADS)
            sq_ctx = group_heads(heads(sq_c, SWA_HEADS), SWA_KV_HEADS)
            ctx_branches = (
                fourier_mix_pallas(fa_c),
                flat_heads(softmax_attend(gq_ctx, gk_c, gv_c)),
                hyena_mix_pallas(hy_c, conv_w[l], conv_b[l], filt, hyena_bias[l]),
                flat_heads(softmax_attend(sq_ctx, sk_c, sv_c, sinks)),
            )
            branches = [jnp.concatenate([bl, bc.reshape(n_c, -1).astype(bl.dtype)], axis=0)
                        for bl, bc in zip(branches, ctx_branches)]
        merged = merge_gated(h, branches, w_gate[l].astype(BF16), w_branch[l].astype(BF16), m_rows)
        router_wp = jnp.zeros((d, ROUTER_PAD), F32).at[:, :N_EXPERTS].set(router_w[l])
        router_bp = jnp.zeros((1, ROUTER_PAD), F32).at[0, :N_EXPERTS].set(router_b[l])
        x1, h_moe, logits = outproj_postnorm(merged, w_out[l].astype(BF16), xs, mods[2], mods[3], mods[4],
                                             ln1_g[l][None, :], ln1_b[l][None, :], router_wp, router_bp, tps, batch)

        w_pack = prep_expert_weights(exp_w_gate_up[l], exp_b_gate_up[l], exp_w_down[l], exp_b_down[l])
        y = moe_ffn(h_moe, logits[:, :N_EXPERTS], w_pack)
        gate5 = mods[5][:, 0, :]
        x_lat = post_norm(x1[:n_l].reshape(batch, n_lat, d), gate5[:batch, None, :] * y[:n_l].reshape(batch, n_lat, d),
                          ln2_g[l], ln2_b[l]).reshape(n_l, d)
        if ctx_continues:
            x_ctx = post_norm(x1[n_l:], gate5[batch] * y[n_l:], ln2_g[l], ln2_b[l])
            xs = jnp.concatenate([x_lat, x_ctx], axis=0)
        else:
            xs = x_lat
    return xs.reshape(batch, n_lat, d)
```

```python
import functools
import math

import jax
import jax.numpy as jnp
import numpy as np
from jax import lax
from jax.experimental import pallas as pl
from jax.experimental.pallas import tpu as pltpu

D_MODEL = 2048
DEPTH = 2
GRID_W = 64
FNET_GROUPS = 4
FNET_GROUP_DIM = 128
GQA_HEADS = 4
GQA_KV_HEADS = 2
GQA_HEAD_DIM = 128
HYENA_WIDTH = 512
HYENA_ORDER = 2
HYENA_EMB_DIM = 33
HYENA_DECAY_TARGET = 1e-2
HYENA_SHORT_DECAY_PCT = 0.3
HYENA_LONG_DECAY_PCT = 1.5
SWA_HEADS = 8
SWA_KV_HEADS = 2
SWA_HEAD_DIM = 64
WINDOW = 128
Q_BLOCK = 128
ROPE_THETA = 10000.0
N_EXPERTS = 32
TOP_K = 4
D_EXPERT = 1536
SWIGLU_LIMIT = 7.0
SWIGLU_ALPHA = 1.702
MOE_BLOCK = 128
LN_EPS = 1e-5
RMS_EPS = 1e-6
DEEPNORM_ALPHA = (2 * DEPTH) ** 0.25
SPLIT_SIZES = (FNET_GROUPS * FNET_GROUP_DIM,
               GQA_HEADS * GQA_HEAD_DIM, GQA_KV_HEADS * GQA_HEAD_DIM, GQA_KV_HEADS * GQA_HEAD_DIM,
               3 * HYENA_WIDTH,
               SWA_HEADS * SWA_HEAD_DIM, SWA_KV_HEADS * SWA_HEAD_DIM, SWA_KV_HEADS * SWA_HEAD_DIM)
F32 = jnp.float32
BF16 = jnp.bfloat16


def _mm_body(a_ref, b_ref, o_ref):
    o_ref[...] = jnp.dot(a_ref[...], b_ref[...], preferred_element_type=F32).astype(o_ref.dtype)


def _pick(n, cands):
    for c in cands:
        if n % c == 0:
            return c
    return n


def matmul(a, b, out_dtype=F32):
    m, k = a.shape
    n = b.shape[1]
    tm = _pick(m, (512, 256, 128))
    tn = _pick(n, (2048, 1280, 1024, 768, 512, 256, 128))
    return pl.pallas_call(
        _mm_body,
        grid=(n // tn, m // tm),
        in_specs=[pl.BlockSpec((tm, k), lambda j, i: (i, 0)),
                  pl.BlockSpec((k, tn), lambda j, i: (0, j))],
        out_specs=pl.BlockSpec((tm, tn), lambda j, i: (i, j)),
        out_shape=jax.ShapeDtypeStruct((m, n), out_dtype),
        compiler_params=pltpu.CompilerParams(dimension_semantics=("parallel", "parallel"),
                                             vmem_limit_bytes=VMEM_LIMIT),
        name="mm",
    )(a.astype(BF16), b.astype(BF16))


def mm3(x, w):
    b, n, k = x.shape
    return matmul(x.reshape(b * n, k), w).reshape(b, n, w.shape[1])


ROW_TILE = 512


def _group_map(tiles_per_sample, batch):
    return lambda i, *_: (jnp.minimum(i // tiles_per_sample, batch), 0, 0)


def _ln(x):
    xc = x - jnp.mean(x, axis=-1, keepdims=True)
    return xc * lax.rsqrt(jnp.mean(xc * xc, axis=-1, keepdims=True) + LN_EPS)


def _lnmod_body(x_ref, sh_ref, sc_ref, o_ref):
    o_ref[...] = (_ln(x_ref[...]) * (1.0 + sc_ref[...]) + sh_ref[...]).astype(o_ref.dtype)


def ln_modulate(x, shift, scale, tiles_per_sample, batch):
    t, d = x.shape
    gmap = _group_map(tiles_per_sample, batch)
    return pl.pallas_call(
        _lnmod_body,
        grid=(t // ROW_TILE,),
        in_specs=[pl.BlockSpec((ROW_TILE, d), lambda i: (i, 0)),
                  pl.BlockSpec((None, 1, d), gmap), pl.BlockSpec((None, 1, d), gmap)],
        out_specs=pl.BlockSpec((ROW_TILE, d), lambda i: (i, 0)),
        out_shape=jax.ShapeDtypeStruct((t, d), BF16),
        compiler_params=pltpu.CompilerParams(dimension_semantics=("parallel",)),
        name="ln_modulate",
    )(x, shift, scale)


def matmul_cols(a, w, col0, ncols, out_dtype):
    t, k = a.shape
    tn = _pick(ncols, (512, 256, 128))
    assert col0 % tn == 0 and t % ROW_TILE == 0
    c0 = col0 // tn
    return pl.pallas_call(
        _mm_body,
        grid=(ncols // tn, t // ROW_TILE),
        in_specs=[pl.BlockSpec((ROW_TILE, k), lambda j, i: (i, 0)),
                  pl.BlockSpec((k, tn), lambda j, i: (0, c0 + j))],
        out_specs=pl.BlockSpec((ROW_TILE, tn), lambda j, i: (i, j)),
        out_shape=jax.ShapeDtypeStruct((t, ncols), out_dtype),
        compiler_params=pltpu.CompilerParams(dimension_semantics=("parallel", "parallel"),
                                             vmem_limit_bytes=VMEM_LIMIT),
        name="in_proj",
    )(a, w)


def _rope_body(x_ref, c_ref, s_ref, g_ref, o_ref, *, quarter, rms):
    x = x_ref[...]
    width = x.shape[1]
    reps = width // LANES
    if rms:
        x = jnp.concatenate(
            [xh * lax.rsqrt(jnp.mean(xh * xh, axis=-1, keepdims=True) + RMS_EPS) * g_ref[...]
             for xh in (x[:, r * LANES:(r + 1) * LANES] for r in range(reps))], axis=1)
    lane = lax.broadcasted_iota(jnp.int32, x.shape, 1)
    first = (lane % (2 * quarter)) < quarter
    partner = jnp.where(first, pltpu.roll(x, width - quarter, 1), pltpu.roll(x, quarter, 1))
    c = jnp.concatenate([c_ref[...]] * reps, axis=1)
    s = jnp.concatenate([s_ref[...]] * reps, axis=1)
    o_ref[...] = (x * c + partner * s).astype(o_ref.dtype)


def norm_rope(x, cos_t, sin_t, gain, quarter, tiles_per_sample, batch):
    t, w = x.shape
    n_lat_tiles = tiles_per_sample * batch
    tmap = lambda i: (jnp.where(i < n_lat_tiles, i % tiles_per_sample, tiles_per_sample), 0)
    rms = gain is not None
    g = gain.astype(F32)[None, :] if rms else jnp.ones((1, LANES), F32)
    return pl.pallas_call(
        functools.partial(_rope_body, quarter=quarter, rms=rms),
        grid=(t // ROW_TILE,),
        in_specs=[pl.BlockSpec((ROW_TILE, w), lambda i: (i, 0)),
                  pl.BlockSpec((ROW_TILE, LANES), tmap), pl.BlockSpec((ROW_TILE, LANES), tmap),
                  pl.BlockSpec((1, LANES), lambda i: (0, 0))],
        out_specs=pl.BlockSpec((ROW_TILE, w), lambda i: (i, 0)),
        out_shape=jax.ShapeDtypeStruct((t, w), BF16),
        compiler_params=pltpu.CompilerParams(dimension_semantics=("parallel",)),
        name="norm_rope",
    )(x, cos_t, sin_t, g)


def rope_tables(rows, head_dim):
    cos, sin = axial_rope(rows, head_dim)
    c = jnp.concatenate([cos[:, 0], cos[:, 0], cos[:, 1], cos[:, 1]], axis=-1)
    s = jnp.concatenate([-sin[:, 0], sin[:, 0], -sin[:, 1], sin[:, 1]], axis=-1)
    reps = LANES // head_dim
    c, s = jnp.tile(c, (1, reps)), jnp.tile(s, (1, reps))
    return (jnp.concatenate([c, jnp.ones((ROW_TILE, LANES), F32)], axis=0),
            jnp.concatenate([s, jnp.zeros((ROW_TILE, LANES), F32)], axis=0))


def _merge_body(h_ref, b0_ref, b1_ref, b2_ref, b3_ref, wg_ref, wb_ref, o_ref, acc_ref, br_ref):
    j = pl.program_id(1)
    for jj, b_ref in enumerate((b0_ref, b1_ref, b2_ref, b3_ref)):
        @pl.when(j == jj)
        def _():
            br_ref[...] = b_ref[...].astype(BF16)

    gate = jax.nn.sigmoid(jnp.dot(h_ref[...], wg_ref[...], preferred_element_type=F32))
    term = gate * jnp.dot(br_ref[...], wb_ref[...], preferred_element_type=F32)

    @pl.when(j == 0)
    def _():
        acc_ref[...] = term

    @pl.when(j > 0)
    def _():
        acc_ref[...] += term

    @pl.when(j == pl.num_programs(1) - 1)
    def _():
        o_ref[...] = acc_ref[...].astype(o_ref.dtype)


def merge_gated(h, branches, w_gate, w_branch, rows):
    d = h.shape[1]
    bw = w_branch.shape[1]
    nb = len(branches)
    row = lambda i, j: (i, 0)
    return pl.pallas_call(
        _merge_body,
        grid=(rows // ROW_TILE, nb),
        in_specs=[pl.BlockSpec((ROW_TILE, d), row)] + [pl.BlockSpec((ROW_TILE, bw), row)] * nb
                 + [pl.BlockSpec((None, d, d), lambda i, j: (j, 0, 0)),
                    pl.BlockSpec((None, bw, d), lambda i, j: (j, 0, 0))],
        out_specs=pl.BlockSpec((ROW_TILE, d), row),
        out_shape=jax.ShapeDtypeStruct((rows, d), BF16),
        scratch_shapes=[pltpu.VMEM((ROW_TILE, d), F32), pltpu.VMEM((ROW_TILE, bw), BF16)],
        compiler_params=pltpu.CompilerParams(dimension_semantics=("parallel", "arbitrary"),
                                             vmem_limit_bytes=VMEM_LIMIT),
        name="merge_gated",
    )(h, *branches, w_gate, w_branch)


OUT_TILE = 256
ROUTER_PAD = 128


def _outproj_body(m_ref, w_ref, x_ref, g2_ref, sh_ref, sc_ref, lg_ref, lb_ref, rw_ref, rb_ref,
                  x1_ref, hm_ref, lo_ref):
    mix = jnp.dot(m_ref[...], w_ref[...], preferred_element_type=F32)
    x1 = _ln(DEEPNORM_ALPHA * x_ref[...] + g2_ref[...] * mix) * lg_ref[...] + lb_ref[...]
    x1_ref[...] = x1
    hm = _ln(x1) * (1.0 + sc_ref[...]) + sh_ref[...]
    hm_hi = hm.astype(BF16)
    hm_ref[...] = hm_hi
    hm_lo = (hm - hm_hi.astype(F32)).astype(BF16)
    hh = jnp.dot(hm_hi, rw_ref[...], preferred_element_type=F32)
    lh = jnp.dot(hm_lo, rw_ref[:, :ROUTER_PAD], preferred_element_type=F32)
    lo_ref[...] = hh[:, :ROUTER_PAD] + hh[:, ROUTER_PAD:] + lh + rb_ref[...]


def outproj_postnorm(merged, w_out, x, gate2, shift3, scale4, ln_g, ln_b, router_w, router_b,
                     tiles_per_sample, batch):
    rows, d = merged.shape
    gmap = _group_map(tiles_per_sample * (ROW_TILE // OUT_TILE), batch)
    row = lambda i: (i, 0)
    const = lambda i: (0, 0)
    return pl.pallas_call(
        _outproj_body,
        grid=(rows // OUT_TILE,),
        in_specs=[pl.BlockSpec((OUT_TILE, d), row), pl.BlockSpec((d, d), const), pl.BlockSpec((OUT_TILE, d), row),
                  pl.BlockSpec((None, 1, d), gmap), pl.BlockSpec((None, 1, d), gmap), pl.BlockSpec((None, 1, d), gmap),
                  pl.BlockSpec((1, d), const), pl.BlockSpec((1, d), const),
                  pl.BlockSpec((d, 2 * ROUTER_PAD), const), pl.BlockSpec((1, ROUTER_PAD), const)],
        out_specs=[pl.BlockSpec((OUT_TILE, d), row), pl.BlockSpec((OUT_TILE, d), row),
                   pl.BlockSpec((OUT_TILE, ROUTER_PAD), row)],
        out_shape=[jax.ShapeDtypeStruct((rows, d), F32), jax.ShapeDtypeStruct((rows, d), BF16),
                   jax.ShapeDtypeStruct((rows, ROUTER_PAD), F32)],
        compiler_params=pltpu.CompilerParams(dimension_semantics=("parallel",), vmem_limit_bytes=VMEM_LIMIT),
        name="outproj_postnorm",
    )(merged, w_out, x, gate2, shift3, scale4, ln_g, ln_b, router_w, router_b)


GATTN_TQ = 512
GATTN_TK = 256
GATTN_UNROLL = 4
VMEM_LIMIT = 56 * 1024 * 1024
LANES = 128
LOG2E = math.log2(math.e)


def _gattn_body(q_ref, kc_ref, vc_ref, k_ref, v_ref, o_ref, m_ref, acc_ref, *, c2):
    tq = q_ref.shape[0]
    hd = k_ref.shape[1]
    q2 = jnp.concatenate([q_ref[:, :hd], q_ref[:, hd:]], axis=0)
    m_ref[...] = jnp.full(m_ref.shape, -jnp.inf, F32)
    acc_ref[...] = jnp.zeros(acc_ref.shape, F32)

    def step(k, v):
        s = lax.dot_general(q2, k, (((1,), (1,)), ((), ())), preferred_element_type=F32) * c2
        m_old = m_ref[...]
        m_new = jnp.maximum(m_old, jnp.max(s, axis=-1, keepdims=True))
        alpha = jnp.exp2(m_old - m_new)
        p = jnp.concatenate([jnp.exp2(s[:, j * LANES:(j + 1) * LANES] - m_new)
                             for j in range(k.shape[0] // LANES)], axis=1)
        acc_ref[...] = (jnp.concatenate([alpha, alpha], axis=1) * acc_ref[...]
                        + jnp.dot(p.astype(BF16), v, preferred_element_type=F32))
        m_ref[...] = m_new

    step(kc_ref[...], vc_ref[...])

    def loop(c, carry):
        off = pl.multiple_of(c * GATTN_TK, GATTN_TK)
        step(k_ref[pl.ds(off, GATTN_TK), :], v_ref[pl.ds(off, GATTN_TK), :])
        return carry

    lax.fori_loop(0, k_ref.shape[0] // GATTN_TK, loop, 0, unroll=GATTN_UNROLL)
    acc = acc_ref[...]
    out = acc[:, :hd] / acc[:, hd:]
    o_ref[...] = jnp.concatenate([out[:tq], out[tq:]], axis=1).astype(o_ref.dtype)


def _with_ones(v, n_heads):
    r = v.shape[0]
    v3 = v.reshape(r, n_heads, -1)
    return jnp.concatenate([v3, jnp.ones_like(v3)], axis=-1).reshape(r, -1)


def global_attention(q, k, v, k_ctx, v_ctx, batch):
    hd = GQA_HEAD_DIM
    n = q.shape[0] // batch
    n_c = k_ctx.shape[0] // batch
    g = GQA_HEADS // GQA_KV_HEADS
    tq = min(GATTN_TQ, n)
    assert g == 2 and hd == LANES and n % tq == 0 and n % (GATTN_TK * GATTN_UNROLL) == 0 and n_c % LANES == 0
    nq = n // tq
    return pl.pallas_call(
        functools.partial(_gattn_body, c2=hd ** -0.5 * LOG2E),
        grid=(batch, GQA_KV_HEADS, nq),
        in_specs=[pl.BlockSpec((tq, g * hd), lambda b, h, i: (b * nq + i, h)),
                  pl.BlockSpec((n_c, hd), lambda b, h, i: (b, h)),
                  pl.BlockSpec((n_c, 2 * hd), lambda b, h, i: (b, h)),
                  pl.BlockSpec((n, hd), lambda b, h, i: (b, h)),
                  pl.BlockSpec((n, 2 * hd), lambda b, h, i: (b, h))],
        out_specs=pl.BlockSpec((tq, g * hd), lambda b, h, i: (b * nq + i, h)),
        out_shape=jax.ShapeDtypeStruct(q.shape, BF16),
        scratch_shapes=[pltpu.VMEM((g * tq, hd), F32), pltpu.VMEM((g * tq, 2 * hd), F32)],
        compiler_params=pltpu.CompilerParams(dimension_semantics=("parallel", "parallel", "parallel"),
                                             vmem_limit_bytes=VMEM_LIMIT),
        name="gattn",
    )(q, k_ctx, _with_ones(v_ctx, GQA_KV_HEADS), k, _with_ones(v, GQA_KV_HEADS))


WATTN_TQ = 256


def _wattn_body(sink_ref, q_ref, kc_ref, vc_ref, kp_ref, kx_ref, kn_ref, vp_ref, vx_ref, vn_ref, o_ref,
                *, scale, n, nq):
    tq = q_ref.shape[0]
    hd = SWA_HEAD_DIM
    g = SWA_HEADS // SWA_KV_HEADS
    h = pl.program_id(1)
    i = pl.program_id(2)
    k_band = jnp.concatenate([kp_ref[...], kx_ref[...], kn_ref[...]], axis=0)
    v_band = jnp.concatenate([vp_ref[...], vx_ref[...], vn_ref[...]], axis=0)
    qpos = i * tq + lax.broadcasted_iota(jnp.int32, (tq, tq + 2 * WINDOW), 0)
    kpos = i * tq - WINDOW + lax.broadcasted_iota(jnp.int32, (tq, tq + 2 * WINDOW), 1)
    valid = (jnp.abs(kpos - qpos) <= WINDOW) & (kpos >= 0) & (kpos < n)
    outs = []
    for j in range(g):
        q = q_ref[:, j * hd:(j + 1) * hd]
        s_c = lax.dot_general(q, kc_ref[...], (((1,), (1,)), ((), ())), preferred_element_type=F32) * scale
        s_b = lax.dot_general(q, k_band, (((1,), (1,)), ((), ())), preferred_element_type=F32) * scale
        s_b = jnp.where(valid, s_b, -jnp.inf)
        sink = sink_ref[h * g + j]
        m = jnp.maximum(jnp.maximum(jnp.max(s_c, -1, keepdims=True), jnp.max(s_b, -1, keepdims=True)), sink)
        p_c = jnp.exp(s_c - m)
        p_b = jnp.exp(s_b - m)
        l = jnp.sum(p_c, -1, keepdims=True) + jnp.sum(p_b, -1, keepdims=True) + jnp.exp(sink - m)
        o = (jnp.dot(p_c.astype(BF16), vc_ref[...], preferred_element_type=F32)
             + jnp.dot(p_b.astype(BF16), v_band, preferred_element_type=F32))
        outs.append(o / l)
    o_ref[...] = jnp.concatenate(outs, axis=1).astype(o_ref.dtype)


def window_attention(q, k, v, k_ctx, v_ctx, sinks, batch):
    hd = SWA_HEAD_DIM
    n = q.shape[0] // batch
    n_c = k_ctx.shape[0] // batch
    g = SWA_HEADS // SWA_KV_HEADS
    tq = WATTN_TQ
    assert n % tq == 0 and tq % WINDOW == 0
    nq = n // tq
    r = tq // WINDOW
    nw = n // WINDOW
    k_h = k.reshape(batch * n, SWA_KV_HEADS, hd).transpose(1, 0, 2)
    v_h = v.reshape(batch * n, SWA_KV_HEADS, hd).transpose(1, 0, 2)
    kc_h = k_ctx.reshape(batch * n_c, SWA_KV_HEADS, hd).transpose(1, 0, 2)
    vc_h = v_ctx.reshape(batch * n_c, SWA_KV_HEADS, hd).transpose(1, 0, 2)
    prev_map = lambda b, h, i, s: (h, b * nw + jnp.maximum(i * r - 1, 0), 0)
    cur_map = lambda b, h, i, s: (h, b * nq + i, 0)
    next_map = lambda b, h, i, s: (h, b * nw + jnp.minimum((i + 1) * r, nw - 1), 0)
    ctx_map = lambda b, h, i, s: (h, b, 0)
    grid_spec = pltpu.PrefetchScalarGridSpec(
        num_scalar_prefetch=1,
        grid=(batch, SWA_KV_HEADS, nq),
        in_specs=[pl.BlockSpec((tq, g * hd), lambda b, h, i, s: (b * nq + i, h)),
                  pl.BlockSpec((None, n_c, hd), ctx_map),
                  pl.BlockSpec((None, n_c, hd), ctx_map),
                  pl.BlockSpec((None, WINDOW, hd), prev_map),
                  pl.BlockSpec((None, tq, hd), cur_map),
                  pl.BlockSpec((None, WINDOW, hd), next_map),
                  pl.BlockSpec((None, WINDOW, hd), prev_map),
                  pl.BlockSpec((None, tq, hd), cur_map),
                  pl.BlockSpec((None, WINDOW, hd), next_map)],
        out_specs=pl.BlockSpec((tq, g * hd), lambda b, h, i, s: (b * nq + i, h)),
    )
    return pl.pallas_call(
        functools.partial(_wattn_body, scale=hd ** -0.5, n=n, nq=nq),
        grid_spec=grid_spec,
        out_shape=jax.ShapeDtypeStruct(q.shape, BF16),
        compiler_params=pltpu.CompilerParams(dimension_semantics=("parallel", "parallel", "parallel"),
                                             vmem_limit_bytes=VMEM_LIMIT),
        name="wattn",
    )(sinks.astype(F32), q, kc_h, vc_h, k_h, k_h, k_h, v_h, v_h, v_h)


MOE_TB = 256
MOE_FC = 512


def _ffn_body(blk_e_ref, nused_ref, x_ref, wg_ref, wu_ref, bg_ref, bu_ref, wd_ref, bd_ref, o_ref):
    i = pl.program_id(0)

    @pl.when(i < nused_ref[0])
    def _():
        x = x_ref[...]
        acc = jnp.zeros(o_ref.shape, F32) + bd_ref[...]
        for c in range(D_EXPERT // MOE_FC):
            sl = slice(c * MOE_FC, (c + 1) * MOE_FC)
            gate = jnp.dot(x, wg_ref[:, sl], preferred_element_type=F32) + bg_ref[:, sl]
            up = jnp.dot(x, wu_ref[:, sl], preferred_element_type=F32) + bu_ref[:, sl]
            gate = jnp.minimum(gate, SWIGLU_LIMIT)
            up = jnp.clip(up, -SWIGLU_LIMIT, SWIGLU_LIMIT)
            glu = gate * jax.nn.sigmoid(gate * SWIGLU_ALPHA)
            act = ((up + 1.0) * glu).astype(BF16)
            acc = acc + jnp.dot(act, wd_ref[sl, :], preferred_element_type=F32)
        o_ref[...] = acc.astype(o_ref.dtype)

    @pl.when(i >= nused_ref[0])
    def _():
        o_ref[...] = jnp.zeros(o_ref.shape, o_ref.dtype)


def expert_ffn(buf, blk_e, n_used, w_gate, w_up, b_gate, b_up, w_dn, b_dn):
    n_rows, d = buf.shape
    n_blk = n_rows // MOE_TB
    f = w_gate.shape[2]
    wmap = lambda i, e, u: (e[i], 0, 0)
    grid_spec = pltpu.PrefetchScalarGridSpec(
        num_scalar_prefetch=2,
        grid=(n_blk,),
        in_specs=[pl.BlockSpec((MOE_TB, d), lambda i, e, u: (i, 0)),
                  pl.BlockSpec((None, d, f), wmap),
                  pl.BlockSpec((None, d, f), wmap),
                  pl.BlockSpec((None, 1, f), wmap),
                  pl.BlockSpec((None, 1, f), wmap),
                  pl.BlockSpec((None, f, d), wmap),
                  pl.BlockSpec((None, 1, d), wmap)],
        out_specs=pl.BlockSpec((MOE_TB, d), lambda i, e, u: (i, 0)),
    )
    return pl.pallas_call(
        _ffn_body,
        grid_spec=grid_spec,
        out_shape=jax.ShapeDtypeStruct((n_rows, d), BF16),
        compiler_params=pltpu.CompilerParams(dimension_semantics=("arbitrary",),
                                             vmem_limit_bytes=VMEM_LIMIT),
        name="expert_ffn",
    )(blk_e, n_used, buf, w_gate, w_up, b_gate, b_up, w_dn, b_dn)


def _deint_body(w_ref, p_ref, g_ref, u_ref):
    y = jnp.dot(w_ref[...].astype(BF16), p_ref[...], preferred_element_type=F32)
    half = g_ref.shape[1]
    g_ref[...] = y[:, :half].astype(BF16)
    u_ref[...] = y[:, half:].astype(BF16)


def split_gate_up(w_gu_all, layer):
    _, e, d, f2 = w_gu_all.shape
    blk = 2 * LANES
    src = jnp.arange(blk)[:, None]
    dst = jnp.arange(blk)[None, :]
    perm = (src == jnp.where(dst < LANES, 2 * dst, 2 * (dst - LANES) + 1)).astype(BF16)
    out = jax.ShapeDtypeStruct((e, d, f2 // 2), BF16)
    return pl.pallas_call(
        _deint_body,
        grid=(e, f2 // blk),
        in_specs=[pl.BlockSpec((None, None, d, blk), lambda i, j: (layer, i, 0, j)),
                  pl.BlockSpec((blk, blk), lambda i, j: (0, 0))],
        out_specs=[pl.BlockSpec((None, d, LANES), lambda i, j: (i, 0, j)),
                   pl.BlockSpec((None, d, LANES), lambda i, j: (i, 0, j))],
        out_shape=[out, out],
        compiler_params=pltpu.CompilerParams(dimension_semantics=("parallel", "parallel")),
        name="split_gate_up",
    )(w_gu_all, perm)


FFT_N2 = 128
FFT_SPLIT_MIN = 1024


def _angles(num, den):
    return (2.0 * math.pi / den) * (num % den).astype(F32)


def dft_stage_a(n1, k_in, real_input):
    ang = _angles(jnp.arange(n1)[:, None] * jnp.arange(k_in)[None, :], n1)
    c, s = jnp.cos(ang), jnp.sin(ang)
    if real_input:
        return jnp.concatenate([c, -s], axis=0).astype(BF16)
    return jnp.block([[c, s], [-s, c]]).astype(BF16)


def dft_stage_a_inv(n1, k_out):
    ang = _angles(jnp.arange(k_out)[:, None] * jnp.arange(n1)[None, :], n1)
    c, s = jnp.cos(ang), jnp.sin(ang)
    return jnp.block([[c, -s], [s, c]]).astype(BF16)


def dft_stage_b(n1, n2):
    k1 = jnp.arange(n1)[:, None, None]
    k2 = jnp.arange(n2)[None, :, None]
    m = jnp.arange(n2)[None, None, :]
    ang = _angles(m * (k1 + n1 * k2), n1 * n2)
    c, s = jnp.cos(ang), jnp.sin(ang)
    top = jnp.concatenate([c, s], axis=2)
    bot = jnp.concatenate([-s, c], axis=2)
    return jnp.concatenate([top, bot], axis=1).astype(BF16)


SUBLANES = 8


def _lmm_body(m_ref, x_ref, o_ref):
    m = m_ref[...]
    for s in range(x_ref.shape[1]):
        o_ref[:, s, :] = jnp.dot(m, x_ref[:, s, :].astype(BF16), preferred_element_type=F32)


def left_matmul(m, x, n2):
    g, rows, c = x.shape
    k = rows // n2
    r = m.shape[0]
    assert m.shape[1] == k and n2 % SUBLANES == 0 and c % LANES == 0
    x5 = x.reshape(g, k, n2 // SUBLANES, SUBLANES, c)
    out = pl.pallas_call(
        _lmm_body,
        grid=(g, n2 // SUBLANES, c // LANES),
        in_specs=[pl.BlockSpec((r, k), lambda i, j, l: (0, 0)),
                  pl.BlockSpec((None, k, None, SUBLANES, LANES), lambda i, j, l: (i, 0, j, 0, l))],
        out_specs=pl.BlockSpec((None, r, None, SUBLANES, LANES), lambda i, j, l: (i, 0, j, 0, l)),
        out_shape=jax.ShapeDtypeStruct((g, r, n2 // SUBLANES, SUBLANES, c), F32),
        compiler_params=pltpu.CompilerParams(dimension_semantics=("parallel", "parallel", "parallel"),
                                             vmem_limit_bytes=VMEM_LIMIT),
        name="left_matmul",
    )(m, x5)
    return out.reshape(g, r * n2, c)


def _conv_b_body(t_ref, gs_ref, h_ref, u_ref):
    n2 = t_ref.shape[1]
    c = t_ref.shape[2]
    gs = gs_ref[...]
    y = jnp.dot(gs, t_ref[...].reshape(2 * n2, c).astype(BF16), preferred_element_type=F32)
    yr, yi = y[:n2], y[n2:]
    hr, hi = h_ref[0], h_ref[1]
    z = jnp.concatenate([yr * hr - yi * hi, yr * hi + yi * hr], axis=0).astype(BF16)
    u = lax.dot_general(gs, z, (((0,), (0,)), ((), ())), preferred_element_type=F32)
    u_ref[...] = u.reshape(2, n2, c)


def conv_stage_b(t, gs, h):
    _, n1, n2, c = t.shape
    spec = pl.BlockSpec((2, None, n2, c), lambda i: (0, i, 0, 0))
    return pl.pallas_call(
        _conv_b_body,
        grid=(n1,),
        in_specs=[spec, pl.BlockSpec((None, 2 * n2, 2 * n2), lambda i: (i, 0, 0)), spec],
        out_specs=spec,
        out_shape=jax.ShapeDtypeStruct(t.shape, F32),
        compiler_params=pltpu.CompilerParams(dimension_semantics=("parallel",), vmem_limit_bytes=VMEM_LIMIT),
        name="conv_stage_b",
    )(t, gs, h)


def _filt_b_body(t_ref, gs_ref, ss_ref, h_ref):
    n2 = t_ref.shape[1]
    c = t_ref.shape[2]
    w = HYENA_WIDTH
    y = jnp.dot(gs_ref[...], t_ref[...].reshape(2 * n2, c).astype(BF16), preferred_element_type=F32)
    yr, yi = y[:n2], y[n2:]
    ss = ss_ref[...]
    for o in range(HYENA_ORDER):
        f0, b0 = (2 * o) * w, (2 * o + 1) * w
        scale = lax.rsqrt(ss[:, f0:f0 + w] + ss[:, b0:b0 + w] + 1e-6)
        h_ref[o, 0] = (yr[:, f0:f0 + w] + yr[:, b0:b0 + w]) * scale
        h_ref[o, 1] = (yi[:, f0:f0 + w] - yi[:, b0:b0 + w]) * scale


def filter_stage_b(t, gs, sumsq):
    _, n1, n2, c = t.shape
    w = HYENA_WIDTH
    return pl.pallas_call(
        _filt_b_body,
        grid=(n1,),
        in_specs=[pl.BlockSpec((2, None, n2, c), lambda i: (0, i, 0, 0)),
                  pl.BlockSpec((None, 2 * n2, 2 * n2), lambda i: (i, 0, 0)),
                  pl.BlockSpec((1, c), lambda i: (0, 0))],
        out_specs=pl.BlockSpec((HYENA_ORDER, 2, None, n2, w), lambda i: (0, 0, i, 0, 0)),
        out_shape=jax.ShapeDtypeStruct((HYENA_ORDER, 2, n1, n2, w), F32),
        compiler_params=pltpu.CompilerParams(dimension_semantics=("parallel",), vmem_limit_bytes=VMEM_LIMIT),
        name="filter_stage_b",
    )(t, gs, sumsq)


def _fnet_b_body(t_ref, gs_ref, cs_ref, o_ref):
    n2 = t_ref.shape[1]
    c = t_ref.shape[2]
    y = jnp.dot(gs_ref[...], t_ref[...].reshape(2 * n2, c).astype(BF16), preferred_element_type=F32)
    yr, yi = y[:n2].astype(BF16), y[n2:].astype(BF16)
    gd = FNET_GROUP_DIM
    outs = []
    for g in range(c // gd):
        v = jnp.concatenate([yr[:, g * gd:(g + 1) * gd], yi[:, g * gd:(g + 1) * gd]], axis=1)
        outs.append(jnp.dot(v, cs_ref[...], preferred_element_type=F32))
    o_ref[...] = jnp.concatenate(outs, axis=1).astype(o_ref.dtype)


def fnet_stage_b(t, gs, cs, out_dtype):
    b, _, n1, n2, c = t.shape
    return pl.pallas_call(
        _fnet_b_body,
        grid=(b, n1),
        in_specs=[pl.BlockSpec((None, 2, None, n2, c), lambda i, j: (i, 0, j, 0, 0)),
                  pl.BlockSpec((None, 2 * n2, 2 * n2), lambda i, j: (j, 0, 0)),
                  pl.BlockSpec(cs.shape, lambda i, j: (0, 0))],
        out_specs=pl.BlockSpec((None, n2, c), lambda i, j: (i, 0, j)),
        out_shape=jax.ShapeDtypeStruct((b, n2, n1 * c), out_dtype),
        compiler_params=pltpu.CompilerParams(dimension_semantics=("parallel", "parallel"),
                                             vmem_limit_bytes=VMEM_LIMIT),
        name="fnet_stage_b",
    )(t, gs, cs)


def fourier_mix_pallas(u):
    b, n, c = u.shape
    gd = FNET_GROUP_DIM
    n2 = FFT_N2 if n >= FFT_SPLIT_MIN else n
    n1 = n // n2
    gs = dft_stage_b(n1, n2)
    if n1 > 1:
        t = left_matmul(dft_stage_a(n1, n1, True), u.astype(F32), n2).reshape(b, 2, n1, n2, c)
    else:
        t = jnp.stack([u.astype(F32), jnp.zeros(u.shape, F32)], axis=1).reshape(b, 2, 1, n2, c)
    ang = _angles(jnp.arange(gd)[:, None] * jnp.arange(gd)[None, :], gd)
    cs = (jnp.concatenate([jnp.cos(ang), jnp.sin(ang)], axis=0) * (n * gd) ** -0.5).astype(BF16)
    out = fnet_stage_b(t, gs, cs, u.dtype)
    return out.reshape(b, n2, n1, c).reshape(b, n, c)


FILT_TM = 512
FILT_LANES = 128


def _filter_body(ff_ref, w1_ref, b1_ref, fr1_ref, w2_ref, b2_ref, fr2_ref, w3_ref, dl_ref, taps_ref, ss_ref, *, n):
    i = pl.program_id(0)
    tm = taps_ref.shape[0]
    hi = lax.Precision.HIGHEST
    pos = (i * tm + lax.broadcasted_iota(jnp.int32, (tm, FILT_LANES), 0)).astype(F32)
    lane = lax.broadcasted_iota(jnp.int32, (tm, FILT_LANES), 1)
    t = pos / (n - 1.0) if n > 1 else pos * 0.0
    omega = (2.0 * math.pi) * pos / n
    arg = ff_ref[...] * omega
    bands = (HYENA_EMB_DIM - 1) // 2
    z = jnp.where(lane == 0, t, jnp.where(lane <= bands, jnp.cos(arg),
                                          jnp.where(lane <= 2 * bands, -jnp.sin(arg), 0.0)))
    h = jnp.sin(fr1_ref[...] * (jnp.dot(z, w1_ref[...], precision=hi, preferred_element_type=F32) + b1_ref[...]))
    h = jnp.sin(fr2_ref[...] * (jnp.dot(h, w2_ref[...], precision=hi, preferred_element_type=F32) + b2_ref[...]))
    h = jnp.dot(h, w3_ref[...], precision=hi, preferred_element_type=F32)
    width = h.shape[1]
    h = h * jnp.exp(-t[:, :1] * dl_ref[...])
    row = i * tm + lax.broadcasted_iota(jnp.int32, (tm, width), 0)
    col = lax.broadcasted_iota(jnp.int32, (tm, width), 1)
    h = jnp.where((row == 0) & ((col // HYENA_WIDTH) % 2 == 1), 0.0, h)
    taps_ref[...] = h

    @pl.when(i == 0)
    def _():
        ss_ref[...] = jnp.zeros(ss_ref.shape, F32)

    ss_ref[...] += jnp.sum(h * h, axis=0, keepdims=True)


def hyena_taps(n, w1, b1, fr1, w2, b2, fr2, w3):
    bands = (HYENA_EMB_DIM - 1) // 2
    fd = w1.shape[1]
    width = w3.shape[1]
    freqs = jnp.linspace(1e-4, bands - 1, bands, dtype=F32)
    featfreq = jnp.zeros((1, FILT_LANES), F32).at[0, 1:1 + bands].set(freqs).at[0, 1 + bands:1 + 2 * bands].set(freqs)
    w1p = jnp.zeros((FILT_LANES, fd), F32).at[:HYENA_EMB_DIM].set(w1.astype(F32))
    max_decay = math.log(HYENA_DECAY_TARGET) / HYENA_SHORT_DECAY_PCT
    min_decay = math.log(HYENA_DECAY_TARGET) / HYENA_LONG_DECAY_PCT
    deltas = jnp.abs(jnp.linspace(min_decay, max_decay, HYENA_WIDTH, dtype=F32))
    dl = jnp.tile(deltas, width // HYENA_WIDTH)[None, :]
    tm = min(FILT_TM, n)
    full = lambda a: pl.BlockSpec(a.shape, lambda i: (0,) * a.ndim)
    ops = (featfreq, w1p, b1.astype(F32)[None, :], fr1.astype(F32)[None, :], w2.astype(F32),
           b2.astype(F32)[None, :], fr2.astype(F32)[None, :], w3.astype(F32), dl)
    return pl.pallas_call(
        functools.partial(_filter_body, n=n),
        grid=(n // tm,),
        in_specs=[full(a) for a in ops],
        out_specs=[pl.BlockSpec((tm, width), lambda i: (i, 0)), pl.BlockSpec((1, width), lambda i: (0, 0))],
        out_shape=[jax.ShapeDtypeStruct((n, width), F32), jax.ShapeDtypeStruct((1, width), F32)],
        compiler_params=pltpu.CompilerParams(dimension_semantics=("arbitrary",), vmem_limit_bytes=VMEM_LIMIT),
        name="hyena_taps",
    )(*ops)


def _fft_split(n):
    if n >= FFT_SPLIT_MIN:
        return 2 * n // FFT_N2, FFT_N2, n // FFT_N2
    return 1, 2 * n, 1


def hyena_mix_pallas(zc, conv_w, conv_b, filt, bias):
    batch, n, _ = zc.shape
    assert batch == 2
    w = HYENA_WIDTH
    u = short_conv(zc, conv_w, conv_b).astype(F32)
    x1, x2, v = jnp.split(u, 3, axis=-1)
    n1, n2, k1 = _fft_split(n)
    gs = dft_stage_b(n1, n2)
    taps, sumsq = hyena_taps(n, *filt)
    if n1 > 1:
        t = left_matmul(dft_stage_a(n1, k1, True), taps[None], n2).reshape(2, n1, n2, taps.shape[1])
    else:
        t = jnp.stack([jnp.pad(taps, ((0, n), (0, 0))), jnp.zeros((2 * n, taps.shape[1]), F32)])[:, None]
    spec = filter_stage_b(t, gs, sumsq)
    fwd = dft_stage_a(n1, k1, False)
    inv = dft_stage_a_inv(n1, k1)

    def long_conv(z, order):
        if n1 > 1:
            t = left_matmul(fwd, z.reshape(1, 2 * n, w), n2).reshape(2, n1, n2, w)
            uu = conv_stage_b(t, gs, spec[order])
            y = left_matmul(inv, uu.reshape(1, 2 * n1 * n2, w), n2).reshape(2, n, w)
        else:
            t = jnp.pad(z, ((0, 0), (0, n), (0, 0)))[:, None]
            y = conv_stage_b(t, gs, spec[order])[:, 0, :n]
        return y * (1.0 / (2 * n)) + z * bias[order].astype(F32)

    y = x1 * long_conv(v, 0)
    y = x2 * long_conv(y, 1)
    return y.astype(zc.dtype)


def layer_norm(x):
    xf = x.astype(F32)
    xc = xf - jnp.mean(xf, -1, keepdims=True)
    return xc * lax.rsqrt(jnp.mean(xc * xc, -1, keepdims=True) + LN_EPS)


def modulate(x, shift, scale):
    return (layer_norm(x) * (1.0 + scale.astype(F32)) + shift.astype(F32)).astype(x.dtype)


def post_norm(x, update, gain, bias):
    y = layer_norm(DEEPNORM_ALPHA * x.astype(F32) + update.astype(F32))
    return (y * gain.astype(F32) + bias.astype(F32)).astype(x.dtype)


def rms_norm(x, gain):
    xf = x.astype(F32)
    y = xf * lax.rsqrt(jnp.mean(xf * xf, -1, keepdims=True) + RMS_EPS) * gain.astype(F32)
    return y.astype(x.dtype)


def split_projection(z):
    points = np.cumsum(SPLIT_SIZES)[:-1].tolist()
    return jnp.split(z, points, axis=-1)


def heads(z, n_heads):
    b, n, w = z.shape
    return z.reshape(b, n, n_heads, w // n_heads)


def group_heads(q, n_kv):
    b, n, h, hd = q.shape
    return q.reshape(b, n, n_kv, h // n_kv, hd)


def flat_heads(o):
    return o.reshape(o.shape[0], o.shape[1], -1)


def axial_rope(rows, head_dim):
    n_freq = head_dim // 4
    inv_freq = jnp.power(ROPE_THETA, -jnp.arange(n_freq, dtype=F32) / n_freq)
    t = jnp.arange(rows * GRID_W)
    row = (t // GRID_W).astype(F32)
    col = (t % GRID_W).astype(F32)
    ang = jnp.stack([row[:, None] * inv_freq, col[:, None] * inv_freq], axis=1)
    return jnp.cos(ang), jnp.sin(ang)


def apply_rope(x, rope):
    cos, sin = rope
    b, n, h, hd = x.shape
    xr = x.astype(F32).reshape(b, n, h, 2, 2, hd // 4)
    x1, x2 = xr[..., 0, :], xr[..., 1, :]
    c = cos[None, :, None]
    s = sin[None, :, None]
    out = jnp.stack([x1 * c - x2 * s, x2 * c + x1 * s], axis=-2)
    return out.reshape(x.shape).astype(x.dtype)


def softmax_attend(q, k, v, sinks=None):
    scale = q.shape[-1] ** -0.5
    s = jnp.einsum("bqkgd,bskd->bkgqs", q, k, preferred_element_type=F32) * scale
    if sinks is None:
        p = jax.nn.softmax(s, axis=-1)
    else:
        sink_col = jnp.broadcast_to(sinks.astype(F32)[None, :, :, None, None], s.shape[:-1] + (1,))
        p = jax.nn.softmax(jnp.concatenate([sink_col, s], axis=-1), axis=-1)[..., 1:]
    return jnp.einsum("bkgqs,bskd->bqkgd", p.astype(v.dtype), v)


def global_attention_latent(q, k, v, k_ctx, v_ctx):
    b, n = q.shape[:2]
    k_all = jnp.concatenate([k_ctx, k], axis=1)
    v_all = jnp.concatenate([v_ctx, v], axis=1)
    qb = q.reshape((b, n // Q_BLOCK, Q_BLOCK) + q.shape[2:]).swapaxes(0, 1)
    out = lax.map(lambda q_blk: softmax_attend(q_blk, k_all, v_all), qb)
    return out.swapaxes(0, 1).reshape(b, n, -1)


def window_attention_latent(q, k, v, k_ctx, v_ctx, sinks):
    b, n = q.shape[:2]
    n_blk = n // Q_BLOCK
    n_c = k_ctx.shape[1]
    scale = q.shape[-1] ** -0.5
    pad = ((0, 0), (Q_BLOCK, Q_BLOCK), (0, 0), (0, 0))
    kp = jnp.pad(k, pad)
    vp = jnp.pad(v, pad)
    qb = q.reshape((b, n_blk, Q_BLOCK) + q.shape[2:]).swapaxes(0, 1)
    qi = jnp.arange(Q_BLOCK)[:, None]
    kj = jnp.arange(3 * Q_BLOCK)[None, :]
    in_window = jnp.abs(kj - Q_BLOCK - qi) <= WINDOW
    sink_val = sinks.astype(F32)[None, :, :, None, None]

    def block(args):
        q_blk, i = args
        kb = lax.dynamic_slice_in_dim(kp, i * Q_BLOCK, 3 * Q_BLOCK, axis=1)
        vb = lax.dynamic_slice_in_dim(vp, i * Q_BLOCK, 3 * Q_BLOCK, axis=1)
        kpos = i * Q_BLOCK - Q_BLOCK + kj
        valid = in_window & (kpos >= 0) & (kpos < n)
        s_ctx = jnp.einsum("bqkgd,bskd->bkgqs", q_blk, k_ctx, preferred_element_type=F32) * scale
        s_band = jnp.einsum("bqkgd,bskd->bkgqs", q_blk, kb, preferred_element_type=F32) * scale
        s_band = jnp.where(valid, s_band, -jnp.inf)
        sink_col = jnp.broadcast_to(sink_val, s_ctx.shape[:-1] + (1,))
        p = jax.nn.softmax(jnp.concatenate([sink_col, s_ctx, s_band], axis=-1), axis=-1).astype(v.dtype)
        return (jnp.einsum("bkgqs,bskd->bqkgd", p[..., 1:1 + n_c], v_ctx)
                + jnp.einsum("bkgqs,bskd->bqkgd", p[..., 1 + n_c:], vb))

    out = lax.map(block, (qb, jnp.arange(n_blk)))
    return out.swapaxes(0, 1).reshape(b, n, -1)


def fourier_mix(u):
    b, n, _ = u.shape
    ug = u.astype(F32).reshape(b, n, FNET_GROUPS, FNET_GROUP_DIM)
    y = jnp.fft.fft2(ug, axes=(1, 3), norm="ortho").real
    return y.reshape(b, n, FNET_GROUPS * FNET_GROUP_DIM).astype(u.dtype)


def short_conv(u, w, bias):
    n = u.shape[1]
    up = jnp.pad(u, ((0, 0), (1, 1), (0, 0)))
    return up[:, :n] * w[0] + up[:, 1:n + 1] * w[1] + up[:, 2:] * w[2] + bias


def hyena_filter_spectrum(n, w1, b1, fr1, w2, b2, fr2, w3):
    t = jnp.linspace(0.0, 1.0, n, dtype=F32)[:, None]
    bands = (HYENA_EMB_DIM - 1) // 2
    omega = 2.0 * math.pi * jnp.arange(n, dtype=F32)[:, None] / n
    freqs = jnp.linspace(1e-4, bands - 1, bands, dtype=F32)[None, :]
    z = jnp.concatenate([t, jnp.cos(freqs * omega), -jnp.sin(freqs * omega)], axis=-1)
    h = jnp.sin(fr1.astype(F32) * (z @ w1.astype(F32) + b1.astype(F32)))
    h = jnp.sin(fr2.astype(F32) * (h @ w2.astype(F32) + b2.astype(F32)))
    h = (h @ w3.astype(F32)).reshape(n, HYENA_ORDER, 2, HYENA_WIDTH)
    max_decay = math.log(HYENA_DECAY_TARGET) / HYENA_SHORT_DECAY_PCT
    min_decay = math.log(HYENA_DECAY_TARGET) / HYENA_LONG_DECAY_PCT
    deltas = jnp.abs(jnp.linspace(min_decay, max_decay, HYENA_WIDTH, dtype=F32))
    h = h * jnp.exp(-t * deltas)[:, None, None, :]
    taps = jnp.concatenate([h[:, :, 0],
                            jnp.zeros((1, HYENA_ORDER, HYENA_WIDTH), F32),
                            h[:0:-1, :, 1]], axis=0)
    taps = taps * lax.rsqrt(jnp.sum(taps * taps, axis=0, keepdims=True) + 1e-6)
    return jnp.fft.rfft(taps, axis=0)


def hyena_mix(zc, conv_w, conv_b, filt, bias):
    n = zc.shape[1]
    u = short_conv(zc, conv_w, conv_b).astype(F32)
    x1, x2, v = jnp.split(u, 3, axis=-1)
    spec = hyena_filter_spectrum(n, *filt)

    def long_conv(z, order):
        zf = jnp.fft.rfft(z, n=2 * n, axis=1)
        y = jnp.fft.irfft(zf * spec[None, :, order], n=2 * n, axis=1)[:, :n]
        return y + z * bias[order].astype(F32)

    y = x1 * long_conv(v, 0)
    y = x2 * long_conv(y, 1)
    return y.astype(zc.dtype)


def merge_branches(h, branches, w_gate_l, w_branch_l, w_out_l):
    merged = sum(jax.nn.sigmoid(mm3(h, w_gate_l[i])) * mm3(br, w_branch_l[i]) for i, br in enumerate(branches))
    return mm3(merged, w_out_l)


def moe_ffn(tokens, logits, w_pack):
    n_tok, d = tokens.shape
    i32 = jnp.int32
    top_v, top_i = lax.top_k(logits, TOP_K)
    weights = jax.nn.softmax(top_v, axis=-1)
    flat_e = top_i.reshape(-1).astype(i32)
    n_asg = n_tok * TOP_K
    e_ids = jnp.arange(N_EXPERTS, dtype=i32)
    onehot = (flat_e[:, None] == e_ids[None, :]).astype(i32)
    csum = jnp.cumsum(onehot, axis=0)
    rank = jnp.sum(onehot * csum, axis=1) - 1
    counts = csum[-1]
    padded = (counts + MOE_TB - 1) // MOE_TB * MOE_TB
    padded_end = jnp.cumsum(padded)
    padded_start = padded_end - padded
    dest = jnp.sum(onehot * padded_start[None, :], axis=1) + rank
    n_rows = n_asg + N_EXPERTS * MOE_TB
    n_blk = n_rows // MOE_TB
    blk_start = jnp.arange(n_blk, dtype=i32) * MOE_TB
    blk_e = jnp.minimum(jnp.sum((blk_start[:, None] >= padded_end[None, :]).astype(i32), axis=1), N_EXPERTS - 1)
    n_used = (padded_end[-1:] // MOE_TB).astype(i32)
    pad_need = (padded - counts)[:, None]
    pad_key = jnp.where(jnp.arange(MOE_TB, dtype=i32)[None, :] < pad_need, 2 * e_ids[:, None] + 1, 2 * N_EXPERTS)
    keys = jnp.concatenate([2 * flat_e, pad_key.reshape(-1)])
    vals = jnp.concatenate([jnp.arange(n_asg, dtype=i32) // TOP_K, jnp.zeros((N_EXPERTS * MOE_TB,), i32)])
    _, src = lax.sort_key_val(keys, vals, is_stable=True)
    buf = tokens[src]
    y = expert_ffn(buf, blk_e, n_used, *w_pack)
    dest = dest.reshape(n_tok, TOP_K)
    return [y[dest[:, k]] for k in range(TOP_K)], weights


def prep_expert_weights(w_gu_all, b_gu, w_dn, b_dn, layer):
    packed = tuple(split_gate_up(w_gu_all, layer)) + (
        b_gu[:, None, 0::2].astype(F32), b_gu[:, None, 1::2].astype(F32),
        w_dn.astype(BF16), b_dn[:, None, :].astype(F32))
    return packed


def _combine_body(x_ref, y0_ref, y1_ref, y2_ref, y3_ref, w_ref, g5_ref, lg_ref, lb_ref, o_ref):
    w = w_ref[...]
    y = sum(w[:, k:k + 1] * y_ref[...].astype(F32) for k, y_ref in enumerate((y0_ref, y1_ref, y2_ref, y3_ref)))
    o_ref[...] = _ln(DEEPNORM_ALPHA * x_ref[...] + g5_ref[...] * y) * lg_ref[...] + lb_ref[...]


def combine_postnorm(x1, ys, weights, gate5, ln_g, ln_b, tiles_per_sample, batch):
    rows, d = x1.shape
    gmap = _group_map(tiles_per_sample * (ROW_TILE // OUT_TILE), batch)
    row = lambda i: (i, 0)
    const = lambda i: (0, 0)
    return pl.pallas_call(
        _combine_body,
        grid=(rows // OUT_TILE,),
        in_specs=[pl.BlockSpec((OUT_TILE, d), row)] + [pl.BlockSpec((OUT_TILE, d), row)] * TOP_K
                 + [pl.BlockSpec((OUT_TILE, TOP_K), row), pl.BlockSpec((None, 1, d), gmap),
                    pl.BlockSpec((1, d), const), pl.BlockSpec((1, d), const)],
        out_specs=pl.BlockSpec((OUT_TILE, d), row),
        out_shape=jax.ShapeDtypeStruct((rows, d), F32),
        compiler_params=pltpu.CompilerParams(dimension_semantics=("parallel",), vmem_limit_bytes=VMEM_LIMIT),
        name="combine_postnorm",
    )(x1, *ys, weights, gate5, ln_g, ln_b)


def kernel(x, c, ctx, c_ctx, mod_w, mod_b, w_in, gqa_q_gain, gqa_k_gain, conv_w, conv_b,
           filt_w1, filt_b1, filt_freq1, filt_w2, filt_b2, filt_freq2, filt_w3, hyena_bias,
           swa_sinks, w_branch, w_gate, w_out, ln1_g, ln1_b, router_w, router_b,
           exp_w_gate_up, exp_b_gate_up, exp_w_down, exp_b_down, ln2_g, ln2_b):
    batch, n_lat, d = x.shape
    n_ctx = ctx.shape[1]
    rows = n_lat // GRID_W
    rope_gqa = rope_tables(rows, GQA_HEAD_DIM)
    rope_swa = rope_tables(rows, SWA_HEAD_DIM)
    n_l, n_c = batch * n_lat, batch * n_ctx
    assert n_lat % ROW_TILE == 0 and n_c % ROW_TILE == 0
    tps = n_lat // ROW_TILE
    xs = jnp.concatenate([x.reshape(n_l, d), ctx.reshape(n_c, d)], axis=0)
    cond = jnp.concatenate([c, c_ctx[None, :]], axis=0)
    offs = np.cumsum((0,) + SPLIT_SIZES)
    for l in range(DEPTH):
        ctx_continues = l < DEPTH - 1
        m_rows = n_l + n_c if ctx_continues else n_l
        mods = (jax.nn.silu(cond) @ mod_w[l] + mod_b[l]).reshape(batch + 1, 6, 1, d).swapaxes(0, 1)
        filt = (filt_w1[l], filt_b1[l], filt_freq1[l], filt_w2[l], filt_b2[l], filt_freq2[l], filt_w3[l])
        sinks = swa_sinks[l].reshape(SWA_KV_HEADS, SWA_HEADS // SWA_KV_HEADS)

        h = ln_modulate(xs, mods[0], mods[1], tps, batch)
        w_in_b = w_in[l].astype(BF16)
        seg_dtypes = (F32, F32, F32, BF16, F32, F32, F32, BF16)
        segs = [matmul_cols(h, w_in_b, int(offs[i]), SPLIT_SIZES[i], seg_dtypes[i]) for i in range(len(SPLIT_SIZES))]
        lat3 = lambda t: t[:n_l].reshape(batch, n_lat, -1)
        ctx3 = lambda t: t[n_l:].reshape(batch, n_ctx, -1)
        fa, gq, gk, gv, hy, sq, sk, sv = segs
        fa_l, hy_l, fa_c, hy_c = lat3(fa), lat3(hy), ctx3(fa), ctx3(hy)
        gq = norm_rope(gq, *rope_gqa, gqa_q_gain[l], GQA_HEAD_DIM // 4, tps, batch)
        gk = norm_rope(gk, *rope_gqa, gqa_k_gain[l], GQA_HEAD_DIM // 4, tps, batch)
        sq = norm_rope(sq, *rope_swa, None, SWA_HEAD_DIM // 4, tps, batch)
        sk = norm_rope(sk, *rope_swa, None, SWA_HEAD_DIM // 4, tps, batch)
        ga = global_attention(gq[:n_l], gk[:n_l], gv[:n_l], gk[n_l:], gv[n_l:], batch)
        wa = window_attention(sq[:n_l], sk[:n_l], sv[:n_l], sk[n_l:], sv[n_l:], swa_sinks[l], batch)
        branches = [
            fourier_mix_pallas(fa_l).reshape(n_l, -1),
            ga,
            hyena_mix_pallas(hy_l, conv_w[l], conv_b[l], filt, hyena_bias[l]).reshape(n_l, -1),
            wa,
        ]
        if ctx_continues:
            gq_ctx = group_heads(heads(ctx3(gq), GQA_HEADS), GQA_KV_HEADS)
            sq_ctx = group_heads(heads(ctx3(sq), SWA_HEADS), SWA_KV_HEADS)
            ctx_branches = (
                fourier_mix_pallas(fa_c),
                flat_heads(softmax_attend(gq_ctx, heads(ctx3(gk), GQA_KV_HEADS), heads(ctx3(gv), GQA_KV_HEADS))),
                hyena_mix_pallas(hy_c, conv_w[l], conv_b[l], filt, hyena_bias[l]),
                flat_heads(softmax_attend(sq_ctx, heads(ctx3(sk), SWA_KV_HEADS), heads(ctx3(sv), SWA_KV_HEADS), sinks)),
            )
            branches = [jnp.concatenate([bl, bc.reshape(n_c, -1).astype(bl.dtype)], axis=0)
                        for bl, bc in zip(branches, ctx_branches)]
        merged = merge_gated(h, branches, w_gate[l].astype(BF16), w_branch[l].astype(BF16), m_rows)
        router_wp = jnp.zeros((d, ROUTER_PAD), F32).at[:, :N_EXPERTS].set(router_w[l])
        router_hi = router_wp.astype(BF16)
        router_hl = jnp.concatenate([router_hi, (router_wp - router_hi.astype(F32)).astype(BF16)], axis=1)
        router_bp = jnp.zeros((1, ROUTER_PAD), F32).at[0, :N_EXPERTS].set(router_b[l])
        x1, h_moe, logits = outproj_postnorm(merged, w_out[l].astype(BF16), xs, mods[2], mods[3], mods[4],
                                             ln1_g[l][None, :], ln1_b[l][None, :], router_hl, router_bp, tps, batch)

        w_pack = prep_expert_weights(exp_w_gate_up, exp_b_gate_up[l], exp_w_down[l], exp_b_down[l], l)
        ys, weights = moe_ffn(h_moe, logits[:, :N_EXPERTS], w_pack)
        xs = combine_postnorm(x1, ys, weights, mods[5], ln2_g[l][None, :], ln2_b[l][None, :], tps, batch)
    return xs.reshape(batch, n_lat, d)
```

```python
import functools
import math

import jax
import jax.numpy as jnp
import numpy as np
from jax import lax
from jax.experimental import pallas as pl
from jax.experimental.pallas import tpu as pltpu

D_MODEL = 2048
DEPTH = 2
GRID_W = 64
FNET_GROUPS = 4
FNET_GROUP_DIM = 128
GQA_HEADS = 4
GQA_KV_HEADS = 2
GQA_HEAD_DIM = 128
HYENA_WIDTH = 512
HYENA_ORDER = 2
HYENA_EMB_DIM = 33
HYENA_DECAY_TARGET = 1e-2
HYENA_SHORT_DECAY_PCT = 0.3
HYENA_LONG_DECAY_PCT = 1.5
SWA_HEADS = 8
SWA_KV_HEADS = 2
SWA_HEAD_DIM = 64
WINDOW = 128
Q_BLOCK = 128
ROPE_THETA = 10000.0
N_EXPERTS = 32
TOP_K = 4
D_EXPERT = 1536
SWIGLU_LIMIT = 7.0
SWIGLU_ALPHA = 1.702
MOE_BLOCK = 128
LN_EPS = 1e-5
RMS_EPS = 1e-6
DEEPNORM_ALPHA = (2 * DEPTH) ** 0.25
SPLIT_SIZES = (FNET_GROUPS * FNET_GROUP_DIM,
               GQA_HEADS * GQA_HEAD_DIM, GQA_KV_HEADS * GQA_HEAD_DIM, GQA_KV_HEADS * GQA_HEAD_DIM,
               3 * HYENA_WIDTH,
               SWA_HEADS * SWA_HEAD_DIM, SWA_KV_HEADS * SWA_HEAD_DIM, SWA_KV_HEADS * SWA_HEAD_DIM)
F32 = jnp.float32
BF16 = jnp.bfloat16


def _mm_body(a_ref, b_ref, o_ref):
    o_ref[...] = jnp.dot(a_ref[...], b_ref[...], preferred_element_type=F32).astype(o_ref.dtype)


def _pick(n, cands):
    for c in cands:
        if n % c == 0:
            return c
    return n


def matmul(a, b, out_dtype=F32):
    m, k = a.shape
    n = b.shape[1]
    tm = _pick(m, (512, 256, 128))
    tn = _pick(n, (2048, 1280, 1024, 768, 512, 256, 128))
    return pl.pallas_call(
        _mm_body,
        grid=(n // tn, m // tm),
        in_specs=[pl.BlockSpec((tm, k), lambda j, i: (i, 0)),
                  pl.BlockSpec((k, tn), lambda j, i: (0, j))],
        out_specs=pl.BlockSpec((tm, tn), lambda j, i: (i, j)),
        out_shape=jax.ShapeDtypeStruct((m, n), out_dtype),
        compiler_params=pltpu.CompilerParams(dimension_semantics=("parallel", "parallel"),
                                             vmem_limit_bytes=VMEM_LIMIT),
        name="mm",
    )(a.astype(BF16), b.astype(BF16))


def mm3(x, w):
    b, n, k = x.shape
    return matmul(x.reshape(b * n, k), w).reshape(b, n, w.shape[1])


ROW_TILE = 512


def _group_map(tiles_per_sample, batch):
    return lambda i, *_: (jnp.minimum(i // tiles_per_sample, batch), 0, 0)


def _ln(x):
    xc = x - jnp.mean(x, axis=-1, keepdims=True)
    return xc * lax.rsqrt(jnp.mean(xc * xc, axis=-1, keepdims=True) + LN_EPS)


def _lnmod_body(x_ref, sh_ref, sc_ref, o_ref):
    o_ref[...] = (_ln(x_ref[...]) * (1.0 + sc_ref[...]) + sh_ref[...]).astype(o_ref.dtype)


def ln_modulate(x, shift, scale, tiles_per_sample, batch):
    t, d = x.shape
    gmap = _group_map(tiles_per_sample, batch)
    return pl.pallas_call(
        _lnmod_body,
        grid=(t // ROW_TILE,),
        in_specs=[pl.BlockSpec((ROW_TILE, d), lambda i: (i, 0)),
                  pl.BlockSpec((None, 1, d), gmap), pl.BlockSpec((None, 1, d), gmap)],
        out_specs=pl.BlockSpec((ROW_TILE, d), lambda i: (i, 0)),
        out_shape=jax.ShapeDtypeStruct((t, d), BF16),
        compiler_params=pltpu.CompilerParams(dimension_semantics=("parallel",)),
        name="ln_modulate",
    )(x, shift, scale)


def matmul_cols(a, w, col0, ncols, out_dtype):
    t, k = a.shape
    tn = _pick(ncols, (512, 256, 128))
    assert col0 % tn == 0 and t % ROW_TILE == 0
    c0 = col0 // tn
    return pl.pallas_call(
        _mm_body,
        grid=(ncols // tn, t // ROW_TILE),
        in_specs=[pl.BlockSpec((ROW_TILE, k), lambda j, i: (i, 0)),
                  pl.BlockSpec((k, tn), lambda j, i: (0, c0 + j))],
        out_specs=pl.BlockSpec((ROW_TILE, tn), lambda j, i: (i, j)),
        out_shape=jax.ShapeDtypeStruct((t, ncols), out_dtype),
        compiler_params=pltpu.CompilerParams(dimension_semantics=("parallel", "parallel"),
                                             vmem_limit_bytes=VMEM_LIMIT),
        name="in_proj",
    )(a, w)


def _rope_body(x_ref, c_ref, s_ref, g_ref, o_ref, *, quarter, rms):
    x = x_ref[...]
    width = x.shape[1]
    reps = width // LANES
    if rms:
        x = jnp.concatenate(
            [xh * lax.rsqrt(jnp.mean(xh * xh, axis=-1, keepdims=True) + RMS_EPS) * g_ref[...]
             for xh in (x[:, r * LANES:(r + 1) * LANES] for r in range(reps))], axis=1)
    lane = lax.broadcasted_iota(jnp.int32, x.shape, 1)
    first = (lane % (2 * quarter)) < quarter
    partner = jnp.where(first, pltpu.roll(x, width - quarter, 1), pltpu.roll(x, quarter, 1))
    c = jnp.concatenate([c_ref[...]] * reps, axis=1)
    s = jnp.concatenate([s_ref[...]] * reps, axis=1)
    o_ref[...] = (x * c + partner * s).astype(o_ref.dtype)


def norm_rope(x, cos_t, sin_t, gain, quarter, tiles_per_sample, batch):
    t, w = x.shape
    n_lat_tiles = tiles_per_sample * batch
    tmap = lambda i: (jnp.where(i < n_lat_tiles, i % tiles_per_sample, tiles_per_sample), 0)
    rms = gain is not None
    g = gain.astype(F32)[None, :] if rms else jnp.ones((1, LANES), F32)
    return pl.pallas_call(
        functools.partial(_rope_body, quarter=quarter, rms=rms),
        grid=(t // ROW_TILE,),
        in_specs=[pl.BlockSpec((ROW_TILE, w), lambda i: (i, 0)),
                  pl.BlockSpec((ROW_TILE, LANES), tmap), pl.BlockSpec((ROW_TILE, LANES), tmap),
                  pl.BlockSpec((1, LANES), lambda i: (0, 0))],
        out_specs=pl.BlockSpec((ROW_TILE, w), lambda i: (i, 0)),
        out_shape=jax.ShapeDtypeStruct((t, w), BF16),
        compiler_params=pltpu.CompilerParams(dimension_semantics=("parallel",)),
        name="norm_rope",
    )(x, cos_t, sin_t, g)


def rope_tables(rows, head_dim):
    cos, sin = axial_rope(rows, head_dim)
    c = jnp.concatenate([cos[:, 0], cos[:, 0], cos[:, 1], cos[:, 1]], axis=-1)
    s = jnp.concatenate([-sin[:, 0], sin[:, 0], -sin[:, 1], sin[:, 1]], axis=-1)
    reps = LANES // head_dim
    c, s = jnp.tile(c, (1, reps)), jnp.tile(s, (1, reps))
    return (jnp.concatenate([c, jnp.ones((ROW_TILE, LANES), F32)], axis=0),
            jnp.concatenate([s, jnp.zeros((ROW_TILE, LANES), F32)], axis=0))


def _merge_body(h_ref, b0_ref, b1_ref, b2_ref, b3_ref, wg_ref, wb_ref, o_ref, acc_ref, br_ref):
    j = pl.program_id(1)
    for jj, b_ref in enumerate((b0_ref, b1_ref, b2_ref, b3_ref)):
        @pl.when(j == jj)
        def _():
            br_ref[...] = b_ref[...].astype(BF16)

    gate = jax.nn.sigmoid(jnp.dot(h_ref[...], wg_ref[...], preferred_element_type=F32))
    term = gate * jnp.dot(br_ref[...], wb_ref[...], preferred_element_type=F32)

    @pl.when(j == 0)
    def _():
        acc_ref[...] = term

    @pl.when(j > 0)
    def _():
        acc_ref[...] += term

    @pl.when(j == pl.num_programs(1) - 1)
    def _():
        o_ref[...] = acc_ref[...].astype(o_ref.dtype)


def merge_gated(h, branches, w_gate, w_branch, rows):
    d = h.shape[1]
    bw = w_branch.shape[1]
    nb = len(branches)
    row = lambda i, j: (i, 0)
    return pl.pallas_call(
        _merge_body,
        grid=(rows // ROW_TILE, nb),
        in_specs=[pl.BlockSpec((ROW_TILE, d), row)] + [pl.BlockSpec((ROW_TILE, bw), row)] * nb
                 + [pl.BlockSpec((None, d, d), lambda i, j: (j, 0, 0)),
                    pl.BlockSpec((None, bw, d), lambda i, j: (j, 0, 0))],
        out_specs=pl.BlockSpec((ROW_TILE, d), row),
        out_shape=jax.ShapeDtypeStruct((rows, d), BF16),
        scratch_shapes=[pltpu.VMEM((ROW_TILE, d), F32), pltpu.VMEM((ROW_TILE, bw), BF16)],
        compiler_params=pltpu.CompilerParams(dimension_semantics=("parallel", "arbitrary"),
                                             vmem_limit_bytes=VMEM_LIMIT),
        name="merge_gated",
    )(h, *branches, w_gate, w_branch)


OUT_TILE = 256
ROUTER_PAD = 128


def _outproj_body(m_ref, w_ref, x_ref, g2_ref, sh_ref, sc_ref, lg_ref, lb_ref, rw_ref, rb_ref,
                  x1_ref, hm_ref, lo_ref):
    mix = jnp.dot(m_ref[...], w_ref[...], preferred_element_type=F32)
    x1 = _ln(DEEPNORM_ALPHA * x_ref[...] + g2_ref[...] * mix) * lg_ref[...] + lb_ref[...]
    x1_ref[...] = x1
    hm = _ln(x1) * (1.0 + sc_ref[...]) + sh_ref[...]
    hm_hi = hm.astype(BF16)
    hm_ref[...] = hm_hi
    hm_lo = (hm - hm_hi.astype(F32)).astype(BF16)
    hh = jnp.dot(hm_hi, rw_ref[...], preferred_element_type=F32)
    lh = jnp.dot(hm_lo, rw_ref[:, :ROUTER_PAD], preferred_element_type=F32)
    lo_ref[...] = hh[:, :ROUTER_PAD] + hh[:, ROUTER_PAD:] + lh + rb_ref[...]


def outproj_postnorm(merged, w_out, x, gate2, shift3, scale4, ln_g, ln_b, router_w, router_b,
                     tiles_per_sample, batch):
    rows, d = merged.shape
    gmap = _group_map(tiles_per_sample * (ROW_TILE // OUT_TILE), batch)
    row = lambda i: (i, 0)
    const = lambda i: (0, 0)
    return pl.pallas_call(
        _outproj_body,
        grid=(rows // OUT_TILE,),
        in_specs=[pl.BlockSpec((OUT_TILE, d), row), pl.BlockSpec((d, d), const), pl.BlockSpec((OUT_TILE, d), row),
                  pl.BlockSpec((None, 1, d), gmap), pl.BlockSpec((None, 1, d), gmap), pl.BlockSpec((None, 1, d), gmap),
                  pl.BlockSpec((1, d), const), pl.BlockSpec((1, d), const),
                  pl.BlockSpec((d, 2 * ROUTER_PAD), const), pl.BlockSpec((1, ROUTER_PAD), const)],
        out_specs=[pl.BlockSpec((OUT_TILE, d), row), pl.BlockSpec((OUT_TILE, d), row),
                   pl.BlockSpec((OUT_TILE, ROUTER_PAD), row)],
        out_shape=[jax.ShapeDtypeStruct((rows, d), F32), jax.ShapeDtypeStruct((rows, d), BF16),
                   jax.ShapeDtypeStruct((rows, ROUTER_PAD), F32)],
        compiler_params=pltpu.CompilerParams(dimension_semantics=("parallel",), vmem_limit_bytes=VMEM_LIMIT),
        name="outproj_postnorm",
    )(merged, w_out, x, gate2, shift3, scale4, ln_g, ln_b, router_w, router_b)


GATTN_TQ = 512
GATTN_TK = 256
GATTN_UNROLL = 4
VMEM_LIMIT = 56 * 1024 * 1024
LANES = 128
LOG2E = math.log2(math.e)


def _gattn_body(q_ref, kc_ref, vc_ref, k_ref, v_ref, o_ref, m_ref, acc_ref, *, c2):
    tq = q_ref.shape[0]
    hd = k_ref.shape[1]
    q2 = jnp.concatenate([q_ref[:, :hd], q_ref[:, hd:]], axis=0)
    m_ref[...] = jnp.full(m_ref.shape, -jnp.inf, F32)
    acc_ref[...] = jnp.zeros(acc_ref.shape, F32)

    def step(k, v):
        s = lax.dot_general(q2, k, (((1,), (1,)), ((), ())), preferred_element_type=F32) * c2
        m_old = m_ref[...]
        m_new = jnp.maximum(m_old, jnp.max(s, axis=-1, keepdims=True))
        alpha = jnp.exp2(m_old - m_new)
        p = jnp.concatenate([jnp.exp2(s[:, j * LANES:(j + 1) * LANES] - m_new)
                             for j in range(k.shape[0] // LANES)], axis=1)
        acc_ref[...] = (jnp.concatenate([alpha, alpha], axis=1) * acc_ref[...]
                        + jnp.dot(p.astype(BF16), v, preferred_element_type=F32))
        m_ref[...] = m_new

    step(kc_ref[...], vc_ref[...])

    def loop(c, carry):
        off = pl.multiple_of(c * GATTN_TK, GATTN_TK)
        step(k_ref[pl.ds(off, GATTN_TK), :], v_ref[pl.ds(off, GATTN_TK), :])
        return carry

    lax.fori_loop(0, k_ref.shape[0] // GATTN_TK, loop, 0, unroll=GATTN_UNROLL)
    acc = acc_ref[...]
    out = acc[:, :hd] / acc[:, hd:]
    o_ref[...] = jnp.concatenate([out[:tq], out[tq:]], axis=1).astype(o_ref.dtype)


def _with_ones(v, n_heads):
    r = v.shape[0]
    v3 = v.reshape(r, n_heads, -1)
    return jnp.concatenate([v3, jnp.ones_like(v3)], axis=-1).reshape(r, -1)


def global_attention(q, k, v, k_ctx, v_ctx, batch):
    hd = GQA_HEAD_DIM
    n = q.shape[0] // batch
    n_c = k_ctx.shape[0] // batch
    g = GQA_HEADS // GQA_KV_HEADS
    tq = min(GATTN_TQ, n)
    assert g == 2 and hd == LANES and n % tq == 0 and n % (GATTN_TK * GATTN_UNROLL) == 0 and n_c % LANES == 0
    nq = n // tq
    return pl.pallas_call(
        functools.partial(_gattn_body, c2=hd ** -0.5 * LOG2E),
        grid=(batch, GQA_KV_HEADS, nq),
        in_specs=[pl.BlockSpec((tq, g * hd), lambda b, h, i: (b * nq + i, h)),
                  pl.BlockSpec((n_c, hd), lambda b, h, i: (b, h)),
                  pl.BlockSpec((n_c, 2 * hd), lambda b, h, i: (b, h)),
                  pl.BlockSpec((n, hd), lambda b, h, i: (b, h)),
                  pl.BlockSpec((n, 2 * hd), lambda b, h, i: (b, h))],
        out_specs=pl.BlockSpec((tq, g * hd), lambda b, h, i: (b * nq + i, h)),
        out_shape=jax.ShapeDtypeStruct(q.shape, BF16),
        scratch_shapes=[pltpu.VMEM((g * tq, hd), F32), pltpu.VMEM((g * tq, 2 * hd), F32)],
        compiler_params=pltpu.CompilerParams(dimension_semantics=("parallel", "parallel", "parallel"),
                                             vmem_limit_bytes=VMEM_LIMIT),
        name="gattn",
    )(q, k_ctx, _with_ones(v_ctx, GQA_KV_HEADS), k, _with_ones(v, GQA_KV_HEADS))


WATTN_TQ = 256


def _wattn_body(sink_ref, q_ref, kc_ref, vc_ref, kp_ref, kx_ref, kn_ref, vp_ref, vx_ref, vn_ref, o_ref,
                *, scale, n, nq):
    tq = q_ref.shape[0]
    hd = SWA_HEAD_DIM
    g = SWA_HEADS // SWA_KV_HEADS
    h = pl.program_id(1)
    i = pl.program_id(2)
    k_band = jnp.concatenate([kp_ref[...], kx_ref[...], kn_ref[...]], axis=0)
    v_band = jnp.concatenate([vp_ref[...], vx_ref[...], vn_ref[...]], axis=0)
    qpos = i * tq + lax.broadcasted_iota(jnp.int32, (tq, tq + 2 * WINDOW), 0)
    kpos = i * tq - WINDOW + lax.broadcasted_iota(jnp.int32, (tq, tq + 2 * WINDOW), 1)
    valid = (jnp.abs(kpos - qpos) <= WINDOW) & (kpos >= 0) & (kpos < n)
    outs = []
    for j in range(g):
        q = q_ref[:, j * hd:(j + 1) * hd]
        s_c = lax.dot_general(q, kc_ref[...], (((1,), (1,)), ((), ())), preferred_element_type=F32) * scale
        s_b = lax.dot_general(q, k_band, (((1,), (1,)), ((), ())), preferred_element_type=F32) * scale
        s_b = jnp.where(valid, s_b, -jnp.inf)
        sink = sink_ref[h * g + j]
        m = jnp.maximum(jnp.maximum(jnp.max(s_c, -1, keepdims=True), jnp.max(s_b, -1, keepdims=True)), sink)
        p_c = jnp.exp(s_c - m)
        p_b = jnp.exp(s_b - m)
        l = jnp.sum(p_c, -1, keepdims=True) + jnp.sum(p_b, -1, keepdims=True) + jnp.exp(sink - m)
        o = (jnp.dot(p_c.astype(BF16), vc_ref[...], preferred_element_type=F32)
             + jnp.dot(p_b.astype(BF16), v_band, preferred_element_type=F32))
        outs.append(o / l)
    o_ref[...] = jnp.concatenate(outs, axis=1).astype(o_ref.dtype)


def window_attention(q, k, v, k_ctx, v_ctx, sinks, batch):
    hd = SWA_HEAD_DIM
    n = q.shape[0] // batch
    n_c = k_ctx.shape[0] // batch
    g = SWA_HEADS // SWA_KV_HEADS
    tq = WATTN_TQ
    assert n % tq == 0 and tq % WINDOW == 0
    nq = n // tq
    r = tq // WINDOW
    nw = n // WINDOW
    k_h = k.reshape(batch * n, SWA_KV_HEADS, hd).transpose(1, 0, 2)
    v_h = v.reshape(batch * n, SWA_KV_HEADS, hd).transpose(1, 0, 2)
    kc_h = k_ctx.reshape(batch * n_c, SWA_KV_HEADS, hd).transpose(1, 0, 2)
    vc_h = v_ctx.reshape(batch * n_c, SWA_KV_HEADS, hd).transpose(1, 0, 2)
    prev_map = lambda b, h, i, s: (h, b * nw + jnp.maximum(i * r - 1, 0), 0)
    cur_map = lambda b, h, i, s: (h, b * nq + i, 0)
    next_map = lambda b, h, i, s: (h, b * nw + jnp.minimum((i + 1) * r, nw - 1), 0)
    ctx_map = lambda b, h, i, s: (h, b, 0)
    grid_spec = pltpu.PrefetchScalarGridSpec(
        num_scalar_prefetch=1,
        grid=(batch, SWA_KV_HEADS, nq),
        in_specs=[pl.BlockSpec((tq, g * hd), lambda b, h, i, s: (b * nq + i, h)),
                  pl.BlockSpec((None, n_c, hd), ctx_map),
                  pl.BlockSpec((None, n_c, hd), ctx_map),
                  pl.BlockSpec((None, WINDOW, hd), prev_map),
                  pl.BlockSpec((None, tq, hd), cur_map),
                  pl.BlockSpec((None, WINDOW, hd), next_map),
                  pl.BlockSpec((None, WINDOW, hd), prev_map),
                  pl.BlockSpec((None, tq, hd), cur_map),
                  pl.BlockSpec((None, WINDOW, hd), next_map)],
        out_specs=pl.BlockSpec((tq, g * hd), lambda b, h, i, s: (b * nq + i, h)),
    )
    return pl.pallas_call(
        functools.partial(_wattn_body, scale=hd ** -0.5, n=n, nq=nq),
        grid_spec=grid_spec,
        out_shape=jax.ShapeDtypeStruct(q.shape, BF16),
        compiler_params=pltpu.CompilerParams(dimension_semantics=("parallel", "parallel", "parallel"),
                                             vmem_limit_bytes=VMEM_LIMIT),
        name="wattn",
    )(sinks.astype(F32), q, kc_h, vc_h, k_h, k_h, k_h, v_h, v_h, v_h)


MOE_TB = 512
MOE_FC = 512


def _ffn_body(blk_e_ref, nused_ref, x_ref, wg_ref, wu_ref, bg_ref, bu_ref, wd_ref, bd_ref, o_ref):
    i = pl.program_id(0)

    @pl.when(i < nused_ref[0])
    def _():
        x = x_ref[...]
        acc = jnp.zeros(o_ref.shape, F32) + bd_ref[...]
        for c in range(D_EXPERT // MOE_FC):
            sl = slice(c * MOE_FC, (c + 1) * MOE_FC)
            gate = jnp.dot(x, wg_ref[:, sl], preferred_element_type=F32) + bg_ref[:, sl]
            up = jnp.dot(x, wu_ref[:, sl], preferred_element_type=F32) + bu_ref[:, sl]
            gate = jnp.minimum(gate, SWIGLU_LIMIT)
            up = jnp.clip(up, -SWIGLU_LIMIT, SWIGLU_LIMIT)
            glu = gate * jax.nn.sigmoid(gate * SWIGLU_ALPHA)
            act = ((up + 1.0) * glu).astype(BF16)
            acc = acc + jnp.dot(act, wd_ref[sl, :], preferred_element_type=F32)
        o_ref[...] = acc.astype(o_ref.dtype)

    @pl.when(i >= nused_ref[0])
    def _():
        o_ref[...] = jnp.zeros(o_ref.shape, o_ref.dtype)


def expert_ffn(buf, blk_e, n_used, w_gate, w_up, b_gate, b_up, w_dn, b_dn):
    n_rows, d = buf.shape
    n_blk = n_rows // MOE_TB
    f = w_gate.shape[2]
    wmap = lambda i, e, u: (e[i], 0, 0)
    grid_spec = pltpu.PrefetchScalarGridSpec(
        num_scalar_prefetch=2,
        grid=(n_blk,),
        in_specs=[pl.BlockSpec((MOE_TB, d), lambda i, e, u: (i, 0)),
                  pl.BlockSpec((None, d, f), wmap),
                  pl.BlockSpec((None, d, f), wmap),
                  pl.BlockSpec((None, 1, f), wmap),
                  pl.BlockSpec((None, 1, f), wmap),
                  pl.BlockSpec((None, f, d), wmap),
                  pl.BlockSpec((None, 1, d), wmap)],
        out_specs=pl.BlockSpec((MOE_TB, d), lambda i, e, u: (i, 0)),
    )
    return pl.pallas_call(
        _ffn_body,
        grid_spec=grid_spec,
        out_shape=jax.ShapeDtypeStruct((n_rows, d), BF16),
        compiler_params=pltpu.CompilerParams(dimension_semantics=("arbitrary",),
                                             vmem_limit_bytes=VMEM_LIMIT),
        name="expert_ffn",
    )(blk_e, n_used, buf, w_gate, w_up, b_gate, b_up, w_dn, b_dn)


SPLIT_ROWS = 512


def _deint_body(w_ref, p_ref, g_ref, u_ref):
    blk = p_ref.shape[0]
    for c in range(w_ref.shape[1] // blk):
        y = jnp.dot(w_ref[:, c * blk:(c + 1) * blk].astype(BF16), p_ref[...], preferred_element_type=F32)
        g_ref[:, c * LANES:(c + 1) * LANES] = y[:, :LANES].astype(BF16)
        u_ref[:, c * LANES:(c + 1) * LANES] = y[:, LANES:].astype(BF16)


def split_gate_up(w_gu_all, layer):
    _, e, d, f2 = w_gu_all.shape
    blk = 2 * LANES
    src = jnp.arange(blk)[:, None]
    dst = jnp.arange(blk)[None, :]
    perm = (src == jnp.where(dst < LANES, 2 * dst, 2 * (dst - LANES) + 1)).astype(BF16)
    out = jax.ShapeDtypeStruct((e, d, f2 // 2), BF16)
    return pl.pallas_call(
        _deint_body,
        grid=(e, d // SPLIT_ROWS),
        in_specs=[pl.BlockSpec((None, None, SPLIT_ROWS, f2), lambda i, j: (layer, i, j, 0)),
                  pl.BlockSpec((blk, blk), lambda i, j: (0, 0))],
        out_specs=[pl.BlockSpec((None, SPLIT_ROWS, f2 // 2), lambda i, j: (i, j, 0)),
                   pl.BlockSpec((None, SPLIT_ROWS, f2 // 2), lambda i, j: (i, j, 0))],
        out_shape=[out, out],
        compiler_params=pltpu.CompilerParams(dimension_semantics=("parallel", "parallel"),
                                             vmem_limit_bytes=VMEM_LIMIT),
        name="split_gate_up",
    )(w_gu_all, perm)


FFT_N2 = 128
FFT_SPLIT_MIN = 1024


def _angles(num, den):
    return (2.0 * math.pi / den) * (num % den).astype(F32)


def dft_stage_a(n1, k_in, real_input):
    ang = _angles(jnp.arange(n1)[:, None] * jnp.arange(k_in)[None, :], n1)
    c, s = jnp.cos(ang), jnp.sin(ang)
    if real_input:
        return jnp.concatenate([c, -s], axis=0).astype(BF16)
    return jnp.block([[c, s], [-s, c]]).astype(BF16)


def dft_stage_a_inv(n1, k_out):
    ang = _angles(jnp.arange(k_out)[:, None] * jnp.arange(n1)[None, :], n1)
    c, s = jnp.cos(ang), jnp.sin(ang)
    return jnp.block([[c, -s], [s, c]]).astype(BF16)


def dft_stage_b(n1, n2):
    k1 = jnp.arange(n1)[:, None, None]
    k2 = jnp.arange(n2)[None, :, None]
    m = jnp.arange(n2)[None, None, :]
    ang = _angles(m * (k1 + n1 * k2), n1 * n2)
    c, s = jnp.cos(ang), jnp.sin(ang)
    top = jnp.concatenate([c, s], axis=2)
    bot = jnp.concatenate([-s, c], axis=2)
    return jnp.concatenate([top, bot], axis=1).astype(BF16)


SUBLANES = 8


def _lmm_body(m_ref, x_ref, o_ref):
    m = m_ref[...]
    k, sub, lanes = x_ref.shape
    r = o_ref.shape[0]
    x2 = x_ref.reshape(k * sub, lanes)
    o2 = o_ref.reshape(r * sub, lanes)
    for s in range(sub):
        xs = x2[pl.ds(s, k, stride=sub), :].astype(BF16)
        o2[pl.ds(s, r, stride=sub), :] = jnp.dot(m, xs, preferred_element_type=F32)


def left_matmul(m, x, n2):
    g, rows, c = x.shape
    k = rows // n2
    r = m.shape[0]
    assert m.shape[1] == k and n2 % SUBLANES == 0 and c % LANES == 0
    x5 = x.reshape(g, k, n2 // SUBLANES, SUBLANES, c)
    out = pl.pallas_call(
        _lmm_body,
        grid=(g, n2 // SUBLANES, c // LANES),
        in_specs=[pl.BlockSpec((r, k), lambda i, j, l: (0, 0)),
                  pl.BlockSpec((None, k, None, SUBLANES, LANES), lambda i, j, l: (i, 0, j, 0, l))],
        out_specs=pl.BlockSpec((None, r, None, SUBLANES, LANES), lambda i, j, l: (i, 0, j, 0, l)),
        out_shape=jax.ShapeDtypeStruct((g, r, n2 // SUBLANES, SUBLANES, c), F32),
        compiler_params=pltpu.CompilerParams(dimension_semantics=("parallel", "parallel", "parallel"),
                                             vmem_limit_bytes=VMEM_LIMIT),
        name="left_matmul",
    )(m, x5)
    return out.reshape(g, r * n2, c)


def _lmm_gate_body(m_ref, x_ref, a_ref, z_ref, b_ref, o_ref, *, scale):
    m = m_ref[...]
    k, sub, lanes = x_ref.shape
    r = o_ref.shape[0]
    x2 = x_ref.reshape(k * sub, lanes)
    a2 = a_ref.reshape(r * sub, lanes)
    z2 = z_ref.reshape(r * sub, lanes)
    o2 = o_ref.reshape(r * sub, lanes)
    for s in range(sub):
        rows = pl.ds(s, r, stride=sub)
        y = jnp.dot(m, x2[pl.ds(s, k, stride=sub), :].astype(BF16), preferred_element_type=F32)
        o2[rows, :] = a2[rows, :] * (y * scale + z2[rows, :] * b_ref[...])


def left_matmul_gated(m, x, n2, gate, z, bias, scale):
    g, rows, c = x.shape
    k = rows // n2
    r = m.shape[0]
    assert m.shape[1] == k and gate.shape == (g, r * n2, c) and z.shape == gate.shape
    view = lambda t, lead: t.reshape(g, lead, n2 // SUBLANES, SUBLANES, c)
    blk = lambda lead: pl.BlockSpec((None, lead, None, SUBLANES, LANES), lambda i, j, l: (i, 0, j, 0, l))
    out = pl.pallas_call(
        functools.partial(_lmm_gate_body, scale=scale),
        grid=(g, n2 // SUBLANES, c // LANES),
        in_specs=[pl.BlockSpec((r, k), lambda i, j, l: (0, 0)), blk(k), blk(r), blk(r),
                  pl.BlockSpec((1, LANES), lambda i, j, l: (0, l))],
        out_specs=blk(r),
        out_shape=jax.ShapeDtypeStruct((g, r, n2 // SUBLANES, SUBLANES, c), F32),
        compiler_params=pltpu.CompilerParams(dimension_semantics=("parallel", "parallel", "parallel"),
                                             vmem_limit_bytes=VMEM_LIMIT),
        name="left_matmul_gated",
    )(m, view(x, k), view(gate, r), view(z, r), bias.astype(F32)[None, :])
    return out.reshape(g, r * n2, c)


def _sconv_body(p_ref, x_ref, n_ref, w_ref, b_ref, o1_ref, o2_ref, o3_ref, *, tiles_per_sample):
    i = pl.program_id(0)
    pos = i % tiles_per_sample
    x = x_ref[...]
    rows = x.shape[0]
    before = jnp.where(pos == 0, 0.0, p_ref[SUBLANES - 1:SUBLANES, :])
    after = jnp.where(pos == tiles_per_sample - 1, 0.0, n_ref[0:1, :])
    ridx = lax.broadcasted_iota(jnp.int32, x.shape, 0)
    up = jnp.where(ridx == 0, before, pltpu.roll(x, 1, 0))
    dn = jnp.where(ridx == rows - 1, after, pltpu.roll(x, rows - 1, 0))
    y = up * w_ref[0:1, :] + x * w_ref[1:2, :] + dn * w_ref[2:3, :] + b_ref[...]
    w = o1_ref.shape[1]
    o1_ref[...] = y[:, :w]
    o2_ref[...] = y[:, w:2 * w]
    o3_ref[...] = y[:, 2 * w:]


def short_conv_split(z, col_blocks, conv_w, conv_b, tiles_per_sample, batch):
    t, _ = z.shape
    wid = conv_w.shape[1]
    n_rows = batch * tiles_per_sample * ROW_TILE
    per = ROW_TILE // SUBLANES
    last = t // SUBLANES - 1
    out = jax.ShapeDtypeStruct((n_rows, wid // 3), F32)
    return pl.pallas_call(
        functools.partial(_sconv_body, tiles_per_sample=tiles_per_sample),
        grid=(batch * tiles_per_sample,),
        in_specs=[pl.BlockSpec((SUBLANES, wid), lambda i: (jnp.maximum(i * per - 1, 0), col_blocks)),
                  pl.BlockSpec((ROW_TILE, wid), lambda i: (i, col_blocks)),
                  pl.BlockSpec((SUBLANES, wid), lambda i: (jnp.minimum((i + 1) * per, last), col_blocks)),
                  pl.BlockSpec((3, wid), lambda i: (0, 0)), pl.BlockSpec((1, wid), lambda i: (0, 0))],
        out_specs=[pl.BlockSpec((ROW_TILE, wid // 3), lambda i: (i, 0))] * 3,
        out_shape=[out, out, out],
        compiler_params=pltpu.CompilerParams(dimension_semantics=("parallel",), vmem_limit_bytes=VMEM_LIMIT),
        name="short_conv_split",
    )(z, z, z, conv_w.astype(F32), conv_b.astype(F32)[None, :])


def _conv_b_body(t_ref, gs_ref, h_ref, u_ref):
    n2 = t_ref.shape[1]
    c = t_ref.shape[2]
    gs = gs_ref[...]
    y = jnp.dot(gs, t_ref[...].reshape(2 * n2, c).astype(BF16), preferred_element_type=F32)
    yr, yi = y[:n2], y[n2:]
    hr, hi = h_ref[0], h_ref[1]
    z = jnp.concatenate([yr * hr - yi * hi, yr * hi + yi * hr], axis=0).astype(BF16)
    u = lax.dot_general(gs, z, (((0,), (0,)), ((), ())), preferred_element_type=F32)
    u_ref[...] = u.reshape(2, n2, c)


def conv_stage_b(t, gs, h):
    _, n1, n2, c = t.shape
    spec = pl.BlockSpec((2, None, n2, c), lambda i: (0, i, 0, 0))
    return pl.pallas_call(
        _conv_b_body,
        grid=(n1,),
        in_specs=[spec, pl.BlockSpec((None, 2 * n2, 2 * n2), lambda i: (i, 0, 0)), spec],
        out_specs=spec,
        out_shape=jax.ShapeDtypeStruct(t.shape, F32),
        compiler_params=pltpu.CompilerParams(dimension_semantics=("parallel",), vmem_limit_bytes=VMEM_LIMIT),
        name="conv_stage_b",
    )(t, gs, h)


def _filt_b_body(t_ref, gs_ref, ss_ref, h_ref):
    n2 = t_ref.shape[1]
    c = t_ref.shape[2]
    w = HYENA_WIDTH
    y = jnp.dot(gs_ref[...], t_ref[...].reshape(2 * n2, c).astype(BF16), preferred_element_type=F32)
    yr, yi = y[:n2], y[n2:]
    ss = ss_ref[...]
    for o in range(HYENA_ORDER):
        f0, b0 = (2 * o) * w, (2 * o + 1) * w
        scale = lax.rsqrt(ss[:, f0:f0 + w] + ss[:, b0:b0 + w] + 1e-6)
        h_ref[o, 0] = (yr[:, f0:f0 + w] + yr[:, b0:b0 + w]) * scale
        h_ref[o, 1] = (yi[:, f0:f0 + w] - yi[:, b0:b0 + w]) * scale


def filter_stage_b(t, gs, sumsq):
    _, n1, n2, c = t.shape
    w = HYENA_WIDTH
    return pl.pallas_call(
        _filt_b_body,
        grid=(n1,),
        in_specs=[pl.BlockSpec((2, None, n2, c), lambda i: (0, i, 0, 0)),
                  pl.BlockSpec((None, 2 * n2, 2 * n2), lambda i: (i, 0, 0)),
                  pl.BlockSpec((1, c), lambda i: (0, 0))],
        out_specs=pl.BlockSpec((HYENA_ORDER, 2, None, n2, w), lambda i: (0, 0, i, 0, 0)),
        out_shape=jax.ShapeDtypeStruct((HYENA_ORDER, 2, n1, n2, w), F32),
        compiler_params=pltpu.CompilerParams(dimension_semantics=("parallel",), vmem_limit_bytes=VMEM_LIMIT),
        name="filter_stage_b",
    )(t, gs, sumsq)


def _fnet_b_body(t_ref, gs_ref, cs_ref, o_ref):
    n2 = t_ref.shape[1]
    c = t_ref.shape[2]
    y = jnp.dot(gs_ref[...], t_ref[...].reshape(2 * n2, c).astype(BF16), preferred_element_type=F32)
    yr, yi = y[:n2].astype(BF16), y[n2:].astype(BF16)
    gd = FNET_GROUP_DIM
    outs = []
    for g in range(c // gd):
        v = jnp.concatenate([yr[:, g * gd:(g + 1) * gd], yi[:, g * gd:(g + 1) * gd]], axis=1)
        outs.append(jnp.dot(v, cs_ref[...], preferred_element_type=F32))
    o_ref[...] = jnp.concatenate(outs, axis=1).astype(o_ref.dtype)


def fnet_stage_b(t, gs, cs, out_dtype):
    b, _, n1, n2, c = t.shape
    return pl.pallas_call(
        _fnet_b_body,
        grid=(b, n1),
        in_specs=[pl.BlockSpec((None, 2, None, n2, c), lambda i, j: (i, 0, j, 0, 0)),
                  pl.BlockSpec((None, 2 * n2, 2 * n2), lambda i, j: (j, 0, 0)),
                  pl.BlockSpec(cs.shape, lambda i, j: (0, 0))],
        out_specs=pl.BlockSpec((None, n2, c), lambda i, j: (i, 0, j)),
        out_shape=jax.ShapeDtypeStruct((b, n2, n1 * c), out_dtype),
        compiler_params=pltpu.CompilerParams(dimension_semantics=("parallel", "parallel"),
                                             vmem_limit_bytes=VMEM_LIMIT),
        name="fnet_stage_b",
    )(t, gs, cs)


def fourier_mix_pallas(u):
    b, n, c = u.shape
    gd = FNET_GROUP_DIM
    n2 = FFT_N2 if n >= FFT_SPLIT_MIN else n
    n1 = n // n2
    gs = dft_stage_b(n1, n2)
    if n1 > 1:
        t = left_matmul(dft_stage_a(n1, n1, True), u.astype(F32), n2).reshape(b, 2, n1, n2, c)
    else:
        t = jnp.stack([u.astype(F32), jnp.zeros(u.shape, F32)], axis=1).reshape(b, 2, 1, n2, c)
    ang = _angles(jnp.arange(gd)[:, None] * jnp.arange(gd)[None, :], gd)
    cs = (jnp.concatenate([jnp.cos(ang), jnp.sin(ang)], axis=0) * (n * gd) ** -0.5).astype(BF16)
    out = fnet_stage_b(t, gs, cs, u.dtype)
    return out.reshape(b, n2, n1, c).reshape(b, n, c)


FILT_TM = 512
FILT_LANES = 128


def _filter_body(ff_ref, w1_ref, b1_ref, fr1_ref, w2_ref, b2_ref, fr2_ref, w3_ref, dl_ref, taps_ref, ss_ref, *, n):
    i = pl.program_id(0)
    tm = taps_ref.shape[0]
    hi = lax.Precision.HIGHEST
    pos = (i * tm + lax.broadcasted_iota(jnp.int32, (tm, FILT_LANES), 0)).astype(F32)
    lane = lax.broadcasted_iota(jnp.int32, (tm, FILT_LANES), 1)
    t = pos / (n - 1.0) if n > 1 else pos * 0.0
    omega = (2.0 * math.pi) * pos / n
    arg = ff_ref[...] * omega
    bands = (HYENA_EMB_DIM - 1) // 2
    z = jnp.where(lane == 0, t, jnp.where(lane <= bands, jnp.cos(arg),
                                          jnp.where(lane <= 2 * bands, -jnp.sin(arg), 0.0)))
    h = jnp.sin(fr1_ref[...] * (jnp.dot(z, w1_ref[...], precision=hi, preferred_element_type=F32) + b1_ref[...]))
    h = jnp.sin(fr2_ref[...] * (jnp.dot(h, w2_ref[...], precision=hi, preferred_element_type=F32) + b2_ref[...]))
    h = jnp.dot(h, w3_ref[...], precision=hi, preferred_element_type=F32)
    width = h.shape[1]
    h = h * jnp.exp(-t[:, :1] * dl_ref[...])
    row = i * tm + lax.broadcasted_iota(jnp.int32, (tm, width), 0)
    col = lax.broadcasted_iota(jnp.int32, (tm, width), 1)
    h = jnp.where((row == 0) & ((col // HYENA_WIDTH) % 2 == 1), 0.0, h)
    taps_ref[...] = h

    @pl.when(i == 0)
    def _():
        ss_ref[...] = jnp.zeros(ss_ref.shape, F32)

    ss_ref[...] += jnp.sum(h * h, axis=0, keepdims=True)


def hyena_taps(n, w1, b1, fr1, w2, b2, fr2, w3):
    bands = (HYENA_EMB_DIM - 1) // 2
    fd = w1.shape[1]
    width = w3.shape[1]
    freqs = jnp.linspace(1e-4, bands - 1, bands, dtype=F32)
    featfreq = jnp.zeros((1, FILT_LANES), F32).at[0, 1:1 + bands].set(freqs).at[0, 1 + bands:1 + 2 * bands].set(freqs)
    w1p = jnp.zeros((FILT_LANES, fd), F32).at[:HYENA_EMB_DIM].set(w1.astype(F32))
    max_decay = math.log(HYENA_DECAY_TARGET) / HYENA_SHORT_DECAY_PCT
    min_decay = math.log(HYENA_DECAY_TARGET) / HYENA_LONG_DECAY_PCT
    deltas = jnp.abs(jnp.linspace(min_decay, max_decay, HYENA_WIDTH, dtype=F32))
    dl = jnp.tile(deltas, width // HYENA_WIDTH)[None, :]
    tm = min(FILT_TM, n)
    full = lambda a: pl.BlockSpec(a.shape, lambda i: (0,) * a.ndim)
    ops = (featfreq, w1p, b1.astype(F32)[None, :], fr1.astype(F32)[None, :], w2.astype(F32),
           b2.astype(F32)[None, :], fr2.astype(F32)[None, :], w3.astype(F32), dl)
    return pl.pallas_call(
        functools.partial(_filter_body, n=n),
        grid=(n // tm,),
        in_specs=[full(a) for a in ops],
        out_specs=[pl.BlockSpec((tm, width), lambda i: (i, 0)), pl.BlockSpec((1, width), lambda i: (0, 0))],
        out_shape=[jax.ShapeDtypeStruct((n, width), F32), jax.ShapeDtypeStruct((1, width), F32)],
        compiler_params=pltpu.CompilerParams(dimension_semantics=("arbitrary",), vmem_limit_bytes=VMEM_LIMIT),
        name="hyena_taps",
    )(*ops)


def _fft_split(n):
    if n >= FFT_SPLIT_MIN:
        return 2 * n // FFT_N2, FFT_N2, n // FFT_N2
    return 1, 2 * n, 1


def hyena_mix_pallas(x1, x2, v, filt, bias):
    batch, n, w = v.shape
    assert batch == 2
    n1, n2, k1 = _fft_split(n)
    gs = dft_stage_b(n1, n2)
    taps, sumsq = hyena_taps(n, *filt)
    if n1 > 1:
        t = left_matmul(dft_stage_a(n1, k1, True), taps[None], n2).reshape(2, n1, n2, taps.shape[1])
    else:
        t = jnp.stack([jnp.pad(taps, ((0, n), (0, 0))), jnp.zeros((2 * n, taps.shape[1]), F32)])[:, None]
    spec = filter_stage_b(t, gs, sumsq)
    fwd = dft_stage_a(n1, k1, False)
    inv = dft_stage_a_inv(n1, k1)

    def gated_long_conv(gate, z, order):
        scale = 1.0 / (2 * n)
        if n1 > 1:
            t = left_matmul(fwd, z.reshape(1, 2 * n, w), n2).reshape(2, n1, n2, w)
            uu = conv_stage_b(t, gs, spec[order])
            return left_matmul_gated(inv, uu.reshape(1, 2 * n1 * n2, w), n2, gate.reshape(1, 2 * n, w),
                                     z.reshape(1, 2 * n, w), bias[order], scale).reshape(2, n, w)
        t = jnp.pad(z, ((0, 0), (0, n), (0, 0)))[:, None]
        y = conv_stage_b(t, gs, spec[order])[:, 0, :n]
        return gate * (y * scale + z * bias[order].astype(F32))

    return gated_long_conv(x2, gated_long_conv(x1, v, 0), 1)


def layer_norm(x):
    xf = x.astype(F32)
    xc = xf - jnp.mean(xf, -1, keepdims=True)
    return xc * lax.rsqrt(jnp.mean(xc * xc, -1, keepdims=True) + LN_EPS)


def modulate(x, shift, scale):
    return (layer_norm(x) * (1.0 + scale.astype(F32)) + shift.astype(F32)).astype(x.dtype)


def post_norm(x, update, gain, bias):
    y = layer_norm(DEEPNORM_ALPHA * x.astype(F32) + update.astype(F32))
    return (y * gain.astype(F32) + bias.astype(F32)).astype(x.dtype)


def rms_norm(x, gain):
    xf = x.astype(F32)
    y = xf * lax.rsqrt(jnp.mean(xf * xf, -1, keepdims=True) + RMS_EPS) * gain.astype(F32)
    return y.astype(x.dtype)


def split_projection(z):
    points = np.cumsum(SPLIT_SIZES)[:-1].tolist()
    return jnp.split(z, points, axis=-1)


def heads(z, n_heads):
    b, n, w = z.shape
    return z.reshape(b, n, n_heads, w // n_heads)


def group_heads(q, n_kv):
    b, n, h, hd = q.shape
    return q.reshape(b, n, n_kv, h // n_kv, hd)


def flat_heads(o):
    return o.reshape(o.shape[0], o.shape[1], -1)


def axial_rope(rows, head_dim):
    n_freq = head_dim // 4
    inv_freq = jnp.power(ROPE_THETA, -jnp.arange(n_freq, dtype=F32) / n_freq)
    t = jnp.arange(rows * GRID_W)
    row = (t // GRID_W).astype(F32)
    col = (t % GRID_W).astype(F32)
    ang = jnp.stack([row[:, None] * inv_freq, col[:, None] * inv_freq], axis=1)
    return jnp.cos(ang), jnp.sin(ang)


def apply_rope(x, rope):
    cos, sin = rope
    b, n, h, hd = x.shape
    xr = x.astype(F32).reshape(b, n, h, 2, 2, hd // 4)
    x1, x2 = xr[..., 0, :], xr[..., 1, :]
    c = cos[None, :, None]
    s = sin[None, :, None]
    out = jnp.stack([x1 * c - x2 * s, x2 * c + x1 * s], axis=-2)
    return out.reshape(x.shape).astype(x.dtype)


def softmax_attend(q, k, v, sinks=None):
    scale = q.shape[-1] ** -0.5
    s = jnp.einsum("bqkgd,bskd->bkgqs", q, k, preferred_element_type=F32) * scale
    if sinks is None:
        p = jax.nn.softmax(s, axis=-1)
    else:
        sink_col = jnp.broadcast_to(sinks.astype(F32)[None, :, :, None, None], s.shape[:-1] + (1,))
        p = jax.nn.softmax(jnp.concatenate([sink_col, s], axis=-1), axis=-1)[..., 1:]
    return jnp.einsum("bkgqs,bskd->bqkgd", p.astype(v.dtype), v)


def global_attention_latent(q, k, v, k_ctx, v_ctx):
    b, n = q.shape[:2]
    k_all = jnp.concatenate([k_ctx, k], axis=1)
    v_all = jnp.concatenate([v_ctx, v], axis=1)
    qb = q.reshape((b, n // Q_BLOCK, Q_BLOCK) + q.shape[2:]).swapaxes(0, 1)
    out = lax.map(lambda q_blk: softmax_attend(q_blk, k_all, v_all), qb)
    return out.swapaxes(0, 1).reshape(b, n, -1)


def window_attention_latent(q, k, v, k_ctx, v_ctx, sinks):
    b, n = q.shape[:2]
    n_blk = n // Q_BLOCK
    n_c = k_ctx.shape[1]
    scale = q.shape[-1] ** -0.5
    pad = ((0, 0), (Q_BLOCK, Q_BLOCK), (0, 0), (0, 0))
    kp = jnp.pad(k, pad)
    vp = jnp.pad(v, pad)
    qb = q.reshape((b, n_blk, Q_BLOCK) + q.shape[2:]).swapaxes(0, 1)
    qi = jnp.arange(Q_BLOCK)[:, None]
    kj = jnp.arange(3 * Q_BLOCK)[None, :]
    in_window = jnp.abs(kj - Q_BLOCK - qi) <= WINDOW
    sink_val = sinks.astype(F32)[None, :, :, None, None]

    def block(args):
        q_blk, i = args
        kb = lax.dynamic_slice_in_dim(kp, i * Q_BLOCK, 3 * Q_BLOCK, axis=1)
        vb = lax.dynamic_slice_in_dim(vp, i * Q_BLOCK, 3 * Q_BLOCK, axis=1)
        kpos = i * Q_BLOCK - Q_BLOCK + kj
        valid = in_window & (kpos >= 0) & (kpos < n)
        s_ctx = jnp.einsum("bqkgd,bskd->bkgqs", q_blk, k_ctx, preferred_element_type=F32) * scale
        s_band = jnp.einsum("bqkgd,bskd->bkgqs", q_blk, kb, preferred_element_type=F32) * scale
        s_band = jnp.where(valid, s_band, -jnp.inf)
        sink_col = jnp.broadcast_to(sink_val, s_ctx.shape[:-1] + (1,))
        p = jax.nn.softmax(jnp.concatenate([sink_col, s_ctx, s_band], axis=-1), axis=-1).astype(v.dtype)
        return (jnp.einsum("bkgqs,bskd->bqkgd", p[..., 1:1 + n_c], v_ctx)
                + jnp.einsum("bkgqs,bskd->bqkgd", p[..., 1 + n_c:], vb))

    out = lax.map(block, (qb, jnp.arange(n_blk)))
    return out.swapaxes(0, 1).reshape(b, n, -1)


def fourier_mix(u):
    b, n, _ = u.shape
    ug = u.astype(F32).reshape(b, n, FNET_GROUPS, FNET_GROUP_DIM)
    y = jnp.fft.fft2(ug, axes=(1, 3), norm="ortho").real
    return y.reshape(b, n, FNET_GROUPS * FNET_GROUP_DIM).astype(u.dtype)


def short_conv(u, w, bias):
    n = u.shape[1]
    up = jnp.pad(u, ((0, 0), (1, 1), (0, 0)))
    return up[:, :n] * w[0] + up[:, 1:n + 1] * w[1] + up[:, 2:] * w[2] + bias


def hyena_filter_spectrum(n, w1, b1, fr1, w2, b2, fr2, w3):
    t = jnp.linspace(0.0, 1.0, n, dtype=F32)[:, None]
    bands = (HYENA_EMB_DIM - 1) // 2
    omega = 2.0 * math.pi * jnp.arange(n, dtype=F32)[:, None] / n
    freqs = jnp.linspace(1e-4, bands - 1, bands, dtype=F32)[None, :]
    z = jnp.concatenate([t, jnp.cos(freqs * omega), -jnp.sin(freqs * omega)], axis=-1)
    h = jnp.sin(fr1.astype(F32) * (z @ w1.astype(F32) + b1.astype(F32)))
    h = jnp.sin(fr2.astype(F32) * (h @ w2.astype(F32) + b2.astype(F32)))
    h = (h @ w3.astype(F32)).reshape(n, HYENA_ORDER, 2, HYENA_WIDTH)
    max_decay = math.log(HYENA_DECAY_TARGET) / HYENA_SHORT_DECAY_PCT
    min_decay = math.log(HYENA_DECAY_TARGET) / HYENA_LONG_DECAY_PCT
    deltas = jnp.abs(jnp.linspace(min_decay, max_decay, HYENA_WIDTH, dtype=F32))
    h = h * jnp.exp(-t * deltas)[:, None, None, :]
    taps = jnp.concatenate([h[:, :, 0],
                            jnp.zeros((1, HYENA_ORDER, HYENA_WIDTH), F32),
                            h[:0:-1, :, 1]], axis=0)
    taps = taps * lax.rsqrt(jnp.sum(taps * taps, axis=0, keepdims=True) + 1e-6)
    return jnp.fft.rfft(taps, axis=0)


def hyena_mix(zc, conv_w, conv_b, filt, bias):
    n = zc.shape[1]
    u = short_conv(zc, conv_w, conv_b).astype(F32)
    x1, x2, v = jnp.split(u, 3, axis=-1)
    spec = hyena_filter_spectrum(n, *filt)

    def long_conv(z, order):
        zf = jnp.fft.rfft(z, n=2 * n, axis=1)
        y = jnp.fft.irfft(zf * spec[None, :, order], n=2 * n, axis=1)[:, :n]
        return y + z * bias[order].astype(F32)

    y = x1 * long_conv(v, 0)
    y = x2 * long_conv(y, 1)
    return y.astype(zc.dtype)


def merge_branches(h, branches, w_gate_l, w_branch_l, w_out_l):
    merged = sum(jax.nn.sigmoid(mm3(h, w_gate_l[i])) * mm3(br, w_branch_l[i]) for i, br in enumerate(branches))
    return mm3(merged, w_out_l)


def moe_ffn(tokens, logits, w_pack):
    n_tok, d = tokens.shape
    i32 = jnp.int32
    top_v, top_i = lax.top_k(logits, TOP_K)
    weights = jax.nn.softmax(top_v, axis=-1)
    flat_e = top_i.reshape(-1).astype(i32)
    n_asg = n_tok * TOP_K
    e_ids = jnp.arange(N_EXPERTS, dtype=i32)
    onehot = (flat_e[:, None] == e_ids[None, :]).astype(i32)
    csum = jnp.cumsum(onehot, axis=0)
    rank = jnp.sum(onehot * csum, axis=1) - 1
    counts = csum[-1]
    padded = (counts + MOE_TB - 1) // MOE_TB * MOE_TB
    padded_end = jnp.cumsum(padded)
    padded_start = padded_end - padded
    dest = jnp.sum(onehot * padded_start[None, :], axis=1) + rank
    n_rows = n_asg + N_EXPERTS * MOE_TB
    n_blk = n_rows // MOE_TB
    blk_start = jnp.arange(n_blk, dtype=i32) * MOE_TB
    blk_e = jnp.minimum(jnp.sum((blk_start[:, None] >= padded_end[None, :]).astype(i32), axis=1), N_EXPERTS - 1)
    n_used = (padded_end[-1:] // MOE_TB).astype(i32)
    pad_need = (padded - counts)[:, None]
    pad_key = jnp.where(jnp.arange(MOE_TB, dtype=i32)[None, :] < pad_need, 2 * e_ids[:, None] + 1, 2 * N_EXPERTS)
    keys = jnp.concatenate([2 * flat_e, pad_key.reshape(-1)])
    vals = jnp.concatenate([jnp.arange(n_asg, dtype=i32) // TOP_K, jnp.zeros((N_EXPERTS * MOE_TB,), i32)])
    _, src = lax.sort_key_val(keys, vals, is_stable=True)
    buf = tokens[src]
    y = expert_ffn(buf, blk_e, n_used, *w_pack)
    dest = dest.reshape(n_tok, TOP_K)
    return [y[dest[:, k]] for k in range(TOP_K)], weights


def prep_expert_weights(w_gu_all, b_gu, w_dn, b_dn, layer):
    packed = tuple(split_gate_up(w_gu_all, layer)) + (
        b_gu[:, None, 0::2].astype(F32), b_gu[:, None, 1::2].astype(F32),
        w_dn.astype(BF16), b_dn[:, None, :].astype(F32))
    return packed


def _combine_body(x_ref, y0_ref, y1_ref, y2_ref, y3_ref, w_ref, g5_ref, lg_ref, lb_ref, o_ref):
    w = w_ref[...]
    y = sum(w[:, k:k + 1] * y_ref[...].astype(F32) for k, y_ref in enumerate((y0_ref, y1_ref, y2_ref, y3_ref)))
    o_ref[...] = _ln(DEEPNORM_ALPHA * x_ref[...] + g5_ref[...] * y) * lg_ref[...] + lb_ref[...]


def combine_postnorm(x1, ys, weights, gate5, ln_g, ln_b, tiles_per_sample, batch):
    rows, d = x1.shape
    gmap = _group_map(tiles_per_sample * (ROW_TILE // OUT_TILE), batch)
    row = lambda i: (i, 0)
    const = lambda i: (0, 0)
    return pl.pallas_call(
        _combine_body,
        grid=(rows // OUT_TILE,),
        in_specs=[pl.BlockSpec((OUT_TILE, d), row)] + [pl.BlockSpec((OUT_TILE, d), row)] * TOP_K
                 + [pl.BlockSpec((OUT_TILE, TOP_K), row), pl.BlockSpec((None, 1, d), gmap),
                    pl.BlockSpec((1, d), const), pl.BlockSpec((1, d), const)],
        out_specs=pl.BlockSpec((OUT_TILE, d), row),
        out_shape=jax.ShapeDtypeStruct((rows, d), F32),
        compiler_params=pltpu.CompilerParams(dimension_semantics=("parallel",), vmem_limit_bytes=VMEM_LIMIT),
        name="combine_postnorm",
    )(x1, *ys, weights, gate5, ln_g, ln_b)


def kernel(x, c, ctx, c_ctx, mod_w, mod_b, w_in, gqa_q_gain, gqa_k_gain, conv_w, conv_b,
           filt_w1, filt_b1, filt_freq1, filt_w2, filt_b2, filt_freq2, filt_w3, hyena_bias,
           swa_sinks, w_branch, w_gate, w_out, ln1_g, ln1_b, router_w, router_b,
           exp_w_gate_up, exp_b_gate_up, exp_w_down, exp_b_down, ln2_g, ln2_b):
    batch, n_lat, d = x.shape
    n_ctx = ctx.shape[1]
    rows = n_lat // GRID_W
    rope_gqa = rope_tables(rows, GQA_HEAD_DIM)
    rope_swa = rope_tables(rows, SWA_HEAD_DIM)
    n_l, n_c = batch * n_lat, batch * n_ctx
    assert n_lat % ROW_TILE == 0 and n_c % ROW_TILE == 0
    tps = n_lat // ROW_TILE
    xs = jnp.concatenate([x.reshape(n_l, d), ctx.reshape(n_c, d)], axis=0)
    cond = jnp.concatenate([c, c_ctx[None, :]], axis=0)
    offs = np.cumsum((0,) + SPLIT_SIZES)
    for l in range(DEPTH):
        ctx_continues = l < DEPTH - 1
        m_rows = n_l + n_c if ctx_continues else n_l
        mods = (jax.nn.silu(cond) @ mod_w[l] + mod_b[l]).reshape(batch + 1, 6, 1, d).swapaxes(0, 1)
        filt = (filt_w1[l], filt_b1[l], filt_freq1[l], filt_w2[l], filt_b2[l], filt_freq2[l], filt_w3[l])
        sinks = swa_sinks[l].reshape(SWA_KV_HEADS, SWA_HEADS // SWA_KV_HEADS)

        h = ln_modulate(xs, mods[0], mods[1], tps, batch)
        w_in_b = w_in[l].astype(BF16)
        seg_dtypes = (F32, F32, F32, BF16, F32, F32, F32, BF16)
        segs = [matmul_cols(h, w_in_b, int(offs[i]), SPLIT_SIZES[i], seg_dtypes[i]) for i in range(len(SPLIT_SIZES))]
        lat3 = lambda t: t[:n_l].reshape(batch, n_lat, -1)
        ctx3 = lambda t: t[n_l:].reshape(batch, n_ctx, -1)
        fa, gq, gk, gv, hy, sq, sk, sv = segs
        fa_l, fa_c, hy_c = lat3(fa), ctx3(fa), ctx3(hy)
        hy_parts = [p.reshape(batch, n_lat, -1) for p in short_conv_split(hy, 0, conv_w[l], conv_b[l], tps, batch)]
        gq = norm_rope(gq, *rope_gqa, gqa_q_gain[l], GQA_HEAD_DIM // 4, tps, batch)
        gk = norm_rope(gk, *rope_gqa, gqa_k_gain[l], GQA_HEAD_DIM // 4, tps, batch)
        sq = norm_rope(sq, *rope_swa, None, SWA_HEAD_DIM // 4, tps, batch)
        sk = norm_rope(sk, *rope_swa, None, SWA_HEAD_DIM // 4, tps, batch)
        ga = global_attention(gq[:n_l], gk[:n_l], gv[:n_l], gk[n_l:], gv[n_l:], batch)
        wa = window_attention(sq[:n_l], sk[:n_l], sv[:n_l], sk[n_l:], sv[n_l:], swa_sinks[l], batch)
        branches = [
            fourier_mix_pallas(fa_l).reshape(n_l, -1),
            ga,
            hyena_mix_pallas(*hy_parts, filt, hyena_bias[l]).reshape(n_l, -1),
            wa,
        ]
        if ctx_continues:
            gq_ctx = group_heads(heads(ctx3(gq), GQA_HEADS), GQA_KV_HEADS)
            sq_ctx = group_heads(heads(ctx3(sq), SWA_HEADS), SWA_KV_HEADS)
            ctx_branches = (
                fourier_mix_pallas(fa_c),
                flat_heads(softmax_attend(gq_ctx, heads(ctx3(gk), GQA_KV_HEADS), heads(ctx3(gv), GQA_KV_HEADS))),
                hyena_mix_pallas(*jnp.split(short_conv(hy_c, conv_w[l], conv_b[l]), 3, axis=-1), filt, hyena_bias[l]),
                flat_heads(softmax_attend(sq_ctx, heads(ctx3(sk), SWA_KV_HEADS), heads(ctx3(sv), SWA_KV_HEADS), sinks)),
            )
            branches = [jnp.concatenate([bl, bc.reshape(n_c, -1).astype(bl.dtype)], axis=0)
                        for bl, bc in zip(branches, ctx_branches)]
        merged = merge_gated(h, branches, w_gate[l].astype(BF16), w_branch[l].astype(BF16), m_rows)
        router_wp = jnp.zeros((d, ROUTER_PAD), F32).at[:, :N_EXPERTS].set(router_w[l])
        router_hi = router_wp.astype(BF16)
        router_hl = jnp.concatenate([router_hi, (router_wp - router_hi.astype(F32)).astype(BF16)], axis=1)
        router_bp = jnp.zeros((1, ROUTER_PAD), F32).at[0, :N_EXPERTS].set(router_b[l])
        x1, h_moe, logits = outproj_postnorm(merged, w_out[l].astype(BF16), xs, mods[2], mods[3], mods[4],
                                             ln1_g[l][None, :], ln1_b[l][None, :], router_hl, router_bp, tps, batch)

        w_pack = prep_expert_weights(exp_w_gate_up, exp_b_gate_up[l], exp_w_down[l], exp_b_down[l], l)
        ys, weights = moe_ffn(h_moe, logits[:, :N_EXPERTS], w_pack)
        xs = combine_postnorm(x1, ys, weights, mods[5], ln2_g[l][None, :], ln2_b[l][None, :], tps, batch)
    return xs.reshape(batch, n_lat, d)
```

```python
import functools
import math

import jax
import jax.numpy as jnp
import numpy as np
from jax import lax
from jax.experimental import pallas as pl
from jax.experimental.pallas import tpu as pltpu

D_MODEL = 2048
DEPTH = 2
GRID_W = 64
FNET_GROUPS = 4
FNET_GROUP_DIM = 128
GQA_HEADS = 4
GQA_KV_HEADS = 2
GQA_HEAD_DIM = 128
HYENA_WIDTH = 512
HYENA_ORDER = 2
HYENA_EMB_DIM = 33
HYENA_DECAY_TARGET = 1e-2
HYENA_SHORT_DECAY_PCT = 0.3
HYENA_LONG_DECAY_PCT = 1.5
SWA_HEADS = 8
SWA_KV_HEADS = 2
SWA_HEAD_DIM = 64
WINDOW = 128
Q_BLOCK = 128
ROPE_THETA = 10000.0
N_EXPERTS = 32
TOP_K = 4
D_EXPERT = 1536
SWIGLU_LIMIT = 7.0
SWIGLU_ALPHA = 1.702
MOE_BLOCK = 128
LN_EPS = 1e-5
RMS_EPS = 1e-6
DEEPNORM_ALPHA = (2 * DEPTH) ** 0.25
SPLIT_SIZES = (FNET_GROUPS * FNET_GROUP_DIM,
               GQA_HEADS * GQA_HEAD_DIM, GQA_KV_HEADS * GQA_HEAD_DIM, GQA_KV_HEADS * GQA_HEAD_DIM,
               3 * HYENA_WIDTH,
               SWA_HEADS * SWA_HEAD_DIM, SWA_KV_HEADS * SWA_HEAD_DIM, SWA_KV_HEADS * SWA_HEAD_DIM)
F32 = jnp.float32
BF16 = jnp.bfloat16


def _mm_body(a_ref, b_ref, o_ref):
    o_ref[...] = jnp.dot(a_ref[...], b_ref[...], preferred_element_type=F32).astype(o_ref.dtype)


def _pick(n, cands):
    for c in cands:
        if n % c == 0:
            return c
    return n


def matmul(a, b, out_dtype=F32):
    m, k = a.shape
    n = b.shape[1]
    tm = _pick(m, (512, 256, 128))
    tn = _pick(n, (2048, 1280, 1024, 768, 512, 256, 128))
    return pl.pallas_call(
        _mm_body,
        grid=(n // tn, m // tm),
        in_specs=[pl.BlockSpec((tm, k), lambda j, i: (i, 0)),
                  pl.BlockSpec((k, tn), lambda j, i: (0, j))],
        out_specs=pl.BlockSpec((tm, tn), lambda j, i: (i, j)),
        out_shape=jax.ShapeDtypeStruct((m, n), out_dtype),
        compiler_params=pltpu.CompilerParams(dimension_semantics=("parallel", "parallel"),
                                             vmem_limit_bytes=VMEM_LIMIT),
        name="mm",
    )(a.astype(BF16), b.astype(BF16))


def mm3(x, w):
    b, n, k = x.shape
    return matmul(x.reshape(b * n, k), w).reshape(b, n, w.shape[1])


ROW_TILE = 512


def _group_map(tiles_per_sample, batch):
    return lambda i, *_: (jnp.minimum(i // tiles_per_sample, batch), 0, 0)


def _ln(x):
    xc = x - jnp.mean(x, axis=-1, keepdims=True)
    return xc * lax.rsqrt(jnp.mean(xc * xc, axis=-1, keepdims=True) + LN_EPS)


def _lnmod_body(x_ref, sh_ref, sc_ref, o_ref):
    o_ref[...] = (_ln(x_ref[...]) * (1.0 + sc_ref[...]) + sh_ref[...]).astype(o_ref.dtype)


def ln_modulate(x, shift, scale, tiles_per_sample, batch):
    t, d = x.shape
    gmap = _group_map(tiles_per_sample, batch)
    return pl.pallas_call(
        _lnmod_body,
        grid=(t // ROW_TILE,),
        in_specs=[pl.BlockSpec((ROW_TILE, d), lambda i: (i, 0)),
                  pl.BlockSpec((None, 1, d), gmap), pl.BlockSpec((None, 1, d), gmap)],
        out_specs=pl.BlockSpec((ROW_TILE, d), lambda i: (i, 0)),
        out_shape=jax.ShapeDtypeStruct((t, d), BF16),
        compiler_params=pltpu.CompilerParams(dimension_semantics=("parallel",)),
        name="ln_modulate",
    )(x, shift, scale)


def matmul_cols(a, w, col0, ncols, out_dtype):
    t, k = a.shape
    tn = _pick(ncols, (512, 256, 128))
    assert col0 % tn == 0 and t % ROW_TILE == 0
    c0 = col0 // tn
    return pl.pallas_call(
        _mm_body,
        grid=(ncols // tn, t // ROW_TILE),
        in_specs=[pl.BlockSpec((ROW_TILE, k), lambda j, i: (i, 0)),
                  pl.BlockSpec((k, tn), lambda j, i: (0, c0 + j))],
        out_specs=pl.BlockSpec((ROW_TILE, tn), lambda j, i: (i, j)),
        out_shape=jax.ShapeDtypeStruct((t, ncols), out_dtype),
        compiler_params=pltpu.CompilerParams(dimension_semantics=("parallel", "parallel"),
                                             vmem_limit_bytes=VMEM_LIMIT),
        name="in_proj",
    )(a, w)


def _norm_rope(x, c, s, gain, quarter):
    width = x.shape[1]
    reps = width // LANES
    if gain is not None:
        x = jnp.concatenate(
            [xh * lax.rsqrt(jnp.mean(xh * xh, axis=-1, keepdims=True) + RMS_EPS) * gain
             for xh in (x[:, r * LANES:(r + 1) * LANES] for r in range(reps))], axis=1)
    lane = lax.broadcasted_iota(jnp.int32, x.shape, 1)
    first = (lane % (2 * quarter)) < quarter
    partner = jnp.where(first, pltpu.roll(x, width - quarter, 1), pltpu.roll(x, quarter, 1))
    return x * jnp.concatenate([c] * reps, axis=1) + partner * jnp.concatenate([s] * reps, axis=1)


IN_TILE = 256


def _inproj_body(x_ref, sh_ref, sc_ref, w_ref, cg_ref, sg_ref, cs_ref, ss_ref, gq_ref, gk_ref,
                 h_ref, fa_ref, q_ref, k_ref, v_ref, hy_ref, sq_ref, sk_ref, sv_ref):
    h = (_ln(x_ref[...]) * (1.0 + sc_ref[...]) + sh_ref[...]).astype(BF16)
    h_ref[...] = h
    offs = np.cumsum((0,) + SPLIT_SIZES)
    seg = lambda i: jnp.dot(h, w_ref[:, int(offs[i]):int(offs[i + 1])], preferred_element_type=F32)
    gq4, sq4 = GQA_HEAD_DIM // 4, SWA_HEAD_DIM // 4
    fa_ref[...] = seg(0)
    q_ref[...] = _norm_rope(seg(1), cg_ref[...], sg_ref[...], gq_ref[...], gq4).astype(BF16)
    k_ref[...] = _norm_rope(seg(2), cg_ref[...], sg_ref[...], gk_ref[...], gq4).astype(BF16)
    v_ref[...] = seg(3).astype(BF16)
    hy_ref[...] = seg(4)
    sq_ref[...] = _norm_rope(seg(5), cs_ref[...], ss_ref[...], None, sq4).astype(BF16)
    sk_ref[...] = _norm_rope(seg(6), cs_ref[...], ss_ref[...], None, sq4).astype(BF16)
    sv_ref[...] = seg(7).astype(BF16)


def in_projection(x, shift, scale, w_in, rope_g, rope_s, gain_q, gain_k, tiles_per_sample, batch):
    t, d = x.shape
    per = ROW_TILE // IN_TILE
    gmap = _group_map(tiles_per_sample * per, batch)
    lat_tiles = tiles_per_sample * per
    tmap = lambda i: (jnp.where(i < lat_tiles * batch, i % lat_tiles, lat_tiles), 0)
    row = lambda i: (i, 0)
    const = lambda i: (0, 0)
    tab = pl.BlockSpec((IN_TILE, LANES), tmap)
    vec = pl.BlockSpec((1, LANES), const)
    widths = (d,) + SPLIT_SIZES
    dtypes = (BF16, F32, BF16, BF16, BF16, F32, BF16, BF16, BF16)
    return pl.pallas_call(
        _inproj_body,
        grid=(t // IN_TILE,),
        in_specs=[pl.BlockSpec((IN_TILE, d), row), pl.BlockSpec((None, 1, d), gmap), pl.BlockSpec((None, 1, d), gmap),
                  pl.BlockSpec(w_in.shape, const, pipeline_mode=pl.Buffered(1)), tab, tab, tab, tab, vec, vec],
        out_specs=[pl.BlockSpec((IN_TILE, w), row) for w in widths],
        out_shape=[jax.ShapeDtypeStruct((t, w), dt) for w, dt in zip(widths, dtypes)],
        compiler_params=pltpu.CompilerParams(dimension_semantics=("parallel",), vmem_limit_bytes=VMEM_LIMIT),
        name="in_projection",
    )(x, shift, scale, w_in, *rope_g, *rope_s, gain_q.astype(F32)[None, :], gain_k.astype(F32)[None, :])


def rope_tables(rows, head_dim):
    cos, sin = axial_rope(rows, head_dim)
    c = jnp.concatenate([cos[:, 0], cos[:, 0], cos[:, 1], cos[:, 1]], axis=-1)
    s = jnp.concatenate([-sin[:, 0], sin[:, 0], -sin[:, 1], sin[:, 1]], axis=-1)
    reps = LANES // head_dim
    c, s = jnp.tile(c, (1, reps)), jnp.tile(s, (1, reps))
    return (jnp.concatenate([c, jnp.ones((ROW_TILE, LANES), F32)], axis=0),
            jnp.concatenate([s, jnp.zeros((ROW_TILE, LANES), F32)], axis=0))


MERGE_TILE = 256


def _merge_body(h_ref, b0_ref, b1_ref, b2_ref, b3_ref, wg_ref, wb_ref, o_ref):
    h = h_ref[...]
    acc = None
    for j, b_ref in enumerate((b0_ref, b1_ref, b2_ref, b3_ref)):
        gate = jax.nn.sigmoid(jnp.dot(h, wg_ref[j], preferred_element_type=F32))
        term = gate * jnp.dot(b_ref[...].astype(BF16), wb_ref[j], preferred_element_type=F32)
        acc = term if acc is None else acc + term
    o_ref[...] = acc.astype(o_ref.dtype)


def merge_gated(h, branches, w_gate, w_branch, rows):
    d = h.shape[1]
    nb, bw, _ = w_branch.shape
    assert nb == len(branches) == 4
    row = lambda i: (i, 0)
    whole = lambda i: (0, 0, 0)
    return pl.pallas_call(
        _merge_body,
        grid=(rows // MERGE_TILE,),
        in_specs=[pl.BlockSpec((MERGE_TILE, d), row)] + [pl.BlockSpec((MERGE_TILE, bw), row)] * nb
                 + [pl.BlockSpec((nb, d, d), whole, pipeline_mode=pl.Buffered(1)),
                    pl.BlockSpec((nb, bw, d), whole, pipeline_mode=pl.Buffered(1))],
        out_specs=pl.BlockSpec((MERGE_TILE, d), row),
        out_shape=jax.ShapeDtypeStruct((rows, d), BF16),
        compiler_params=pltpu.CompilerParams(dimension_semantics=("parallel",), vmem_limit_bytes=VMEM_LIMIT),
        name="merge_gated",
    )(h, *branches, w_gate, w_branch)


OUT_TILE = 256
ROUTER_PAD = 128


def _outproj_body(m_ref, w_ref, x_ref, g2_ref, sh_ref, sc_ref, lg_ref, lb_ref, rw_ref, rb_ref,
                  x1_ref, hm_ref, lo_ref):
    mix = jnp.dot(m_ref[...], w_ref[...], preferred_element_type=F32)
    x1 = _ln(DEEPNORM_ALPHA * x_ref[...] + g2_ref[...] * mix) * lg_ref[...] + lb_ref[...]
    x1_ref[...] = x1
    hm = _ln(x1) * (1.0 + sc_ref[...]) + sh_ref[...]
    hm_hi = hm.astype(BF16)
    hm_ref[...] = hm_hi
    hm_lo = (hm - hm_hi.astype(F32)).astype(BF16)
    hh = jnp.dot(hm_hi, rw_ref[...], preferred_element_type=F32)
    lh = jnp.dot(hm_lo, rw_ref[:, :ROUTER_PAD], preferred_element_type=F32)
    lo_ref[...] = hh[:, :ROUTER_PAD] + hh[:, ROUTER_PAD:] + lh + rb_ref[...]


def outproj_postnorm(merged, w_out, x, gate2, shift3, scale4, ln_g, ln_b, router_w, router_b,
                     tiles_per_sample, batch):
    rows, d = merged.shape
    gmap = _group_map(tiles_per_sample * (ROW_TILE // OUT_TILE), batch)
    row = lambda i: (i, 0)
    const = lambda i: (0, 0)
    return pl.pallas_call(
        _outproj_body,
        grid=(rows // OUT_TILE,),
        in_specs=[pl.BlockSpec((OUT_TILE, d), row), pl.BlockSpec((d, d), const), pl.BlockSpec((OUT_TILE, d), row),
                  pl.BlockSpec((None, 1, d), gmap), pl.BlockSpec((None, 1, d), gmap), pl.BlockSpec((None, 1, d), gmap),
                  pl.BlockSpec((1, d), const), pl.BlockSpec((1, d), const),
                  pl.BlockSpec((d, 2 * ROUTER_PAD), const), pl.BlockSpec((1, ROUTER_PAD), const)],
        out_specs=[pl.BlockSpec((OUT_TILE, d), row), pl.BlockSpec((OUT_TILE, d), row),
                   pl.BlockSpec((OUT_TILE, ROUTER_PAD), row)],
        out_shape=[jax.ShapeDtypeStruct((rows, d), F32), jax.ShapeDtypeStruct((rows, d), BF16),
                   jax.ShapeDtypeStruct((rows, ROUTER_PAD), F32)],
        compiler_params=pltpu.CompilerParams(dimension_semantics=("parallel",), vmem_limit_bytes=VMEM_LIMIT),
        name="outproj_postnorm",
    )(merged, w_out, x, gate2, shift3, scale4, ln_g, ln_b, router_w, router_b)


GATTN_TQ = 512
GATTN_TK = 256
GATTN_UNROLL = 4
VMEM_LIMIT = 56 * 1024 * 1024
LANES = 128
LOG2E = math.log2(math.e)


def _gattn_body(q_ref, kc_ref, vc_ref, k_ref, v_ref, o_ref, m_ref, acc_ref, *, c2):
    tq = q_ref.shape[0]
    hd = k_ref.shape[1]
    q2 = jnp.concatenate([q_ref[:, :hd], q_ref[:, hd:]], axis=0)
    m_ref[...] = jnp.full(m_ref.shape, -jnp.inf, F32)
    acc_ref[...] = jnp.zeros(acc_ref.shape, F32)

    def step(k, v):
        s = lax.dot_general(q2, k, (((1,), (1,)), ((), ())), preferred_element_type=F32) * c2
        m_old = m_ref[...]
        m_new = jnp.maximum(m_old, jnp.max(s, axis=-1, keepdims=True))
        alpha = jnp.exp2(m_old - m_new)
        p = jnp.concatenate([jnp.exp2(s[:, j * LANES:(j + 1) * LANES] - m_new)
                             for j in range(k.shape[0] // LANES)], axis=1)
        acc_ref[...] = (jnp.concatenate([alpha, alpha], axis=1) * acc_ref[...]
                        + jnp.dot(p.astype(BF16), v, preferred_element_type=F32))
        m_ref[...] = m_new

    step(kc_ref[...], vc_ref[...])

    def loop(c, carry):
        off = pl.multiple_of(c * GATTN_TK, GATTN_TK)
        step(k_ref[pl.ds(off, GATTN_TK), :], v_ref[pl.ds(off, GATTN_TK), :])
        return carry

    lax.fori_loop(0, k_ref.shape[0] // GATTN_TK, loop, 0, unroll=GATTN_UNROLL)
    acc = acc_ref[...]
    out = acc[:, :hd] / acc[:, hd:]
    o_ref[...] = jnp.concatenate([out[:tq], out[tq:]], axis=1).astype(o_ref.dtype)


def _with_ones(v, n_heads):
    r = v.shape[0]
    v3 = v.reshape(r, n_heads, -1)
    return jnp.concatenate([v3, jnp.ones_like(v3)], axis=-1).reshape(r, -1)


def global_attention(q, k, v, k_ctx, v_ctx, batch):
    hd = GQA_HEAD_DIM
    n = q.shape[0] // batch
    n_c = k_ctx.shape[0] // batch
    g = GQA_HEADS // GQA_KV_HEADS
    tq = min(GATTN_TQ, n)
    assert g == 2 and hd == LANES and n % tq == 0 and n % (GATTN_TK * GATTN_UNROLL) == 0 and n_c % LANES == 0
    nq = n // tq
    return pl.pallas_call(
        functools.partial(_gattn_body, c2=hd ** -0.5 * LOG2E),
        grid=(batch, GQA_KV_HEADS, nq),
        in_specs=[pl.BlockSpec((tq, g * hd), lambda b, h, i: (b * nq + i, h)),
                  pl.BlockSpec((n_c, hd), lambda b, h, i: (b, h)),
                  pl.BlockSpec((n_c, 2 * hd), lambda b, h, i: (b, h)),
                  pl.BlockSpec((n, hd), lambda b, h, i: (b, h)),
                  pl.BlockSpec((n, 2 * hd), lambda b, h, i: (b, h))],
        out_specs=pl.BlockSpec((tq, g * hd), lambda b, h, i: (b * nq + i, h)),
        out_shape=jax.ShapeDtypeStruct(q.shape, BF16),
        scratch_shapes=[pltpu.VMEM((g * tq, hd), F32), pltpu.VMEM((g * tq, 2 * hd), F32)],
        compiler_params=pltpu.CompilerParams(dimension_semantics=("parallel", "parallel", "parallel"),
                                             vmem_limit_bytes=VMEM_LIMIT),
        name="gattn",
    )(q, k_ctx, _with_ones(v_ctx, GQA_KV_HEADS), k, _with_ones(v, GQA_KV_HEADS))


WATTN_TQ = 256


def _wattn_body(sink_ref, q_ref, kc_ref, vc_ref, kp_ref, kx_ref, kn_ref, vp_ref, vx_ref, vn_ref, o_ref,
                *, scale, n, nq):
    tq = q_ref.shape[0]
    hd = SWA_HEAD_DIM
    g = SWA_HEADS // SWA_KV_HEADS
    h = pl.program_id(1)
    i = pl.program_id(2)
    k_band = jnp.concatenate([kp_ref[...], kx_ref[...], kn_ref[...]], axis=0)
    v_band = jnp.concatenate([vp_ref[...], vx_ref[...], vn_ref[...]], axis=0)
    qpos = i * tq + lax.broadcasted_iota(jnp.int32, (tq, tq + 2 * WINDOW), 0)
    kpos = i * tq - WINDOW + lax.broadcasted_iota(jnp.int32, (tq, tq + 2 * WINDOW), 1)
    valid = (jnp.abs(kpos - qpos) <= WINDOW) & (kpos >= 0) & (kpos < n)
    outs = []
    for j in range(g):
        q = q_ref[:, j * hd:(j + 1) * hd]
        s_c = lax.dot_general(q, kc_ref[...], (((1,), (1,)), ((), ())), preferred_element_type=F32) * scale
        s_b = lax.dot_general(q, k_band, (((1,), (1,)), ((), ())), preferred_element_type=F32) * scale
        s_b = jnp.where(valid, s_b, -jnp.inf)
        sink = sink_ref[h * g + j]
        m = jnp.maximum(jnp.maximum(jnp.max(s_c, -1, keepdims=True), jnp.max(s_b, -1, keepdims=True)), sink)
        p_c = jnp.exp(s_c - m)
        p_b = jnp.exp(s_b - m)
        l = jnp.sum(p_c, -1, keepdims=True) + jnp.sum(p_b, -1, keepdims=True) + jnp.exp(sink - m)
        o = (jnp.dot(p_c.astype(BF16), vc_ref[...], preferred_element_type=F32)
             + jnp.dot(p_b.astype(BF16), v_band, preferred_element_type=F32))
        outs.append(o / l)
    o_ref[...] = jnp.concatenate(outs, axis=1).astype(o_ref.dtype)


def window_attention(q, k, v, k_ctx, v_ctx, sinks, batch):
    hd = SWA_HEAD_DIM
    n = q.shape[0] // batch
    n_c = k_ctx.shape[0] // batch
    g = SWA_HEADS // SWA_KV_HEADS
    tq = WATTN_TQ
    assert n % tq == 0 and tq % WINDOW == 0
    nq = n // tq
    r = tq // WINDOW
    nw = n // WINDOW
    k_h = k.reshape(batch * n, SWA_KV_HEADS, hd).transpose(1, 0, 2)
    v_h = v.reshape(batch * n, SWA_KV_HEADS, hd).transpose(1, 0, 2)
    kc_h = k_ctx.reshape(batch * n_c, SWA_KV_HEADS, hd).transpose(1, 0, 2)
    vc_h = v_ctx.reshape(batch * n_c, SWA_KV_HEADS, hd).transpose(1, 0, 2)
    prev_map = lambda b, h, i, s: (h, b * nw + jnp.maximum(i * r - 1, 0), 0)
    cur_map = lambda b, h, i, s: (h, b * nq + i, 0)
    next_map = lambda b, h, i, s: (h, b * nw + jnp.minimum((i + 1) * r, nw - 1), 0)
    ctx_map = lambda b, h, i, s: (h, b, 0)
    grid_spec = pltpu.PrefetchScalarGridSpec(
        num_scalar_prefetch=1,
        grid=(batch, SWA_KV_HEADS, nq),
        in_specs=[pl.BlockSpec((tq, g * hd), lambda b, h, i, s: (b * nq + i, h)),
                  pl.BlockSpec((None, n_c, hd), ctx_map),
                  pl.BlockSpec((None, n_c, hd), ctx_map),
                  pl.BlockSpec((None, WINDOW, hd), prev_map),
                  pl.BlockSpec((None, tq, hd), cur_map),
                  pl.BlockSpec((None, WINDOW, hd), next_map),
                  pl.BlockSpec((None, WINDOW, hd), prev_map),
                  pl.BlockSpec((None, tq, hd), cur_map),
                  pl.BlockSpec((None, WINDOW, hd), next_map)],
        out_specs=pl.BlockSpec((tq, g * hd), lambda b, h, i, s: (b * nq + i, h)),
    )
    return pl.pallas_call(
        functools.partial(_wattn_body, scale=hd ** -0.5, n=n, nq=nq),
        grid_spec=grid_spec,
        out_shape=jax.ShapeDtypeStruct(q.shape, BF16),
        compiler_params=pltpu.CompilerParams(dimension_semantics=("parallel", "parallel", "parallel"),
                                             vmem_limit_bytes=VMEM_LIMIT),
        name="wattn",
    )(sinks.astype(F32), q, kc_h, vc_h, k_h, k_h, k_h, v_h, v_h, v_h)


MOE_TB = 512
MOE_FC = 512


def _ffn_body(blk_e_ref, nused_ref, x_ref, wg_ref, wu_ref, bg_ref, bu_ref, wd_ref, bd_ref, o_ref):
    i = pl.program_id(0)

    @pl.when(i < nused_ref[0])
    def _():
        x = x_ref[...]
        acc = jnp.zeros(o_ref.shape, F32) + bd_ref[...]
        for c in range(D_EXPERT // MOE_FC):
            sl = slice(c * MOE_FC, (c + 1) * MOE_FC)
            gate = jnp.dot(x, wg_ref[:, sl], preferred_element_type=F32) + bg_ref[:, sl]
            up = jnp.dot(x, wu_ref[:, sl], preferred_element_type=F32) + bu_ref[:, sl]
            gate = jnp.minimum(gate, SWIGLU_LIMIT)
            up = jnp.clip(up, -SWIGLU_LIMIT, SWIGLU_LIMIT)
            glu = gate * jax.nn.sigmoid(gate * SWIGLU_ALPHA)
            act = ((up + 1.0) * glu).astype(BF16)
            acc = acc + jnp.dot(act, wd_ref[sl, :], preferred_element_type=F32)
        o_ref[...] = acc.astype(o_ref.dtype)

    @pl.when(i >= nused_ref[0])
    def _():
        o_ref[...] = jnp.zeros(o_ref.shape, o_ref.dtype)


def expert_ffn(buf, blk_e, n_used, w_gate, w_up, b_gate, b_up, w_dn, b_dn):
    n_rows, d = buf.shape
    n_blk = n_rows // MOE_TB
    f = w_gate.shape[2]
    wmap = lambda i, e, u: (e[i], 0, 0)
    grid_spec = pltpu.PrefetchScalarGridSpec(
        num_scalar_prefetch=2,
        grid=(n_blk,),
        in_specs=[pl.BlockSpec((MOE_TB, d), lambda i, e, u: (i, 0)),
                  pl.BlockSpec((None, d, f), wmap),
                  pl.BlockSpec((None, d, f), wmap),
                  pl.BlockSpec((None, 1, f), wmap),
                  pl.BlockSpec((None, 1, f), wmap),
                  pl.BlockSpec((None, f, d), wmap),
                  pl.BlockSpec((None, 1, d), wmap)],
        out_specs=pl.BlockSpec((MOE_TB, d), lambda i, e, u: (i, 0)),
    )
    return pl.pallas_call(
        _ffn_body,
        grid_spec=grid_spec,
        out_shape=jax.ShapeDtypeStruct((n_rows, d), BF16),
        compiler_params=pltpu.CompilerParams(dimension_semantics=("arbitrary",),
                                             vmem_limit_bytes=VMEM_LIMIT),
        name="expert_ffn",
    )(blk_e, n_used, buf, w_gate, w_up, b_gate, b_up, w_dn, b_dn)


SPLIT_ROWS = 512


def _deint_body(w_ref, p_ref, g_ref, u_ref):
    blk = p_ref.shape[0]
    for c in range(w_ref.shape[1] // blk):
        y = jnp.dot(w_ref[:, c * blk:(c + 1) * blk].astype(BF16), p_ref[...], preferred_element_type=F32)
        g_ref[:, c * LANES:(c + 1) * LANES] = y[:, :LANES].astype(BF16)
        u_ref[:, c * LANES:(c + 1) * LANES] = y[:, LANES:].astype(BF16)


def split_gate_up(w_gu_all, layer):
    _, e, d, f2 = w_gu_all.shape
    blk = 2 * LANES
    src = jnp.arange(blk)[:, None]
    dst = jnp.arange(blk)[None, :]
    perm = (src == jnp.where(dst < LANES, 2 * dst, 2 * (dst - LANES) + 1)).astype(BF16)
    out = jax.ShapeDtypeStruct((e, d, f2 // 2), BF16)
    return pl.pallas_call(
        _deint_body,
        grid=(e, d // SPLIT_ROWS),
        in_specs=[pl.BlockSpec((None, None, SPLIT_ROWS, f2), lambda i, j: (layer, i, j, 0)),
                  pl.BlockSpec((blk, blk), lambda i, j: (0, 0))],
        out_specs=[pl.BlockSpec((None, SPLIT_ROWS, f2 // 2), lambda i, j: (i, j, 0)),
                   pl.BlockSpec((None, SPLIT_ROWS, f2 // 2), lambda i, j: (i, j, 0))],
        out_shape=[out, out],
        compiler_params=pltpu.CompilerParams(dimension_semantics=("parallel", "parallel"),
                                             vmem_limit_bytes=VMEM_LIMIT),
        name="split_gate_up",
    )(w_gu_all, perm)


FFT_N2 = 128
FFT_SPLIT_MIN = 1024


def _angles(num, den):
    return (2.0 * math.pi / den) * (num % den).astype(F32)


def dft_stage_a(n1, k_in, real_input):
    ang = _angles(jnp.arange(n1)[:, None] * jnp.arange(k_in)[None, :], n1)
    c, s = jnp.cos(ang), jnp.sin(ang)
    if real_input:
        return jnp.concatenate([c, -s], axis=0).astype(BF16)
    return jnp.block([[c, s], [-s, c]]).astype(BF16)


def dft_stage_a_inv(n1, k_out):
    ang = _angles(jnp.arange(k_out)[:, None] * jnp.arange(n1)[None, :], n1)
    c, s = jnp.cos(ang), jnp.sin(ang)
    return jnp.block([[c, -s], [s, c]]).astype(BF16)


def dft_stage_b(n1, n2):
    k1 = jnp.arange(n1)[:, None, None]
    k2 = jnp.arange(n2)[None, :, None]
    m = jnp.arange(n2)[None, None, :]
    ang = _angles(m * (k1 + n1 * k2), n1 * n2)
    c, s = jnp.cos(ang), jnp.sin(ang)
    top = jnp.concatenate([c, s], axis=2)
    bot = jnp.concatenate([-s, c], axis=2)
    return jnp.concatenate([top, bot], axis=1).astype(BF16)


SUBLANES = 8


def _lmm_body(m_ref, x_ref, o_ref):
    m = m_ref[...]
    k, sub, lanes = x_ref.shape
    r = o_ref.shape[0]
    x2 = x_ref.reshape(k * sub, lanes)
    o2 = o_ref.reshape(r * sub, lanes)
    for s in range(sub):
        xs = x2[pl.ds(s, k, stride=sub), :].astype(BF16)
        o2[pl.ds(s, r, stride=sub), :] = jnp.dot(m, xs, preferred_element_type=F32)


def left_matmul(m, x, n2):
    g, rows, c = x.shape
    k = rows // n2
    r = m.shape[0]
    assert m.shape[1] == k and n2 % SUBLANES == 0 and c % LANES == 0
    x5 = x.reshape(g, k, n2 // SUBLANES, SUBLANES, c)
    out = pl.pallas_call(
        _lmm_body,
        grid=(g, n2 // SUBLANES, c // LANES),
        in_specs=[pl.BlockSpec((r, k), lambda i, j, l: (0, 0)),
                  pl.BlockSpec((None, k, None, SUBLANES, LANES), lambda i, j, l: (i, 0, j, 0, l))],
        out_specs=pl.BlockSpec((None, r, None, SUBLANES, LANES), lambda i, j, l: (i, 0, j, 0, l)),
        out_shape=jax.ShapeDtypeStruct((g, r, n2 // SUBLANES, SUBLANES, c), F32),
        compiler_params=pltpu.CompilerParams(dimension_semantics=("parallel", "parallel", "parallel"),
                                             vmem_limit_bytes=VMEM_LIMIT),
        name="left_matmul",
    )(m, x5)
    return out.reshape(g, r * n2, c)


def _lmm_gate_body(m_ref, x_ref, a_ref, z_ref, b_ref, o_ref, *, scale):
    m = m_ref[...]
    k, sub, lanes = x_ref.shape
    r = o_ref.shape[0]
    x2 = x_ref.reshape(k * sub, lanes)
    a2 = a_ref.reshape(r * sub, lanes)
    z2 = z_ref.reshape(r * sub, lanes)
    o2 = o_ref.reshape(r * sub, lanes)
    for s in range(sub):
        rows = pl.ds(s, r, stride=sub)
        y = jnp.dot(m, x2[pl.ds(s, k, stride=sub), :].astype(BF16), preferred_element_type=F32)
        o2[rows, :] = a2[rows, :] * (y * scale + z2[rows, :] * b_ref[...])


def left_matmul_gated(m, x, n2, gate, z, bias, scale):
    g, rows, c = x.shape
    k = rows // n2
    r = m.shape[0]
    assert m.shape[1] == k and gate.shape == (g, r * n2, c) and z.shape == gate.shape
    view = lambda t, lead: t.reshape(g, lead, n2 // SUBLANES, SUBLANES, c)
    blk = lambda lead: pl.BlockSpec((None, lead, None, SUBLANES, LANES), lambda i, j, l: (i, 0, j, 0, l))
    out = pl.pallas_call(
        functools.partial(_lmm_gate_body, scale=scale),
        grid=(g, n2 // SUBLANES, c // LANES),
        in_specs=[pl.BlockSpec((r, k), lambda i, j, l: (0, 0)), blk(k), blk(r), blk(r),
                  pl.BlockSpec((1, LANES), lambda i, j, l: (0, l))],
        out_specs=blk(r),
        out_shape=jax.ShapeDtypeStruct((g, r, n2 // SUBLANES, SUBLANES, c), F32),
        compiler_params=pltpu.CompilerParams(dimension_semantics=("parallel", "parallel", "parallel"),
                                             vmem_limit_bytes=VMEM_LIMIT),
        name="left_matmul_gated",
    )(m, view(x, k), view(gate, r), view(z, r), bias.astype(F32)[None, :])
    return out.reshape(g, r * n2, c)


def _sconv_body(p_ref, x_ref, n_ref, w_ref, b_ref, o1_ref, o2_ref, o3_ref, *, tiles_per_sample):
    i = pl.program_id(0)
    pos = i % tiles_per_sample
    x = x_ref[...]
    rows = x.shape[0]
    before = jnp.where(pos == 0, 0.0, p_ref[SUBLANES - 1:SUBLANES, :])
    after = jnp.where(pos == tiles_per_sample - 1, 0.0, n_ref[0:1, :])
    ridx = lax.broadcasted_iota(jnp.int32, x.shape, 0)
    up = jnp.where(ridx == 0, before, pltpu.roll(x, 1, 0))
    dn = jnp.where(ridx == rows - 1, after, pltpu.roll(x, rows - 1, 0))
    y = up * w_ref[0:1, :] + x * w_ref[1:2, :] + dn * w_ref[2:3, :] + b_ref[...]
    w = o1_ref.shape[1]
    o1_ref[...] = y[:, :w]
    o2_ref[...] = y[:, w:2 * w]
    o3_ref[...] = y[:, 2 * w:]


def short_conv_split(z, col_blocks, conv_w, conv_b, tiles_per_sample, batch):
    t, _ = z.shape
    wid = conv_w.shape[1]
    n_rows = batch * tiles_per_sample * ROW_TILE
    per = ROW_TILE // SUBLANES
    last = t // SUBLANES - 1
    out = jax.ShapeDtypeStruct((n_rows, wid // 3), F32)
    return pl.pallas_call(
        functools.partial(_sconv_body, tiles_per_sample=tiles_per_sample),
        grid=(batch * tiles_per_sample,),
        in_specs=[pl.BlockSpec((SUBLANES, wid), lambda i: (jnp.maximum(i * per - 1, 0), col_blocks)),
                  pl.BlockSpec((ROW_TILE, wid), lambda i: (i, col_blocks)),
                  pl.BlockSpec((SUBLANES, wid), lambda i: (jnp.minimum((i + 1) * per, last), col_blocks)),
                  pl.BlockSpec((3, wid), lambda i: (0, 0)), pl.BlockSpec((1, wid), lambda i: (0, 0))],
        out_specs=[pl.BlockSpec((ROW_TILE, wid // 3), lambda i: (i, 0))] * 3,
        out_shape=[out, out, out],
        compiler_params=pltpu.CompilerParams(dimension_semantics=("parallel",), vmem_limit_bytes=VMEM_LIMIT),
        name="short_conv_split",
    )(z, z, z, conv_w.astype(F32), conv_b.astype(F32)[None, :])


def _conv_b_body(t_ref, gs_ref, h_ref, u_ref):
    n2 = t_ref.shape[1]
    c = t_ref.shape[2]
    gs = gs_ref[...]
    y = jnp.dot(gs, t_ref[...].reshape(2 * n2, c).astype(BF16), preferred_element_type=F32)
    yr, yi = y[:n2], y[n2:]
    hr, hi = h_ref[0], h_ref[1]
    z = jnp.concatenate([yr * hr - yi * hi, yr * hi + yi * hr], axis=0).astype(BF16)
    u = lax.dot_general(gs, z, (((0,), (0,)), ((), ())), preferred_element_type=F32)
    u_ref[...] = u.reshape(2, n2, c)


def conv_stage_b(t, gs, h):
    _, n1, n2, c = t.shape
    spec = pl.BlockSpec((2, None, n2, c), lambda i: (0, i, 0, 0))
    return pl.pallas_call(
        _conv_b_body,
        grid=(n1,),
        in_specs=[spec, pl.BlockSpec((None, 2 * n2, 2 * n2), lambda i: (i, 0, 0)), spec],
        out_specs=spec,
        out_shape=jax.ShapeDtypeStruct(t.shape, F32),
        compiler_params=pltpu.CompilerParams(dimension_semantics=("parallel",), vmem_limit_bytes=VMEM_LIMIT),
        name="conv_stage_b",
    )(t, gs, h)


def _filt_b_body(t_ref, gs_ref, ss_ref, h_ref):
    n2 = t_ref.shape[1]
    c = t_ref.shape[2]
    w = HYENA_WIDTH
    y = jnp.dot(gs_ref[...], t_ref[...].reshape(2 * n2, c).astype(BF16), preferred_element_type=F32)
    yr, yi = y[:n2], y[n2:]
    ss = ss_ref[...]
    for o in range(HYENA_ORDER):
        f0, b0 = (2 * o) * w, (2 * o + 1) * w
        scale = lax.rsqrt(ss[:, f0:f0 + w] + ss[:, b0:b0 + w] + 1e-6)
        h_ref[o, 0] = (yr[:, f0:f0 + w] + yr[:, b0:b0 + w]) * scale
        h_ref[o, 1] = (yi[:, f0:f0 + w] - yi[:, b0:b0 + w]) * scale


def filter_stage_b(t, gs, sumsq):
    _, n1, n2, c = t.shape
    w = HYENA_WIDTH
    return pl.pallas_call(
        _filt_b_body,
        grid=(n1,),
        in_specs=[pl.BlockSpec((2, None, n2, c), lambda i: (0, i, 0, 0)),
                  pl.BlockSpec((None, 2 * n2, 2 * n2), lambda i: (i, 0, 0)),
                  pl.BlockSpec((1, c), lambda i: (0, 0))],
        out_specs=pl.BlockSpec((HYENA_ORDER, 2, None, n2, w), lambda i: (0, 0, i, 0, 0)),
        out_shape=jax.ShapeDtypeStruct((HYENA_ORDER, 2, n1, n2, w), F32),
        compiler_params=pltpu.CompilerParams(dimension_semantics=("parallel",), vmem_limit_bytes=VMEM_LIMIT),
        name="filter_stage_b",
    )(t, gs, sumsq)


def _fnet_b_body(t_ref, gs_ref, cs_ref, o_ref):
    n2 = t_ref.shape[1]
    c = t_ref.shape[2]
    y = jnp.dot(gs_ref[...], t_ref[...].reshape(2 * n2, c).astype(BF16), preferred_element_type=F32)
    yr, yi = y[:n2].astype(BF16), y[n2:].astype(BF16)
    gd = FNET_GROUP_DIM
    outs = []
    for g in range(c // gd):
        v = jnp.concatenate([yr[:, g * gd:(g + 1) * gd], yi[:, g * gd:(g + 1) * gd]], axis=1)
        outs.append(jnp.dot(v, cs_ref[...], preferred_element_type=F32))
    o_ref[...] = jnp.concatenate(outs, axis=1).astype(o_ref.dtype)


def fnet_stage_b(t, gs, cs, out_dtype):
    b, _, n1, n2, c = t.shape
    return pl.pallas_call(
        _fnet_b_body,
        grid=(b, n1),
        in_specs=[pl.BlockSpec((None, 2, None, n2, c), lambda i, j: (i, 0, j, 0, 0)),
                  pl.BlockSpec((None, 2 * n2, 2 * n2), lambda i, j: (j, 0, 0)),
                  pl.BlockSpec(cs.shape, lambda i, j: (0, 0))],
        out_specs=pl.BlockSpec((None, n2, c), lambda i, j: (i, 0, j)),
        out_shape=jax.ShapeDtypeStruct((b, n2, n1 * c), out_dtype),
        compiler_params=pltpu.CompilerParams(dimension_semantics=("parallel", "parallel"),
                                             vmem_limit_bytes=VMEM_LIMIT),
        name="fnet_stage_b",
    )(t, gs, cs)


def fourier_mix_pallas(u):
    b, n, c = u.shape
    gd = FNET_GROUP_DIM
    n2 = FFT_N2 if n >= FFT_SPLIT_MIN else n
    n1 = n // n2
    gs = dft_stage_b(n1, n2)
    if n1 > 1:
        t = left_matmul(dft_stage_a(n1, n1, True), u.astype(F32), n2).reshape(b, 2, n1, n2, c)
    else:
        t = jnp.stack([u.astype(F32), jnp.zeros(u.shape, F32)], axis=1).reshape(b, 2, 1, n2, c)
    ang = _angles(jnp.arange(gd)[:, None] * jnp.arange(gd)[None, :], gd)
    cs = (jnp.concatenate([jnp.cos(ang), jnp.sin(ang)], axis=0) * (n * gd) ** -0.5).astype(BF16)
    out = fnet_stage_b(t, gs, cs, u.dtype)
    return out.reshape(b, n2, n1, c).reshape(b, n, c)


FILT_TM = 512
FILT_LANES = 128


def _filter_body(ff_ref, w1_ref, b1_ref, fr1_ref, w2_ref, b2_ref, fr2_ref, w3_ref, dl_ref, taps_ref, ss_ref, *, n):
    i = pl.program_id(0)
    tm = taps_ref.shape[0]
    hi = lax.Precision.HIGHEST
    pos = (i * tm + lax.broadcasted_iota(jnp.int32, (tm, FILT_LANES), 0)).astype(F32)
    lane = lax.broadcasted_iota(jnp.int32, (tm, FILT_LANES), 1)
    t = pos / (n - 1.0) if n > 1 else pos * 0.0
    omega = (2.0 * math.pi) * pos / n
    arg = ff_ref[...] * omega
    bands = (HYENA_EMB_DIM - 1) // 2
    z = jnp.where(lane == 0, t, jnp.where(lane <= bands, jnp.cos(arg),
                                          jnp.where(lane <= 2 * bands, -jnp.sin(arg), 0.0)))
    h = jnp.sin(fr1_ref[...] * (jnp.dot(z, w1_ref[...], precision=hi, preferred_element_type=F32) + b1_ref[...]))
    h = jnp.sin(fr2_ref[...] * (jnp.dot(h, w2_ref[...], precision=hi, preferred_element_type=F32) + b2_ref[...]))
    h = jnp.dot(h, w3_ref[...], precision=hi, preferred_element_type=F32)
    width = h.shape[1]
    h = h * jnp.exp(-t[:, :1] * dl_ref[...])
    row = i * tm + lax.broadcasted_iota(jnp.int32, (tm, width), 0)
    col = lax.broadcasted_iota(jnp.int32, (tm, width), 1)
    h = jnp.where((row == 0) & ((col // HYENA_WIDTH) % 2 == 1), 0.0, h)
    taps_ref[...] = h

    @pl.when(i == 0)
    def _():
        ss_ref[...] = jnp.zeros(ss_ref.shape, F32)

    ss_ref[...] += jnp.sum(h * h, axis=0, keepdims=True)


def hyena_taps(n, w1, b1, fr1, w2, b2, fr2, w3):
    bands = (HYENA_EMB_DIM - 1) // 2
    fd = w1.shape[1]
    width = w3.shape[1]
    freqs = jnp.linspace(1e-4, bands - 1, bands, dtype=F32)
    featfreq = jnp.zeros((1, FILT_LANES), F32).at[0, 1:1 + bands].set(freqs).at[0, 1 + bands:1 + 2 * bands].set(freqs)
    w1p = jnp.zeros((FILT_LANES, fd), F32).at[:HYENA_EMB_DIM].set(w1.astype(F32))
    max_decay = math.log(HYENA_DECAY_TARGET) / HYENA_SHORT_DECAY_PCT
    min_decay = math.log(HYENA_DECAY_TARGET) / HYENA_LONG_DECAY_PCT
    deltas = jnp.abs(jnp.linspace(min_decay, max_decay, HYENA_WIDTH, dtype=F32))
    dl = jnp.tile(deltas, width // HYENA_WIDTH)[None, :]
    tm = min(FILT_TM, n)
    full = lambda a: pl.BlockSpec(a.shape, lambda i: (0,) * a.ndim)
    ops = (featfreq, w1p, b1.astype(F32)[None, :], fr1.astype(F32)[None, :], w2.astype(F32),
           b2.astype(F32)[None, :], fr2.astype(F32)[None, :], w3.astype(F32), dl)
    return pl.pallas_call(
        functools.partial(_filter_body, n=n),
        grid=(n // tm,),
        in_specs=[full(a) for a in ops],
        out_specs=[pl.BlockSpec((tm, width), lambda i: (i, 0)), pl.BlockSpec((1, width), lambda i: (0, 0))],
        out_shape=[jax.ShapeDtypeStruct((n, width), F32), jax.ShapeDtypeStruct((1, width), F32)],
        compiler_params=pltpu.CompilerParams(dimension_semantics=("arbitrary",), vmem_limit_bytes=VMEM_LIMIT),
        name="hyena_taps",
    )(*ops)


def _fft_split(n):
    if n >= FFT_SPLIT_MIN:
        return 2 * n // FFT_N2, FFT_N2, n // FFT_N2
    return 1, 2 * n, 1


def hyena_mix_pallas(x1, x2, v, filt, bias):
    batch, n, w = v.shape
    assert batch == 2
    n1, n2, k1 = _fft_split(n)
    gs = dft_stage_b(n1, n2)
    taps, sumsq = hyena_taps(n, *filt)
    if n1 > 1:
        t = left_matmul(dft_stage_a(n1, k1, True), taps[None], n2).reshape(2, n1, n2, taps.shape[1])
    else:
        t = jnp.stack([jnp.pad(taps, ((0, n), (0, 0))), jnp.zeros((2 * n, taps.shape[1]), F32)])[:, None]
    spec = filter_stage_b(t, gs, sumsq)
    fwd = dft_stage_a(n1, k1, False)
    inv = dft_stage_a_inv(n1, k1)

    def gated_long_conv(gate, z, order):
        scale = 1.0 / (2 * n)
        if n1 > 1:
            t = left_matmul(fwd, z.reshape(1, 2 * n, w), n2).reshape(2, n1, n2, w)
            uu = conv_stage_b(t, gs, spec[order])
            return left_matmul_gated(inv, uu.reshape(1, 2 * n1 * n2, w), n2, gate.reshape(1, 2 * n, w),
                                     z.reshape(1, 2 * n, w), bias[order], scale).reshape(2, n, w)
        t = jnp.pad(z, ((0, 0), (0, n), (0, 0)))[:, None]
        y = conv_stage_b(t, gs, spec[order])[:, 0, :n]
        return gate * (y * scale + z * bias[order].astype(F32))

    return gated_long_conv(x2, gated_long_conv(x1, v, 0), 1)


def layer_norm(x):
    xf = x.astype(F32)
    xc = xf - jnp.mean(xf, -1, keepdims=True)
    return xc * lax.rsqrt(jnp.mean(xc * xc, -1, keepdims=True) + LN_EPS)


def modulate(x, shift, scale):
    return (layer_norm(x) * (1.0 + scale.astype(F32)) + shift.astype(F32)).astype(x.dtype)


def post_norm(x, update, gain, bias):
    y = layer_norm(DEEPNORM_ALPHA * x.astype(F32) + update.astype(F32))
    return (y * gain.astype(F32) + bias.astype(F32)).astype(x.dtype)


def rms_norm(x, gain):
    xf = x.astype(F32)
    y = xf * lax.rsqrt(jnp.mean(xf * xf, -1, keepdims=True) + RMS_EPS) * gain.astype(F32)
    return y.astype(x.dtype)


def split_projection(z):
    points = np.cumsum(SPLIT_SIZES)[:-1].tolist()
    return jnp.split(z, points, axis=-1)


def heads(z, n_heads):
    b, n, w = z.shape
    return z.reshape(b, n, n_heads, w // n_heads)


def group_heads(q, n_kv):
    b, n, h, hd = q.shape
    return q.reshape(b, n, n_kv, h // n_kv, hd)


def flat_heads(o):
    return o.reshape(o.shape[0], o.shape[1], -1)


def axial_rope(rows, head_dim):
    n_freq = head_dim // 4
    inv_freq = jnp.power(ROPE_THETA, -jnp.arange(n_freq, dtype=F32) / n_freq)
    t = jnp.arange(rows * GRID_W)
    row = (t // GRID_W).astype(F32)
    col = (t % GRID_W).astype(F32)
    ang = jnp.stack([row[:, None] * inv_freq, col[:, None] * inv_freq], axis=1)
    return jnp.cos(ang), jnp.sin(ang)


def apply_rope(x, rope):
    cos, sin = rope
    b, n, h, hd = x.shape
    xr = x.astype(F32).reshape(b, n, h, 2, 2, hd // 4)
    x1, x2 = xr[..., 0, :], xr[..., 1, :]
    c = cos[None, :, None]
    s = sin[None, :, None]
    out = jnp.stack([x1 * c - x2 * s, x2 * c + x1 * s], axis=-2)
    return out.reshape(x.shape).astype(x.dtype)


def softmax_attend(q, k, v, sinks=None):
    scale = q.shape[-1] ** -0.5
    s = jnp.einsum("bqkgd,bskd->bkgqs", q, k, preferred_element_type=F32) * scale
    if sinks is None:
        p = jax.nn.softmax(s, axis=-1)
    else:
        sink_col = jnp.broadcast_to(sinks.astype(F32)[None, :, :, None, None], s.shape[:-1] + (1,))
        p = jax.nn.softmax(jnp.concatenate([sink_col, s], axis=-1), axis=-1)[..., 1:]
    return jnp.einsum("bkgqs,bskd->bqkgd", p.astype(v.dtype), v)


def global_attention_latent(q, k, v, k_ctx, v_ctx):
    b, n = q.shape[:2]
    k_all = jnp.concatenate([k_ctx, k], axis=1)
    v_all = jnp.concatenate([v_ctx, v], axis=1)
    qb = q.reshape((b, n // Q_BLOCK, Q_BLOCK) + q.shape[2:]).swapaxes(0, 1)
    out = lax.map(lambda q_blk: softmax_attend(q_blk, k_all, v_all), qb)
    return out.swapaxes(0, 1).reshape(b, n, -1)


def window_attention_latent(q, k, v, k_ctx, v_ctx, sinks):
    b, n = q.shape[:2]
    n_blk = n // Q_BLOCK
    n_c = k_ctx.shape[1]
    scale = q.shape[-1] ** -0.5
    pad = ((0, 0), (Q_BLOCK, Q_BLOCK), (0, 0), (0, 0))
    kp = jnp.pad(k, pad)
    vp = jnp.pad(v, pad)
    qb = q.reshape((b, n_blk, Q_BLOCK) + q.shape[2:]).swapaxes(0, 1)
    qi = jnp.arange(Q_BLOCK)[:, None]
    kj = jnp.arange(3 * Q_BLOCK)[None, :]
    in_window = jnp.abs(kj - Q_BLOCK - qi) <= WINDOW
    sink_val = sinks.astype(F32)[None, :, :, None, None]

    def block(args):
        q_blk, i = args
        kb = lax.dynamic_slice_in_dim(kp, i * Q_BLOCK, 3 * Q_BLOCK, axis=1)
        vb = lax.dynamic_slice_in_dim(vp, i * Q_BLOCK, 3 * Q_BLOCK, axis=1)
        kpos = i * Q_BLOCK - Q_BLOCK + kj
        valid = in_window & (kpos >= 0) & (kpos < n)
        s_ctx = jnp.einsum("bqkgd,bskd->bkgqs", q_blk, k_ctx, preferred_element_type=F32) * scale
        s_band = jnp.einsum("bqkgd,bskd->bkgqs", q_blk, kb, preferred_element_type=F32) * scale
        s_band = jnp.where(valid, s_band, -jnp.inf)
        sink_col = jnp.broadcast_to(sink_val, s_ctx.shape[:-1] + (1,))
        p = jax.nn.softmax(jnp.concatenate([sink_col, s_ctx, s_band], axis=-1), axis=-1).astype(v.dtype)
        return (jnp.einsum("bkgqs,bskd->bqkgd", p[..., 1:1 + n_c], v_ctx)
                + jnp.einsum("bkgqs,bskd->bqkgd", p[..., 1 + n_c:], vb))

    out = lax.map(block, (qb, jnp.arange(n_blk)))
    return out.swapaxes(0, 1).reshape(b, n, -1)


def fourier_mix(u):
    b, n, _ = u.shape
    ug = u.astype(F32).reshape(b, n, FNET_GROUPS, FNET_GROUP_DIM)
    y = jnp.fft.fft2(ug, axes=(1, 3), norm="ortho").real
    return y.reshape(b, n, FNET_GROUPS * FNET_GROUP_DIM).astype(u.dtype)


def short_conv(u, w, bias):
    n = u.shape[1]
    up = jnp.pad(u, ((0, 0), (1, 1), (0, 0)))
    return up[:, :n] * w[0] + up[:, 1:n + 1] * w[1] + up[:, 2:] * w[2] + bias


def hyena_filter_spectrum(n, w1, b1, fr1, w2, b2, fr2, w3):
    t = jnp.linspace(0.0, 1.0, n, dtype=F32)[:, None]
    bands = (HYENA_EMB_DIM - 1) // 2
    omega = 2.0 * math.pi * jnp.arange(n, dtype=F32)[:, None] / n
    freqs = jnp.linspace(1e-4, bands - 1, bands, dtype=F32)[None, :]
    z = jnp.concatenate([t, jnp.cos(freqs * omega), -jnp.sin(freqs * omega)], axis=-1)
    h = jnp.sin(fr1.astype(F32) * (z @ w1.astype(F32) + b1.astype(F32)))
    h = jnp.sin(fr2.astype(F32) * (h @ w2.astype(F32) + b2.astype(F32)))
    h = (h @ w3.astype(F32)).reshape(n, HYENA_ORDER, 2, HYENA_WIDTH)
    max_decay = math.log(HYENA_DECAY_TARGET) / HYENA_SHORT_DECAY_PCT
    min_decay = math.log(HYENA_DECAY_TARGET) / HYENA_LONG_DECAY_PCT
    deltas = jnp.abs(jnp.linspace(min_decay, max_decay, HYENA_WIDTH, dtype=F32))
    h = h * jnp.exp(-t * deltas)[:, None, None, :]
    taps = jnp.concatenate([h[:, :, 0],
                            jnp.zeros((1, HYENA_ORDER, HYENA_WIDTH), F32),
                            h[:0:-1, :, 1]], axis=0)
    taps = taps * lax.rsqrt(jnp.sum(taps * taps, axis=0, keepdims=True) + 1e-6)
    return jnp.fft.rfft(taps, axis=0)


def hyena_mix(zc, conv_w, conv_b, filt, bias):
    n = zc.shape[1]
    u = short_conv(zc, conv_w, conv_b).astype(F32)
    x1, x2, v = jnp.split(u, 3, axis=-1)
    spec = hyena_filter_spectrum(n, *filt)

    def long_conv(z, order):
        zf = jnp.fft.rfft(z, n=2 * n, axis=1)
        y = jnp.fft.irfft(zf * spec[None, :, order], n=2 * n, axis=1)[:, :n]
        return y + z * bias[order].astype(F32)

    y = x1 * long_conv(v, 0)
    y = x2 * long_conv(y, 1)
    return y.astype(zc.dtype)


def merge_branches(h, branches, w_gate_l, w_branch_l, w_out_l):
    merged = sum(jax.nn.sigmoid(mm3(h, w_gate_l[i])) * mm3(br, w_branch_l[i]) for i, br in enumerate(branches))
    return mm3(merged, w_out_l)


def moe_ffn(tokens, logits, w_pack):
    n_tok, d = tokens.shape
    i32 = jnp.int32
    top_v, top_i = lax.top_k(logits, TOP_K)
    weights = jax.nn.softmax(top_v, axis=-1)
    flat_e = top_i.reshape(-1).astype(i32)
    n_asg = n_tok * TOP_K
    e_ids = jnp.arange(N_EXPERTS, dtype=i32)
    onehot = (flat_e[:, None] == e_ids[None, :]).astype(i32)
    csum = jnp.cumsum(onehot, axis=0)
    rank = jnp.sum(onehot * csum, axis=1) - 1
    counts = csum[-1]
    padded = (counts + MOE_TB - 1) // MOE_TB * MOE_TB
    padded_end = jnp.cumsum(padded)
    padded_start = padded_end - padded
    dest = jnp.sum(onehot * padded_start[None, :], axis=1) + rank
    n_rows = n_asg + N_EXPERTS * MOE_TB
    n_blk = n_rows // MOE_TB
    blk_start = jnp.arange(n_blk, dtype=i32) * MOE_TB
    blk_e = jnp.minimum(jnp.sum((blk_start[:, None] >= padded_end[None, :]).astype(i32), axis=1), N_EXPERTS - 1)
    n_used = (padded_end[-1:] // MOE_TB).astype(i32)
    pad_need = (padded - counts)[:, None]
    pad_key = jnp.where(jnp.arange(MOE_TB, dtype=i32)[None, :] < pad_need, 2 * e_ids[:, None] + 1, 2 * N_EXPERTS)
    keys = jnp.concatenate([2 * flat_e, pad_key.reshape(-1)])
    vals = jnp.concatenate([jnp.arange(n_asg, dtype=i32) // TOP_K, jnp.zeros((N_EXPERTS * MOE_TB,), i32)])
    _, src = lax.sort_key_val(keys, vals, is_stable=True)
    buf = tokens[src]
    y = expert_ffn(buf, blk_e, n_used, *w_pack)
    dest = dest.reshape(n_tok, TOP_K)
    return [y[dest[:, k]] for k in range(TOP_K)], weights


def prep_expert_weights(w_gu_all, b_gu, w_dn, b_dn, layer):
    packed = tuple(split_gate_up(w_gu_all, layer)) + (
        b_gu[:, None, 0::2].astype(F32), b_gu[:, None, 1::2].astype(F32),
        w_dn.astype(BF16), b_dn[:, None, :].astype(F32))
    return packed


def _combine_body(x_ref, y0_ref, y1_ref, y2_ref, y3_ref, w_ref, g5_ref, lg_ref, lb_ref, o_ref):
    w = w_ref[...]
    y = sum(w[:, k:k + 1] * y_ref[...].astype(F32) for k, y_ref in enumerate((y0_ref, y1_ref, y2_ref, y3_ref)))
    o_ref[...] = _ln(DEEPNORM_ALPHA * x_ref[...] + g5_ref[...] * y) * lg_ref[...] + lb_ref[...]


def combine_postnorm(x1, ys, weights, gate5, ln_g, ln_b, tiles_per_sample, batch):
    rows, d = x1.shape
    gmap = _group_map(tiles_per_sample * (ROW_TILE // OUT_TILE), batch)
    row = lambda i: (i, 0)
    const = lambda i: (0, 0)
    return pl.pallas_call(
        _combine_body,
        grid=(rows // OUT_TILE,),
        in_specs=[pl.BlockSpec((OUT_TILE, d), row)] + [pl.BlockSpec((OUT_TILE, d), row)] * TOP_K
                 + [pl.BlockSpec((OUT_TILE, TOP_K), row), pl.BlockSpec((None, 1, d), gmap),
                    pl.BlockSpec((1, d), const), pl.BlockSpec((1, d), const)],
        out_specs=pl.BlockSpec((OUT_TILE, d), row),
        out_shape=jax.ShapeDtypeStruct((rows, d), F32),
        compiler_params=pltpu.CompilerParams(dimension_semantics=("parallel",), vmem_limit_bytes=VMEM_LIMIT),
        name="combine_postnorm",
    )(x1, *ys, weights, gate5, ln_g, ln_b)


def kernel(x, c, ctx, c_ctx, mod_w, mod_b, w_in, gqa_q_gain, gqa_k_gain, conv_w, conv_b,
           filt_w1, filt_b1, filt_freq1, filt_w2, filt_b2, filt_freq2, filt_w3, hyena_bias,
           swa_sinks, w_branch, w_gate, w_out, ln1_g, ln1_b, router_w, router_b,
           exp_w_gate_up, exp_b_gate_up, exp_w_down, exp_b_down, ln2_g, ln2_b):
    batch, n_lat, d = x.shape
    n_ctx = ctx.shape[1]
    rows = n_lat // GRID_W
    rope_gqa = rope_tables(rows, GQA_HEAD_DIM)
    rope_swa = rope_tables(rows, SWA_HEAD_DIM)
    n_l, n_c = batch * n_lat, batch * n_ctx
    assert n_lat % ROW_TILE == 0 and n_c % ROW_TILE == 0
    tps = n_lat // ROW_TILE
    xs = jnp.concatenate([x.reshape(n_l, d), ctx.reshape(n_c, d)], axis=0)
    cond = jnp.concatenate([c, c_ctx[None, :]], axis=0)
    offs = np.cumsum((0,) + SPLIT_SIZES)
    for l in range(DEPTH):
        ctx_continues = l < DEPTH - 1
        m_rows = n_l + n_c if ctx_continues else n_l
        mods = (jax.nn.silu(cond) @ mod_w[l] + mod_b[l]).reshape(batch + 1, 6, 1, d).swapaxes(0, 1)
        filt = (filt_w1[l], filt_b1[l], filt_freq1[l], filt_w2[l], filt_b2[l], filt_freq2[l], filt_w3[l])
        sinks = swa_sinks[l].reshape(SWA_KV_HEADS, SWA_HEADS // SWA_KV_HEADS)

        h, fa, gq, gk, gv, hy, sq, sk, sv = in_projection(
            xs, mods[0], mods[1], w_in[l].astype(BF16), rope_gqa, rope_swa, gqa_q_gain[l], gqa_k_gain[l], tps, batch)
        lat3 = lambda t: t[:n_l].reshape(batch, n_lat, -1)
        ctx3 = lambda t: t[n_l:].reshape(batch, n_ctx, -1)
        fa_l, fa_c, hy_c = lat3(fa), ctx3(fa), ctx3(hy)
        hy_parts = [p.reshape(batch, n_lat, -1) for p in short_conv_split(hy, 0, conv_w[l], conv_b[l], tps, batch)]
        ga = global_attention(gq[:n_l], gk[:n_l], gv[:n_l], gk[n_l:], gv[n_l:], batch)
        wa = window_attention(sq[:n_l], sk[:n_l], sv[:n_l], sk[n_l:], sv[n_l:], swa_sinks[l], batch)
        branches = [
            fourier_mix_pallas(fa_l).reshape(n_l, -1),
            ga,
            hyena_mix_pallas(*hy_parts, filt, hyena_bias[l]).reshape(n_l, -1),
            wa,
        ]
        if ctx_continues:
            gq_ctx = group_heads(heads(ctx3(gq), GQA_HEADS), GQA_KV_HEADS)
            sq_ctx = group_heads(heads(ctx3(sq), SWA_HEADS), SWA_KV_HEADS)
            ctx_branches = (
                fourier_mix_pallas(fa_c),
                flat_heads(softmax_attend(gq_ctx, heads(ctx3(gk), GQA_KV_HEADS), heads(ctx3(gv), GQA_KV_HEADS))),
                hyena_mix_pallas(*jnp.split(short_conv(hy_c, conv_w[l], conv_b[l]), 3, axis=-1), filt, hyena_bias[l]),
                flat_heads(softmax_attend(sq_ctx, heads(ctx3(sk), SWA_KV_HEADS), heads(ctx3(sv), SWA_KV_HEADS), sinks)),
            )
            branches = [jnp.concatenate([bl, bc.reshape(n_c, -1).astype(bl.dtype)], axis=0)
                        for bl, bc in zip(branches, ctx_branches)]
        merged = merge_gated(h, branches, w_gate[l].astype(BF16), w_branch[l].astype(BF16), m_rows)
        router_wp = jnp.zeros((d, ROUTER_PAD), F32).at[:, :N_EXPERTS].set(router_w[l])
        router_hi = router_wp.astype(BF16)
        router_hl = jnp.concatenate([router_hi, (router_wp - router_hi.astype(F32)).astype(BF16)], axis=1)
        router_bp = jnp.zeros((1, ROUTER_PAD), F32).at[0, :N_EXPERTS].set(router_b[l])
        x1, h_moe, logits = outproj_postnorm(merged, w_out[l].astype(BF16), xs, mods[2], mods[3], mods[4],
                                             ln1_g[l][None, :], ln1_b[l][None, :], router_hl, router_bp, tps, batch)

        w_pack = prep_expert_weights(exp_w_gate_up, exp_b_gate_up[l], exp_w_down[l], exp_b_down[l], l)
        ys, weights = moe_ffn(h_moe, logits[:, :N_EXPERTS], w_pack)
        xs = combine_postnorm(x1, ys, weights, mods[5], ln2_g[l][None, :], ln2_b[l][None, :], tps, batch)
    return xs.reshape(batch, n_lat, d)
```

```python
import functools
import math

import jax
import jax.numpy as jnp
import numpy as np
from jax import lax
from jax.experimental import pallas as pl
from jax.experimental.pallas import tpu as pltpu

DEPTH = 2
GRID_W = 64
FNET_GROUPS = 4
FNET_GROUP_DIM = 128
GQA_HEADS = 4
GQA_KV_HEADS = 2
GQA_HEAD_DIM = 128
HYENA_WIDTH = 512
HYENA_ORDER = 2
HYENA_EMB_DIM = 33
HYENA_DECAY_TARGET = 1e-2
HYENA_SHORT_DECAY_PCT = 0.3
HYENA_LONG_DECAY_PCT = 1.5
SWA_HEADS = 8
SWA_KV_HEADS = 2
SWA_HEAD_DIM = 64
WINDOW = 128
ROPE_THETA = 10000.0
N_EXPERTS = 32
TOP_K = 4
D_EXPERT = 1536
SWIGLU_LIMIT = 7.0
SWIGLU_ALPHA = 1.702
LN_EPS = 1e-5
RMS_EPS = 1e-6
DEEPNORM_ALPHA = (2 * DEPTH) ** 0.25
SPLIT_SIZES = (FNET_GROUPS * FNET_GROUP_DIM,
               GQA_HEADS * GQA_HEAD_DIM, GQA_KV_HEADS * GQA_HEAD_DIM, GQA_KV_HEADS * GQA_HEAD_DIM,
               3 * HYENA_WIDTH,
               SWA_HEADS * SWA_HEAD_DIM, SWA_KV_HEADS * SWA_HEAD_DIM, SWA_KV_HEADS * SWA_HEAD_DIM)
F32 = jnp.float32
BF16 = jnp.bfloat16


ROW_TILE = 512


def _group_map(tiles_per_sample, batch):
    return lambda i, *_: (jnp.minimum(i // tiles_per_sample, batch), 0, 0)


def _ln(x):
    xc = x - jnp.mean(x, axis=-1, keepdims=True)
    return xc * lax.rsqrt(jnp.mean(xc * xc, axis=-1, keepdims=True) + LN_EPS)


def _norm_rope(x, c, s, gain, quarter):
    width = x.shape[1]
    reps = width // LANES
    if gain is not None:
        x = jnp.concatenate(
            [xh * lax.rsqrt(jnp.mean(xh * xh, axis=-1, keepdims=True) + RMS_EPS) * gain
             for xh in (x[:, r * LANES:(r + 1) * LANES] for r in range(reps))], axis=1)
    lane = lax.broadcasted_iota(jnp.int32, x.shape, 1)
    first = (lane % (2 * quarter)) < quarter
    partner = jnp.where(first, pltpu.roll(x, width - quarter, 1), pltpu.roll(x, quarter, 1))
    return x * jnp.concatenate([c] * reps, axis=1) + partner * jnp.concatenate([s] * reps, axis=1)


IN_TILE = 256


def _inproj_body(x_ref, sh_ref, sc_ref, w_ref, cg_ref, sg_ref, cs_ref, ss_ref, gq_ref, gk_ref,
                 h_ref, fa_ref, q_ref, k_ref, v_ref, hy_ref, sq_ref, sk_ref, sv_ref):
    h = (_ln(x_ref[...]) * (1.0 + sc_ref[...]) + sh_ref[...]).astype(BF16)
    h_ref[...] = h
    offs = np.cumsum((0,) + SPLIT_SIZES)
    seg = lambda i: jnp.dot(h, w_ref[:, int(offs[i]):int(offs[i + 1])], preferred_element_type=F32)
    gq4, sq4 = GQA_HEAD_DIM // 4, SWA_HEAD_DIM // 4
    fa_ref[...] = seg(0)
    q_ref[...] = _norm_rope(seg(1), cg_ref[...], sg_ref[...], gq_ref[...], gq4).astype(BF16)
    k_ref[...] = _norm_rope(seg(2), cg_ref[...], sg_ref[...], gk_ref[...], gq4).astype(BF16)
    v_ref[...] = seg(3).astype(BF16)
    hy_ref[...] = seg(4)
    sq_ref[...] = _norm_rope(seg(5), cs_ref[...], ss_ref[...], None, sq4).astype(BF16)
    sk_ref[...] = _norm_rope(seg(6), cs_ref[...], ss_ref[...], None, sq4).astype(BF16)
    sv_ref[...] = seg(7).astype(BF16)


def in_projection(x, shift, scale, w_in, rope_g, rope_s, gain_q, gain_k, tiles_per_sample, batch):
    t, d = x.shape
    per = ROW_TILE // IN_TILE
    gmap = _group_map(tiles_per_sample * per, batch)
    lat_tiles = tiles_per_sample * per
    tmap = lambda i: (jnp.where(i < lat_tiles * batch, i % lat_tiles, lat_tiles), 0)
    row = lambda i: (i, 0)
    const = lambda i: (0, 0)
    tab = pl.BlockSpec((IN_TILE, LANES), tmap)
    vec = pl.BlockSpec((1, LANES), const)
    widths = (d,) + SPLIT_SIZES
    dtypes = (BF16, F32, BF16, BF16, BF16, F32, BF16, BF16, BF16)
    return pl.pallas_call(
        _inproj_body,
        grid=(t // IN_TILE,),
        in_specs=[pl.BlockSpec((IN_TILE, d), row), pl.BlockSpec((None, 1, d), gmap), pl.BlockSpec((None, 1, d), gmap),
                  pl.BlockSpec(w_in.shape, const, pipeline_mode=pl.Buffered(1)), tab, tab, tab, tab, vec, vec],
        out_specs=[pl.BlockSpec((IN_TILE, w), row) for w in widths],
        out_shape=[jax.ShapeDtypeStruct((t, w), dt) for w, dt in zip(widths, dtypes)],
        compiler_params=pltpu.CompilerParams(dimension_semantics=("parallel",), vmem_limit_bytes=VMEM_LIMIT),
        name="in_projection",
    )(x, shift, scale, w_in, *rope_g, *rope_s, gain_q.astype(F32)[None, :], gain_k.astype(F32)[None, :])


def rope_tables(rows, head_dim):
    cos, sin = axial_rope(rows, head_dim)
    c = jnp.concatenate([cos[:, 0], cos[:, 0], cos[:, 1], cos[:, 1]], axis=-1)
    s = jnp.concatenate([-sin[:, 0], sin[:, 0], -sin[:, 1], sin[:, 1]], axis=-1)
    reps = LANES // head_dim
    c, s = jnp.tile(c, (1, reps)), jnp.tile(s, (1, reps))
    return (jnp.concatenate([c, jnp.ones((ROW_TILE, LANES), F32)], axis=0),
            jnp.concatenate([s, jnp.zeros((ROW_TILE, LANES), F32)], axis=0))


MERGE_TILE = 256


def _merge_body(h_ref, b0_ref, b1_ref, b2_ref, b3_ref, wg_ref, wb_ref, o_ref):
    h = h_ref[...]
    acc = None
    for j, b_ref in enumerate((b0_ref, b1_ref, b2_ref, b3_ref)):
        gate = jax.nn.sigmoid(jnp.dot(h, wg_ref[j], preferred_element_type=F32))
        term = gate * jnp.dot(b_ref[...].astype(BF16), wb_ref[j], preferred_element_type=F32)
        acc = term if acc is None else acc + term
    o_ref[...] = acc.astype(o_ref.dtype)


def merge_gated(h, branches, w_gate, w_branch, rows):
    d = h.shape[1]
    nb, bw, _ = w_branch.shape
    assert nb == len(branches) == 4
    row = lambda i: (i, 0)
    whole = lambda i: (0, 0, 0)
    return pl.pallas_call(
        _merge_body,
        grid=(rows // MERGE_TILE,),
        in_specs=[pl.BlockSpec((MERGE_TILE, d), row)] + [pl.BlockSpec((MERGE_TILE, bw), row)] * nb
                 + [pl.BlockSpec((nb, d, d), whole, pipeline_mode=pl.Buffered(1)),
                    pl.BlockSpec((nb, bw, d), whole, pipeline_mode=pl.Buffered(1))],
        out_specs=pl.BlockSpec((MERGE_TILE, d), row),
        out_shape=jax.ShapeDtypeStruct((rows, d), BF16),
        compiler_params=pltpu.CompilerParams(dimension_semantics=("parallel",), vmem_limit_bytes=VMEM_LIMIT),
        name="merge_gated",
    )(h, *branches, w_gate, w_branch)


OUT_TILE = 256
ROUTER_PAD = 128


def _outproj_body(m_ref, w_ref, x_ref, g2_ref, sh_ref, sc_ref, lg_ref, lb_ref, rw_ref, rb_ref,
                  x1_ref, hm_ref, lo_ref):
    mix = jnp.dot(m_ref[...], w_ref[...], preferred_element_type=F32)
    x1 = _ln(DEEPNORM_ALPHA * x_ref[...] + g2_ref[...] * mix) * lg_ref[...] + lb_ref[...]
    x1_ref[...] = x1
    hm = _ln(x1) * (1.0 + sc_ref[...]) + sh_ref[...]
    hm_hi = hm.astype(BF16)
    hm_ref[...] = hm_hi
    hm_lo = (hm - hm_hi.astype(F32)).astype(BF16)
    hh = jnp.dot(hm_hi, rw_ref[...], preferred_element_type=F32)
    lh = jnp.dot(hm_lo, rw_ref[:, :ROUTER_PAD], preferred_element_type=F32)
    lo_ref[...] = hh[:, :ROUTER_PAD] + hh[:, ROUTER_PAD:] + lh + rb_ref[...]


def outproj_postnorm(merged, w_out, x, gate2, shift3, scale4, ln_g, ln_b, router_w, router_b,
                     tiles_per_sample, batch):
    rows, d = merged.shape
    gmap = _group_map(tiles_per_sample * (ROW_TILE // OUT_TILE), batch)
    row = lambda i: (i, 0)
    const = lambda i: (0, 0)
    return pl.pallas_call(
        _outproj_body,
        grid=(rows // OUT_TILE,),
        in_specs=[pl.BlockSpec((OUT_TILE, d), row), pl.BlockSpec((d, d), const), pl.BlockSpec((OUT_TILE, d), row),
                  pl.BlockSpec((None, 1, d), gmap), pl.BlockSpec((None, 1, d), gmap), pl.BlockSpec((None, 1, d), gmap),
                  pl.BlockSpec((1, d), const), pl.BlockSpec((1, d), const),
                  pl.BlockSpec((d, 2 * ROUTER_PAD), const), pl.BlockSpec((1, ROUTER_PAD), const)],
        out_specs=[pl.BlockSpec((OUT_TILE, d), row), pl.BlockSpec((OUT_TILE, d), row),
                   pl.BlockSpec((OUT_TILE, ROUTER_PAD), row)],
        out_shape=[jax.ShapeDtypeStruct((rows, d), F32), jax.ShapeDtypeStruct((rows, d), BF16),
                   jax.ShapeDtypeStruct((rows, ROUTER_PAD), F32)],
        compiler_params=pltpu.CompilerParams(dimension_semantics=("parallel",), vmem_limit_bytes=VMEM_LIMIT),
        name="outproj_postnorm",
    )(merged, w_out, x, gate2, shift3, scale4, ln_g, ln_b, router_w, router_b)


GATTN_TQ = 512
GATTN_TK = 256
GATTN_UNROLL = 16
VMEM_LIMIT = 56 * 1024 * 1024
LANES = 128
LOG2E = math.log2(math.e)


def _gattn_body(q_ref, kc_ref, vc_ref, k_ref, v_ref, o_ref, m_ref, acc_ref, *, c2):
    tq = q_ref.shape[0]
    hd = k_ref.shape[1]
    q2 = jnp.concatenate([q_ref[:, :hd], q_ref[:, hd:]], axis=0)
    m_ref[...] = jnp.full(m_ref.shape, -jnp.inf, F32)
    acc_ref[...] = jnp.zeros(acc_ref.shape, F32)

    def step(k, v):
        s = lax.dot_general(q2, k, (((1,), (1,)), ((), ())), preferred_element_type=F32) * c2
        m_old = m_ref[...]
        m_new = jnp.maximum(m_old, jnp.max(s, axis=-1, keepdims=True))
        alpha = jnp.exp2(m_old - m_new)
        p = jnp.concatenate([jnp.exp2(s[:, j * LANES:(j + 1) * LANES] - m_new)
                             for j in range(k.shape[0] // LANES)], axis=1)
        acc_ref[...] = (jnp.concatenate([alpha, alpha], axis=1) * acc_ref[...]
                        + jnp.dot(p.astype(BF16), v, preferred_element_type=F32))
        m_ref[...] = m_new

    step(kc_ref[...], vc_ref[...])

    def loop(c, carry):
        off = pl.multiple_of(c * GATTN_TK, GATTN_TK)
        step(k_ref[pl.ds(off, GATTN_TK), :], v_ref[pl.ds(off, GATTN_TK), :])
        return carry

    n_chunks = k_ref.shape[0] // GATTN_TK
    lax.fori_loop(0, n_chunks, loop, 0, unroll=math.gcd(GATTN_UNROLL, n_chunks))
    acc = acc_ref[...]
    out = acc[:, :hd] / acc[:, hd:]
    o_ref[...] = jnp.concatenate([out[:tq], out[tq:]], axis=1).astype(o_ref.dtype)


def _with_ones(v, n_heads):
    r = v.shape[0]
    v3 = v.reshape(r, n_heads, -1)
    return jnp.concatenate([v3, jnp.ones_like(v3)], axis=-1).reshape(r, -1)


def global_attention(q, k, v, k_ctx, v_ctx, batch):
    hd = GQA_HEAD_DIM
    n = q.shape[0] // batch
    n_c = k_ctx.shape[0] // batch
    g = GQA_HEADS // GQA_KV_HEADS
    tq = min(GATTN_TQ, n)
    assert g == 2 and hd == LANES and n % tq == 0 and n % GATTN_TK == 0 and n_c % LANES == 0
    nq = n // tq
    return pl.pallas_call(
        functools.partial(_gattn_body, c2=hd ** -0.5 * LOG2E),
        grid=(batch, GQA_KV_HEADS, nq),
        in_specs=[pl.BlockSpec((tq, g * hd), lambda b, h, i: (b * nq + i, h)),
                  pl.BlockSpec((n_c, hd), lambda b, h, i: (b, h)),
                  pl.BlockSpec((n_c, 2 * hd), lambda b, h, i: (b, h)),
                  pl.BlockSpec((n, hd), lambda b, h, i: (b, h)),
                  pl.BlockSpec((n, 2 * hd), lambda b, h, i: (b, h))],
        out_specs=pl.BlockSpec((tq, g * hd), lambda b, h, i: (b * nq + i, h)),
        out_shape=jax.ShapeDtypeStruct(q.shape, BF16),
        scratch_shapes=[pltpu.VMEM((g * tq, hd), F32), pltpu.VMEM((g * tq, 2 * hd), F32)],
        compiler_params=pltpu.CompilerParams(dimension_semantics=("parallel", "parallel", "parallel"),
                                             vmem_limit_bytes=VMEM_LIMIT),
        name="gattn",
    )(q, k_ctx, _with_ones(v_ctx, GQA_KV_HEADS), k, _with_ones(v, GQA_KV_HEADS))


WATTN_TQ = 256


def _wattn_body(sink_ref, q_ref, kc_ref, vc_ref, kp_ref, kx_ref, kn_ref, vp_ref, vx_ref, vn_ref, o_ref,
                *, scale, n, nq):
    tq = q_ref.shape[0]
    hd = SWA_HEAD_DIM
    g = SWA_HEADS // SWA_KV_HEADS
    h = pl.program_id(1)
    i = pl.program_id(2)
    k_band = jnp.concatenate([kp_ref[...], kx_ref[...], kn_ref[...]], axis=0)
    v_band = jnp.concatenate([vp_ref[...], vx_ref[...], vn_ref[...]], axis=0)
    qpos = i * tq + lax.broadcasted_iota(jnp.int32, (tq, tq + 2 * WINDOW), 0)
    kpos = i * tq - WINDOW + lax.broadcasted_iota(jnp.int32, (tq, tq + 2 * WINDOW), 1)
    valid = (jnp.abs(kpos - qpos) <= WINDOW) & (kpos >= 0) & (kpos < n)
    outs = []
    for j in range(g):
        q = q_ref[:, j * hd:(j + 1) * hd]
        s_c = lax.dot_general(q, kc_ref[...], (((1,), (1,)), ((), ())), preferred_element_type=F32) * scale
        s_b = lax.dot_general(q, k_band, (((1,), (1,)), ((), ())), preferred_element_type=F32) * scale
        s_b = jnp.where(valid, s_b, -jnp.inf)
        sink = sink_ref[h * g + j]
        m = jnp.maximum(jnp.maximum(jnp.max(s_c, -1, keepdims=True), jnp.max(s_b, -1, keepdims=True)), sink)
        p_c = jnp.exp(s_c - m)
        p_b = jnp.exp(s_b - m)
        l = jnp.sum(p_c, -1, keepdims=True) + jnp.sum(p_b, -1, keepdims=True) + jnp.exp(sink - m)
        o = (jnp.dot(p_c.astype(BF16), vc_ref[...], preferred_element_type=F32)
             + jnp.dot(p_b.astype(BF16), v_band, preferred_element_type=F32))
        outs.append(o / l)
    o_ref[...] = jnp.concatenate(outs, axis=1).astype(o_ref.dtype)


def window_attention(q, k, v, k_ctx, v_ctx, sinks, batch):
    hd = SWA_HEAD_DIM
    n = q.shape[0] // batch
    n_c = k_ctx.shape[0] // batch
    g = SWA_HEADS // SWA_KV_HEADS
    tq = WATTN_TQ
    assert n % tq == 0 and tq % WINDOW == 0
    nq = n // tq
    r = tq // WINDOW
    nw = n // WINDOW
    k_h = k.reshape(batch * n, SWA_KV_HEADS, hd).transpose(1, 0, 2)
    v_h = v.reshape(batch * n, SWA_KV_HEADS, hd).transpose(1, 0, 2)
    kc_h = k_ctx.reshape(batch * n_c, SWA_KV_HEADS, hd).transpose(1, 0, 2)
    vc_h = v_ctx.reshape(batch * n_c, SWA_KV_HEADS, hd).transpose(1, 0, 2)
    prev_map = lambda b, h, i, s: (h, b * nw + jnp.maximum(i * r - 1, 0), 0)
    cur_map = lambda b, h, i, s: (h, b * nq + i, 0)
    next_map = lambda b, h, i, s: (h, b * nw + jnp.minimum((i + 1) * r, nw - 1), 0)
    ctx_map = lambda b, h, i, s: (h, b, 0)
    grid_spec = pltpu.PrefetchScalarGridSpec(
        num_scalar_prefetch=1,
        grid=(batch, SWA_KV_HEADS, nq),
        in_specs=[pl.BlockSpec((tq, g * hd), lambda b, h, i, s: (b * nq + i, h)),
                  pl.BlockSpec((None, n_c, hd), ctx_map),
                  pl.BlockSpec((None, n_c, hd), ctx_map),
                  pl.BlockSpec((None, WINDOW, hd), prev_map),
                  pl.BlockSpec((None, tq, hd), cur_map),
                  pl.BlockSpec((None, WINDOW, hd), next_map),
                  pl.BlockSpec((None, WINDOW, hd), prev_map),
                  pl.BlockSpec((None, tq, hd), cur_map),
                  pl.BlockSpec((None, WINDOW, hd), next_map)],
        out_specs=pl.BlockSpec((tq, g * hd), lambda b, h, i, s: (b * nq + i, h)),
    )
    return pl.pallas_call(
        functools.partial(_wattn_body, scale=hd ** -0.5, n=n, nq=nq),
        grid_spec=grid_spec,
        out_shape=jax.ShapeDtypeStruct(q.shape, BF16),
        compiler_params=pltpu.CompilerParams(dimension_semantics=("parallel", "parallel", "parallel"),
                                             vmem_limit_bytes=VMEM_LIMIT),
        name="wattn",
    )(sinks.astype(F32), q, kc_h, vc_h, k_h, k_h, k_h, v_h, v_h, v_h)


MOE_TB = 512
MOE_FC = 512


def _ffn_body(blk_e_ref, nused_ref, x_ref, wg_ref, wu_ref, bg_ref, bu_ref, wd_ref, bd_ref, o_ref):
    i = pl.program_id(0)

    @pl.when(i < nused_ref[0])
    def _():
        x = x_ref[...]
        acc = jnp.zeros(o_ref.shape, F32) + bd_ref[...]
        for c in range(D_EXPERT // MOE_FC):
            sl = slice(c * MOE_FC, (c + 1) * MOE_FC)
            gate = jnp.dot(x, wg_ref[:, sl], preferred_element_type=F32) + bg_ref[:, sl]
            up = jnp.dot(x, wu_ref[:, sl], preferred_element_type=F32) + bu_ref[:, sl]
            gate = jnp.minimum(gate, SWIGLU_LIMIT)
            up = jnp.clip(up, -SWIGLU_LIMIT, SWIGLU_LIMIT)
            glu = gate * jax.nn.sigmoid(gate * SWIGLU_ALPHA)
            act = ((up + 1.0) * glu).astype(BF16)
            acc = acc + jnp.dot(act, wd_ref[sl, :], preferred_element_type=F32)
        o_ref[...] = acc.astype(o_ref.dtype)

    @pl.when(i >= nused_ref[0])
    def _():
        o_ref[...] = jnp.zeros(o_ref.shape, o_ref.dtype)


def expert_ffn(buf, blk_e, n_used, w_gate, w_up, b_gate, b_up, w_dn_all, b_dn, layer):
    n_rows, d = buf.shape
    n_blk = n_rows // MOE_TB
    f = w_gate.shape[2]
    wmap = lambda i, e, u: (e[i], 0, 0)
    grid_spec = pltpu.PrefetchScalarGridSpec(
        num_scalar_prefetch=2,
        grid=(n_blk,),
        in_specs=[pl.BlockSpec((MOE_TB, d), lambda i, e, u: (i, 0)),
                  pl.BlockSpec((None, d, f), wmap),
                  pl.BlockSpec((None, d, f), wmap),
                  pl.BlockSpec((None, 1, f), wmap),
                  pl.BlockSpec((None, 1, f), wmap),
                  pl.BlockSpec((None, None, f, d), lambda i, e, u: (layer, e[i], 0, 0)),
                  pl.BlockSpec((None, 1, d), wmap)],
        out_specs=pl.BlockSpec((MOE_TB, d), lambda i, e, u: (i, 0)),
    )
    return pl.pallas_call(
        _ffn_body,
        grid_spec=grid_spec,
        out_shape=jax.ShapeDtypeStruct((n_rows, d), BF16),
        compiler_params=pltpu.CompilerParams(dimension_semantics=("arbitrary",),
                                             vmem_limit_bytes=VMEM_LIMIT),
        name="expert_ffn",
    )(blk_e, n_used, buf, w_gate, w_up, b_gate, b_up, w_dn_all, b_dn)


SPLIT_ROWS = 512


def _deint_body(w_ref, p_ref, g_ref, u_ref):
    blk = p_ref.shape[0]
    for c in range(w_ref.shape[1] // blk):
        y = jnp.dot(w_ref[:, c * blk:(c + 1) * blk].astype(BF16), p_ref[...], preferred_element_type=F32)
        g_ref[:, c * LANES:(c + 1) * LANES] = y[:, :LANES].astype(BF16)
        u_ref[:, c * LANES:(c + 1) * LANES] = y[:, LANES:].astype(BF16)


def split_gate_up(w_gu_all, layer):
    _, e, d, f2 = w_gu_all.shape
    blk = 2 * LANES
    src = jnp.arange(blk)[:, None]
    dst = jnp.arange(blk)[None, :]
    perm = (src == jnp.where(dst < LANES, 2 * dst, 2 * (dst - LANES) + 1)).astype(BF16)
    out = jax.ShapeDtypeStruct((e, d, f2 // 2), BF16)
    return pl.pallas_call(
        _deint_body,
        grid=(e, d // SPLIT_ROWS),
        in_specs=[pl.BlockSpec((None, None, SPLIT_ROWS, f2), lambda i, j: (layer, i, j, 0)),
                  pl.BlockSpec((blk, blk), lambda i, j: (0, 0))],
        out_specs=[pl.BlockSpec((None, SPLIT_ROWS, f2 // 2), lambda i, j: (i, j, 0)),
                   pl.BlockSpec((None, SPLIT_ROWS, f2 // 2), lambda i, j: (i, j, 0))],
        out_shape=[out, out],
        compiler_params=pltpu.CompilerParams(dimension_semantics=("parallel", "parallel"),
                                             vmem_limit_bytes=VMEM_LIMIT),
        name="split_gate_up",
    )(w_gu_all, perm)


FFT_N2 = 128
FFT_SPLIT_MIN = 1024


def _angles(num, den):
    return (2.0 * math.pi / den) * (num % den).astype(F32)


def dft_stage_a(n1, k_in, real_input):
    ang = _angles(jnp.arange(n1)[:, None] * jnp.arange(k_in)[None, :], n1)
    c, s = jnp.cos(ang), jnp.sin(ang)
    if real_input:
        return jnp.concatenate([c, -s], axis=0).astype(BF16)
    return jnp.block([[c, s], [-s, c]]).astype(BF16)


def dft_stage_a_inv(n1, k_out):
    ang = _angles(jnp.arange(k_out)[:, None] * jnp.arange(n1)[None, :], n1)
    c, s = jnp.cos(ang), jnp.sin(ang)
    return jnp.block([[c, -s], [s, c]]).astype(BF16)


def dft_stage_b(n1, n2):
    k1 = jnp.arange(n1)[:, None, None]
    k2 = jnp.arange(n2)[None, :, None]
    m = jnp.arange(n2)[None, None, :]
    ang = _angles(m * (k1 + n1 * k2), n1 * n2)
    c, s = jnp.cos(ang), jnp.sin(ang)
    top = jnp.concatenate([c, s], axis=2)
    bot = jnp.concatenate([-s, c], axis=2)
    return jnp.concatenate([top, bot], axis=1).astype(BF16)


SUBLANES = 8


def _lmm_body(m_ref, x_ref, o_ref):
    m = m_ref[...]
    k, sub, lanes = x_ref.shape
    r = o_ref.shape[0]
    x2 = x_ref.reshape(k * sub, lanes)
    o2 = o_ref.reshape(r * sub, lanes)
    for s in range(sub):
        xs = x2[pl.ds(s, k, stride=sub), :].astype(BF16)
        o2[pl.ds(s, r, stride=sub), :] = jnp.dot(m, xs, preferred_element_type=F32)


def left_matmul(m, x, n2):
    g, rows, c = x.shape
    k = rows // n2
    r = m.shape[0]
    assert m.shape[1] == k and n2 % SUBLANES == 0 and c % LANES == 0
    x5 = x.reshape(g, k, n2 // SUBLANES, SUBLANES, c)
    out = pl.pallas_call(
        _lmm_body,
        grid=(g, n2 // SUBLANES, c // LANES),
        in_specs=[pl.BlockSpec((r, k), lambda i, j, l: (0, 0)),
                  pl.BlockSpec((None, k, None, SUBLANES, LANES), lambda i, j, l: (i, 0, j, 0, l))],
        out_specs=pl.BlockSpec((None, r, None, SUBLANES, LANES), lambda i, j, l: (i, 0, j, 0, l)),
        out_shape=jax.ShapeDtypeStruct((g, r, n2 // SUBLANES, SUBLANES, c), F32),
        compiler_params=pltpu.CompilerParams(dimension_semantics=("parallel", "parallel", "parallel"),
                                             vmem_limit_bytes=VMEM_LIMIT),
        name="left_matmul",
    )(m, x5)
    return out.reshape(g, r * n2, c)


def _lmm_gate_body(m_ref, x_ref, a_ref, z_ref, b_ref, o_ref, *, scale):
    m = m_ref[...]
    k, sub, lanes = x_ref.shape
    r = o_ref.shape[0]
    x2 = x_ref.reshape(k * sub, lanes)
    a2 = a_ref.reshape(r * sub, lanes)
    z2 = z_ref.reshape(r * sub, lanes)
    o2 = o_ref.reshape(r * sub, lanes)
    for s in range(sub):
        rows = pl.ds(s, r, stride=sub)
        y = jnp.dot(m, x2[pl.ds(s, k, stride=sub), :].astype(BF16), preferred_element_type=F32)
        o2[rows, :] = a2[rows, :] * (y * scale + z2[rows, :] * b_ref[...])


def left_matmul_gated(m, x, n2, gate, z, bias, scale):
    g, rows, c = x.shape
    k = rows // n2
    r = m.shape[0]
    assert m.shape[1] == k and gate.shape == (g, r * n2, c) and z.shape == gate.shape
    view = lambda t, lead: t.reshape(g, lead, n2 // SUBLANES, SUBLANES, c)
    blk = lambda lead: pl.BlockSpec((None, lead, None, SUBLANES, LANES), lambda i, j, l: (i, 0, j, 0, l))
    out = pl.pallas_call(
        functools.partial(_lmm_gate_body, scale=scale),
        grid=(g, n2 // SUBLANES, c // LANES),
        in_specs=[pl.BlockSpec((r, k), lambda i, j, l: (0, 0)), blk(k), blk(r), blk(r),
                  pl.BlockSpec((1, LANES), lambda i, j, l: (0, l))],
        out_specs=blk(r),
        out_shape=jax.ShapeDtypeStruct((g, r, n2 // SUBLANES, SUBLANES, c), F32),
        compiler_params=pltpu.CompilerParams(dimension_semantics=("parallel", "parallel", "parallel"),
                                             vmem_limit_bytes=VMEM_LIMIT),
        name="left_matmul_gated",
    )(m, view(x, k), view(gate, r), view(z, r), bias.astype(F32)[None, :])
    return out.reshape(g, r * n2, c)


def _sconv_body(p_ref, x_ref, n_ref, w_ref, b_ref, o1_ref, o2_ref, o3_ref, *, tiles_per_sample):
    i = pl.program_id(0)
    pos = i % tiles_per_sample
    x = x_ref[...]
    rows = x.shape[0]
    before = jnp.where(pos == 0, 0.0, p_ref[SUBLANES - 1:SUBLANES, :])
    after = jnp.where(pos == tiles_per_sample - 1, 0.0, n_ref[0:1, :])
    ridx = lax.broadcasted_iota(jnp.int32, x.shape, 0)
    up = jnp.where(ridx == 0, before, pltpu.roll(x, 1, 0))
    dn = jnp.where(ridx == rows - 1, after, pltpu.roll(x, rows - 1, 0))
    y = up * w_ref[0:1, :] + x * w_ref[1:2, :] + dn * w_ref[2:3, :] + b_ref[...]
    w = o1_ref.shape[1]
    o1_ref[...] = y[:, :w]
    o2_ref[...] = y[:, w:2 * w]
    o3_ref[...] = y[:, 2 * w:]


def short_conv_split(z, col_blocks, conv_w, conv_b, tiles_per_sample, batch):
    t, _ = z.shape
    wid = conv_w.shape[1]
    n_rows = batch * tiles_per_sample * ROW_TILE
    per = ROW_TILE // SUBLANES
    last = t // SUBLANES - 1
    out = jax.ShapeDtypeStruct((n_rows, wid // 3), F32)
    return pl.pallas_call(
        functools.partial(_sconv_body, tiles_per_sample=tiles_per_sample),
        grid=(batch * tiles_per_sample,),
        in_specs=[pl.BlockSpec((SUBLANES, wid), lambda i: (jnp.maximum(i * per - 1, 0), col_blocks)),
                  pl.BlockSpec((ROW_TILE, wid), lambda i: (i, col_blocks)),
                  pl.BlockSpec((SUBLANES, wid), lambda i: (jnp.minimum((i + 1) * per, last), col_blocks)),
                  pl.BlockSpec((3, wid), lambda i: (0, 0)), pl.BlockSpec((1, wid), lambda i: (0, 0))],
        out_specs=[pl.BlockSpec((ROW_TILE, wid // 3), lambda i: (i, 0))] * 3,
        out_shape=[out, out, out],
        compiler_params=pltpu.CompilerParams(dimension_semantics=("parallel",), vmem_limit_bytes=VMEM_LIMIT),
        name="short_conv_split",
    )(z, z, z, conv_w.astype(F32), conv_b.astype(F32)[None, :])


def _conv_b_body(t_ref, gs_ref, h_ref, u_ref):
    n2 = t_ref.shape[1]
    c = t_ref.shape[2]
    gs = gs_ref[...]
    y = jnp.dot(gs, t_ref[...].reshape(2 * n2, c).astype(BF16), preferred_element_type=F32)
    yr, yi = y[:n2], y[n2:]
    hr, hi = h_ref[0], h_ref[1]
    z = jnp.concatenate([yr * hr - yi * hi, yr * hi + yi * hr], axis=0).astype(BF16)
    u = lax.dot_general(gs, z, (((0,), (0,)), ((), ())), preferred_element_type=F32)
    u_ref[...] = u.reshape(2, n2, c)


def conv_stage_b(t, gs, h):
    _, n1, n2, c = t.shape
    spec = pl.BlockSpec((2, None, n2, c), lambda i: (0, i, 0, 0))
    return pl.pallas_call(
        _conv_b_body,
        grid=(n1,),
        in_specs=[spec, pl.BlockSpec((None, 2 * n2, 2 * n2), lambda i: (i, 0, 0)), spec],
        out_specs=spec,
        out_shape=jax.ShapeDtypeStruct(t.shape, F32),
        compiler_params=pltpu.CompilerParams(dimension_semantics=("parallel",), vmem_limit_bytes=VMEM_LIMIT),
        name="conv_stage_b",
    )(t, gs, h)


def _filt_b_body(t_ref, gs_ref, ss_ref, h_ref):
    n2 = t_ref.shape[1]
    c = t_ref.shape[2]
    w = HYENA_WIDTH
    y = jnp.dot(gs_ref[...], t_ref[...].reshape(2 * n2, c).astype(BF16), preferred_element_type=F32)
    yr, yi = y[:n2], y[n2:]
    ss = ss_ref[...]
    for o in range(HYENA_ORDER):
        f0, b0 = (2 * o) * w, (2 * o + 1) * w
        scale = lax.rsqrt(ss[:, f0:f0 + w] + ss[:, b0:b0 + w] + 1e-6)
        h_ref[o, 0] = (yr[:, f0:f0 + w] + yr[:, b0:b0 + w]) * scale
        h_ref[o, 1] = (yi[:, f0:f0 + w] - yi[:, b0:b0 + w]) * scale


def filter_stage_b(t, gs, sumsq):
    _, n1, n2, c = t.shape
    w = HYENA_WIDTH
    return pl.pallas_call(
        _filt_b_body,
        grid=(n1,),
        in_specs=[pl.BlockSpec((2, None, n2, c), lambda i: (0, i, 0, 0)),
                  pl.BlockSpec((None, 2 * n2, 2 * n2), lambda i: (i, 0, 0)),
                  pl.BlockSpec((1, c), lambda i: (0, 0))],
        out_specs=pl.BlockSpec((HYENA_ORDER, 2, None, n2, w), lambda i: (0, 0, i, 0, 0)),
        out_shape=jax.ShapeDtypeStruct((HYENA_ORDER, 2, n1, n2, w), F32),
        compiler_params=pltpu.CompilerParams(dimension_semantics=("parallel",), vmem_limit_bytes=VMEM_LIMIT),
        name="filter_stage_b",
    )(t, gs, sumsq)


def _fnet_b_body(t_ref, gs_ref, cs_ref, o_ref):
    n2 = t_ref.shape[1]
    c = t_ref.shape[2]
    y = jnp.dot(gs_ref[...], t_ref[...].reshape(2 * n2, c).astype(BF16), preferred_element_type=F32)
    yr, yi = y[:n2].astype(BF16), y[n2:].astype(BF16)
    gd = FNET_GROUP_DIM
    outs = []
    for g in range(c // gd):
        v = jnp.concatenate([yr[:, g * gd:(g + 1) * gd], yi[:, g * gd:(g + 1) * gd]], axis=1)
        outs.append(jnp.dot(v, cs_ref[...], preferred_element_type=F32))
    o_ref[...] = jnp.concatenate(outs, axis=1).astype(o_ref.dtype)


def fnet_stage_b(t, gs, cs, out_dtype):
    b, _, n1, n2, c = t.shape
    return pl.pallas_call(
        _fnet_b_body,
        grid=(b, n1),
        in_specs=[pl.BlockSpec((None, 2, None, n2, c), lambda i, j: (i, 0, j, 0, 0)),
                  pl.BlockSpec((None, 2 * n2, 2 * n2), lambda i, j: (j, 0, 0)),
                  pl.BlockSpec(cs.shape, lambda i, j: (0, 0))],
        out_specs=pl.BlockSpec((None, n2, c), lambda i, j: (i, 0, j)),
        out_shape=jax.ShapeDtypeStruct((b, n2, n1 * c), out_dtype),
        compiler_params=pltpu.CompilerParams(dimension_semantics=("parallel", "parallel"),
                                             vmem_limit_bytes=VMEM_LIMIT),
        name="fnet_stage_b",
    )(t, gs, cs)


def fourier_mix_pallas(u):
    b, n, c = u.shape
    gd = FNET_GROUP_DIM
    n2 = FFT_N2 if n >= FFT_SPLIT_MIN else n
    n1 = n // n2
    gs = dft_stage_b(n1, n2)
    if n1 > 1:
        t = left_matmul(dft_stage_a(n1, n1, True), u.astype(F32), n2).reshape(b, 2, n1, n2, c)
    else:
        t = jnp.stack([u.astype(F32), jnp.zeros(u.shape, F32)], axis=1).reshape(b, 2, 1, n2, c)
    ang = _angles(jnp.arange(gd)[:, None] * jnp.arange(gd)[None, :], gd)
    cs = (jnp.concatenate([jnp.cos(ang), jnp.sin(ang)], axis=0) * (n * gd) ** -0.5).astype(BF16)
    out = fnet_stage_b(t, gs, cs, u.dtype)
    return out.reshape(b, n2, n1, c).reshape(b, n, c)


FILT_TM = 512
FILT_LANES = 128


def _filter_body(ff_ref, w1_ref, b1_ref, fr1_ref, w2_ref, b2_ref, fr2_ref, w3_ref, dl_ref, taps_ref, ss_ref, *, n):
    i = pl.program_id(0)
    tm = taps_ref.shape[0]
    hi = lax.Precision.HIGHEST
    pos = (i * tm + lax.broadcasted_iota(jnp.int32, (tm, FILT_LANES), 0)).astype(F32)
    lane = lax.broadcasted_iota(jnp.int32, (tm, FILT_LANES), 1)
    t = pos / (n - 1.0) if n > 1 else pos * 0.0
    omega = (2.0 * math.pi) * pos / n
    arg = ff_ref[...] * omega
    bands = (HYENA_EMB_DIM - 1) // 2
    z = jnp.where(lane == 0, t, jnp.where(lane <= bands, jnp.cos(arg),
                                          jnp.where(lane <= 2 * bands, -jnp.sin(arg), 0.0)))
    h = jnp.sin(fr1_ref[...] * (jnp.dot(z, w1_ref[...], precision=hi, preferred_element_type=F32) + b1_ref[...]))
    h = jnp.sin(fr2_ref[...] * (jnp.dot(h, w2_ref[...], precision=hi, preferred_element_type=F32) + b2_ref[...]))
    h = jnp.dot(h, w3_ref[...], precision=hi, preferred_element_type=F32)
    width = h.shape[1]
    h = h * jnp.exp(-t[:, :1] * dl_ref[...])
    row = i * tm + lax.broadcasted_iota(jnp.int32, (tm, width), 0)
    col = lax.broadcasted_iota(jnp.int32, (tm, width), 1)
    h = jnp.where((row == 0) & ((col // HYENA_WIDTH) % 2 == 1), 0.0, h)
    taps_ref[...] = h

    @pl.when(i == 0)
    def _():
        ss_ref[...] = jnp.zeros(ss_ref.shape, F32)

    ss_ref[...] += jnp.sum(h * h, axis=0, keepdims=True)


def hyena_taps(n, w1, b1, fr1, w2, b2, fr2, w3):
    bands = (HYENA_EMB_DIM - 1) // 2
    fd = w1.shape[1]
    width = w3.shape[1]
    freqs = jnp.linspace(1e-4, bands - 1, bands, dtype=F32)
    featfreq = jnp.zeros((1, FILT_LANES), F32).at[0, 1:1 + bands].set(freqs).at[0, 1 + bands:1 + 2 * bands].set(freqs)
    w1p = jnp.zeros((FILT_LANES, fd), F32).at[:HYENA_EMB_DIM].set(w1.astype(F32))
    max_decay = math.log(HYENA_DECAY_TARGET) / HYENA_SHORT_DECAY_PCT
    min_decay = math.log(HYENA_DECAY_TARGET) / HYENA_LONG_DECAY_PCT
    deltas = jnp.abs(jnp.linspace(min_decay, max_decay, HYENA_WIDTH, dtype=F32))
    dl = jnp.tile(deltas, width // HYENA_WIDTH)[None, :]
    tm = min(FILT_TM, n)
    full = lambda a: pl.BlockSpec(a.shape, lambda i: (0,) * a.ndim)
    ops = (featfreq, w1p, b1.astype(F32)[None, :], fr1.astype(F32)[None, :], w2.astype(F32),
           b2.astype(F32)[None, :], fr2.astype(F32)[None, :], w3.astype(F32), dl)
    return pl.pallas_call(
        functools.partial(_filter_body, n=n),
        grid=(n // tm,),
        in_specs=[full(a) for a in ops],
        out_specs=[pl.BlockSpec((tm, width), lambda i: (i, 0)), pl.BlockSpec((1, width), lambda i: (0, 0))],
        out_shape=[jax.ShapeDtypeStruct((n, width), F32), jax.ShapeDtypeStruct((1, width), F32)],
        compiler_params=pltpu.CompilerParams(dimension_semantics=("arbitrary",), vmem_limit_bytes=VMEM_LIMIT),
        name="hyena_taps",
    )(*ops)


def _fft_split(n):
    if n >= FFT_SPLIT_MIN:
        return 2 * n // FFT_N2, FFT_N2, n // FFT_N2
    return 1, 2 * n, 1


def hyena_mix_pallas(x1, x2, v, filt, bias):
    batch, n, w = v.shape
    assert batch == 2
    n1, n2, k1 = _fft_split(n)
    gs = dft_stage_b(n1, n2)
    taps, sumsq = hyena_taps(n, *filt)
    if n1 > 1:
        t = left_matmul(dft_stage_a(n1, k1, True), taps[None], n2).reshape(2, n1, n2, taps.shape[1])
    else:
        t = jnp.stack([jnp.pad(taps, ((0, n), (0, 0))), jnp.zeros((2 * n, taps.shape[1]), F32)])[:, None]
    spec = filter_stage_b(t, gs, sumsq)
    fwd = dft_stage_a(n1, k1, False)
    inv = dft_stage_a_inv(n1, k1)

    def gated_long_conv(gate, z, order):
        scale = 1.0 / (2 * n)
        if n1 > 1:
            t = left_matmul(fwd, z.reshape(1, 2 * n, w), n2).reshape(2, n1, n2, w)
            uu = conv_stage_b(t, gs, spec[order])
            return left_matmul_gated(inv, uu.reshape(1, 2 * n1 * n2, w), n2, gate.reshape(1, 2 * n, w),
                                     z.reshape(1, 2 * n, w), bias[order], scale).reshape(2, n, w)
        t = jnp.pad(z, ((0, 0), (0, n), (0, 0)))[:, None]
        y = conv_stage_b(t, gs, spec[order])[:, 0, :n]
        return gate * (y * scale + z * bias[order].astype(F32))

    return gated_long_conv(x2, gated_long_conv(x1, v, 0), 1)


def heads(z, n_heads):
    b, n, w = z.shape
    return z.reshape(b, n, n_heads, w // n_heads)


def group_heads(q, n_kv):
    b, n, h, hd = q.shape
    return q.reshape(b, n, n_kv, h // n_kv, hd)


def flat_heads(o):
    return o.reshape(o.shape[0], o.shape[1], -1)


def axial_rope(rows, head_dim):
    n_freq = head_dim // 4
    inv_freq = jnp.power(ROPE_THETA, -jnp.arange(n_freq, dtype=F32) / n_freq)
    t = jnp.arange(rows * GRID_W)
    row = (t // GRID_W).astype(F32)
    col = (t % GRID_W).astype(F32)
    ang = jnp.stack([row[:, None] * inv_freq, col[:, None] * inv_freq], axis=1)
    return jnp.cos(ang), jnp.sin(ang)


def softmax_attend(q, k, v, sinks=None):
    scale = q.shape[-1] ** -0.5
    s = jnp.einsum("bqkgd,bskd->bkgqs", q, k, preferred_element_type=F32) * scale
    if sinks is None:
        p = jax.nn.softmax(s, axis=-1)
    else:
        sink_col = jnp.broadcast_to(sinks.astype(F32)[None, :, :, None, None], s.shape[:-1] + (1,))
        p = jax.nn.softmax(jnp.concatenate([sink_col, s], axis=-1), axis=-1)[..., 1:]
    return jnp.einsum("bkgqs,bskd->bqkgd", p.astype(v.dtype), v)


def short_conv(u, w, bias):
    n = u.shape[1]
    up = jnp.pad(u, ((0, 0), (1, 1), (0, 0)))
    return up[:, :n] * w[0] + up[:, 1:n + 1] * w[1] + up[:, 2:] * w[2] + bias


def moe_ffn(tokens, logits, w_pack, layer):
    n_tok, d = tokens.shape
    i32 = jnp.int32
    top_v, top_i = lax.top_k(logits, TOP_K)
    weights = jax.nn.softmax(top_v, axis=-1)
    flat_e = top_i.reshape(-1).astype(i32)
    n_asg = n_tok * TOP_K
    e_ids = jnp.arange(N_EXPERTS, dtype=i32)
    onehot = (flat_e[:, None] == e_ids[None, :]).astype(i32)
    csum = jnp.cumsum(onehot, axis=0)
    rank = jnp.sum(onehot * csum, axis=1) - 1
    counts = csum[-1]
    padded = (counts + MOE_TB - 1) // MOE_TB * MOE_TB
    padded_end = jnp.cumsum(padded)
    padded_start = padded_end - padded
    dest = jnp.sum(onehot * padded_start[None, :], axis=1) + rank
    n_rows = n_asg + N_EXPERTS * MOE_TB
    n_blk = n_rows // MOE_TB
    blk_start = jnp.arange(n_blk, dtype=i32) * MOE_TB
    blk_e = jnp.minimum(jnp.sum((blk_start[:, None] >= padded_end[None, :]).astype(i32), axis=1), N_EXPERTS - 1)
    n_used = (padded_end[-1:] // MOE_TB).astype(i32)
    pad_need = (padded - counts)[:, None]
    pad_key = jnp.where(jnp.arange(MOE_TB, dtype=i32)[None, :] < pad_need, 2 * e_ids[:, None] + 1, 2 * N_EXPERTS)
    keys = jnp.concatenate([2 * flat_e, pad_key.reshape(-1)])
    vals = jnp.concatenate([jnp.arange(n_asg, dtype=i32) // TOP_K, jnp.zeros((N_EXPERTS * MOE_TB,), i32)])
    _, src = lax.sort_key_val(keys, vals, is_stable=True)
    buf = tokens[src]
    y = expert_ffn(buf, blk_e, n_used, *w_pack, layer)
    dest = dest.reshape(n_tok, TOP_K)
    return [y[dest[:, k]] for k in range(TOP_K)], weights


def prep_expert_weights(w_gu_all, b_gu, w_dn_all, b_dn, layer):
    packed = tuple(split_gate_up(w_gu_all, layer)) + (
        b_gu[:, None, 0::2].astype(F32), b_gu[:, None, 1::2].astype(F32),
        w_dn_all, b_dn[:, None, :].astype(F32))
    return packed


def _combine_body(x_ref, y0_ref, y1_ref, y2_ref, y3_ref, w_ref, g5_ref, lg_ref, lb_ref, o_ref):
    w = w_ref[...]
    y = sum(w[:, k:k + 1] * y_ref[...].astype(F32) for k, y_ref in enumerate((y0_ref, y1_ref, y2_ref, y3_ref)))
    o_ref[...] = _ln(DEEPNORM_ALPHA * x_ref[...] + g5_ref[...] * y) * lg_ref[...] + lb_ref[...]


def combine_postnorm(x1, ys, weights, gate5, ln_g, ln_b, tiles_per_sample, batch):
    rows, d = x1.shape
    gmap = _group_map(tiles_per_sample * (ROW_TILE // OUT_TILE), batch)
    row = lambda i: (i, 0)
    const = lambda i: (0, 0)
    return pl.pallas_call(
        _combine_body,
        grid=(rows // OUT_TILE,),
        in_specs=[pl.BlockSpec((OUT_TILE, d), row)] + [pl.BlockSpec((OUT_TILE, d), row)] * TOP_K
                 + [pl.BlockSpec((OUT_TILE, TOP_K), row), pl.BlockSpec((None, 1, d), gmap),
                    pl.BlockSpec((1, d), const), pl.BlockSpec((1, d), const)],
        out_specs=pl.BlockSpec((OUT_TILE, d), row),
        out_shape=jax.ShapeDtypeStruct((rows, d), F32),
        compiler_params=pltpu.CompilerParams(dimension_semantics=("parallel",), vmem_limit_bytes=VMEM_LIMIT),
        name="combine_postnorm",
    )(x1, *ys, weights, gate5, ln_g, ln_b)


def kernel(x, c, ctx, c_ctx, mod_w, mod_b, w_in, gqa_q_gain, gqa_k_gain, conv_w, conv_b,
           filt_w1, filt_b1, filt_freq1, filt_w2, filt_b2, filt_freq2, filt_w3, hyena_bias,
           swa_sinks, w_branch, w_gate, w_out, ln1_g, ln1_b, router_w, router_b,
           exp_w_gate_up, exp_b_gate_up, exp_w_down, exp_b_down, ln2_g, ln2_b):
    batch, n_lat, d = x.shape
    n_ctx = ctx.shape[1]
    rows = n_lat // GRID_W
    rope_gqa = rope_tables(rows, GQA_HEAD_DIM)
    rope_swa = rope_tables(rows, SWA_HEAD_DIM)
    n_l, n_c = batch * n_lat, batch * n_ctx
    assert n_lat % ROW_TILE == 0 and n_c % ROW_TILE == 0
    tps = n_lat // ROW_TILE
    xs = jnp.concatenate([x.reshape(n_l, d), ctx.reshape(n_c, d)], axis=0)
    cond = jnp.concatenate([c, c_ctx[None, :]], axis=0)
    w_down_b = exp_w_down.astype(BF16)
    for l in range(DEPTH):
        ctx_continues = l < DEPTH - 1
        m_rows = n_l + n_c if ctx_continues else n_l
        mods = (jax.nn.silu(cond) @ mod_w[l] + mod_b[l]).reshape(batch + 1, 6, 1, d).swapaxes(0, 1)
        filt = (filt_w1[l], filt_b1[l], filt_freq1[l], filt_w2[l], filt_b2[l], filt_freq2[l], filt_w3[l])
        sinks = swa_sinks[l].reshape(SWA_KV_HEADS, SWA_HEADS // SWA_KV_HEADS)

        h, fa, gq, gk, gv, hy, sq, sk, sv = in_projection(
            xs, mods[0], mods[1], w_in[l].astype(BF16), rope_gqa, rope_swa, gqa_q_gain[l], gqa_k_gain[l], tps, batch)
        lat3 = lambda t: t[:n_l].reshape(batch, n_lat, -1)
        ctx3 = lambda t: t[n_l:].reshape(batch, n_ctx, -1)
        fa_l, fa_c, hy_c = lat3(fa), ctx3(fa), ctx3(hy)
        hy_parts = [p.reshape(batch, n_lat, -1) for p in short_conv_split(hy, 0, conv_w[l], conv_b[l], tps, batch)]
        ga = global_attention(gq[:n_l], gk[:n_l], gv[:n_l], gk[n_l:], gv[n_l:], batch)
        wa = window_attention(sq[:n_l], sk[:n_l], sv[:n_l], sk[n_l:], sv[n_l:], swa_sinks[l], batch)
        branches = [
            fourier_mix_pallas(fa_l).reshape(n_l, -1),
            ga,
            hyena_mix_pallas(*hy_parts, filt, hyena_bias[l]).reshape(n_l, -1),
            wa,
        ]
        if ctx_continues:
            gq_ctx = group_heads(heads(ctx3(gq), GQA_HEADS), GQA_KV_HEADS)
            sq_ctx = group_heads(heads(ctx3(sq), SWA_HEADS), SWA_KV_HEADS)
            ctx_branches = (
                fourier_mix_pallas(fa_c),
                flat_heads(softmax_attend(gq_ctx, heads(ctx3(gk), GQA_KV_HEADS), heads(ctx3(gv), GQA_KV_HEADS))),
                hyena_mix_pallas(*jnp.split(short_conv(hy_c, conv_w[l], conv_b[l]), 3, axis=-1), filt, hyena_bias[l]),
                flat_heads(softmax_attend(sq_ctx, heads(ctx3(sk), SWA_KV_HEADS), heads(ctx3(sv), SWA_KV_HEADS), sinks)),
            )
            branches = [jnp.concatenate([bl, bc.reshape(n_c, -1).astype(bl.dtype)], axis=0)
                        for bl, bc in zip(branches, ctx_branches)]
        merged = merge_gated(h, branches, w_gate[l].astype(BF16), w_branch[l].astype(BF16), m_rows)
        router_wp = jnp.zeros((d, ROUTER_PAD), F32).at[:, :N_EXPERTS].set(router_w[l])
        router_hi = router_wp.astype(BF16)
        router_hl = jnp.concatenate([router_hi, (router_wp - router_hi.astype(F32)).astype(BF16)], axis=1)
        router_bp = jnp.zeros((1, ROUTER_PAD), F32).at[0, :N_EXPERTS].set(router_b[l])
        x1, h_moe, logits = outproj_postnorm(merged, w_out[l].astype(BF16), xs, mods[2], mods[3], mods[4],
                                             ln1_g[l][None, :], ln1_b[l][None, :], router_hl, router_bp, tps, batch)

        w_pack = prep_expert_weights(exp_w_gate_up, exp_b_gate_up[l], w_down_b, exp_b_down[l], l)
        ys, weights = moe_ffn(h_moe, logits[:, :N_EXPERTS], w_pack, l)
        xs = combine_postnorm(x1, ys, weights, mods[5], ln2_g[l][None, :], ln2_b[l][None, :], tps, batch)
    return xs.reshape(batch, n_lat, d)
```

```python
import functools
import math

import jax
import jax.numpy as jnp
import numpy as np
from jax import lax
from jax.experimental import pallas as pl
from jax.experimental.pallas import tpu as pltpu

DEPTH = 2
GRID_W = 64
FNET_GROUPS = 4
FNET_GROUP_DIM = 128
GQA_HEADS = 4
GQA_KV_HEADS = 2
GQA_HEAD_DIM = 128
HYENA_WIDTH = 512
HYENA_ORDER = 2
HYENA_EMB_DIM = 33
HYENA_DECAY_TARGET = 1e-2
HYENA_SHORT_DECAY_PCT = 0.3
HYENA_LONG_DECAY_PCT = 1.5
SWA_HEADS = 8
SWA_KV_HEADS = 2
SWA_HEAD_DIM = 64
WINDOW = 128
ROPE_THETA = 10000.0
N_EXPERTS = 32
TOP_K = 4
D_EXPERT = 1536
SWIGLU_LIMIT = 7.0
SWIGLU_ALPHA = 1.702
LN_EPS = 1e-5
RMS_EPS = 1e-6
DEEPNORM_ALPHA = (2 * DEPTH) ** 0.25
SPLIT_SIZES = (FNET_GROUPS * FNET_GROUP_DIM,
               GQA_HEADS * GQA_HEAD_DIM, GQA_KV_HEADS * GQA_HEAD_DIM, GQA_KV_HEADS * GQA_HEAD_DIM,
               3 * HYENA_WIDTH,
               SWA_HEADS * SWA_HEAD_DIM, SWA_KV_HEADS * SWA_HEAD_DIM, SWA_KV_HEADS * SWA_HEAD_DIM)
F32 = jnp.float32
BF16 = jnp.bfloat16


ROW_TILE = 512


def _group_map(tiles_per_sample, batch):
    return lambda i, *_: (jnp.minimum(i // tiles_per_sample, batch), 0, 0)


def _ln(x):
    xc = x - jnp.mean(x, axis=-1, keepdims=True)
    return xc * lax.rsqrt(jnp.mean(xc * xc, axis=-1, keepdims=True) + LN_EPS)


def _norm_rope(x, c, s, gain, quarter):
    width = x.shape[1]
    reps = width // LANES
    if gain is not None:
        x = jnp.concatenate(
            [xh * lax.rsqrt(jnp.mean(xh * xh, axis=-1, keepdims=True) + RMS_EPS) * gain
             for xh in (x[:, r * LANES:(r + 1) * LANES] for r in range(reps))], axis=1)
    lane = lax.broadcasted_iota(jnp.int32, x.shape, 1)
    first = (lane % (2 * quarter)) < quarter
    partner = jnp.where(first, pltpu.roll(x, width - quarter, 1), pltpu.roll(x, quarter, 1))
    return x * jnp.concatenate([c] * reps, axis=1) + partner * jnp.concatenate([s] * reps, axis=1)


IN_TILE = 256


def _inproj_body(x_ref, sh_ref, sc_ref, w_ref, cg_ref, sg_ref, cs_ref, ss_ref, gq_ref, gk_ref,
                 h_ref, fa_ref, q_ref, k_ref, v_ref, hy_ref, sq_ref, sk_ref, sv_ref):
    h = (_ln(x_ref[...]) * (1.0 + sc_ref[...]) + sh_ref[...]).astype(BF16)
    h_ref[...] = h
    offs = np.cumsum((0,) + SPLIT_SIZES)
    seg = lambda i: jnp.dot(h, w_ref[:, int(offs[i]):int(offs[i + 1])], preferred_element_type=F32)
    gq4, sq4 = GQA_HEAD_DIM // 4, SWA_HEAD_DIM // 4
    fa_ref[...] = seg(0)
    q_ref[...] = _norm_rope(seg(1), cg_ref[...], sg_ref[...], gq_ref[...], gq4).astype(BF16)
    k_ref[...] = _norm_rope(seg(2), cg_ref[...], sg_ref[...], gk_ref[...], gq4).astype(BF16)
    v_ref[...] = seg(3).astype(BF16)
    hy_ref[...] = seg(4)
    sq_ref[...] = _norm_rope(seg(5), cs_ref[...], ss_ref[...], None, sq4).astype(BF16)
    sk_ref[...] = _norm_rope(seg(6), cs_ref[...], ss_ref[...], None, sq4).astype(BF16)
    sv_ref[...] = seg(7).astype(BF16)


def in_projection(x, shift, scale, w_in, rope_g, rope_s, gain_q, gain_k, tiles_per_sample, batch):
    t, d = x.shape
    per = ROW_TILE // IN_TILE
    gmap = _group_map(tiles_per_sample * per, batch)
    lat_tiles = tiles_per_sample * per
    tmap = lambda i: (jnp.where(i < lat_tiles * batch, i % lat_tiles, lat_tiles), 0)
    row = lambda i: (i, 0)
    const = lambda i: (0, 0)
    tab = pl.BlockSpec((IN_TILE, LANES), tmap)
    vec = pl.BlockSpec((1, LANES), const)
    widths = (d,) + SPLIT_SIZES
    dtypes = (BF16, F32, BF16, BF16, BF16, F32, BF16, BF16, BF16)
    return pl.pallas_call(
        _inproj_body,
        grid=(t // IN_TILE,),
        in_specs=[pl.BlockSpec((IN_TILE, d), row), pl.BlockSpec((None, 1, d), gmap), pl.BlockSpec((None, 1, d), gmap),
                  pl.BlockSpec(w_in.shape, const, pipeline_mode=pl.Buffered(1)), tab, tab, tab, tab, vec, vec],
        out_specs=[pl.BlockSpec((IN_TILE, w), row) for w in widths],
        out_shape=[jax.ShapeDtypeStruct((t, w), dt) for w, dt in zip(widths, dtypes)],
        compiler_params=pltpu.CompilerParams(dimension_semantics=("parallel",), vmem_limit_bytes=VMEM_LIMIT),
        name="in_projection",
    )(x, shift, scale, w_in, *rope_g, *rope_s, gain_q.astype(F32)[None, :], gain_k.astype(F32)[None, :])


def rope_tables(rows, head_dim):
    cos, sin = axial_rope(rows, head_dim)
    c = jnp.concatenate([cos[:, 0], cos[:, 0], cos[:, 1], cos[:, 1]], axis=-1)
    s = jnp.concatenate([-sin[:, 0], sin[:, 0], -sin[:, 1], sin[:, 1]], axis=-1)
    reps = LANES // head_dim
    c, s = jnp.tile(c, (1, reps)), jnp.tile(s, (1, reps))
    return (jnp.concatenate([c, jnp.ones((ROW_TILE, LANES), F32)], axis=0),
            jnp.concatenate([s, jnp.zeros((ROW_TILE, LANES), F32)], axis=0))


MERGE_TILE = 256


def _merge_body(h_ref, b0_ref, b1_ref, b2_ref, b3_ref, wg_ref, wb_ref, o_ref):
    h = h_ref[...]
    acc = None
    for j, b_ref in enumerate((b0_ref, b1_ref, b2_ref, b3_ref)):
        gate = jax.nn.sigmoid(jnp.dot(h, wg_ref[j], preferred_element_type=F32))
        term = gate * jnp.dot(b_ref[...].astype(BF16), wb_ref[j], preferred_element_type=F32)
        acc = term if acc is None else acc + term
    o_ref[...] = acc.astype(o_ref.dtype)


def merge_gated(h, branches, w_gate, w_branch, rows):
    d = h.shape[1]
    nb, bw, _ = w_branch.shape
    assert nb == len(branches) == 4
    row = lambda i: (i, 0)
    whole = lambda i: (0, 0, 0)
    return pl.pallas_call(
        _merge_body,
        grid=(rows // MERGE_TILE,),
        in_specs=[pl.BlockSpec((MERGE_TILE, d), row)] + [pl.BlockSpec((MERGE_TILE, bw), row)] * nb
                 + [pl.BlockSpec((nb, d, d), whole, pipeline_mode=pl.Buffered(1)),
                    pl.BlockSpec((nb, bw, d), whole, pipeline_mode=pl.Buffered(1))],
        out_specs=pl.BlockSpec((MERGE_TILE, d), row),
        out_shape=jax.ShapeDtypeStruct((rows, d), BF16),
        compiler_params=pltpu.CompilerParams(dimension_semantics=("parallel",), vmem_limit_bytes=VMEM_LIMIT),
        name="merge_gated",
    )(h, *branches, w_gate, w_branch)


OUT_TILE = 256
ROUTER_PAD = 128


def _outproj_body(m_ref, w_ref, x_ref, g2_ref, sh_ref, sc_ref, lg_ref, lb_ref, rw_ref, rb_ref,
                  x1_ref, hm_ref, lo_ref):
    mix = jnp.dot(m_ref[...], w_ref[...], preferred_element_type=F32)
    x1 = _ln(DEEPNORM_ALPHA * x_ref[...] + g2_ref[...] * mix) * lg_ref[...] + lb_ref[...]
    x1_ref[...] = x1
    hm = _ln(x1) * (1.0 + sc_ref[...]) + sh_ref[...]
    hm_hi = hm.astype(BF16)
    hm_ref[...] = hm_hi
    hm_lo = (hm - hm_hi.astype(F32)).astype(BF16)
    hh = jnp.dot(hm_hi, rw_ref[...], preferred_element_type=F32)
    lh = jnp.dot(hm_lo, rw_ref[:, :ROUTER_PAD], preferred_element_type=F32)
    lo_ref[...] = hh[:, :ROUTER_PAD] + hh[:, ROUTER_PAD:] + lh + rb_ref[...]


def outproj_postnorm(merged, w_out, x, gate2, shift3, scale4, ln_g, ln_b, router_w, router_b,
                     tiles_per_sample, batch):
    rows, d = merged.shape
    gmap = _group_map(tiles_per_sample * (ROW_TILE // OUT_TILE), batch)
    row = lambda i: (i, 0)
    const = lambda i: (0, 0)
    return pl.pallas_call(
        _outproj_body,
        grid=(rows // OUT_TILE,),
        in_specs=[pl.BlockSpec((OUT_TILE, d), row), pl.BlockSpec((d, d), const), pl.BlockSpec((OUT_TILE, d), row),
                  pl.BlockSpec((None, 1, d), gmap), pl.BlockSpec((None, 1, d), gmap), pl.BlockSpec((None, 1, d), gmap),
                  pl.BlockSpec((1, d), const), pl.BlockSpec((1, d), const),
                  pl.BlockSpec((d, 2 * ROUTER_PAD), const), pl.BlockSpec((1, ROUTER_PAD), const)],
        out_specs=[pl.BlockSpec((OUT_TILE, d), row), pl.BlockSpec((OUT_TILE, d), row),
                   pl.BlockSpec((OUT_TILE, ROUTER_PAD), row)],
        out_shape=[jax.ShapeDtypeStruct((rows, d), F32), jax.ShapeDtypeStruct((rows, d), BF16),
                   jax.ShapeDtypeStruct((rows, ROUTER_PAD), F32)],
        compiler_params=pltpu.CompilerParams(dimension_semantics=("parallel",), vmem_limit_bytes=VMEM_LIMIT),
        name="outproj_postnorm",
    )(merged, w_out, x, gate2, shift3, scale4, ln_g, ln_b, router_w, router_b)


GATTN_TQ = 512
GATTN_TK = 256
GATTN_UNROLL = 16
VMEM_LIMIT = 56 * 1024 * 1024
LANES = 128
LOG2E = math.log2(math.e)


def _gattn_body(q_ref, kc_ref, vc_ref, k_ref, v_ref, o_ref, m_ref, acc_ref, *, c2):
    tq = q_ref.shape[0]
    hd = k_ref.shape[1]
    q2 = jnp.concatenate([q_ref[:, :hd], q_ref[:, hd:]], axis=0)
    m_ref[...] = jnp.full(m_ref.shape, -jnp.inf, F32)
    acc_ref[...] = jnp.zeros(acc_ref.shape, F32)

    def step(k, v):
        s = lax.dot_general(q2, k, (((1,), (1,)), ((), ())), preferred_element_type=F32) * c2
        m_old = m_ref[...]
        m_new = jnp.maximum(m_old, jnp.max(s, axis=-1, keepdims=True))
        alpha = jnp.exp2(m_old - m_new)
        p = jnp.concatenate([jnp.exp2(s[:, j * LANES:(j + 1) * LANES] - m_new)
                             for j in range(k.shape[0] // LANES)], axis=1)
        acc_ref[...] = (jnp.concatenate([alpha, alpha], axis=1) * acc_ref[...]
                        + jnp.dot(p.astype(BF16), v, preferred_element_type=F32))
        m_ref[...] = m_new

    step(kc_ref[...], vc_ref[...])

    def loop(c, carry):
        off = pl.multiple_of(c * GATTN_TK, GATTN_TK)
        step(k_ref[pl.ds(off, GATTN_TK), :], v_ref[pl.ds(off, GATTN_TK), :])
        return carry

    n_chunks = k_ref.shape[0] // GATTN_TK
    lax.fori_loop(0, n_chunks, loop, 0, unroll=math.gcd(GATTN_UNROLL, n_chunks))
    acc = acc_ref[...]
    out = acc[:, :hd] / acc[:, hd:]
    o_ref[...] = jnp.concatenate([out[:tq], out[tq:]], axis=1).astype(o_ref.dtype)


def _with_ones(v, n_heads):
    r = v.shape[0]
    v3 = v.reshape(r, n_heads, -1)
    return jnp.concatenate([v3, jnp.ones_like(v3)], axis=-1).reshape(r, -1)


def global_attention(q, k, v, k_ctx, v_ctx, batch):
    hd = GQA_HEAD_DIM
    n = q.shape[0] // batch
    n_c = k_ctx.shape[0] // batch
    g = GQA_HEADS // GQA_KV_HEADS
    tq = min(GATTN_TQ, n)
    assert g == 2 and hd == LANES and n % tq == 0 and n % GATTN_TK == 0 and n_c % LANES == 0
    nq = n // tq
    return pl.pallas_call(
        functools.partial(_gattn_body, c2=hd ** -0.5 * LOG2E),
        grid=(batch, GQA_KV_HEADS, nq),
        in_specs=[pl.BlockSpec((tq, g * hd), lambda b, h, i: (b * nq + i, h)),
                  pl.BlockSpec((n_c, hd), lambda b, h, i: (b, h)),
                  pl.BlockSpec((n_c, 2 * hd), lambda b, h, i: (b, h)),
                  pl.BlockSpec((n, hd), lambda b, h, i: (b, h)),
                  pl.BlockSpec((n, 2 * hd), lambda b, h, i: (b, h))],
        out_specs=pl.BlockSpec((tq, g * hd), lambda b, h, i: (b * nq + i, h)),
        out_shape=jax.ShapeDtypeStruct(q.shape, BF16),
        scratch_shapes=[pltpu.VMEM((g * tq, hd), F32), pltpu.VMEM((g * tq, 2 * hd), F32)],
        compiler_params=pltpu.CompilerParams(dimension_semantics=("parallel", "parallel", "parallel"),
                                             vmem_limit_bytes=VMEM_LIMIT),
        name="gattn",
    )(q, k_ctx, _with_ones(v_ctx, GQA_KV_HEADS), k, _with_ones(v, GQA_KV_HEADS))


WATTN_TQ = 256


def _wattn_body(sink_ref, q_ref, kc_ref, vc_ref, kp_ref, kx_ref, kn_ref, vp_ref, vx_ref, vn_ref, o_ref,
                *, scale, n, nq):
    tq = q_ref.shape[0]
    hd = SWA_HEAD_DIM
    g = SWA_HEADS // SWA_KV_HEADS
    h = pl.program_id(1)
    i = pl.program_id(2)
    k_band = jnp.concatenate([kp_ref[...], kx_ref[...], kn_ref[...]], axis=0)
    v_band = jnp.concatenate([vp_ref[...], vx_ref[...], vn_ref[...]], axis=0)
    qpos = i * tq + lax.broadcasted_iota(jnp.int32, (tq, tq + 2 * WINDOW), 0)
    kpos = i * tq - WINDOW + lax.broadcasted_iota(jnp.int32, (tq, tq + 2 * WINDOW), 1)
    valid = (jnp.abs(kpos - qpos) <= WINDOW) & (kpos >= 0) & (kpos < n)
    outs = []
    for j in range(g):
        q = q_ref[:, j * hd:(j + 1) * hd]
        s_c = lax.dot_general(q, kc_ref[...], (((1,), (1,)), ((), ())), preferred_element_type=F32) * scale
        s_b = lax.dot_general(q, k_band, (((1,), (1,)), ((), ())), preferred_element_type=F32) * scale
        s_b = jnp.where(valid, s_b, -jnp.inf)
        sink = sink_ref[h * g + j]
        m = jnp.maximum(jnp.maximum(jnp.max(s_c, -1, keepdims=True), jnp.max(s_b, -1, keepdims=True)), sink)
        p_c = jnp.exp(s_c - m)
        p_b = jnp.exp(s_b - m)
        l = jnp.sum(p_c, -1, keepdims=True) + jnp.sum(p_b, -1, keepdims=True) + jnp.exp(sink - m)
        o = (jnp.dot(p_c.astype(BF16), vc_ref[...], preferred_element_type=F32)
             + jnp.dot(p_b.astype(BF16), v_band, preferred_element_type=F32))
        outs.append(o / l)
    o_ref[...] = jnp.concatenate(outs, axis=1).astype(o_ref.dtype)


def window_attention(q, k, v, k_ctx, v_ctx, sinks, batch):
    hd = SWA_HEAD_DIM
    n = q.shape[0] // batch
    n_c = k_ctx.shape[0] // batch
    g = SWA_HEADS // SWA_KV_HEADS
    tq = WATTN_TQ
    assert n % tq == 0 and tq % WINDOW == 0
    nq = n // tq
    r = tq // WINDOW
    nw = n // WINDOW
    k_h = k.reshape(batch * n, SWA_KV_HEADS, hd).transpose(1, 0, 2)
    v_h = v.reshape(batch * n, SWA_KV_HEADS, hd).transpose(1, 0, 2)
    kc_h = k_ctx.reshape(batch * n_c, SWA_KV_HEADS, hd).transpose(1, 0, 2)
    vc_h = v_ctx.reshape(batch * n_c, SWA_KV_HEADS, hd).transpose(1, 0, 2)
    prev_map = lambda b, h, i, s: (h, b * nw + jnp.maximum(i * r - 1, 0), 0)
    cur_map = lambda b, h, i, s: (h, b * nq + i, 0)
    next_map = lambda b, h, i, s: (h, b * nw + jnp.minimum((i + 1) * r, nw - 1), 0)
    ctx_map = lambda b, h, i, s: (h, b, 0)
    grid_spec = pltpu.PrefetchScalarGridSpec(
        num_scalar_prefetch=1,
        grid=(batch, SWA_KV_HEADS, nq),
        in_specs=[pl.BlockSpec((tq, g * hd), lambda b, h, i, s: (b * nq + i, h)),
                  pl.BlockSpec((None, n_c, hd), ctx_map),
                  pl.BlockSpec((None, n_c, hd), ctx_map),
                  pl.BlockSpec((None, WINDOW, hd), prev_map),
                  pl.BlockSpec((None, tq, hd), cur_map),
                  pl.BlockSpec((None, WINDOW, hd), next_map),
                  pl.BlockSpec((None, WINDOW, hd), prev_map),
                  pl.BlockSpec((None, tq, hd), cur_map),
                  pl.BlockSpec((None, WINDOW, hd), next_map)],
        out_specs=pl.BlockSpec((tq, g * hd), lambda b, h, i, s: (b * nq + i, h)),
    )
    return pl.pallas_call(
        functools.partial(_wattn_body, scale=hd ** -0.5, n=n, nq=nq),
        grid_spec=grid_spec,
        out_shape=jax.ShapeDtypeStruct(q.shape, BF16),
        compiler_params=pltpu.CompilerParams(dimension_semantics=("parallel", "parallel", "parallel"),
                                             vmem_limit_bytes=VMEM_LIMIT),
        name="wattn",
    )(sinks.astype(F32), q, kc_h, vc_h, k_h, k_h, k_h, v_h, v_h, v_h)


MOE_TB = 512
MOE_FC = 512


def _ffn_body(blk_e_ref, nused_ref, x_ref, wg_ref, wu_ref, bg_ref, bu_ref, wd_ref, bd_ref, o_ref):
    i = pl.program_id(0)

    @pl.when(i < nused_ref[0])
    def _():
        x = x_ref[...]
        acc = jnp.zeros(o_ref.shape, F32) + bd_ref[...]
        for c in range(D_EXPERT // MOE_FC):
            sl = slice(c * MOE_FC, (c + 1) * MOE_FC)
            gate = jnp.dot(x, wg_ref[:, sl], preferred_element_type=F32) + bg_ref[:, sl]
            up = jnp.dot(x, wu_ref[:, sl], preferred_element_type=F32) + bu_ref[:, sl]
            gate = jnp.minimum(gate, SWIGLU_LIMIT)
            up = jnp.clip(up, -SWIGLU_LIMIT, SWIGLU_LIMIT)
            glu = gate * jax.nn.sigmoid(gate * SWIGLU_ALPHA)
            act = ((up + 1.0) * glu).astype(BF16)
            acc = acc + jnp.dot(act, wd_ref[sl, :], preferred_element_type=F32)
        o_ref[...] = acc.astype(o_ref.dtype)

    @pl.when(i >= nused_ref[0])
    def _():
        o_ref[...] = jnp.zeros(o_ref.shape, o_ref.dtype)


def expert_ffn(buf, blk_e, n_used, w_gate, w_up, b_gate, b_up, w_dn_all, b_dn, layer):
    n_rows, d = buf.shape
    n_blk = n_rows // MOE_TB
    f = w_gate.shape[2]
    wmap = lambda i, e, u: (e[i], 0, 0)
    grid_spec = pltpu.PrefetchScalarGridSpec(
        num_scalar_prefetch=2,
        grid=(n_blk,),
        in_specs=[pl.BlockSpec((MOE_TB, d), lambda i, e, u: (i, 0)),
                  pl.BlockSpec((None, d, f), wmap),
                  pl.BlockSpec((None, d, f), wmap),
                  pl.BlockSpec((None, 1, f), wmap),
                  pl.BlockSpec((None, 1, f), wmap),
                  pl.BlockSpec((None, None, f, d), lambda i, e, u: (layer, e[i], 0, 0)),
                  pl.BlockSpec((None, 1, d), wmap)],
        out_specs=pl.BlockSpec((MOE_TB, d), lambda i, e, u: (i, 0)),
    )
    return pl.pallas_call(
        _ffn_body,
        grid_spec=grid_spec,
        out_shape=jax.ShapeDtypeStruct((n_rows, d), BF16),
        compiler_params=pltpu.CompilerParams(dimension_semantics=("arbitrary",),
                                             vmem_limit_bytes=VMEM_LIMIT),
        name="expert_ffn",
    )(blk_e, n_used, buf, w_gate, w_up, b_gate, b_up, w_dn_all, b_dn)


SPLIT_ROWS = 512
WEIGHT_STREAMS = 4


def _deint_body(*refs):
    w_refs, (p_ref, g_ref, u_ref) = refs[:WEIGHT_STREAMS], refs[WEIGHT_STREAMS:]
    blk = p_ref.shape[0]
    rows = w_refs[0].shape[0]
    for q, w_ref in enumerate(w_refs):
        rs = slice(q * rows, (q + 1) * rows)
        for c in range(w_ref.shape[1] // blk):
            y = jnp.dot(w_ref[:, c * blk:(c + 1) * blk].astype(BF16), p_ref[...], preferred_element_type=F32)
            g_ref[rs, c * LANES:(c + 1) * LANES] = y[:, :LANES].astype(BF16)
            u_ref[rs, c * LANES:(c + 1) * LANES] = y[:, LANES:].astype(BF16)


def split_gate_up(w_gu_all, layer):
    _, e, d, f2 = w_gu_all.shape
    blk = 2 * LANES
    src = jnp.arange(blk)[:, None]
    dst = jnp.arange(blk)[None, :]
    perm = (src == jnp.where(dst < LANES, 2 * dst, 2 * (dst - LANES) + 1)).astype(BF16)
    out = jax.ShapeDtypeStruct((e, d, f2 // 2), BF16)
    sub = SPLIT_ROWS // WEIGHT_STREAMS
    stream = lambda q: pl.BlockSpec((None, None, sub, f2), lambda i, j: (layer, i, WEIGHT_STREAMS * j + q, 0))
    return pl.pallas_call(
        _deint_body,
        grid=(e, d // SPLIT_ROWS),
        in_specs=[stream(q) for q in range(WEIGHT_STREAMS)] + [pl.BlockSpec((blk, blk), lambda i, j: (0, 0))],
        out_specs=[pl.BlockSpec((None, SPLIT_ROWS, f2 // 2), lambda i, j: (i, j, 0)),
                   pl.BlockSpec((None, SPLIT_ROWS, f2 // 2), lambda i, j: (i, j, 0))],
        out_shape=[out, out],
        compiler_params=pltpu.CompilerParams(dimension_semantics=("parallel", "parallel"),
                                             vmem_limit_bytes=VMEM_LIMIT),
        name="split_gate_up",
    )(*([w_gu_all] * WEIGHT_STREAMS), perm)


def _cast_body(*refs):
    w_refs, o_ref = refs[:WEIGHT_STREAMS], refs[WEIGHT_STREAMS]
    rows = w_refs[0].shape[0]
    for q, w_ref in enumerate(w_refs):
        o_ref[q * rows:(q + 1) * rows, :] = w_ref[...].astype(o_ref.dtype)


def cast_weights_bf16(w):
    nl, e, r, c = w.shape
    sub = r // WEIGHT_STREAMS
    assert r % (WEIGHT_STREAMS * 16) == 0
    stream = lambda q: pl.BlockSpec((None, None, sub, c), lambda l, i: (l, i, q, 0))
    return pl.pallas_call(
        _cast_body,
        grid=(nl, e),
        in_specs=[stream(q) for q in range(WEIGHT_STREAMS)],
        out_specs=pl.BlockSpec((None, None, r, c), lambda l, i: (l, i, 0, 0)),
        out_shape=jax.ShapeDtypeStruct(w.shape, BF16),
        compiler_params=pltpu.CompilerParams(dimension_semantics=("parallel", "parallel"),
                                             vmem_limit_bytes=VMEM_LIMIT),
        name="cast_weights_bf16",
    )(*([w] * WEIGHT_STREAMS))


FFT_N2 = 128
FFT_SPLIT_MIN = 1024


def _angles(num, den):
    return (2.0 * math.pi / den) * (num % den).astype(F32)


def dft_stage_a(n1, k_in, real_input):
    ang = _angles(jnp.arange(n1)[:, None] * jnp.arange(k_in)[None, :], n1)
    c, s = jnp.cos(ang), jnp.sin(ang)
    if real_input:
        return jnp.concatenate([c, -s], axis=0).astype(BF16)
    return jnp.block([[c, s], [-s, c]]).astype(BF16)


def dft_stage_a_inv(n1, k_out):
    ang = _angles(jnp.arange(k_out)[:, None] * jnp.arange(n1)[None, :], n1)
    c, s = jnp.cos(ang), jnp.sin(ang)
    return jnp.block([[c, -s], [s, c]]).astype(BF16)


def dft_stage_b(n1, n2):
    k1 = jnp.arange(n1)[:, None, None]
    k2 = jnp.arange(n2)[None, :, None]
    m = jnp.arange(n2)[None, None, :]
    ang = _angles(m * (k1 + n1 * k2), n1 * n2)
    c, s = jnp.cos(ang), jnp.sin(ang)
    top = jnp.concatenate([c, s], axis=2)
    bot = jnp.concatenate([-s, c], axis=2)
    return jnp.concatenate([top, bot], axis=1).astype(BF16)


SUBLANES = 8


def _lmm_body(m_ref, x_ref, o_ref):
    m = m_ref[...]
    k, sub, lanes = x_ref.shape
    r = o_ref.shape[0]
    x2 = x_ref.reshape(k * sub, lanes)
    o2 = o_ref.reshape(r * sub, lanes)
    for s in range(sub):
        xs = x2[pl.ds(s, k, stride=sub), :].astype(BF16)
        o2[pl.ds(s, r, stride=sub), :] = jnp.dot(m, xs, preferred_element_type=F32)


def left_matmul(m, x, n2):
    g, rows, c = x.shape
    k = rows // n2
    r = m.shape[0]
    assert m.shape[1] == k and n2 % SUBLANES == 0 and c % LANES == 0
    x5 = x.reshape(g, k, n2 // SUBLANES, SUBLANES, c)
    out = pl.pallas_call(
        _lmm_body,
        grid=(g, n2 // SUBLANES, c // LANES),
        in_specs=[pl.BlockSpec((r, k), lambda i, j, l: (0, 0)),
                  pl.BlockSpec((None, k, None, SUBLANES, LANES), lambda i, j, l: (i, 0, j, 0, l))],
        out_specs=pl.BlockSpec((None, r, None, SUBLANES, LANES), lambda i, j, l: (i, 0, j, 0, l)),
        out_shape=jax.ShapeDtypeStruct((g, r, n2 // SUBLANES, SUBLANES, c), F32),
        compiler_params=pltpu.CompilerParams(dimension_semantics=("parallel", "parallel", "parallel"),
                                             vmem_limit_bytes=VMEM_LIMIT),
        name="left_matmul",
    )(m, x5)
    return out.reshape(g, r * n2, c)


def _lmm_gate_body(m_ref, x_ref, a_ref, z_ref, b_ref, o_ref, *, scale):
    m = m_ref[...]
    k, sub, lanes = x_ref.shape
    r = o_ref.shape[0]
    x2 = x_ref.reshape(k * sub, lanes)
    a2 = a_ref.reshape(r * sub, lanes)
    z2 = z_ref.reshape(r * sub, lanes)
    o2 = o_ref.reshape(r * sub, lanes)
    for s in range(sub):
        rows = pl.ds(s, r, stride=sub)
        y = jnp.dot(m, x2[pl.ds(s, k, stride=sub), :].astype(BF16), preferred_element_type=F32)
        o2[rows, :] = a2[rows, :] * (y * scale + z2[rows, :] * b_ref[...])


def left_matmul_gated(m, x, n2, gate, z, bias, scale):
    g, rows, c = x.shape
    k = rows // n2
    r = m.shape[0]
    assert m.shape[1] == k and gate.shape == (g, r * n2, c) and z.shape == gate.shape
    view = lambda t, lead: t.reshape(g, lead, n2 // SUBLANES, SUBLANES, c)
    blk = lambda lead: pl.BlockSpec((None, lead, None, SUBLANES, LANES), lambda i, j, l: (i, 0, j, 0, l))
    out = pl.pallas_call(
        functools.partial(_lmm_gate_body, scale=scale),
        grid=(g, n2 // SUBLANES, c // LANES),
        in_specs=[pl.BlockSpec((r, k), lambda i, j, l: (0, 0)), blk(k), blk(r), blk(r),
                  pl.BlockSpec((1, LANES), lambda i, j, l: (0, l))],
        out_specs=blk(r),
        out_shape=jax.ShapeDtypeStruct((g, r, n2 // SUBLANES, SUBLANES, c), F32),
        compiler_params=pltpu.CompilerParams(dimension_semantics=("parallel", "parallel", "parallel"),
                                             vmem_limit_bytes=VMEM_LIMIT),
        name="left_matmul_gated",
    )(m, view(x, k), view(gate, r), view(z, r), bias.astype(F32)[None, :])
    return out.reshape(g, r * n2, c)


def _sconv_body(p_ref, x_ref, n_ref, w_ref, b_ref, o1_ref, o2_ref, o3_ref, *, tiles_per_sample):
    i = pl.program_id(0)
    pos = i % tiles_per_sample
    x = x_ref[...]
    rows = x.shape[0]
    before = jnp.where(pos == 0, 0.0, p_ref[SUBLANES - 1:SUBLANES, :])
    after = jnp.where(pos == tiles_per_sample - 1, 0.0, n_ref[0:1, :])
    ridx = lax.broadcasted_iota(jnp.int32, x.shape, 0)
    up = jnp.where(ridx == 0, before, pltpu.roll(x, 1, 0))
    dn = jnp.where(ridx == rows - 1, after, pltpu.roll(x, rows - 1, 0))
    y = up * w_ref[0:1, :] + x * w_ref[1:2, :] + dn * w_ref[2:3, :] + b_ref[...]
    w = o1_ref.shape[1]
    o1_ref[...] = y[:, :w]
    o2_ref[...] = y[:, w:2 * w]
    o3_ref[...] = y[:, 2 * w:]


def short_conv_split(z, col_blocks, conv_w, conv_b, tiles_per_sample, batch):
    t, _ = z.shape
    wid = conv_w.shape[1]
    n_rows = batch * tiles_per_sample * ROW_TILE
    per = ROW_TILE // SUBLANES
    last = t // SUBLANES - 1
    out = jax.ShapeDtypeStruct((n_rows, wid // 3), F32)
    return pl.pallas_call(
        functools.partial(_sconv_body, tiles_per_sample=tiles_per_sample),
        grid=(batch * tiles_per_sample,),
        in_specs=[pl.BlockSpec((SUBLANES, wid), lambda i: (jnp.maximum(i * per - 1, 0), col_blocks)),
                  pl.BlockSpec((ROW_TILE, wid), lambda i: (i, col_blocks)),
                  pl.BlockSpec((SUBLANES, wid), lambda i: (jnp.minimum((i + 1) * per, last), col_blocks)),
                  pl.BlockSpec((3, wid), lambda i: (0, 0)), pl.BlockSpec((1, wid), lambda i: (0, 0))],
        out_specs=[pl.BlockSpec((ROW_TILE, wid // 3), lambda i: (i, 0))] * 3,
        out_shape=[out, out, out],
        compiler_params=pltpu.CompilerParams(dimension_semantics=("parallel",), vmem_limit_bytes=VMEM_LIMIT),
        name="short_conv_split",
    )(z, z, z, conv_w.astype(F32), conv_b.astype(F32)[None, :])


def _conv_b_body(t_ref, gs_ref, h_ref, u_ref):
    n2 = t_ref.shape[1]
    c = t_ref.shape[2]
    gs = gs_ref[...]
    y = jnp.dot(gs, t_ref[...].reshape(2 * n2, c).astype(BF16), preferred_element_type=F32)
    yr, yi = y[:n2], y[n2:]
    hr, hi = h_ref[0], h_ref[1]
    z = jnp.concatenate([yr * hr - yi * hi, yr * hi + yi * hr], axis=0).astype(BF16)
    u = lax.dot_general(gs, z, (((0,), (0,)), ((), ())), preferred_element_type=F32)
    u_ref[...] = u.reshape(2, n2, c)


def conv_stage_b(t, gs, h):
    _, n1, n2, c = t.shape
    spec = pl.BlockSpec((2, None, n2, c), lambda i: (0, i, 0, 0))
    return pl.pallas_call(
        _conv_b_body,
        grid=(n1,),
        in_specs=[spec, pl.BlockSpec((None, 2 * n2, 2 * n2), lambda i: (i, 0, 0)), spec],
        out_specs=spec,
        out_shape=jax.ShapeDtypeStruct(t.shape, F32),
        compiler_params=pltpu.CompilerParams(dimension_semantics=("parallel",), vmem_limit_bytes=VMEM_LIMIT),
        name="conv_stage_b",
    )(t, gs, h)


def _filt_b_body(t_ref, gs_ref, ss_ref, h_ref):
    n2 = t_ref.shape[1]
    c = t_ref.shape[2]
    w = HYENA_WIDTH
    y = jnp.dot(gs_ref[...], t_ref[...].reshape(2 * n2, c).astype(BF16), preferred_element_type=F32)
    yr, yi = y[:n2], y[n2:]
    ss = ss_ref[...]
    for o in range(HYENA_ORDER):
        f0, b0 = (2 * o) * w, (2 * o + 1) * w
        scale = lax.rsqrt(ss[:, f0:f0 + w] + ss[:, b0:b0 + w] + 1e-6)
        h_ref[o, 0] = (yr[:, f0:f0 + w] + yr[:, b0:b0 + w]) * scale
        h_ref[o, 1] = (yi[:, f0:f0 + w] - yi[:, b0:b0 + w]) * scale


def filter_stage_b(t, gs, sumsq):
    _, n1, n2, c = t.shape
    w = HYENA_WIDTH
    return pl.pallas_call(
        _filt_b_body,
        grid=(n1,),
        in_specs=[pl.BlockSpec((2, None, n2, c), lambda i: (0, i, 0, 0)),
                  pl.BlockSpec((None, 2 * n2, 2 * n2), lambda i: (i, 0, 0)),
                  pl.BlockSpec((1, c), lambda i: (0, 0))],
        out_specs=pl.BlockSpec((HYENA_ORDER, 2, None, n2, w), lambda i: (0, 0, i, 0, 0)),
        out_shape=jax.ShapeDtypeStruct((HYENA_ORDER, 2, n1, n2, w), F32),
        compiler_params=pltpu.CompilerParams(dimension_semantics=("parallel",), vmem_limit_bytes=VMEM_LIMIT),
        name="filter_stage_b",
    )(t, gs, sumsq)


def _fnet_b_body(t_ref, gs_ref, cs_ref, o_ref):
    n2 = t_ref.shape[1]
    c = t_ref.shape[2]
    y = jnp.dot(gs_ref[...], t_ref[...].reshape(2 * n2, c).astype(BF16), preferred_element_type=F32)
    yr, yi = y[:n2].astype(BF16), y[n2:].astype(BF16)
    gd = FNET_GROUP_DIM
    outs = []
    for g in range(c // gd):
        v = jnp.concatenate([yr[:, g * gd:(g + 1) * gd], yi[:, g * gd:(g + 1) * gd]], axis=1)
        outs.append(jnp.dot(v, cs_ref[...], preferred_element_type=F32))
    o_ref[...] = jnp.concatenate(outs, axis=1).astype(o_ref.dtype)


def fnet_stage_b(t, gs, cs, out_dtype):
    b, _, n1, n2, c = t.shape
    return pl.pallas_call(
        _fnet_b_body,
        grid=(b, n1),
        in_specs=[pl.BlockSpec((None, 2, None, n2, c), lambda i, j: (i, 0, j, 0, 0)),
                  pl.BlockSpec((None, 2 * n2, 2 * n2), lambda i, j: (j, 0, 0)),
                  pl.BlockSpec(cs.shape, lambda i, j: (0, 0))],
        out_specs=pl.BlockSpec((None, n2, c), lambda i, j: (i, 0, j)),
        out_shape=jax.ShapeDtypeStruct((b, n2, n1 * c), out_dtype),
        compiler_params=pltpu.CompilerParams(dimension_semantics=("parallel", "parallel"),
                                             vmem_limit_bytes=VMEM_LIMIT),
        name="fnet_stage_b",
    )(t, gs, cs)


def fourier_mix_pallas(u):
    b, n, c = u.shape
    gd = FNET_GROUP_DIM
    n2 = FFT_N2 if n >= FFT_SPLIT_MIN else n
    n1 = n // n2
    gs = dft_stage_b(n1, n2)
    if n1 > 1:
        t = left_matmul(dft_stage_a(n1, n1, True), u.astype(F32), n2).reshape(b, 2, n1, n2, c)
    else:
        t = jnp.stack([u.astype(F32), jnp.zeros(u.shape, F32)], axis=1).reshape(b, 2, 1, n2, c)
    ang = _angles(jnp.arange(gd)[:, None] * jnp.arange(gd)[None, :], gd)
    cs = (jnp.concatenate([jnp.cos(ang), jnp.sin(ang)], axis=0) * (n * gd) ** -0.5).astype(BF16)
    out = fnet_stage_b(t, gs, cs, u.dtype)
    return out.reshape(b, n2, n1, c).reshape(b, n, c)


FILT_TM = 512
FILT_LANES = 128


def _filter_body(ff_ref, w1_ref, b1_ref, fr1_ref, w2_ref, b2_ref, fr2_ref, w3_ref, dl_ref, taps_ref, ss_ref, *, n):
    i = pl.program_id(0)
    tm = taps_ref.shape[0]
    hi = lax.Precision.HIGHEST
    pos = (i * tm + lax.broadcasted_iota(jnp.int32, (tm, FILT_LANES), 0)).astype(F32)
    lane = lax.broadcasted_iota(jnp.int32, (tm, FILT_LANES), 1)
    t = pos / (n - 1.0) if n > 1 else pos * 0.0
    omega = (2.0 * math.pi) * pos / n
    arg = ff_ref[...] * omega
    bands = (HYENA_EMB_DIM - 1) // 2
    z = jnp.where(lane == 0, t, jnp.where(lane <= bands, jnp.cos(arg),
                                          jnp.where(lane <= 2 * bands, -jnp.sin(arg), 0.0)))
    h = jnp.sin(fr1_ref[...] * (jnp.dot(z, w1_ref[...], precision=hi, preferred_element_type=F32) + b1_ref[...]))
    h = jnp.sin(fr2_ref[...] * (jnp.dot(h, w2_ref[...], precision=hi, preferred_element_type=F32) + b2_ref[...]))
    h = jnp.dot(h, w3_ref[...], precision=hi, preferred_element_type=F32)
    width = h.shape[1]
    h = h * jnp.exp(-t[:, :1] * dl_ref[...])
    row = i * tm + lax.broadcasted_iota(jnp.int32, (tm, width), 0)
    col = lax.broadcasted_iota(jnp.int32, (tm, width), 1)
    h = jnp.where((row == 0) & ((col // HYENA_WIDTH) % 2 == 1), 0.0, h)
    taps_ref[...] = h

    @pl.when(i == 0)
    def _():
        ss_ref[...] = jnp.zeros(ss_ref.shape, F32)

    ss_ref[...] += jnp.sum(h * h, axis=0, keepdims=True)


def hyena_taps(n, w1, b1, fr1, w2, b2, fr2, w3):
    bands = (HYENA_EMB_DIM - 1) // 2
    fd = w1.shape[1]
    width = w3.shape[1]
    freqs = jnp.linspace(1e-4, bands - 1, bands, dtype=F32)
    featfreq = jnp.zeros((1, FILT_LANES), F32).at[0, 1:1 + bands].set(freqs).at[0, 1 + bands:1 + 2 * bands].set(freqs)
    w1p = jnp.zeros((FILT_LANES, fd), F32).at[:HYENA_EMB_DIM].set(w1.astype(F32))
    max_decay = math.log(HYENA_DECAY_TARGET) / HYENA_SHORT_DECAY_PCT
    min_decay = math.log(HYENA_DECAY_TARGET) / HYENA_LONG_DECAY_PCT
    deltas = jnp.abs(jnp.linspace(min_decay, max_decay, HYENA_WIDTH, dtype=F32))
    dl = jnp.tile(deltas, width // HYENA_WIDTH)[None, :]
    tm = min(FILT_TM, n)
    full = lambda a: pl.BlockSpec(a.shape, lambda i: (0,) * a.ndim)
    ops = (featfreq, w1p, b1.astype(F32)[None, :], fr1.astype(F32)[None, :], w2.astype(F32),
           b2.astype(F32)[None, :], fr2.astype(F32)[None, :], w3.astype(F32), dl)
    return pl.pallas_call(
        functools.partial(_filter_body, n=n),
        grid=(n // tm,),
        in_specs=[full(a) for a in ops],
        out_specs=[pl.BlockSpec((tm, width), lambda i: (i, 0)), pl.BlockSpec((1, width), lambda i: (0, 0))],
        out_shape=[jax.ShapeDtypeStruct((n, width), F32), jax.ShapeDtypeStruct((1, width), F32)],
        compiler_params=pltpu.CompilerParams(dimension_semantics=("arbitrary",), vmem_limit_bytes=VMEM_LIMIT),
        name="hyena_taps",
    )(*ops)


def _fft_split(n):
    if n >= FFT_SPLIT_MIN:
        return 2 * n // FFT_N2, FFT_N2, n // FFT_N2
    return 1, 2 * n, 1


def hyena_mix_pallas(x1, x2, v, filt, bias):
    batch, n, w = v.shape
    assert batch == 2
    n1, n2, k1 = _fft_split(n)
    gs = dft_stage_b(n1, n2)
    taps, sumsq = hyena_taps(n, *filt)
    if n1 > 1:
        t = left_matmul(dft_stage_a(n1, k1, True), taps[None], n2).reshape(2, n1, n2, taps.shape[1])
    else:
        t = jnp.stack([jnp.pad(taps, ((0, n), (0, 0))), jnp.zeros((2 * n, taps.shape[1]), F32)])[:, None]
    spec = filter_stage_b(t, gs, sumsq)
    fwd = dft_stage_a(n1, k1, False)
    inv = dft_stage_a_inv(n1, k1)

    def gated_long_conv(gate, z, order):
        scale = 1.0 / (2 * n)
        if n1 > 1:
            t = left_matmul(fwd, z.reshape(1, 2 * n, w), n2).reshape(2, n1, n2, w)
            uu = conv_stage_b(t, gs, spec[order])
            return left_matmul_gated(inv, uu.reshape(1, 2 * n1 * n2, w), n2, gate.reshape(1, 2 * n, w),
                                     z.reshape(1, 2 * n, w), bias[order], scale).reshape(2, n, w)
        t = jnp.pad(z, ((0, 0), (0, n), (0, 0)))[:, None]
        y = conv_stage_b(t, gs, spec[order])[:, 0, :n]
        return gate * (y * scale + z * bias[order].astype(F32))

    return gated_long_conv(x2, gated_long_conv(x1, v, 0), 1)


def heads(z, n_heads):
    b, n, w = z.shape
    return z.reshape(b, n, n_heads, w // n_heads)


def group_heads(q, n_kv):
    b, n, h, hd = q.shape
    return q.reshape(b, n, n_kv, h // n_kv, hd)


def flat_heads(o):
    return o.reshape(o.shape[0], o.shape[1], -1)


def axial_rope(rows, head_dim):
    n_freq = head_dim // 4
    inv_freq = jnp.power(ROPE_THETA, -jnp.arange(n_freq, dtype=F32) / n_freq)
    t = jnp.arange(rows * GRID_W)
    row = (t // GRID_W).astype(F32)
    col = (t % GRID_W).astype(F32)
    ang = jnp.stack([row[:, None] * inv_freq, col[:, None] * inv_freq], axis=1)
    return jnp.cos(ang), jnp.sin(ang)


def softmax_attend(q, k, v, sinks=None):
    scale = q.shape[-1] ** -0.5
    s = jnp.einsum("bqkgd,bskd->bkgqs", q, k, preferred_element_type=F32) * scale
    if sinks is None:
        p = jax.nn.softmax(s, axis=-1)
    else:
        sink_col = jnp.broadcast_to(sinks.astype(F32)[None, :, :, None, None], s.shape[:-1] + (1,))
        p = jax.nn.softmax(jnp.concatenate([sink_col, s], axis=-1), axis=-1)[..., 1:]
    return jnp.einsum("bkgqs,bskd->bqkgd", p.astype(v.dtype), v)


def short_conv(u, w, bias):
    n = u.shape[1]
    up = jnp.pad(u, ((0, 0), (1, 1), (0, 0)))
    return up[:, :n] * w[0] + up[:, 1:n + 1] * w[1] + up[:, 2:] * w[2] + bias


def moe_ffn(tokens, logits, w_pack, layer):
    n_tok, d = tokens.shape
    i32 = jnp.int32
    top_v, top_i = lax.top_k(logits, TOP_K)
    weights = jax.nn.softmax(top_v, axis=-1)
    flat_e = top_i.reshape(-1).astype(i32)
    n_asg = n_tok * TOP_K
    e_ids = jnp.arange(N_EXPERTS, dtype=i32)
    onehot = (flat_e[:, None] == e_ids[None, :]).astype(i32)
    csum = jnp.cumsum(onehot, axis=0)
    rank = jnp.sum(onehot * csum, axis=1) - 1
    counts = csum[-1]
    padded = (counts + MOE_TB - 1) // MOE_TB * MOE_TB
    padded_end = jnp.cumsum(padded)
    padded_start = padded_end - padded
    dest = jnp.sum(onehot * padded_start[None, :], axis=1) + rank
    n_rows = n_asg + N_EXPERTS * MOE_TB
    n_blk = n_rows // MOE_TB
    blk_start = jnp.arange(n_blk, dtype=i32) * MOE_TB
    blk_e = jnp.minimum(jnp.sum((blk_start[:, None] >= padded_end[None, :]).astype(i32), axis=1), N_EXPERTS - 1)
    n_used = (padded_end[-1:] // MOE_TB).astype(i32)
    pad_need = (padded - counts)[:, None]
    pad_key = jnp.where(jnp.arange(MOE_TB, dtype=i32)[None, :] < pad_need, 2 * e_ids[:, None] + 1, 2 * N_EXPERTS)
    keys = jnp.concatenate([2 * flat_e, pad_key.reshape(-1)])
    vals = jnp.concatenate([jnp.arange(n_asg, dtype=i32) // TOP_K, jnp.zeros((N_EXPERTS * MOE_TB,), i32)])
    _, src = lax.sort_key_val(keys, vals, is_stable=True)
    buf = tokens[src]
    y = expert_ffn(buf, blk_e, n_used, *w_pack, layer)
    dest = dest.reshape(n_tok, TOP_K)
    return [y[dest[:, k]] for k in range(TOP_K)], weights


def prep_expert_weights(w_gu_all, b_gu, w_dn_all, b_dn, layer):
    packed = tuple(split_gate_up(w_gu_all, layer)) + (
        b_gu[:, None, 0::2].astype(F32), b_gu[:, None, 1::2].astype(F32),
        w_dn_all, b_dn[:, None, :].astype(F32))
    return packed


def _combine_body(x_ref, y0_ref, y1_ref, y2_ref, y3_ref, w_ref, g5_ref, lg_ref, lb_ref, o_ref):
    w = w_ref[...]
    y = sum(w[:, k:k + 1] * y_ref[...].astype(F32) for k, y_ref in enumerate((y0_ref, y1_ref, y2_ref, y3_ref)))
    o_ref[...] = _ln(DEEPNORM_ALPHA * x_ref[...] + g5_ref[...] * y) * lg_ref[...] + lb_ref[...]


def combine_postnorm(x1, ys, weights, gate5, ln_g, ln_b, tiles_per_sample, batch):
    rows, d = x1.shape
    gmap = _group_map(tiles_per_sample * (ROW_TILE // OUT_TILE), batch)
    row = lambda i: (i, 0)
    const = lambda i: (0, 0)
    return pl.pallas_call(
        _combine_body,
        grid=(rows // OUT_TILE,),
        in_specs=[pl.BlockSpec((OUT_TILE, d), row)] + [pl.BlockSpec((OUT_TILE, d), row)] * TOP_K
                 + [pl.BlockSpec((OUT_TILE, TOP_K), row), pl.BlockSpec((None, 1, d), gmap),
                    pl.BlockSpec((1, d), const), pl.BlockSpec((1, d), const)],
        out_specs=pl.BlockSpec((OUT_TILE, d), row),
        out_shape=jax.ShapeDtypeStruct((rows, d), F32),
        compiler_params=pltpu.CompilerParams(dimension_semantics=("parallel",), vmem_limit_bytes=VMEM_LIMIT),
        name="combine_postnorm",
    )(x1, *ys, weights, gate5, ln_g, ln_b)


def kernel(x, c, ctx, c_ctx, mod_w, mod_b, w_in, gqa_q_gain, gqa_k_gain, conv_w, conv_b,
           filt_w1, filt_b1, filt_freq1, filt_w2, filt_b2, filt_freq2, filt_w3, hyena_bias,
           swa_sinks, w_branch, w_gate, w_out, ln1_g, ln1_b, router_w, router_b,
           exp_w_gate_up, exp_b_gate_up, exp_w_down, exp_b_down, ln2_g, ln2_b):
    batch, n_lat, d = x.shape
    n_ctx = ctx.shape[1]
    rows = n_lat // GRID_W
    rope_gqa = rope_tables(rows, GQA_HEAD_DIM)
    rope_swa = rope_tables(rows, SWA_HEAD_DIM)
    n_l, n_c = batch * n_lat, batch * n_ctx
    assert n_lat % ROW_TILE == 0 and n_c % ROW_TILE == 0
    tps = n_lat // ROW_TILE
    xs = jnp.concatenate([x.reshape(n_l, d), ctx.reshape(n_c, d)], axis=0)
    cond = jnp.concatenate([c, c_ctx[None, :]], axis=0)
    w_down_b = cast_weights_bf16(exp_w_down)
    for l in range(DEPTH):
        ctx_continues = l < DEPTH - 1
        m_rows = n_l + n_c if ctx_continues else n_l
        mods = (jax.nn.silu(cond) @ mod_w[l] + mod_b[l]).reshape(batch + 1, 6, 1, d).swapaxes(0, 1)
        filt = (filt_w1[l], filt_b1[l], filt_freq1[l], filt_w2[l], filt_b2[l], filt_freq2[l], filt_w3[l])
        sinks = swa_sinks[l].reshape(SWA_KV_HEADS, SWA_HEADS // SWA_KV_HEADS)

        h, fa, gq, gk, gv, hy, sq, sk, sv = in_projection(
            xs, mods[0], mods[1], w_in[l].astype(BF16), rope_gqa, rope_swa, gqa_q_gain[l], gqa_k_gain[l], tps, batch)
        lat3 = lambda t: t[:n_l].reshape(batch, n_lat, -1)
        ctx3 = lambda t: t[n_l:].reshape(batch, n_ctx, -1)
        fa_l, fa_c, hy_c = lat3(fa), ctx3(fa), ctx3(hy)
        hy_parts = [p.reshape(batch, n_lat, -1) for p in short_conv_split(hy, 0, conv_w[l], conv_b[l], tps, batch)]
        ga = global_attention(gq[:n_l], gk[:n_l], gv[:n_l], gk[n_l:], gv[n_l:], batch)
        wa = window_attention(sq[:n_l], sk[:n_l], sv[:n_l], sk[n_l:], sv[n_l:], swa_sinks[l], batch)
        branches = [
            fourier_mix_pallas(fa_l).reshape(n_l, -1),
            ga,
            hyena_mix_pallas(*hy_parts, filt, hyena_bias[l]).reshape(n_l, -1),
            wa,
        ]
        if ctx_continues:
            gq_ctx = group_heads(heads(ctx3(gq), GQA_HEADS), GQA_KV_HEADS)
            sq_ctx = group_heads(heads(ctx3(sq), SWA_HEADS), SWA_KV_HEADS)
            ctx_branches = (
                fourier_mix_pallas(fa_c),
                flat_heads(softmax_attend(gq_ctx, heads(ctx3(gk), GQA_KV_HEADS), heads(ctx3(gv), GQA_KV_HEADS))),
                hyena_mix_pallas(*jnp.split(short_conv(hy_c, conv_w[l], conv_b[l]), 3, axis=-1), filt, hyena_bias[l]),
                flat_heads(softmax_attend(sq_ctx, heads(ctx3(sk), SWA_KV_HEADS), heads(ctx3(sv), SWA_KV_HEADS), sinks)),
            )
            branches = [jnp.concatenate([bl, bc.reshape(n_c, -1).astype(bl.dtype)], axis=0)
                        for bl, bc in zip(branches, ctx_branches)]
        merged = merge_gated(h, branches, w_gate[l].astype(BF16), w_branch[l].astype(BF16), m_rows)
        router_wp = jnp.zeros((d, ROUTER_PAD), F32).at[:, :N_EXPERTS].set(router_w[l])
        router_hi = router_wp.astype(BF16)
        router_hl = jnp.concatenate([router_hi, (router_wp - router_hi.astype(F32)).astype(BF16)], axis=1)
        router_bp = jnp.zeros((1, ROUTER_PAD), F32).at[0, :N_EXPERTS].set(router_b[l])
        x1, h_moe, logits = outproj_postnorm(merged, w_out[l].astype(BF16), xs, mods[2], mods[3], mods[4],
                                             ln1_g[l][None, :], ln1_b[l][None, :], router_hl, router_bp, tps, batch)

        w_pack = prep_expert_weights(exp_w_gate_up, exp_b_gate_up[l], w_down_b, exp_b_down[l], l)
        ys, weights = moe_ffn(h_moe, logits[:, :N_EXPERTS], w_pack, l)
        xs = combine_postnorm(x1, ys, weights, mods[5], ln2_g[l][None, :], ln2_b[l][None, :], tps, batch)
    return xs.reshape(batch, n_lat, d)
```

```python
import functools
import math

import jax
import jax.numpy as jnp
import numpy as np
from jax import lax
from jax.experimental import pallas as pl
from jax.experimental.pallas import tpu as pltpu

DEPTH = 2
GRID_W = 64
FNET_GROUPS = 4
FNET_GROUP_DIM = 128
GQA_HEADS = 4
GQA_KV_HEADS = 2
GQA_HEAD_DIM = 128
HYENA_WIDTH = 512
HYENA_ORDER = 2
HYENA_EMB_DIM = 33
HYENA_DECAY_TARGET = 1e-2
HYENA_SHORT_DECAY_PCT = 0.3
HYENA_LONG_DECAY_PCT = 1.5
SWA_HEADS = 8
SWA_KV_HEADS = 2
SWA_HEAD_DIM = 64
WINDOW = 128
ROPE_THETA = 10000.0
N_EXPERTS = 32
TOP_K = 4
D_EXPERT = 1536
SWIGLU_LIMIT = 7.0
SWIGLU_ALPHA = 1.702
LN_EPS = 1e-5
RMS_EPS = 1e-6
DEEPNORM_ALPHA = (2 * DEPTH) ** 0.25
SPLIT_SIZES = (FNET_GROUPS * FNET_GROUP_DIM,
               GQA_HEADS * GQA_HEAD_DIM, GQA_KV_HEADS * GQA_HEAD_DIM, GQA_KV_HEADS * GQA_HEAD_DIM,
               3 * HYENA_WIDTH,
               SWA_HEADS * SWA_HEAD_DIM, SWA_KV_HEADS * SWA_HEAD_DIM, SWA_KV_HEADS * SWA_HEAD_DIM)
F32 = jnp.float32
BF16 = jnp.bfloat16


ROW_TILE = 512


def _group_map(tiles_per_sample, batch):
    return lambda i, *_: (jnp.minimum(i // tiles_per_sample, batch), 0, 0)


def _ln(x):
    xc = x - jnp.mean(x, axis=-1, keepdims=True)
    return xc * lax.rsqrt(jnp.mean(xc * xc, axis=-1, keepdims=True) + LN_EPS)


def _norm_rope(x, c, s, gain, quarter):
    width = x.shape[1]
    reps = width // LANES
    if gain is not None:
        x = jnp.concatenate(
            [xh * lax.rsqrt(jnp.mean(xh * xh, axis=-1, keepdims=True) + RMS_EPS) * gain
             for xh in (x[:, r * LANES:(r + 1) * LANES] for r in range(reps))], axis=1)
    lane = lax.broadcasted_iota(jnp.int32, x.shape, 1)
    first = (lane % (2 * quarter)) < quarter
    partner = jnp.where(first, pltpu.roll(x, width - quarter, 1), pltpu.roll(x, quarter, 1))
    return x * jnp.concatenate([c] * reps, axis=1) + partner * jnp.concatenate([s] * reps, axis=1)


IN_TILE = 256


def _inproj_body(x_ref, sh_ref, sc_ref, w_ref, cg_ref, sg_ref, cs_ref, ss_ref, gq_ref, gk_ref,
                 h_ref, fa_ref, q_ref, k_ref, v_ref, hy_ref, sq_ref, sk_ref, sv_ref):
    h = (_ln(x_ref[...]) * (1.0 + sc_ref[...]) + sh_ref[...]).astype(BF16)
    h_ref[...] = h
    offs = np.cumsum((0,) + SPLIT_SIZES)
    seg = lambda i: jnp.dot(h, w_ref[:, int(offs[i]):int(offs[i + 1])], preferred_element_type=F32)
    gq4, sq4 = GQA_HEAD_DIM // 4, SWA_HEAD_DIM // 4
    fa_ref[...] = seg(0)
    q_ref[...] = _norm_rope(seg(1), cg_ref[...], sg_ref[...], gq_ref[...], gq4).astype(BF16)
    k_ref[...] = _norm_rope(seg(2), cg_ref[...], sg_ref[...], gk_ref[...], gq4).astype(BF16)
    v_ref[...] = seg(3).astype(BF16)
    hy_ref[...] = seg(4)
    sq_ref[...] = _norm_rope(seg(5), cs_ref[...], ss_ref[...], None, sq4).astype(BF16)
    sk_ref[...] = _norm_rope(seg(6), cs_ref[...], ss_ref[...], None, sq4).astype(BF16)
    sv_ref[...] = seg(7).astype(BF16)


def in_projection(x, shift, scale, w_in, rope_g, rope_s, gain_q, gain_k, tiles_per_sample, batch):
    t, d = x.shape
    per = ROW_TILE // IN_TILE
    gmap = _group_map(tiles_per_sample * per, batch)
    lat_tiles = tiles_per_sample * per
    tmap = lambda i: (jnp.where(i < lat_tiles * batch, i % lat_tiles, lat_tiles), 0)
    row = lambda i: (i, 0)
    const = lambda i: (0, 0)
    tab = pl.BlockSpec((IN_TILE, LANES), tmap)
    vec = pl.BlockSpec((1, LANES), const)
    widths = (d,) + SPLIT_SIZES
    dtypes = (BF16, F32, BF16, BF16, BF16, F32, BF16, BF16, BF16)
    return pl.pallas_call(
        _inproj_body,
        grid=(t // IN_TILE,),
        in_specs=[pl.BlockSpec((IN_TILE, d), row), pl.BlockSpec((None, 1, d), gmap), pl.BlockSpec((None, 1, d), gmap),
                  pl.BlockSpec(w_in.shape, const, pipeline_mode=pl.Buffered(1)), tab, tab, tab, tab, vec, vec],
        out_specs=[pl.BlockSpec((IN_TILE, w), row) for w in widths],
        out_shape=[jax.ShapeDtypeStruct((t, w), dt) for w, dt in zip(widths, dtypes)],
        compiler_params=pltpu.CompilerParams(dimension_semantics=("parallel",), vmem_limit_bytes=VMEM_LIMIT),
        name="in_projection",
    )(x, shift, scale, w_in, *rope_g, *rope_s, gain_q.astype(F32)[None, :], gain_k.astype(F32)[None, :])


def rope_tables(rows, head_dim):
    cos, sin = axial_rope(rows, head_dim)
    c = jnp.concatenate([cos[:, 0], cos[:, 0], cos[:, 1], cos[:, 1]], axis=-1)
    s = jnp.concatenate([-sin[:, 0], sin[:, 0], -sin[:, 1], sin[:, 1]], axis=-1)
    reps = LANES // head_dim
    c, s = jnp.tile(c, (1, reps)), jnp.tile(s, (1, reps))
    return (jnp.concatenate([c, jnp.ones((ROW_TILE, LANES), F32)], axis=0),
            jnp.concatenate([s, jnp.zeros((ROW_TILE, LANES), F32)], axis=0))


MERGE_TILE = 256


def _merge_body(h_ref, b0_ref, b1_ref, b2_ref, b3_ref, wg_ref, wb_ref, o_ref):
    h = h_ref[...]
    acc = None
    for j, b_ref in enumerate((b0_ref, b1_ref, b2_ref, b3_ref)):
        gate = jax.nn.sigmoid(jnp.dot(h, wg_ref[j], preferred_element_type=F32))
        term = gate * jnp.dot(b_ref[...].astype(BF16), wb_ref[j], preferred_element_type=F32)
        acc = term if acc is None else acc + term
    o_ref[...] = acc.astype(o_ref.dtype)


def merge_gated(h, branches, w_gate, w_branch, rows):
    d = h.shape[1]
    nb, bw, _ = w_branch.shape
    assert nb == len(branches) == 4
    row = lambda i: (i, 0)
    whole = lambda i: (0, 0, 0)
    return pl.pallas_call(
        _merge_body,
        grid=(rows // MERGE_TILE,),
        in_specs=[pl.BlockSpec((MERGE_TILE, d), row)] + [pl.BlockSpec((MERGE_TILE, bw), row)] * nb
                 + [pl.BlockSpec((nb, d, d), whole, pipeline_mode=pl.Buffered(1)),
                    pl.BlockSpec((nb, bw, d), whole, pipeline_mode=pl.Buffered(1))],
        out_specs=pl.BlockSpec((MERGE_TILE, d), row),
        out_shape=jax.ShapeDtypeStruct((rows, d), BF16),
        compiler_params=pltpu.CompilerParams(dimension_semantics=("parallel",), vmem_limit_bytes=VMEM_LIMIT),
        name="merge_gated",
    )(h, *branches, w_gate, w_branch)


OUT_TILE = 256
ROUTER_PAD = 128


def _outproj_body(m_ref, w_ref, x_ref, g2_ref, sh_ref, sc_ref, lg_ref, lb_ref, rw_ref, rb_ref,
                  x1_ref, hm_ref, lo_ref):
    mix = jnp.dot(m_ref[...], w_ref[...], preferred_element_type=F32)
    x1 = _ln(DEEPNORM_ALPHA * x_ref[...] + g2_ref[...] * mix) * lg_ref[...] + lb_ref[...]
    x1_ref[...] = x1
    hm = _ln(x1) * (1.0 + sc_ref[...]) + sh_ref[...]
    hm_hi = hm.astype(BF16)
    hm_ref[...] = hm_hi
    hm_lo = (hm - hm_hi.astype(F32)).astype(BF16)
    hh = jnp.dot(hm_hi, rw_ref[...], preferred_element_type=F32)
    lh = jnp.dot(hm_lo, rw_ref[:, :ROUTER_PAD], preferred_element_type=F32)
    lo_ref[...] = hh[:, :ROUTER_PAD] + hh[:, ROUTER_PAD:] + lh + rb_ref[...]


def outproj_postnorm(merged, w_out, x, gate2, shift3, scale4, ln_g, ln_b, router_w, router_b,
                     tiles_per_sample, batch):
    rows, d = merged.shape
    gmap = _group_map(tiles_per_sample * (ROW_TILE // OUT_TILE), batch)
    row = lambda i: (i, 0)
    const = lambda i: (0, 0)
    return pl.pallas_call(
        _outproj_body,
        grid=(rows // OUT_TILE,),
        in_specs=[pl.BlockSpec((OUT_TILE, d), row), pl.BlockSpec((d, d), const), pl.BlockSpec((OUT_TILE, d), row),
                  pl.BlockSpec((None, 1, d), gmap), pl.BlockSpec((None, 1, d), gmap), pl.BlockSpec((None, 1, d), gmap),
                  pl.BlockSpec((1, d), const), pl.BlockSpec((1, d), const),
                  pl.BlockSpec((d, 2 * ROUTER_PAD), const), pl.BlockSpec((1, ROUTER_PAD), const)],
        out_specs=[pl.BlockSpec((OUT_TILE, d), row), pl.BlockSpec((OUT_TILE, d), row),
                   pl.BlockSpec((OUT_TILE, ROUTER_PAD), row)],
        out_shape=[jax.ShapeDtypeStruct((rows, d), F32), jax.ShapeDtypeStruct((rows, d), BF16),
                   jax.ShapeDtypeStruct((rows, ROUTER_PAD), F32)],
        compiler_params=pltpu.CompilerParams(dimension_semantics=("parallel",), vmem_limit_bytes=VMEM_LIMIT),
        name="outproj_postnorm",
    )(merged, w_out, x, gate2, shift3, scale4, ln_g, ln_b, router_w, router_b)


GATTN_TQ = 512
GATTN_TK = 256
GATTN_UNROLL = 16
VMEM_LIMIT = 56 * 1024 * 1024
LANES = 128
LOG2E = math.log2(math.e)


def _gattn_body(q_ref, kc_ref, vc_ref, k_ref, v_ref, o_ref, m_ref, acc_ref, *, c2):
    tq = q_ref.shape[0]
    hd = k_ref.shape[1]
    q2 = jnp.concatenate([q_ref[:, :hd], q_ref[:, hd:]], axis=0)
    m_ref[...] = jnp.full(m_ref.shape, -jnp.inf, F32)
    acc_ref[...] = jnp.zeros(acc_ref.shape, F32)

    def step(k, v):
        s = lax.dot_general(q2, k, (((1,), (1,)), ((), ())), preferred_element_type=F32) * c2
        m_old = m_ref[...]
        m_new = jnp.maximum(m_old, jnp.max(s, axis=-1, keepdims=True))
        alpha = jnp.exp2(m_old - m_new)
        p = jnp.concatenate([jnp.exp2(s[:, j * LANES:(j + 1) * LANES] - m_new)
                             for j in range(k.shape[0] // LANES)], axis=1)
        acc_ref[...] = (jnp.concatenate([alpha, alpha], axis=1) * acc_ref[...]
                        + jnp.dot(p.astype(BF16), v, preferred_element_type=F32))
        m_ref[...] = m_new

    step(kc_ref[...], vc_ref[...])

    def loop(c, carry):
        off = pl.multiple_of(c * GATTN_TK, GATTN_TK)
        step(k_ref[pl.ds(off, GATTN_TK), :], v_ref[pl.ds(off, GATTN_TK), :])
        return carry

    n_chunks = k_ref.shape[0] // GATTN_TK
    lax.fori_loop(0, n_chunks, loop, 0, unroll=math.gcd(GATTN_UNROLL, n_chunks))
    acc = acc_ref[...]
    out = acc[:, :hd] / acc[:, hd:]
    o_ref[...] = jnp.concatenate([out[:tq], out[tq:]], axis=1).astype(o_ref.dtype)


def _with_ones(v, n_heads):
    r = v.shape[0]
    v3 = v.reshape(r, n_heads, -1)
    return jnp.concatenate([v3, jnp.ones_like(v3)], axis=-1).reshape(r, -1)


def global_attention(q, k, v, k_ctx, v_ctx, batch):
    hd = GQA_HEAD_DIM
    n = q.shape[0] // batch
    n_c = k_ctx.shape[0] // batch
    g = GQA_HEADS // GQA_KV_HEADS
    tq = min(GATTN_TQ, n)
    assert g == 2 and hd == LANES and n % tq == 0 and n % GATTN_TK == 0 and n_c % LANES == 0
    nq = n // tq
    return pl.pallas_call(
        functools.partial(_gattn_body, c2=hd ** -0.5 * LOG2E),
        grid=(batch, GQA_KV_HEADS, nq),
        in_specs=[pl.BlockSpec((tq, g * hd), lambda b, h, i: (b * nq + i, h)),
                  pl.BlockSpec((n_c, hd), lambda b, h, i: (b, h)),
                  pl.BlockSpec((n_c, 2 * hd), lambda b, h, i: (b, h)),
                  pl.BlockSpec((n, hd), lambda b, h, i: (b, h)),
                  pl.BlockSpec((n, 2 * hd), lambda b, h, i: (b, h))],
        out_specs=pl.BlockSpec((tq, g * hd), lambda b, h, i: (b * nq + i, h)),
        out_shape=jax.ShapeDtypeStruct(q.shape, BF16),
        scratch_shapes=[pltpu.VMEM((g * tq, hd), F32), pltpu.VMEM((g * tq, 2 * hd), F32)],
        compiler_params=pltpu.CompilerParams(dimension_semantics=("parallel", "parallel", "parallel"),
                                             vmem_limit_bytes=VMEM_LIMIT),
        name="gattn",
    )(q, k_ctx, _with_ones(v_ctx, GQA_KV_HEADS), k, _with_ones(v, GQA_KV_HEADS))


WATTN_TQ = 256


def _wattn_body(sink_ref, q_ref, kc_ref, vc_ref, kp_ref, kx_ref, kn_ref, vp_ref, vx_ref, vn_ref, o_ref,
                *, scale, n, nq):
    tq = q_ref.shape[0]
    hd = SWA_HEAD_DIM
    g = SWA_HEADS // SWA_KV_HEADS
    h = pl.program_id(1)
    i = pl.program_id(2)
    k_band = jnp.concatenate([kp_ref[...], kx_ref[...], kn_ref[...]], axis=0)
    v_band = jnp.concatenate([vp_ref[...], vx_ref[...], vn_ref[...]], axis=0)
    qpos = i * tq + lax.broadcasted_iota(jnp.int32, (tq, tq + 2 * WINDOW), 0)
    kpos = i * tq - WINDOW + lax.broadcasted_iota(jnp.int32, (tq, tq + 2 * WINDOW), 1)
    valid = (jnp.abs(kpos - qpos) <= WINDOW) & (kpos >= 0) & (kpos < n)
    outs = []
    for j in range(g):
        q = q_ref[:, j * hd:(j + 1) * hd]
        s_c = lax.dot_general(q, kc_ref[...], (((1,), (1,)), ((), ())), preferred_element_type=F32) * scale
        s_b = lax.dot_general(q, k_band, (((1,), (1,)), ((), ())), preferred_element_type=F32) * scale
        s_b = jnp.where(valid, s_b, -jnp.inf)
        sink = sink_ref[h * g + j]
        m = jnp.maximum(jnp.maximum(jnp.max(s_c, -1, keepdims=True), jnp.max(s_b, -1, keepdims=True)), sink)
        p_c = jnp.exp(s_c - m)
        p_b = jnp.exp(s_b - m)
        l = jnp.sum(p_c, -1, keepdims=True) + jnp.sum(p_b, -1, keepdims=True) + jnp.exp(sink - m)
        o = (jnp.dot(p_c.astype(BF16), vc_ref[...], preferred_element_type=F32)
             + jnp.dot(p_b.astype(BF16), v_band, preferred_element_type=F32))
        outs.append(o / l)
    o_ref[...] = jnp.concatenate(outs, axis=1).astype(o_ref.dtype)


def window_attention(q, k, v, k_ctx, v_ctx, sinks, batch):
    hd = SWA_HEAD_DIM
    n = q.shape[0] // batch
    n_c = k_ctx.shape[0] // batch
    g = SWA_HEADS // SWA_KV_HEADS
    tq = WATTN_TQ
    assert n % tq == 0 and tq % WINDOW == 0
    nq = n // tq
    r = tq // WINDOW
    nw = n // WINDOW
    k_h = k.reshape(batch * n, SWA_KV_HEADS, hd).transpose(1, 0, 2)
    v_h = v.reshape(batch * n, SWA_KV_HEADS, hd).transpose(1, 0, 2)
    kc_h = k_ctx.reshape(batch * n_c, SWA_KV_HEADS, hd).transpose(1, 0, 2)
    vc_h = v_ctx.reshape(batch * n_c, SWA_KV_HEADS, hd).transpose(1, 0, 2)
    prev_map = lambda b, h, i, s: (h, b * nw + jnp.maximum(i * r - 1, 0), 0)
    cur_map = lambda b, h, i, s: (h, b * nq + i, 0)
    next_map = lambda b, h, i, s: (h, b * nw + jnp.minimum((i + 1) * r, nw - 1), 0)
    ctx_map = lambda b, h, i, s: (h, b, 0)
    grid_spec = pltpu.PrefetchScalarGridSpec(
        num_scalar_prefetch=1,
        grid=(batch, SWA_KV_HEADS, nq),
        in_specs=[pl.BlockSpec((tq, g * hd), lambda b, h, i, s: (b * nq + i, h)),
                  pl.BlockSpec((None, n_c, hd), ctx_map),
                  pl.BlockSpec((None, n_c, hd), ctx_map),
                  pl.BlockSpec((None, WINDOW, hd), prev_map),
                  pl.BlockSpec((None, tq, hd), cur_map),
                  pl.BlockSpec((None, WINDOW, hd), next_map),
                  pl.BlockSpec((None, WINDOW, hd), prev_map),
                  pl.BlockSpec((None, tq, hd), cur_map),
                  pl.BlockSpec((None, WINDOW, hd), next_map)],
        out_specs=pl.BlockSpec((tq, g * hd), lambda b, h, i, s: (b * nq + i, h)),
    )
    return pl.pallas_call(
        functools.partial(_wattn_body, scale=hd ** -0.5, n=n, nq=nq),
        grid_spec=grid_spec,
        out_shape=jax.ShapeDtypeStruct(q.shape, BF16),
        compiler_params=pltpu.CompilerParams(dimension_semantics=("parallel", "parallel", "parallel"),
                                             vmem_limit_bytes=VMEM_LIMIT),
        name="wattn",
    )(sinks.astype(F32), q, kc_h, vc_h, k_h, k_h, k_h, v_h, v_h, v_h)


MOE_TB = 512
MOE_FC = 512


def _ffn_body(blk_e_ref, nused_ref, x_ref, wg_ref, wu_ref, bg_ref, bu_ref, wd_ref, bd_ref, o_ref):
    i = pl.program_id(0)

    @pl.when(i < nused_ref[0])
    def _():
        x = x_ref[...]
        acc = jnp.zeros(o_ref.shape, F32) + bd_ref[...]
        for c in range(D_EXPERT // MOE_FC):
            sl = slice(c * MOE_FC, (c + 1) * MOE_FC)
            gate = jnp.dot(x, wg_ref[:, sl], preferred_element_type=F32) + bg_ref[:, sl]
            up = jnp.dot(x, wu_ref[:, sl], preferred_element_type=F32) + bu_ref[:, sl]
            gate = jnp.minimum(gate, SWIGLU_LIMIT)
            up = jnp.clip(up, -SWIGLU_LIMIT, SWIGLU_LIMIT)
            glu = gate * jax.nn.sigmoid(gate * SWIGLU_ALPHA)
            act = ((up + 1.0) * glu).astype(BF16)
            acc = acc + jnp.dot(act, wd_ref[sl, :], preferred_element_type=F32)
        o_ref[...] = acc.astype(o_ref.dtype)

    @pl.when(i >= nused_ref[0])
    def _():
        o_ref[...] = jnp.zeros(o_ref.shape, o_ref.dtype)


def expert_ffn(buf, blk_e, n_used, w_gate, w_up, b_gate, b_up, w_dn_all, b_dn, layer):
    n_rows, d = buf.shape
    n_blk = n_rows // MOE_TB
    f = w_gate.shape[2]
    wmap = lambda i, e, u: (e[i], 0, 0)
    grid_spec = pltpu.PrefetchScalarGridSpec(
        num_scalar_prefetch=2,
        grid=(n_blk,),
        in_specs=[pl.BlockSpec((MOE_TB, d), lambda i, e, u: (i, 0)),
                  pl.BlockSpec((None, d, f), wmap),
                  pl.BlockSpec((None, d, f), wmap),
                  pl.BlockSpec((None, 1, f), wmap),
                  pl.BlockSpec((None, 1, f), wmap),
                  pl.BlockSpec((None, None, f, d), lambda i, e, u: (layer, e[i], 0, 0)),
                  pl.BlockSpec((None, 1, d), wmap)],
        out_specs=pl.BlockSpec((MOE_TB, d), lambda i, e, u: (i, 0)),
    )
    return pl.pallas_call(
        _ffn_body,
        grid_spec=grid_spec,
        out_shape=jax.ShapeDtypeStruct((n_rows, d), BF16),
        compiler_params=pltpu.CompilerParams(dimension_semantics=("arbitrary",),
                                             vmem_limit_bytes=VMEM_LIMIT),
        name="expert_ffn",
    )(blk_e, n_used, buf, w_gate, w_up, b_gate, b_up, w_dn_all, b_dn)


SPLIT_ROWS = 512
WEIGHT_STREAMS = 4


def _deint_body(*refs):
    w_refs, (p_ref, g_ref, u_ref) = refs[:WEIGHT_STREAMS], refs[WEIGHT_STREAMS:]
    blk = p_ref.shape[0]
    rows = w_refs[0].shape[0]
    for q, w_ref in enumerate(w_refs):
        rs = slice(q * rows, (q + 1) * rows)
        for c in range(w_ref.shape[1] // blk):
            y = jnp.dot(w_ref[:, c * blk:(c + 1) * blk].astype(BF16), p_ref[...], preferred_element_type=F32)
            g_ref[rs, c * LANES:(c + 1) * LANES] = y[:, :LANES].astype(BF16)
            u_ref[rs, c * LANES:(c + 1) * LANES] = y[:, LANES:].astype(BF16)


def split_gate_up(w_gu_all, layer):
    _, e, d, f2 = w_gu_all.shape
    blk = 2 * LANES
    src = jnp.arange(blk)[:, None]
    dst = jnp.arange(blk)[None, :]
    perm = (src == jnp.where(dst < LANES, 2 * dst, 2 * (dst - LANES) + 1)).astype(BF16)
    out = jax.ShapeDtypeStruct((e, d, f2 // 2), BF16)
    sub = SPLIT_ROWS // WEIGHT_STREAMS
    stream = lambda q: pl.BlockSpec((None, None, sub, f2), lambda i, j: (layer, i, WEIGHT_STREAMS * j + q, 0))
    return pl.pallas_call(
        _deint_body,
        grid=(e, d // SPLIT_ROWS),
        in_specs=[stream(q) for q in range(WEIGHT_STREAMS)] + [pl.BlockSpec((blk, blk), lambda i, j: (0, 0))],
        out_specs=[pl.BlockSpec((None, SPLIT_ROWS, f2 // 2), lambda i, j: (i, j, 0)),
                   pl.BlockSpec((None, SPLIT_ROWS, f2 // 2), lambda i, j: (i, j, 0))],
        out_shape=[out, out],
        compiler_params=pltpu.CompilerParams(dimension_semantics=("parallel", "parallel"),
                                             vmem_limit_bytes=VMEM_LIMIT),
        name="split_gate_up",
    )(*([w_gu_all] * WEIGHT_STREAMS), perm)


def _cast_body(*refs):
    w_refs, o_ref = refs[:WEIGHT_STREAMS], refs[WEIGHT_STREAMS]
    rows = w_refs[0].shape[0]
    for q, w_ref in enumerate(w_refs):
        o_ref[q * rows:(q + 1) * rows, :] = w_ref[...].astype(o_ref.dtype)


def cast_weights_bf16(w):
    nl, e, r, c = w.shape
    sub = r // WEIGHT_STREAMS
    assert r % (WEIGHT_STREAMS * 16) == 0
    stream = lambda q: pl.BlockSpec((None, None, sub, c), lambda l, i: (l, i, q, 0))
    return pl.pallas_call(
        _cast_body,
        grid=(nl, e),
        in_specs=[stream(q) for q in range(WEIGHT_STREAMS)],
        out_specs=pl.BlockSpec((None, None, r, c), lambda l, i: (l, i, 0, 0)),
        out_shape=jax.ShapeDtypeStruct(w.shape, BF16),
        compiler_params=pltpu.CompilerParams(dimension_semantics=("parallel", "parallel"),
                                             vmem_limit_bytes=VMEM_LIMIT),
        name="cast_weights_bf16",
    )(*([w] * WEIGHT_STREAMS))


FFT_N2 = 128
FFT_SPLIT_MIN = 1024


def _angles(num, den):
    return (2.0 * math.pi / den) * (num % den).astype(F32)


def dft_stage_a(n1, k_in, real_input):
    ang = _angles(jnp.arange(n1)[:, None] * jnp.arange(k_in)[None, :], n1)
    c, s = jnp.cos(ang), jnp.sin(ang)
    if real_input:
        return jnp.concatenate([c, -s], axis=0).astype(BF16)
    return jnp.block([[c, s], [-s, c]]).astype(BF16)


def dft_stage_a_inv(n1, k_out):
    ang = _angles(jnp.arange(k_out)[:, None] * jnp.arange(n1)[None, :], n1)
    c, s = jnp.cos(ang), jnp.sin(ang)
    return jnp.block([[c, -s], [s, c]]).astype(BF16)


def dft_stage_b(n1, n2):
    k1 = jnp.arange(n1)[:, None, None]
    k2 = jnp.arange(n2)[None, :, None]
    m = jnp.arange(n2)[None, None, :]
    ang = _angles(m * (k1 + n1 * k2), n1 * n2)
    c, s = jnp.cos(ang), jnp.sin(ang)
    top = jnp.concatenate([c, s], axis=2)
    bot = jnp.concatenate([-s, c], axis=2)
    return jnp.concatenate([top, bot], axis=1).astype(BF16)


SUBLANES = 8
U32 = jnp.uint32
HIGH_HALF = 0xFFFF0000


def pack_complex(re, im):
    ur = lax.bitcast_convert_type(re.astype(BF16).astype(F32), U32)
    ui = lax.bitcast_convert_type(im.astype(BF16).astype(F32), U32)
    return (ur >> 16) | (ui & U32(HIGH_HALF))


def unpack_complex(w):
    re = lax.bitcast_convert_type(w << 16, F32).astype(BF16)
    im = lax.bitcast_convert_type(w & U32(HIGH_HALF), F32).astype(BF16)
    return re, im


def _lmm_body(m_ref, x_ref, o_ref):
    m = m_ref[...]
    k, sub, lanes = x_ref.shape
    r = o_ref.shape[0]
    x2 = x_ref.reshape(k * sub, lanes)
    o2 = o_ref.reshape(r * sub, lanes)
    for s in range(sub):
        xs = x2[pl.ds(s, k, stride=sub), :].astype(BF16)
        y = jnp.dot(m, xs, preferred_element_type=F32)
        o2[pl.ds(s, r, stride=sub), :] = pack_complex(y[:r], y[r:])


def left_matmul(m, x, n2):
    g, rows, c = x.shape
    k = rows // n2
    r = m.shape[0] // 2
    assert m.shape[1] == k and n2 % SUBLANES == 0 and c % LANES == 0
    x5 = x.reshape(g, k, n2 // SUBLANES, SUBLANES, c)
    out = pl.pallas_call(
        _lmm_body,
        grid=(g, n2 // SUBLANES, c // LANES),
        in_specs=[pl.BlockSpec((2 * r, k), lambda i, j, l: (0, 0)),
                  pl.BlockSpec((None, k, None, SUBLANES, LANES), lambda i, j, l: (i, 0, j, 0, l))],
        out_specs=pl.BlockSpec((None, r, None, SUBLANES, LANES), lambda i, j, l: (i, 0, j, 0, l)),
        out_shape=jax.ShapeDtypeStruct((g, r, n2 // SUBLANES, SUBLANES, c), U32),
        compiler_params=pltpu.CompilerParams(dimension_semantics=("parallel", "parallel", "parallel"),
                                             vmem_limit_bytes=VMEM_LIMIT),
        name="left_matmul",
    )(m, x5)
    return out.reshape(g, r * n2, c)


def _lmm_gate_body(m_ref, x_ref, a_ref, z_ref, b_ref, o_ref, *, scale):
    m = m_ref[...]
    k, sub, lanes = x_ref.shape
    r = o_ref.shape[0]
    x2 = x_ref.reshape(k * sub, lanes)
    a2 = a_ref.reshape(r * sub, lanes)
    z2 = z_ref.reshape(r * sub, lanes)
    o2 = o_ref.reshape(r * sub, lanes)
    for s in range(sub):
        rows = pl.ds(s, r, stride=sub)
        xs = jnp.concatenate(unpack_complex(x2[pl.ds(s, k, stride=sub), :]), axis=0)
        y = jnp.dot(m, xs, preferred_element_type=F32)
        o2[rows, :] = a2[rows, :] * (y * scale + z2[rows, :] * b_ref[...])


def left_matmul_gated(m, x, n2, gate, z, bias, scale):
    g, rows, c = x.shape
    k = rows // n2
    r = m.shape[0]
    assert m.shape[1] == 2 * k and gate.shape == (g, r * n2, c) and z.shape == gate.shape
    view = lambda t, lead: t.reshape(g, lead, n2 // SUBLANES, SUBLANES, c)
    blk = lambda lead: pl.BlockSpec((None, lead, None, SUBLANES, LANES), lambda i, j, l: (i, 0, j, 0, l))
    out = pl.pallas_call(
        functools.partial(_lmm_gate_body, scale=scale),
        grid=(g, n2 // SUBLANES, c // LANES),
        in_specs=[pl.BlockSpec((r, 2 * k), lambda i, j, l: (0, 0)), blk(k), blk(r), blk(r),
                  pl.BlockSpec((1, LANES), lambda i, j, l: (0, l))],
        out_specs=blk(r),
        out_shape=jax.ShapeDtypeStruct((g, r, n2 // SUBLANES, SUBLANES, c), F32),
        compiler_params=pltpu.CompilerParams(dimension_semantics=("parallel", "parallel", "parallel"),
                                             vmem_limit_bytes=VMEM_LIMIT),
        name="left_matmul_gated",
    )(m, view(x, k), view(gate, r), view(z, r), bias.astype(F32)[None, :])
    return out.reshape(g, r * n2, c)


def _sconv_body(p_ref, x_ref, n_ref, w_ref, b_ref, o1_ref, o2_ref, o3_ref, *, tiles_per_sample):
    i = pl.program_id(0)
    pos = i % tiles_per_sample
    x = x_ref[...]
    rows = x.shape[0]
    before = jnp.where(pos == 0, 0.0, p_ref[SUBLANES - 1:SUBLANES, :])
    after = jnp.where(pos == tiles_per_sample - 1, 0.0, n_ref[0:1, :])
    ridx = lax.broadcasted_iota(jnp.int32, x.shape, 0)
    up = jnp.where(ridx == 0, before, pltpu.roll(x, 1, 0))
    dn = jnp.where(ridx == rows - 1, after, pltpu.roll(x, rows - 1, 0))
    y = up * w_ref[0:1, :] + x * w_ref[1:2, :] + dn * w_ref[2:3, :] + b_ref[...]
    w = o1_ref.shape[1]
    o1_ref[...] = y[:, :w]
    o2_ref[...] = y[:, w:2 * w]
    o3_ref[...] = y[:, 2 * w:]


def short_conv_split(z, col_blocks, conv_w, conv_b, tiles_per_sample, batch):
    t, _ = z.shape
    wid = conv_w.shape[1]
    n_rows = batch * tiles_per_sample * ROW_TILE
    per = ROW_TILE // SUBLANES
    last = t // SUBLANES - 1
    out = jax.ShapeDtypeStruct((n_rows, wid // 3), F32)
    return pl.pallas_call(
        functools.partial(_sconv_body, tiles_per_sample=tiles_per_sample),
        grid=(batch * tiles_per_sample,),
        in_specs=[pl.BlockSpec((SUBLANES, wid), lambda i: (jnp.maximum(i * per - 1, 0), col_blocks)),
                  pl.BlockSpec((ROW_TILE, wid), lambda i: (i, col_blocks)),
                  pl.BlockSpec((SUBLANES, wid), lambda i: (jnp.minimum((i + 1) * per, last), col_blocks)),
                  pl.BlockSpec((3, wid), lambda i: (0, 0)), pl.BlockSpec((1, wid), lambda i: (0, 0))],
        out_specs=[pl.BlockSpec((ROW_TILE, wid // 3), lambda i: (i, 0))] * 3,
        out_shape=[out, out, out],
        compiler_params=pltpu.CompilerParams(dimension_semantics=("parallel",), vmem_limit_bytes=VMEM_LIMIT),
        name="short_conv_split",
    )(z, z, z, conv_w.astype(F32), conv_b.astype(F32)[None, :])


def _stacked(t_ref):
    return jnp.concatenate(unpack_complex(t_ref[...]), axis=0)


def _conv_b_body(t_ref, gs_ref, h_ref, u_ref):
    n2 = t_ref.shape[0]
    gs = gs_ref[...]
    y = jnp.dot(gs, _stacked(t_ref), preferred_element_type=F32)
    yr, yi = y[:n2], y[n2:]
    hr, hi = h_ref[0], h_ref[1]
    z = jnp.concatenate([yr * hr - yi * hi, yr * hi + yi * hr], axis=0).astype(BF16)
    u = lax.dot_general(gs, z, (((0,), (0,)), ((), ())), preferred_element_type=F32)
    u_ref[...] = pack_complex(u[:n2], u[n2:])


def conv_stage_b(t, gs, h):
    n1, n2, c = t.shape
    spec = pl.BlockSpec((None, n2, c), lambda i: (i, 0, 0))
    return pl.pallas_call(
        _conv_b_body,
        grid=(n1,),
        in_specs=[spec, pl.BlockSpec((None, 2 * n2, 2 * n2), lambda i: (i, 0, 0)),
                  pl.BlockSpec((2, None, n2, c), lambda i: (0, i, 0, 0))],
        out_specs=spec,
        out_shape=jax.ShapeDtypeStruct(t.shape, U32),
        compiler_params=pltpu.CompilerParams(dimension_semantics=("parallel",), vmem_limit_bytes=VMEM_LIMIT),
        name="conv_stage_b",
    )(t, gs, h)


def _filt_b_body(t_ref, gs_ref, ss_ref, h_ref):
    n2 = t_ref.shape[0]
    w = HYENA_WIDTH
    y = jnp.dot(gs_ref[...], _stacked(t_ref), preferred_element_type=F32)
    yr, yi = y[:n2], y[n2:]
    ss = ss_ref[...]
    for o in range(HYENA_ORDER):
        f0, b0 = (2 * o) * w, (2 * o + 1) * w
        scale = lax.rsqrt(ss[:, f0:f0 + w] + ss[:, b0:b0 + w] + 1e-6)
        h_ref[o, 0] = (yr[:, f0:f0 + w] + yr[:, b0:b0 + w]) * scale
        h_ref[o, 1] = (yi[:, f0:f0 + w] - yi[:, b0:b0 + w]) * scale


def filter_stage_b(t, gs, sumsq):
    n1, n2, c = t.shape
    w = HYENA_WIDTH
    return pl.pallas_call(
        _filt_b_body,
        grid=(n1,),
        in_specs=[pl.BlockSpec((None, n2, c), lambda i: (i, 0, 0)),
                  pl.BlockSpec((None, 2 * n2, 2 * n2), lambda i: (i, 0, 0)),
                  pl.BlockSpec((1, c), lambda i: (0, 0))],
        out_specs=pl.BlockSpec((HYENA_ORDER, 2, None, n2, w), lambda i: (0, 0, i, 0, 0)),
        out_shape=jax.ShapeDtypeStruct((HYENA_ORDER, 2, n1, n2, w), F32),
        compiler_params=pltpu.CompilerParams(dimension_semantics=("parallel",), vmem_limit_bytes=VMEM_LIMIT),
        name="filter_stage_b",
    )(t, gs, sumsq)


def _fnet_b_body(t_ref, gs_ref, cs_ref, o_ref):
    n2, c = t_ref.shape
    y = jnp.dot(gs_ref[...], _stacked(t_ref), preferred_element_type=F32)
    yr, yi = y[:n2].astype(BF16), y[n2:].astype(BF16)
    gd = FNET_GROUP_DIM
    outs = []
    for g in range(c // gd):
        v = jnp.concatenate([yr[:, g * gd:(g + 1) * gd], yi[:, g * gd:(g + 1) * gd]], axis=1)
        outs.append(jnp.dot(v, cs_ref[...], preferred_element_type=F32))
    o_ref[...] = jnp.concatenate(outs, axis=1).astype(o_ref.dtype)


def fnet_stage_b(t, gs, cs, out_dtype):
    b, n1, n2, c = t.shape
    return pl.pallas_call(
        _fnet_b_body,
        grid=(b, n1),
        in_specs=[pl.BlockSpec((None, None, n2, c), lambda i, j: (i, j, 0, 0)),
                  pl.BlockSpec((None, 2 * n2, 2 * n2), lambda i, j: (j, 0, 0)),
                  pl.BlockSpec(cs.shape, lambda i, j: (0, 0))],
        out_specs=pl.BlockSpec((None, n2, c), lambda i, j: (i, 0, j)),
        out_shape=jax.ShapeDtypeStruct((b, n2, n1 * c), out_dtype),
        compiler_params=pltpu.CompilerParams(dimension_semantics=("parallel", "parallel"),
                                             vmem_limit_bytes=VMEM_LIMIT),
        name="fnet_stage_b",
    )(t, gs, cs)


def fourier_mix_pallas(u):
    b, n, c = u.shape
    gd = FNET_GROUP_DIM
    n2 = FFT_N2 if n >= FFT_SPLIT_MIN else n
    n1 = n // n2
    gs = dft_stage_b(n1, n2)
    if n1 > 1:
        t = left_matmul(dft_stage_a(n1, n1, True), u.astype(F32), n2).reshape(b, n1, n2, c)
    else:
        t = pack_complex(u.astype(F32), jnp.zeros(u.shape, F32)).reshape(b, 1, n2, c)
    ang = _angles(jnp.arange(gd)[:, None] * jnp.arange(gd)[None, :], gd)
    cs = (jnp.concatenate([jnp.cos(ang), jnp.sin(ang)], axis=0) * (n * gd) ** -0.5).astype(BF16)
    out = fnet_stage_b(t, gs, cs, u.dtype)
    return out.reshape(b, n2, n1, c).reshape(b, n, c)


FILT_TM = 512
FILT_LANES = 128


def _filter_body(ff_ref, w1_ref, b1_ref, fr1_ref, w2_ref, b2_ref, fr2_ref, w3_ref, dl_ref, taps_ref, ss_ref, *, n):
    i = pl.program_id(0)
    tm = taps_ref.shape[0]
    hi = lax.Precision.HIGHEST
    pos = (i * tm + lax.broadcasted_iota(jnp.int32, (tm, FILT_LANES), 0)).astype(F32)
    lane = lax.broadcasted_iota(jnp.int32, (tm, FILT_LANES), 1)
    t = pos / (n - 1.0) if n > 1 else pos * 0.0
    omega = (2.0 * math.pi) * pos / n
    arg = ff_ref[...] * omega
    bands = (HYENA_EMB_DIM - 1) // 2
    z = jnp.where(lane == 0, t, jnp.where(lane <= bands, jnp.cos(arg),
                                          jnp.where(lane <= 2 * bands, -jnp.sin(arg), 0.0)))
    h = jnp.sin(fr1_ref[...] * (jnp.dot(z, w1_ref[...], precision=hi, preferred_element_type=F32) + b1_ref[...]))
    h = jnp.sin(fr2_ref[...] * (jnp.dot(h, w2_ref[...], precision=hi, preferred_element_type=F32) + b2_ref[...]))
    h = jnp.dot(h, w3_ref[...], precision=hi, preferred_element_type=F32)
    width = h.shape[1]
    h = h * jnp.exp(-t[:, :1] * dl_ref[...])
    row = i * tm + lax.broadcasted_iota(jnp.int32, (tm, width), 0)
    col = lax.broadcasted_iota(jnp.int32, (tm, width), 1)
    h = jnp.where((row == 0) & ((col // HYENA_WIDTH) % 2 == 1), 0.0, h)
    taps_ref[...] = h

    @pl.when(i == 0)
    def _():
        ss_ref[...] = jnp.zeros(ss_ref.shape, F32)

    ss_ref[...] += jnp.sum(h * h, axis=0, keepdims=True)


def hyena_taps(n, w1, b1, fr1, w2, b2, fr2, w3):
    bands = (HYENA_EMB_DIM - 1) // 2
    fd = w1.shape[1]
    width = w3.shape[1]
    freqs = jnp.linspace(1e-4, bands - 1, bands, dtype=F32)
    featfreq = jnp.zeros((1, FILT_LANES), F32).at[0, 1:1 + bands].set(freqs).at[0, 1 + bands:1 + 2 * bands].set(freqs)
    w1p = jnp.zeros((FILT_LANES, fd), F32).at[:HYENA_EMB_DIM].set(w1.astype(F32))
    max_decay = math.log(HYENA_DECAY_TARGET) / HYENA_SHORT_DECAY_PCT
    min_decay = math.log(HYENA_DECAY_TARGET) / HYENA_LONG_DECAY_PCT
    deltas = jnp.abs(jnp.linspace(min_decay, max_decay, HYENA_WIDTH, dtype=F32))
    dl = jnp.tile(deltas, width // HYENA_WIDTH)[None, :]
    tm = min(FILT_TM, n)
    full = lambda a: pl.BlockSpec(a.shape, lambda i: (0,) * a.ndim)
    ops = (featfreq, w1p, b1.astype(F32)[None, :], fr1.astype(F32)[None, :], w2.astype(F32),
           b2.astype(F32)[None, :], fr2.astype(F32)[None, :], w3.astype(F32), dl)
    return pl.pallas_call(
        functools.partial(_filter_body, n=n),
        grid=(n // tm,),
        in_specs=[full(a) for a in ops],
        out_specs=[pl.BlockSpec((tm, width), lambda i: (i, 0)), pl.BlockSpec((1, width), lambda i: (0, 0))],
        out_shape=[jax.ShapeDtypeStruct((n, width), F32), jax.ShapeDtypeStruct((1, width), F32)],
        compiler_params=pltpu.CompilerParams(dimension_semantics=("arbitrary",), vmem_limit_bytes=VMEM_LIMIT),
        name="hyena_taps",
    )(*ops)


def _fft_split(n):
    if n >= FFT_SPLIT_MIN:
        return 2 * n // FFT_N2, FFT_N2, n // FFT_N2
    return 1, 2 * n, 1


def hyena_mix_pallas(x1, x2, v, filt, bias):
    batch, n, w = v.shape
    assert batch == 2
    n1, n2, k1 = _fft_split(n)
    gs = dft_stage_b(n1, n2)
    taps, sumsq = hyena_taps(n, *filt)
    if n1 > 1:
        t = left_matmul(dft_stage_a(n1, k1, True), taps[None], n2).reshape(n1, n2, taps.shape[1])
    else:
        padded = jnp.pad(taps, ((0, n), (0, 0)))
        t = pack_complex(padded, jnp.zeros_like(padded))[None]
    spec = filter_stage_b(t, gs, sumsq)
    fwd = dft_stage_a(n1, k1, False)
    inv = dft_stage_a_inv(n1, k1)

    def gated_long_conv(gate, z, order):
        scale = 1.0 / (2 * n)
        if n1 > 1:
            t = left_matmul(fwd, z.reshape(1, 2 * n, w), n2).reshape(n1, n2, w)
            uu = conv_stage_b(t, gs, spec[order])
            return left_matmul_gated(inv, uu.reshape(1, n1 * n2, w), n2, gate.reshape(1, 2 * n, w),
                                     z.reshape(1, 2 * n, w), bias[order], scale).reshape(2, n, w)
        zp = jnp.pad(z, ((0, 0), (0, n), (0, 0)))
        uu = conv_stage_b(pack_complex(zp[0], zp[1])[None], gs, spec[order])[0, :n]
        y = jnp.stack(unpack_complex(uu)).astype(F32)
        return gate * (y * scale + z * bias[order].astype(F32))

    return gated_long_conv(x2, gated_long_conv(x1, v, 0), 1)


def heads(z, n_heads):
    b, n, w = z.shape
    return z.reshape(b, n, n_heads, w // n_heads)


def group_heads(q, n_kv):
    b, n, h, hd = q.shape
    return q.reshape(b, n, n_kv, h // n_kv, hd)


def flat_heads(o):
    return o.reshape(o.shape[0], o.shape[1], -1)


def axial_rope(rows, head_dim):
    n_freq = head_dim // 4
    inv_freq = jnp.power(ROPE_THETA, -jnp.arange(n_freq, dtype=F32) / n_freq)
    t = jnp.arange(rows * GRID_W)
    row = (t // GRID_W).astype(F32)
    col = (t % GRID_W).astype(F32)
    ang = jnp.stack([row[:, None] * inv_freq, col[:, None] * inv_freq], axis=1)
    return jnp.cos(ang), jnp.sin(ang)


def softmax_attend(q, k, v, sinks=None):
    scale = q.shape[-1] ** -0.5
    s = jnp.einsum("bqkgd,bskd->bkgqs", q, k, preferred_element_type=F32) * scale
    if sinks is None:
        p = jax.nn.softmax(s, axis=-1)
    else:
        sink_col = jnp.broadcast_to(sinks.astype(F32)[None, :, :, None, None], s.shape[:-1] + (1,))
        p = jax.nn.softmax(jnp.concatenate([sink_col, s], axis=-1), axis=-1)[..., 1:]
    return jnp.einsum("bkgqs,bskd->bqkgd", p.astype(v.dtype), v)


def short_conv(u, w, bias):
    n = u.shape[1]
    up = jnp.pad(u, ((0, 0), (1, 1), (0, 0)))
    return up[:, :n] * w[0] + up[:, 1:n + 1] * w[1] + up[:, 2:] * w[2] + bias


def moe_ffn(tokens, logits, w_pack, layer):
    n_tok, d = tokens.shape
    i32 = jnp.int32
    top_v, top_i = lax.top_k(logits, TOP_K)
    weights = jax.nn.softmax(top_v, axis=-1)
    flat_e = top_i.reshape(-1).astype(i32)
    n_asg = n_tok * TOP_K
    e_ids = jnp.arange(N_EXPERTS, dtype=i32)
    onehot = (flat_e[:, None] == e_ids[None, :]).astype(i32)
    csum = jnp.cumsum(onehot, axis=0)
    rank = jnp.sum(onehot * csum, axis=1) - 1
    counts = csum[-1]
    padded = (counts + MOE_TB - 1) // MOE_TB * MOE_TB
    padded_end = jnp.cumsum(padded)
    padded_start = padded_end - padded
    dest = jnp.sum(onehot * padded_start[None, :], axis=1) + rank
    n_rows = n_asg + N_EXPERTS * MOE_TB
    n_blk = n_rows // MOE_TB
    blk_start = jnp.arange(n_blk, dtype=i32) * MOE_TB
    blk_e = jnp.minimum(jnp.sum((blk_start[:, None] >= padded_end[None, :]).astype(i32), axis=1), N_EXPERTS - 1)
    n_used = (padded_end[-1:] // MOE_TB).astype(i32)
    pad_need = (padded - counts)[:, None]
    pad_key = jnp.where(jnp.arange(MOE_TB, dtype=i32)[None, :] < pad_need, 2 * e_ids[:, None] + 1, 2 * N_EXPERTS)
    keys = jnp.concatenate([2 * flat_e, pad_key.reshape(-1)])
    vals = jnp.concatenate([jnp.arange(n_asg, dtype=i32) // TOP_K, jnp.zeros((N_EXPERTS * MOE_TB,), i32)])
    _, src = lax.sort_key_val(keys, vals, is_stable=True)
    buf = tokens[src]
    y = expert_ffn(buf, blk_e, n_used, *w_pack, layer)
    dest = dest.reshape(n_tok, TOP_K)
    return [y[dest[:, k]] for k in range(TOP_K)], weights


def prep_expert_weights(w_gu_all, b_gu, w_dn_all, b_dn, layer):
    packed = tuple(split_gate_up(w_gu_all, layer)) + (
        b_gu[:, None, 0::2].astype(F32), b_gu[:, None, 1::2].astype(F32),
        w_dn_all, b_dn[:, None, :].astype(F32))
    return packed


def _combine_body(x_ref, y0_ref, y1_ref, y2_ref, y3_ref, w_ref, g5_ref, lg_ref, lb_ref, o_ref):
    w = w_ref[...]
    y = sum(w[:, k:k + 1] * y_ref[...].astype(F32) for k, y_ref in enumerate((y0_ref, y1_ref, y2_ref, y3_ref)))
    o_ref[...] = _ln(DEEPNORM_ALPHA * x_ref[...] + g5_ref[...] * y) * lg_ref[...] + lb_ref[...]


def combine_postnorm(x1, ys, weights, gate5, ln_g, ln_b, tiles_per_sample, batch):
    rows, d = x1.shape
    gmap = _group_map(tiles_per_sample * (ROW_TILE // OUT_TILE), batch)
    row = lambda i: (i, 0)
    const = lambda i: (0, 0)
    return pl.pallas_call(
        _combine_body,
        grid=(rows // OUT_TILE,),
        in_specs=[pl.BlockSpec((OUT_TILE, d), row)] + [pl.BlockSpec((OUT_TILE, d), row)] * TOP_K
                 + [pl.BlockSpec((OUT_TILE, TOP_K), row), pl.BlockSpec((None, 1, d), gmap),
                    pl.BlockSpec((1, d), const), pl.BlockSpec((1, d), const)],
        out_specs=pl.BlockSpec((OUT_TILE, d), row),
        out_shape=jax.ShapeDtypeStruct((rows, d), F32),
        compiler_params=pltpu.CompilerParams(dimension_semantics=("parallel",), vmem_limit_bytes=VMEM_LIMIT),
        name="combine_postnorm",
    )(x1, *ys, weights, gate5, ln_g, ln_b)


def kernel(x, c, ctx, c_ctx, mod_w, mod_b, w_in, gqa_q_gain, gqa_k_gain, conv_w, conv_b,
           filt_w1, filt_b1, filt_freq1, filt_w2, filt_b2, filt_freq2, filt_w3, hyena_bias,
           swa_sinks, w_branch, w_gate, w_out, ln1_g, ln1_b, router_w, router_b,
           exp_w_gate_up, exp_b_gate_up, exp_w_down, exp_b_down, ln2_g, ln2_b):
    batch, n_lat, d = x.shape
    n_ctx = ctx.shape[1]
    rows = n_lat // GRID_W
    rope_gqa = rope_tables(rows, GQA_HEAD_DIM)
    rope_swa = rope_tables(rows, SWA_HEAD_DIM)
    n_l, n_c = batch * n_lat, batch * n_ctx
    assert n_lat % ROW_TILE == 0 and n_c % ROW_TILE == 0
    tps = n_lat // ROW_TILE
    xs = jnp.concatenate([x.reshape(n_l, d), ctx.reshape(n_c, d)], axis=0)
    cond = jnp.concatenate([c, c_ctx[None, :]], axis=0)
    w_down_b = cast_weights_bf16(exp_w_down)
    for l in range(DEPTH):
        ctx_continues = l < DEPTH - 1
        m_rows = n_l + n_c if ctx_continues else n_l
        mods = (jax.nn.silu(cond) @ mod_w[l] + mod_b[l]).reshape(batch + 1, 6, 1, d).swapaxes(0, 1)
        filt = (filt_w1[l], filt_b1[l], filt_freq1[l], filt_w2[l], filt_b2[l], filt_freq2[l], filt_w3[l])
        sinks = swa_sinks[l].reshape(SWA_KV_HEADS, SWA_HEADS // SWA_KV_HEADS)

        h, fa, gq, gk, gv, hy, sq, sk, sv = in_projection(
            xs, mods[0], mods[1], w_in[l].astype(BF16), rope_gqa, rope_swa, gqa_q_gain[l], gqa_k_gain[l], tps, batch)
        lat3 = lambda t: t[:n_l].reshape(batch, n_lat, -1)
        ctx3 = lambda t: t[n_l:].reshape(batch, n_ctx, -1)
        fa_l, fa_c, hy_c = lat3(fa), ctx3(fa), ctx3(hy)
        hy_parts = [p.reshape(batch, n_lat, -1) for p in short_conv_split(hy, 0, conv_w[l], conv_b[l], tps, batch)]
        ga = global_attention(gq[:n_l], gk[:n_l], gv[:n_l], gk[n_l:], gv[n_l:], batch)
        wa = window_attention(sq[:n_l], sk[:n_l], sv[:n_l], sk[n_l:], sv[n_l:], swa_sinks[l], batch)
        branches = [
            fourier_mix_pallas(fa_l).reshape(n_l, -1),
            ga,
            hyena_mix_pallas(*hy_parts, filt, hyena_bias[l]).reshape(n_l, -1),
            wa,
        ]
        if ctx_continues:
            gq_ctx = group_heads(heads(ctx3(gq), GQA_HEADS), GQA_KV_HEADS)
            sq_ctx = group_heads(heads(ctx3(sq), SWA_HEADS), SWA_KV_HEADS)
            ctx_branches = (
                fourier_mix_pallas(fa_c),
                flat_heads(softmax_attend(gq_ctx, heads(ctx3(gk), GQA_KV_HEADS), heads(ctx3(gv), GQA_KV_HEADS))),
                hyena_mix_pallas(*jnp.split(short_conv(hy_c, conv_w[l], conv_b[l]), 3, axis=-1), filt, hyena_bias[l]),
                flat_heads(softmax_attend(sq_ctx, heads(ctx3(sk), SWA_KV_HEADS), heads(ctx3(sv), SWA_KV_HEADS), sinks)),
            )
            branches = [jnp.concatenate([bl, bc.reshape(n_c, -1).astype(bl.dtype)], axis=0)
                        for bl, bc in zip(branches, ctx_branches)]
        merged = merge_gated(h, branches, w_gate[l].astype(BF16), w_branch[l].astype(BF16), m_rows)
        router_wp = jnp.zeros((d, ROUTER_PAD), F32).at[:, :N_EXPERTS].set(router_w[l])
        router_hi = router_wp.astype(BF16)
        router_hl = jnp.concatenate([router_hi, (router_wp - router_hi.astype(F32)).astype(BF16)], axis=1)
        router_bp = jnp.zeros((1, ROUTER_PAD), F32).at[0, :N_EXPERTS].set(router_b[l])
        x1, h_moe, logits = outproj_postnorm(merged, w_out[l].astype(BF16), xs, mods[2], mods[3], mods[4],
                                             ln1_g[l][None, :], ln1_b[l][None, :], router_hl, router_bp, tps, batch)

        w_pack = prep_expert_weights(exp_w_gate_up, exp_b_gate_up[l], w_down_b, exp_b_down[l], l)
        ys, weights = moe_ffn(h_moe, logits[:, :N_EXPERTS], w_pack, l)
        xs = combine_postnorm(x1, ys, weights, mods[5], ln2_g[l][None, :], ln2_b[l][None, :], tps, batch)
    return xs.reshape(batch, n_lat, d)
```

```python
import functools
import math

import jax
import jax.numpy as jnp
import numpy as np
from jax import lax
from jax.experimental import pallas as pl
from jax.experimental.pallas import tpu as pltpu

DEPTH = 2
GRID_W = 64
FNET_GROUPS = 4
FNET_GROUP_DIM = 128
GQA_HEADS = 4
GQA_KV_HEADS = 2
GQA_HEAD_DIM = 128
HYENA_WIDTH = 512
HYENA_ORDER = 2
HYENA_EMB_DIM = 33
HYENA_DECAY_TARGET = 1e-2
HYENA_SHORT_DECAY_PCT = 0.3
HYENA_LONG_DECAY_PCT = 1.5
SWA_HEADS = 8
SWA_KV_HEADS = 2
SWA_HEAD_DIM = 64
WINDOW = 128
ROPE_THETA = 10000.0
N_EXPERTS = 32
TOP_K = 4
D_EXPERT = 1536
SWIGLU_LIMIT = 7.0
SWIGLU_ALPHA = 1.702
LN_EPS = 1e-5
RMS_EPS = 1e-6
DEEPNORM_ALPHA = (2 * DEPTH) ** 0.25
SPLIT_SIZES = (FNET_GROUPS * FNET_GROUP_DIM,
               GQA_HEADS * GQA_HEAD_DIM, GQA_KV_HEADS * GQA_HEAD_DIM, GQA_KV_HEADS * GQA_HEAD_DIM,
               3 * HYENA_WIDTH,
               SWA_HEADS * SWA_HEAD_DIM, SWA_KV_HEADS * SWA_HEAD_DIM, SWA_KV_HEADS * SWA_HEAD_DIM)
F32 = jnp.float32
BF16 = jnp.bfloat16


ROW_TILE = 512


def _group_map(tiles_per_sample, batch):
    return lambda i, *_: (jnp.minimum(i // tiles_per_sample, batch), 0, 0)


def _ln(x):
    xc = x - jnp.mean(x, axis=-1, keepdims=True)
    return xc * lax.rsqrt(jnp.mean(xc * xc, axis=-1, keepdims=True) + LN_EPS)


def _norm_rope(x, c, s, gain, quarter):
    width = x.shape[1]
    reps = width // LANES
    if gain is not None:
        x = jnp.concatenate(
            [xh * lax.rsqrt(jnp.mean(xh * xh, axis=-1, keepdims=True) + RMS_EPS) * gain
             for xh in (x[:, r * LANES:(r + 1) * LANES] for r in range(reps))], axis=1)
    lane = lax.broadcasted_iota(jnp.int32, x.shape, 1)
    first = (lane % (2 * quarter)) < quarter
    partner = jnp.where(first, pltpu.roll(x, width - quarter, 1), pltpu.roll(x, quarter, 1))
    return x * jnp.concatenate([c] * reps, axis=1) + partner * jnp.concatenate([s] * reps, axis=1)


IN_TILE = 256


def _inproj_body(x_ref, sh_ref, sc_ref, w_ref, cg_ref, sg_ref, cs_ref, ss_ref, gq_ref, gk_ref,
                 h_ref, fa_ref, q_ref, k_ref, v_ref, hy_ref, sq_ref, sk_ref, sv_ref):
    h = (_ln(x_ref[...]) * (1.0 + sc_ref[...]) + sh_ref[...]).astype(BF16)
    h_ref[...] = h
    offs = np.cumsum((0,) + SPLIT_SIZES)
    seg = lambda i: jnp.dot(h, w_ref[:, int(offs[i]):int(offs[i + 1])], preferred_element_type=F32)
    gq4, sq4 = GQA_HEAD_DIM // 4, SWA_HEAD_DIM // 4
    fa_ref[...] = seg(0)
    q_ref[...] = _norm_rope(seg(1), cg_ref[...], sg_ref[...], gq_ref[...], gq4).astype(BF16)
    k_ref[...] = _norm_rope(seg(2), cg_ref[...], sg_ref[...], gk_ref[...], gq4).astype(BF16)
    v_ref[...] = seg(3).astype(BF16)
    hy_ref[...] = seg(4)
    sq_ref[...] = _norm_rope(seg(5), cs_ref[...], ss_ref[...], None, sq4).astype(BF16)
    sk_ref[...] = _norm_rope(seg(6), cs_ref[...], ss_ref[...], None, sq4).astype(BF16)
    sv_ref[...] = seg(7).astype(BF16)


def in_projection(x, shift, scale, w_in, rope_g, rope_s, gain_q, gain_k, tiles_per_sample, batch):
    t, d = x.shape
    per = ROW_TILE // IN_TILE
    gmap = _group_map(tiles_per_sample * per, batch)
    lat_tiles = tiles_per_sample * per
    tmap = lambda i: (jnp.where(i < lat_tiles * batch, i % lat_tiles, lat_tiles), 0)
    row = lambda i: (i, 0)
    const = lambda i: (0, 0)
    tab = pl.BlockSpec((IN_TILE, LANES), tmap)
    vec = pl.BlockSpec((1, LANES), const)
    widths = (d,) + SPLIT_SIZES
    dtypes = (BF16, F32, BF16, BF16, BF16, F32, BF16, BF16, BF16)
    return pl.pallas_call(
        _inproj_body,
        grid=(t // IN_TILE,),
        in_specs=[pl.BlockSpec((IN_TILE, d), row), pl.BlockSpec((None, 1, d), gmap), pl.BlockSpec((None, 1, d), gmap),
                  pl.BlockSpec(w_in.shape, const, pipeline_mode=pl.Buffered(1)), tab, tab, tab, tab, vec, vec],
        out_specs=[pl.BlockSpec((IN_TILE, w), row) for w in widths],
        out_shape=[jax.ShapeDtypeStruct((t, w), dt) for w, dt in zip(widths, dtypes)],
        compiler_params=pltpu.CompilerParams(dimension_semantics=("parallel",), vmem_limit_bytes=VMEM_LIMIT),
        name="in_projection",
    )(x, shift, scale, w_in, *rope_g, *rope_s, gain_q.astype(F32)[None, :], gain_k.astype(F32)[None, :])


def rope_tables(rows, head_dim):
    cos, sin = axial_rope(rows, head_dim)
    c = jnp.concatenate([cos[:, 0], cos[:, 0], cos[:, 1], cos[:, 1]], axis=-1)
    s = jnp.concatenate([-sin[:, 0], sin[:, 0], -sin[:, 1], sin[:, 1]], axis=-1)
    reps = LANES // head_dim
    c, s = jnp.tile(c, (1, reps)), jnp.tile(s, (1, reps))
    return (jnp.concatenate([c, jnp.ones((ROW_TILE, LANES), F32)], axis=0),
            jnp.concatenate([s, jnp.zeros((ROW_TILE, LANES), F32)], axis=0))


MERGE_TILE = 256


def _merge_body(h_ref, b0_ref, b1_ref, b2_ref, b3_ref, wg_ref, wb_ref, o_ref):
    h = h_ref[...]
    acc = None
    for j, b_ref in enumerate((b0_ref, b1_ref, b2_ref, b3_ref)):
        gate = jax.nn.sigmoid(jnp.dot(h, wg_ref[j], preferred_element_type=F32))
        term = gate * jnp.dot(b_ref[...].astype(BF16), wb_ref[j], preferred_element_type=F32)
        acc = term if acc is None else acc + term
    o_ref[...] = acc.astype(o_ref.dtype)


def merge_gated(h, branches, w_gate, w_branch, rows):
    d = h.shape[1]
    nb, bw, _ = w_branch.shape
    assert nb == len(branches) == 4
    row = lambda i: (i, 0)
    whole = lambda i: (0, 0, 0)
    return pl.pallas_call(
        _merge_body,
        grid=(rows // MERGE_TILE,),
        in_specs=[pl.BlockSpec((MERGE_TILE, d), row)] + [pl.BlockSpec((MERGE_TILE, bw), row)] * nb
                 + [pl.BlockSpec((nb, d, d), whole, pipeline_mode=pl.Buffered(1)),
                    pl.BlockSpec((nb, bw, d), whole, pipeline_mode=pl.Buffered(1))],
        out_specs=pl.BlockSpec((MERGE_TILE, d), row),
        out_shape=jax.ShapeDtypeStruct((rows, d), BF16),
        compiler_params=pltpu.CompilerParams(dimension_semantics=("parallel",), vmem_limit_bytes=VMEM_LIMIT),
        name="merge_gated",
    )(h, *branches, w_gate, w_branch)


OUT_TILE = 256
ROUTER_PAD = 128


def _outproj_body(m_ref, w_ref, x_ref, g2_ref, sh_ref, sc_ref, lg_ref, lb_ref, rw_ref, rb_ref,
                  x1_ref, hm_ref, lo_ref):
    mix = jnp.dot(m_ref[...], w_ref[...], preferred_element_type=F32)
    x1 = _ln(DEEPNORM_ALPHA * x_ref[...] + g2_ref[...] * mix) * lg_ref[...] + lb_ref[...]
    x1_ref[...] = x1
    hm = _ln(x1) * (1.0 + sc_ref[...]) + sh_ref[...]
    hm_hi = hm.astype(BF16)
    hm_ref[...] = hm_hi
    hm_lo = (hm - hm_hi.astype(F32)).astype(BF16)
    hh = jnp.dot(hm_hi, rw_ref[...], preferred_element_type=F32)
    lh = jnp.dot(hm_lo, rw_ref[:, :ROUTER_PAD], preferred_element_type=F32)
    lo_ref[...] = hh[:, :ROUTER_PAD] + hh[:, ROUTER_PAD:] + lh + rb_ref[...]


def outproj_postnorm(merged, w_out, x, gate2, shift3, scale4, ln_g, ln_b, router_w, router_b,
                     tiles_per_sample, batch):
    rows, d = merged.shape
    gmap = _group_map(tiles_per_sample * (ROW_TILE // OUT_TILE), batch)
    row = lambda i: (i, 0)
    const = lambda i: (0, 0)
    return pl.pallas_call(
        _outproj_body,
        grid=(rows // OUT_TILE,),
        in_specs=[pl.BlockSpec((OUT_TILE, d), row), pl.BlockSpec((d, d), const), pl.BlockSpec((OUT_TILE, d), row),
                  pl.BlockSpec((None, 1, d), gmap), pl.BlockSpec((None, 1, d), gmap), pl.BlockSpec((None, 1, d), gmap),
                  pl.BlockSpec((1, d), const), pl.BlockSpec((1, d), const),
                  pl.BlockSpec((d, 2 * ROUTER_PAD), const), pl.BlockSpec((1, ROUTER_PAD), const)],
        out_specs=[pl.BlockSpec((OUT_TILE, d), row), pl.BlockSpec((OUT_TILE, d), row),
                   pl.BlockSpec((OUT_TILE, ROUTER_PAD), row)],
        out_shape=[jax.ShapeDtypeStruct((rows, d), F32), jax.ShapeDtypeStruct((rows, d), BF16),
                   jax.ShapeDtypeStruct((rows, ROUTER_PAD), F32)],
        compiler_params=pltpu.CompilerParams(dimension_semantics=("parallel",), vmem_limit_bytes=VMEM_LIMIT),
        name="outproj_postnorm",
    )(merged, w_out, x, gate2, shift3, scale4, ln_g, ln_b, router_w, router_b)


GATTN_TQ = 512
GATTN_TK = 256
GATTN_UNROLL = 16
VMEM_LIMIT = 56 * 1024 * 1024
LANES = 128
LOG2E = math.log2(math.e)


def _gattn_body(q_ref, kc_ref, vc_ref, k_ref, v_ref, o_ref, m_ref, acc_ref, *, c2):
    tq = q_ref.shape[0]
    hd = k_ref.shape[1]
    q2 = jnp.concatenate([q_ref[:, :hd], q_ref[:, hd:]], axis=0)
    m_ref[...] = jnp.full(m_ref.shape, -jnp.inf, F32)
    acc_ref[...] = jnp.zeros(acc_ref.shape, F32)

    def step(k, v):
        s = lax.dot_general(q2, k, (((1,), (1,)), ((), ())), preferred_element_type=F32) * c2
        m_old = m_ref[...]
        m_new = jnp.maximum(m_old, jnp.max(s, axis=-1, keepdims=True))
        alpha = jnp.exp2(m_old - m_new)
        p = jnp.concatenate([jnp.exp2(s[:, j * LANES:(j + 1) * LANES] - m_new)
                             for j in range(k.shape[0] // LANES)], axis=1)
        acc_ref[...] = (jnp.concatenate([alpha, alpha], axis=1) * acc_ref[...]
                        + jnp.dot(p.astype(BF16), v, preferred_element_type=F32))
        m_ref[...] = m_new

    step(kc_ref[...], vc_ref[...])

    def loop(c, carry):
        off = pl.multiple_of(c * GATTN_TK, GATTN_TK)
        step(k_ref[pl.ds(off, GATTN_TK), :], v_ref[pl.ds(off, GATTN_TK), :])
        return carry

    n_chunks = k_ref.shape[0] // GATTN_TK
    lax.fori_loop(0, n_chunks, loop, 0, unroll=math.gcd(GATTN_UNROLL, n_chunks))
    acc = acc_ref[...]
    out = acc[:, :hd] / acc[:, hd:]
    o_ref[...] = jnp.concatenate([out[:tq], out[tq:]], axis=1).astype(o_ref.dtype)


def _with_ones(v, n_heads):
    r = v.shape[0]
    v3 = v.reshape(r, n_heads, -1)
    return jnp.concatenate([v3, jnp.ones_like(v3)], axis=-1).reshape(r, -1)


def global_attention(q, k, v, k_ctx, v_ctx, batch):
    hd = GQA_HEAD_DIM
    n = q.shape[0] // batch
    n_c = k_ctx.shape[0] // batch
    g = GQA_HEADS // GQA_KV_HEADS
    tq = min(GATTN_TQ, n)
    assert g == 2 and hd == LANES and n % tq == 0 and n % GATTN_TK == 0 and n_c % LANES == 0
    nq = n // tq
    return pl.pallas_call(
        functools.partial(_gattn_body, c2=hd ** -0.5 * LOG2E),
        grid=(batch, GQA_KV_HEADS, nq),
        in_specs=[pl.BlockSpec((tq, g * hd), lambda b, h, i: (b * nq + i, h)),
                  pl.BlockSpec((n_c, hd), lambda b, h, i: (b, h)),
                  pl.BlockSpec((n_c, 2 * hd), lambda b, h, i: (b, h)),
                  pl.BlockSpec((n, hd), lambda b, h, i: (b, h)),
                  pl.BlockSpec((n, 2 * hd), lambda b, h, i: (b, h))],
        out_specs=pl.BlockSpec((tq, g * hd), lambda b, h, i: (b * nq + i, h)),
        out_shape=jax.ShapeDtypeStruct(q.shape, BF16),
        scratch_shapes=[pltpu.VMEM((g * tq, hd), F32), pltpu.VMEM((g * tq, 2 * hd), F32)],
        compiler_params=pltpu.CompilerParams(dimension_semantics=("parallel", "parallel", "parallel"),
                                             vmem_limit_bytes=VMEM_LIMIT),
        name="gattn",
    )(q, k_ctx, _with_ones(v_ctx, GQA_KV_HEADS), k, _with_ones(v, GQA_KV_HEADS))


WATTN_TQ = 256


def _wattn_body(sink_ref, q_ref, kc_ref, vc_ref, kp_ref, kx_ref, kn_ref, vp_ref, vx_ref, vn_ref, o_ref,
                *, scale, n, nq):
    tq = q_ref.shape[0]
    hd = SWA_HEAD_DIM
    g = SWA_HEADS // SWA_KV_HEADS
    h = pl.program_id(1)
    i = pl.program_id(2)
    k_band = jnp.concatenate([kp_ref[...], kx_ref[...], kn_ref[...]], axis=0)
    v_band = jnp.concatenate([vp_ref[...], vx_ref[...], vn_ref[...]], axis=0)
    qpos = i * tq + lax.broadcasted_iota(jnp.int32, (tq, tq + 2 * WINDOW), 0)
    kpos = i * tq - WINDOW + lax.broadcasted_iota(jnp.int32, (tq, tq + 2 * WINDOW), 1)
    valid = (jnp.abs(kpos - qpos) <= WINDOW) & (kpos >= 0) & (kpos < n)
    outs = []
    for j in range(g):
        q = q_ref[:, j * hd:(j + 1) * hd]
        s_c = lax.dot_general(q, kc_ref[...], (((1,), (1,)), ((), ())), preferred_element_type=F32) * scale
        s_b = lax.dot_general(q, k_band, (((1,), (1,)), ((), ())), preferred_element_type=F32) * scale
        s_b = jnp.where(valid, s_b, -jnp.inf)
        sink = sink_ref[h * g + j]
        m = jnp.maximum(jnp.maximum(jnp.max(s_c, -1, keepdims=True), jnp.max(s_b, -1, keepdims=True)), sink)
        p_c = jnp.exp(s_c - m)
        p_b = jnp.exp(s_b - m)
        l = jnp.sum(p_c, -1, keepdims=True) + jnp.sum(p_b, -1, keepdims=True) + jnp.exp(sink - m)
        o = (jnp.dot(p_c.astype(BF16), vc_ref[...], preferred_element_type=F32)
             + jnp.dot(p_b.astype(BF16), v_band, preferred_element_type=F32))
        outs.append(o / l)
    o_ref[...] = jnp.concatenate(outs, axis=1).astype(o_ref.dtype)


def window_attention(q, k, v, k_ctx, v_ctx, sinks, batch):
    hd = SWA_HEAD_DIM
    n = q.shape[0] // batch
    n_c = k_ctx.shape[0] // batch
    g = SWA_HEADS // SWA_KV_HEADS
    tq = WATTN_TQ
    assert n % tq == 0 and tq % WINDOW == 0
    nq = n // tq
    r = tq // WINDOW
    nw = n // WINDOW
    k_h = k.reshape(batch * n, SWA_KV_HEADS, hd).transpose(1, 0, 2)
    v_h = v.reshape(batch * n, SWA_KV_HEADS, hd).transpose(1, 0, 2)
    kc_h = k_ctx.reshape(batch * n_c, SWA_KV_HEADS, hd).transpose(1, 0, 2)
    vc_h = v_ctx.reshape(batch * n_c, SWA_KV_HEADS, hd).transpose(1, 0, 2)
    prev_map = lambda b, h, i, s: (h, b * nw + jnp.maximum(i * r - 1, 0), 0)
    cur_map = lambda b, h, i, s: (h, b * nq + i, 0)
    next_map = lambda b, h, i, s: (h, b * nw + jnp.minimum((i + 1) * r, nw - 1), 0)
    ctx_map = lambda b, h, i, s: (h, b, 0)
    grid_spec = pltpu.PrefetchScalarGridSpec(
        num_scalar_prefetch=1,
        grid=(batch, SWA_KV_HEADS, nq),
        in_specs=[pl.BlockSpec((tq, g * hd), lambda b, h, i, s: (b * nq + i, h)),
                  pl.BlockSpec((None, n_c, hd), ctx_map),
                  pl.BlockSpec((None, n_c, hd), ctx_map),
                  pl.BlockSpec((None, WINDOW, hd), prev_map),
                  pl.BlockSpec((None, tq, hd), cur_map),
                  pl.BlockSpec((None, WINDOW, hd), next_map),
                  pl.BlockSpec((None, WINDOW, hd), prev_map),
                  pl.BlockSpec((None, tq, hd), cur_map),
                  pl.BlockSpec((None, WINDOW, hd), next_map)],
        out_specs=pl.BlockSpec((tq, g * hd), lambda b, h, i, s: (b * nq + i, h)),
    )
    return pl.pallas_call(
        functools.partial(_wattn_body, scale=hd ** -0.5, n=n, nq=nq),
        grid_spec=grid_spec,
        out_shape=jax.ShapeDtypeStruct(q.shape, BF16),
        compiler_params=pltpu.CompilerParams(dimension_semantics=("parallel", "parallel", "parallel"),
                                             vmem_limit_bytes=VMEM_LIMIT),
        name="wattn",
    )(sinks.astype(F32), q, kc_h, vc_h, k_h, k_h, k_h, v_h, v_h, v_h)


MOE_TB = 512
MOE_FC = 512


def _ffn_body(blk_e_ref, nused_ref, x_ref, wg_ref, wu_ref, bg_ref, bu_ref, wd_ref, bd_ref, o_ref):
    i = pl.program_id(0)

    @pl.when(i < nused_ref[0])
    def _():
        x = x_ref[...]
        acc = jnp.zeros(o_ref.shape, F32) + bd_ref[...]
        for c in range(D_EXPERT // MOE_FC):
            sl = slice(c * MOE_FC, (c + 1) * MOE_FC)
            gate = jnp.dot(x, wg_ref[:, sl], preferred_element_type=F32) + bg_ref[:, sl]
            up = jnp.dot(x, wu_ref[:, sl], preferred_element_type=F32) + bu_ref[:, sl]
            gate = jnp.minimum(gate, SWIGLU_LIMIT)
            up = jnp.clip(up, -SWIGLU_LIMIT, SWIGLU_LIMIT)
            glu = gate * jax.nn.sigmoid(gate * SWIGLU_ALPHA)
            act = ((up + 1.0) * glu).astype(BF16)
            acc = acc + jnp.dot(act, wd_ref[sl, :], preferred_element_type=F32)
        o_ref[...] = acc.astype(o_ref.dtype)

    @pl.when(i >= nused_ref[0])
    def _():
        o_ref[...] = jnp.zeros(o_ref.shape, o_ref.dtype)


def expert_ffn(buf, blk_e, n_used, w_gate, w_up, b_gate, b_up, w_dn_all, b_dn, layer):
    n_rows, d = buf.shape
    n_blk = n_rows // MOE_TB
    f = w_gate.shape[2]
    wmap = lambda i, e, u: (e[i], 0, 0)
    grid_spec = pltpu.PrefetchScalarGridSpec(
        num_scalar_prefetch=2,
        grid=(n_blk,),
        in_specs=[pl.BlockSpec((MOE_TB, d), lambda i, e, u: (i, 0)),
                  pl.BlockSpec((None, d, f), wmap),
                  pl.BlockSpec((None, d, f), wmap),
                  pl.BlockSpec((None, 1, f), wmap),
                  pl.BlockSpec((None, 1, f), wmap),
                  pl.BlockSpec((None, None, f, d), lambda i, e, u: (layer, e[i], 0, 0)),
                  pl.BlockSpec((None, 1, d), wmap)],
        out_specs=pl.BlockSpec((MOE_TB, d), lambda i, e, u: (i, 0)),
    )
    return pl.pallas_call(
        _ffn_body,
        grid_spec=grid_spec,
        out_shape=jax.ShapeDtypeStruct((n_rows, d), BF16),
        compiler_params=pltpu.CompilerParams(dimension_semantics=("arbitrary",),
                                             vmem_limit_bytes=VMEM_LIMIT),
        name="expert_ffn",
    )(blk_e, n_used, buf, w_gate, w_up, b_gate, b_up, w_dn_all, b_dn)


SPLIT_ROWS = 512
WEIGHT_STREAMS = 4


def _deint_body(*refs):
    w_refs, (p_ref, g_ref, u_ref) = refs[:WEIGHT_STREAMS], refs[WEIGHT_STREAMS:]
    blk = p_ref.shape[0]
    rows = w_refs[0].shape[0]
    for q, w_ref in enumerate(w_refs):
        rs = slice(q * rows, (q + 1) * rows)
        for c in range(w_ref.shape[1] // blk):
            y = jnp.dot(w_ref[:, c * blk:(c + 1) * blk].astype(BF16), p_ref[...], preferred_element_type=F32)
            g_ref[rs, c * LANES:(c + 1) * LANES] = y[:, :LANES].astype(BF16)
            u_ref[rs, c * LANES:(c + 1) * LANES] = y[:, LANES:].astype(BF16)


def split_gate_up(w_gu_all, layer):
    _, e, d, f2 = w_gu_all.shape
    blk = 2 * LANES
    src = jnp.arange(blk)[:, None]
    dst = jnp.arange(blk)[None, :]
    perm = (src == jnp.where(dst < LANES, 2 * dst, 2 * (dst - LANES) + 1)).astype(BF16)
    out = jax.ShapeDtypeStruct((e, d, f2 // 2), BF16)
    sub = SPLIT_ROWS // WEIGHT_STREAMS
    stream = lambda q: pl.BlockSpec((None, None, sub, f2), lambda i, j: (layer, i, WEIGHT_STREAMS * j + q, 0))
    return pl.pallas_call(
        _deint_body,
        grid=(e, d // SPLIT_ROWS),
        in_specs=[stream(q) for q in range(WEIGHT_STREAMS)] + [pl.BlockSpec((blk, blk), lambda i, j: (0, 0))],
        out_specs=[pl.BlockSpec((None, SPLIT_ROWS, f2 // 2), lambda i, j: (i, j, 0)),
                   pl.BlockSpec((None, SPLIT_ROWS, f2 // 2), lambda i, j: (i, j, 0))],
        out_shape=[out, out],
        compiler_params=pltpu.CompilerParams(dimension_semantics=("parallel", "parallel"),
                                             vmem_limit_bytes=VMEM_LIMIT),
        name="split_gate_up",
    )(*([w_gu_all] * WEIGHT_STREAMS), perm)


def _cast_body(*refs):
    w_refs, o_ref = refs[:WEIGHT_STREAMS], refs[WEIGHT_STREAMS]
    rows = w_refs[0].shape[0]
    for q, w_ref in enumerate(w_refs):
        o_ref[q * rows:(q + 1) * rows, :] = w_ref[...].astype(o_ref.dtype)


def cast_weights_bf16(w):
    nl, e, r, c = w.shape
    sub = r // WEIGHT_STREAMS
    assert r % (WEIGHT_STREAMS * 16) == 0
    stream = lambda q: pl.BlockSpec((None, None, sub, c), lambda l, i: (l, i, q, 0))
    return pl.pallas_call(
        _cast_body,
        grid=(nl, e),
        in_specs=[stream(q) for q in range(WEIGHT_STREAMS)],
        out_specs=pl.BlockSpec((None, None, r, c), lambda l, i: (l, i, 0, 0)),
        out_shape=jax.ShapeDtypeStruct(w.shape, BF16),
        compiler_params=pltpu.CompilerParams(dimension_semantics=("parallel", "parallel"),
                                             vmem_limit_bytes=VMEM_LIMIT),
        name="cast_weights_bf16",
    )(*([w] * WEIGHT_STREAMS))


FFT_N2 = 128
FFT_SPLIT_MIN = 1024


def _angles(num, den):
    return (2.0 * math.pi / den) * (num % den).astype(F32)


def dft_stage_a(n1, k_in, real_input):
    ang = _angles(jnp.arange(n1)[:, None] * jnp.arange(k_in)[None, :], n1)
    c, s = jnp.cos(ang), jnp.sin(ang)
    if real_input:
        return jnp.concatenate([c, -s], axis=0).astype(BF16)
    return jnp.block([[c, s], [-s, c]]).astype(BF16)


def dft_stage_a_inv(n1, k_out):
    ang = _angles(jnp.arange(k_out)[:, None] * jnp.arange(n1)[None, :], n1)
    c, s = jnp.cos(ang), jnp.sin(ang)
    return jnp.block([[c, -s], [s, c]]).astype(BF16)


def dft_stage_b(n1, n2):
    k1 = jnp.arange(n1)[:, None, None]
    k2 = jnp.arange(n2)[None, :, None]
    m = jnp.arange(n2)[None, None, :]
    ang = _angles(m * (k1 + n1 * k2), n1 * n2)
    c, s = jnp.cos(ang), jnp.sin(ang)
    top = jnp.concatenate([c, s], axis=2)
    bot = jnp.concatenate([-s, c], axis=2)
    return jnp.concatenate([top, bot], axis=1).astype(BF16)


SUBLANES = 8
U32 = jnp.uint32
HIGH_HALF = 0xFFFF0000


def pack_complex(re, im):
    ur = lax.bitcast_convert_type(re.astype(BF16).astype(F32), U32)
    ui = lax.bitcast_convert_type(im.astype(BF16).astype(F32), U32)
    return (ur >> 16) | (ui & U32(HIGH_HALF))


def unpack_complex(w):
    re = lax.bitcast_convert_type(w << 16, F32).astype(BF16)
    im = lax.bitcast_convert_type(w & U32(HIGH_HALF), F32).astype(BF16)
    return re, im


def _lmm_body(m_ref, x_ref, o_ref):
    m = m_ref[...]
    k, sub, lanes = x_ref.shape
    r = o_ref.shape[0]
    x2 = x_ref.reshape(k * sub, lanes)
    o2 = o_ref.reshape(r * sub, lanes)
    for s in range(sub):
        xs = x2[pl.ds(s, k, stride=sub), :].astype(BF16)
        y = jnp.dot(m, xs, preferred_element_type=F32)
        o2[pl.ds(s, r, stride=sub), :] = pack_complex(y[:r], y[r:])


def left_matmul(m, x, n2):
    g, rows, c = x.shape
    k = rows // n2
    r = m.shape[0] // 2
    assert m.shape[1] == k and n2 % SUBLANES == 0 and c % LANES == 0
    x5 = x.reshape(g, k, n2 // SUBLANES, SUBLANES, c)
    out = pl.pallas_call(
        _lmm_body,
        grid=(g, n2 // SUBLANES, c // LANES),
        in_specs=[pl.BlockSpec((2 * r, k), lambda i, j, l: (0, 0)),
                  pl.BlockSpec((None, k, None, SUBLANES, LANES), lambda i, j, l: (i, 0, j, 0, l))],
        out_specs=pl.BlockSpec((None, r, None, SUBLANES, LANES), lambda i, j, l: (i, 0, j, 0, l)),
        out_shape=jax.ShapeDtypeStruct((g, r, n2 // SUBLANES, SUBLANES, c), U32),
        compiler_params=pltpu.CompilerParams(dimension_semantics=("parallel", "parallel", "parallel"),
                                             vmem_limit_bytes=VMEM_LIMIT),
        name="left_matmul",
    )(m, x5)
    return out.reshape(g, r * n2, c)


def _lmm_gate_body(m_ref, x_ref, a_ref, z_ref, b_ref, o_ref, *, scale):
    m = m_ref[...]
    k, sub, lanes = x_ref.shape
    r = o_ref.shape[0]
    x2 = x_ref.reshape(k * sub, lanes)
    a2 = a_ref.reshape(r * sub, lanes)
    z2 = z_ref.reshape(r * sub, lanes)
    o2 = o_ref.reshape(r * sub, lanes)
    for s in range(sub):
        rows = pl.ds(s, r, stride=sub)
        xs = jnp.concatenate(unpack_complex(x2[pl.ds(s, k, stride=sub), :]), axis=0)
        y = jnp.dot(m, xs, preferred_element_type=F32)
        o2[rows, :] = a2[rows, :] * (y * scale + z2[rows, :] * b_ref[...])


def left_matmul_gated(m, x, n2, gate, z, bias, scale):
    g, rows, c = x.shape
    k = rows // n2
    r = m.shape[0]
    assert m.shape[1] == 2 * k and gate.shape == (g, r * n2, c) and z.shape == gate.shape
    view = lambda t, lead: t.reshape(g, lead, n2 // SUBLANES, SUBLANES, c)
    blk = lambda lead: pl.BlockSpec((None, lead, None, SUBLANES, LANES), lambda i, j, l: (i, 0, j, 0, l))
    out = pl.pallas_call(
        functools.partial(_lmm_gate_body, scale=scale),
        grid=(g, n2 // SUBLANES, c // LANES),
        in_specs=[pl.BlockSpec((r, 2 * k), lambda i, j, l: (0, 0)), blk(k), blk(r), blk(r),
                  pl.BlockSpec((1, LANES), lambda i, j, l: (0, l))],
        out_specs=blk(r),
        out_shape=jax.ShapeDtypeStruct((g, r, n2 // SUBLANES, SUBLANES, c), F32),
        compiler_params=pltpu.CompilerParams(dimension_semantics=("parallel", "parallel", "parallel"),
                                             vmem_limit_bytes=VMEM_LIMIT),
        name="left_matmul_gated",
    )(m, view(x, k), view(gate, r), view(z, r), bias.astype(F32)[None, :])
    return out.reshape(g, r * n2, c)


def _sconv_body(p_ref, x_ref, n_ref, w_ref, b_ref, o1_ref, o2_ref, o3_ref, *, tiles_per_sample):
    i = pl.program_id(0)
    pos = i % tiles_per_sample
    x = x_ref[...]
    rows = x.shape[0]
    before = jnp.where(pos == 0, 0.0, p_ref[SUBLANES - 1:SUBLANES, :])
    after = jnp.where(pos == tiles_per_sample - 1, 0.0, n_ref[0:1, :])
    ridx = lax.broadcasted_iota(jnp.int32, x.shape, 0)
    up = jnp.where(ridx == 0, before, pltpu.roll(x, 1, 0))
    dn = jnp.where(ridx == rows - 1, after, pltpu.roll(x, rows - 1, 0))
    y = up * w_ref[0:1, :] + x * w_ref[1:2, :] + dn * w_ref[2:3, :] + b_ref[...]
    w = o1_ref.shape[1]
    o1_ref[...] = y[:, :w]
    o2_ref[...] = y[:, w:2 * w]
    o3_ref[...] = y[:, 2 * w:]


def short_conv_split(z, col_blocks, conv_w, conv_b, tiles_per_sample, batch):
    t, _ = z.shape
    wid = conv_w.shape[1]
    n_rows = batch * tiles_per_sample * ROW_TILE
    per = ROW_TILE // SUBLANES
    last = t // SUBLANES - 1
    out = jax.ShapeDtypeStruct((n_rows, wid // 3), F32)
    return pl.pallas_call(
        functools.partial(_sconv_body, tiles_per_sample=tiles_per_sample),
        grid=(batch * tiles_per_sample,),
        in_specs=[pl.BlockSpec((SUBLANES, wid), lambda i: (jnp.maximum(i * per - 1, 0), col_blocks)),
                  pl.BlockSpec((ROW_TILE, wid), lambda i: (i, col_blocks)),
                  pl.BlockSpec((SUBLANES, wid), lambda i: (jnp.minimum((i + 1) * per, last), col_blocks)),
                  pl.BlockSpec((3, wid), lambda i: (0, 0)), pl.BlockSpec((1, wid), lambda i: (0, 0))],
        out_specs=[pl.BlockSpec((ROW_TILE, wid // 3), lambda i: (i, 0))] * 3,
        out_shape=[out, out, out],
        compiler_params=pltpu.CompilerParams(dimension_semantics=("parallel",), vmem_limit_bytes=VMEM_LIMIT),
        name="short_conv_split",
    )(z, z, z, conv_w.astype(F32), conv_b.astype(F32)[None, :])


def _stacked(t_ref):
    return jnp.concatenate(unpack_complex(t_ref[...]), axis=0)


def _conv_b_body(t_ref, gs_ref, h_ref, u_ref):
    n2 = t_ref.shape[0]
    gs = gs_ref[...]
    y = jnp.dot(gs, _stacked(t_ref), preferred_element_type=F32)
    yr, yi = y[:n2], y[n2:]
    hr, hi = h_ref[0], h_ref[1]
    z = jnp.concatenate([yr * hr - yi * hi, yr * hi + yi * hr], axis=0).astype(BF16)
    u = lax.dot_general(gs, z, (((0,), (0,)), ((), ())), preferred_element_type=F32)
    u_ref[...] = pack_complex(u[:n2], u[n2:])


def conv_stage_b(t, gs, h_all, order):
    n1, n2, c = t.shape
    spec = pl.BlockSpec((None, n2, c), lambda i: (i, 0, 0))
    return pl.pallas_call(
        _conv_b_body,
        grid=(n1,),
        in_specs=[spec, pl.BlockSpec((None, 2 * n2, 2 * n2), lambda i: (i, 0, 0)),
                  pl.BlockSpec((None, 2, None, n2, c), lambda i: (order, 0, i, 0, 0))],
        out_specs=spec,
        out_shape=jax.ShapeDtypeStruct(t.shape, U32),
        compiler_params=pltpu.CompilerParams(dimension_semantics=("parallel",), vmem_limit_bytes=VMEM_LIMIT),
        name="conv_stage_b",
    )(t, gs, h_all)


def _filt_b_body(t_ref, gs_ref, ss_ref, h_ref):
    n2 = t_ref.shape[0]
    w = HYENA_WIDTH
    y = jnp.dot(gs_ref[...], _stacked(t_ref), preferred_element_type=F32)
    yr, yi = y[:n2], y[n2:]
    ss = ss_ref[...]
    for o in range(HYENA_ORDER):
        f0, b0 = (2 * o) * w, (2 * o + 1) * w
        scale = lax.rsqrt(ss[:, f0:f0 + w] + ss[:, b0:b0 + w] + 1e-6)
        h_ref[o, 0] = (yr[:, f0:f0 + w] + yr[:, b0:b0 + w]) * scale
        h_ref[o, 1] = (yi[:, f0:f0 + w] - yi[:, b0:b0 + w]) * scale


def filter_stage_b(t, gs, sumsq):
    n1, n2, c = t.shape
    w = HYENA_WIDTH
    return pl.pallas_call(
        _filt_b_body,
        grid=(n1,),
        in_specs=[pl.BlockSpec((None, n2, c), lambda i: (i, 0, 0)),
                  pl.BlockSpec((None, 2 * n2, 2 * n2), lambda i: (i, 0, 0)),
                  pl.BlockSpec((1, c), lambda i: (0, 0))],
        out_specs=pl.BlockSpec((HYENA_ORDER, 2, None, n2, w), lambda i: (0, 0, i, 0, 0)),
        out_shape=jax.ShapeDtypeStruct((HYENA_ORDER, 2, n1, n2, w), F32),
        compiler_params=pltpu.CompilerParams(dimension_semantics=("parallel",), vmem_limit_bytes=VMEM_LIMIT),
        name="filter_stage_b",
    )(t, gs, sumsq)


def _fnet_b_body(t_ref, gs_ref, cs_ref, o_ref):
    n2, c = t_ref.shape
    y = jnp.dot(gs_ref[...], _stacked(t_ref), preferred_element_type=F32)
    yr, yi = y[:n2].astype(BF16), y[n2:].astype(BF16)
    gd = FNET_GROUP_DIM
    outs = []
    for g in range(c // gd):
        v = jnp.concatenate([yr[:, g * gd:(g + 1) * gd], yi[:, g * gd:(g + 1) * gd]], axis=1)
        outs.append(jnp.dot(v, cs_ref[...], preferred_element_type=F32))
    o_ref[...] = jnp.concatenate(outs, axis=1).astype(o_ref.dtype)


def fnet_stage_b(t, gs, cs, out_dtype):
    b, n1, n2, c = t.shape
    return pl.pallas_call(
        _fnet_b_body,
        grid=(b, n1),
        in_specs=[pl.BlockSpec((None, None, n2, c), lambda i, j: (i, j, 0, 0)),
                  pl.BlockSpec((None, 2 * n2, 2 * n2), lambda i, j: (j, 0, 0)),
                  pl.BlockSpec(cs.shape, lambda i, j: (0, 0))],
        out_specs=pl.BlockSpec((None, n2, c), lambda i, j: (i, 0, j)),
        out_shape=jax.ShapeDtypeStruct((b, n2, n1 * c), out_dtype),
        compiler_params=pltpu.CompilerParams(dimension_semantics=("parallel", "parallel"),
                                             vmem_limit_bytes=VMEM_LIMIT),
        name="fnet_stage_b",
    )(t, gs, cs)


def fourier_mix_pallas(u):
    b, n, c = u.shape
    gd = FNET_GROUP_DIM
    n2 = FFT_N2 if n >= FFT_SPLIT_MIN else n
    n1 = n // n2
    gs = dft_stage_b(n1, n2)
    if n1 > 1:
        t = left_matmul(dft_stage_a(n1, n1, True), u.astype(F32), n2).reshape(b, n1, n2, c)
    else:
        t = pack_complex(u.astype(F32), jnp.zeros(u.shape, F32)).reshape(b, 1, n2, c)
    ang = _angles(jnp.arange(gd)[:, None] * jnp.arange(gd)[None, :], gd)
    cs = (jnp.concatenate([jnp.cos(ang), jnp.sin(ang)], axis=0) * (n * gd) ** -0.5).astype(BF16)
    out = fnet_stage_b(t, gs, cs, u.dtype)
    return out.reshape(b, n2, n1, c).reshape(b, n, c)


FILT_TM = 512
FILT_LANES = 128


def _filter_body(ff_ref, w1_ref, b1_ref, fr1_ref, w2_ref, b2_ref, fr2_ref, w3_ref, dl_ref, taps_ref, ss_ref, *, n):
    i = pl.program_id(0)
    tm = taps_ref.shape[0]
    hi = lax.Precision.HIGHEST
    pos = (i * tm + lax.broadcasted_iota(jnp.int32, (tm, FILT_LANES), 0)).astype(F32)
    lane = lax.broadcasted_iota(jnp.int32, (tm, FILT_LANES), 1)
    t = pos / (n - 1.0) if n > 1 else pos * 0.0
    omega = (2.0 * math.pi) * pos / n
    arg = ff_ref[...] * omega
    bands = (HYENA_EMB_DIM - 1) // 2
    z = jnp.where(lane == 0, t, jnp.where(lane <= bands, jnp.cos(arg),
                                          jnp.where(lane <= 2 * bands, -jnp.sin(arg), 0.0)))
    h = jnp.sin(fr1_ref[...] * (jnp.dot(z, w1_ref[...], precision=hi, preferred_element_type=F32) + b1_ref[...]))
    h = jnp.sin(fr2_ref[...] * (jnp.dot(h, w2_ref[...], precision=hi, preferred_element_type=F32) + b2_ref[...]))
    h = jnp.dot(h, w3_ref[...], precision=hi, preferred_element_type=F32)
    width = h.shape[1]
    h = h * jnp.exp(-t[:, :1] * dl_ref[...])
    row = i * tm + lax.broadcasted_iota(jnp.int32, (tm, width), 0)
    col = lax.broadcasted_iota(jnp.int32, (tm, width), 1)
    h = jnp.where((row == 0) & ((col // HYENA_WIDTH) % 2 == 1), 0.0, h)
    taps_ref[...] = h

    @pl.when(i == 0)
    def _():
        ss_ref[...] = jnp.zeros(ss_ref.shape, F32)

    ss_ref[...] += jnp.sum(h * h, axis=0, keepdims=True)


def hyena_taps(n, w1, b1, fr1, w2, b2, fr2, w3):
    bands = (HYENA_EMB_DIM - 1) // 2
    fd = w1.shape[1]
    width = w3.shape[1]
    freqs = jnp.linspace(1e-4, bands - 1, bands, dtype=F32)
    featfreq = jnp.zeros((1, FILT_LANES), F32).at[0, 1:1 + bands].set(freqs).at[0, 1 + bands:1 + 2 * bands].set(freqs)
    w1p = jnp.zeros((FILT_LANES, fd), F32).at[:HYENA_EMB_DIM].set(w1.astype(F32))
    max_decay = math.log(HYENA_DECAY_TARGET) / HYENA_SHORT_DECAY_PCT
    min_decay = math.log(HYENA_DECAY_TARGET) / HYENA_LONG_DECAY_PCT
    deltas = jnp.abs(jnp.linspace(min_decay, max_decay, HYENA_WIDTH, dtype=F32))
    dl = jnp.tile(deltas, width // HYENA_WIDTH)[None, :]
    tm = min(FILT_TM, n)
    full = lambda a: pl.BlockSpec(a.shape, lambda i: (0,) * a.ndim)
    ops = (featfreq, w1p, b1.astype(F32)[None, :], fr1.astype(F32)[None, :], w2.astype(F32),
           b2.astype(F32)[None, :], fr2.astype(F32)[None, :], w3.astype(F32), dl)
    return pl.pallas_call(
        functools.partial(_filter_body, n=n),
        grid=(n // tm,),
        in_specs=[full(a) for a in ops],
        out_specs=[pl.BlockSpec((tm, width), lambda i: (i, 0)), pl.BlockSpec((1, width), lambda i: (0, 0))],
        out_shape=[jax.ShapeDtypeStruct((n, width), F32), jax.ShapeDtypeStruct((1, width), F32)],
        compiler_params=pltpu.CompilerParams(dimension_semantics=("arbitrary",), vmem_limit_bytes=VMEM_LIMIT),
        name="hyena_taps",
    )(*ops)


def _fft_split(n):
    if n >= FFT_SPLIT_MIN:
        return 2 * n // FFT_N2, FFT_N2, n // FFT_N2
    return 1, 2 * n, 1


def hyena_mix_pallas(x1, x2, v, filt, bias):
    batch, n, w = v.shape
    assert batch == 2
    n1, n2, k1 = _fft_split(n)
    gs = dft_stage_b(n1, n2)
    taps, sumsq = hyena_taps(n, *filt)
    if n1 > 1:
        t = left_matmul(dft_stage_a(n1, k1, True), taps[None], n2).reshape(n1, n2, taps.shape[1])
    else:
        padded = jnp.pad(taps, ((0, n), (0, 0)))
        t = pack_complex(padded, jnp.zeros_like(padded))[None]
    spec = filter_stage_b(t, gs, sumsq)
    fwd = dft_stage_a(n1, k1, False)
    inv = dft_stage_a_inv(n1, k1)

    def gated_long_conv(gate, z, order):
        scale = 1.0 / (2 * n)
        if n1 > 1:
            t = left_matmul(fwd, z.reshape(1, 2 * n, w), n2).reshape(n1, n2, w)
            uu = conv_stage_b(t, gs, spec, order)
            return left_matmul_gated(inv, uu.reshape(1, n1 * n2, w), n2, gate.reshape(1, 2 * n, w),
                                     z.reshape(1, 2 * n, w), bias[order], scale).reshape(2, n, w)
        zp = jnp.pad(z, ((0, 0), (0, n), (0, 0)))
        uu = conv_stage_b(pack_complex(zp[0], zp[1])[None], gs, spec, order)[0, :n]
        y = jnp.stack(unpack_complex(uu)).astype(F32)
        return gate * (y * scale + z * bias[order].astype(F32))

    return gated_long_conv(x2, gated_long_conv(x1, v, 0), 1)


def heads(z, n_heads):
    b, n, w = z.shape
    return z.reshape(b, n, n_heads, w // n_heads)


def group_heads(q, n_kv):
    b, n, h, hd = q.shape
    return q.reshape(b, n, n_kv, h // n_kv, hd)


def flat_heads(o):
    return o.reshape(o.shape[0], o.shape[1], -1)


def axial_rope(rows, head_dim):
    n_freq = head_dim // 4
    inv_freq = jnp.power(ROPE_THETA, -jnp.arange(n_freq, dtype=F32) / n_freq)
    t = jnp.arange(rows * GRID_W)
    row = (t // GRID_W).astype(F32)
    col = (t % GRID_W).astype(F32)
    ang = jnp.stack([row[:, None] * inv_freq, col[:, None] * inv_freq], axis=1)
    return jnp.cos(ang), jnp.sin(ang)


def softmax_attend(q, k, v, sinks=None):
    scale = q.shape[-1] ** -0.5
    s = jnp.einsum("bqkgd,bskd->bkgqs", q, k, preferred_element_type=F32) * scale
    if sinks is None:
        p = jax.nn.softmax(s, axis=-1)
    else:
        sink_col = jnp.broadcast_to(sinks.astype(F32)[None, :, :, None, None], s.shape[:-1] + (1,))
        p = jax.nn.softmax(jnp.concatenate([sink_col, s], axis=-1), axis=-1)[..., 1:]
    return jnp.einsum("bkgqs,bskd->bqkgd", p.astype(v.dtype), v)


def short_conv(u, w, bias):
    n = u.shape[1]
    up = jnp.pad(u, ((0, 0), (1, 1), (0, 0)))
    return up[:, :n] * w[0] + up[:, 1:n + 1] * w[1] + up[:, 2:] * w[2] + bias


def moe_ffn(tokens, logits, w_pack, layer):
    n_tok, d = tokens.shape
    i32 = jnp.int32
    top_v, top_i = lax.top_k(logits, TOP_K)
    weights = jax.nn.softmax(top_v, axis=-1)
    flat_e = top_i.reshape(-1).astype(i32)
    n_asg = n_tok * TOP_K
    e_ids = jnp.arange(N_EXPERTS, dtype=i32)
    onehot = (flat_e[:, None] == e_ids[None, :]).astype(i32)
    csum = jnp.cumsum(onehot, axis=0)
    rank = jnp.sum(onehot * csum, axis=1) - 1
    counts = csum[-1]
    padded = (counts + MOE_TB - 1) // MOE_TB * MOE_TB
    padded_end = jnp.cumsum(padded)
    padded_start = padded_end - padded
    dest = jnp.sum(onehot * padded_start[None, :], axis=1) + rank
    n_rows = n_asg + N_EXPERTS * MOE_TB
    n_blk = n_rows // MOE_TB
    blk_start = jnp.arange(n_blk, dtype=i32) * MOE_TB
    blk_e = jnp.minimum(jnp.sum((blk_start[:, None] >= padded_end[None, :]).astype(i32), axis=1), N_EXPERTS - 1)
    n_used = (padded_end[-1:] // MOE_TB).astype(i32)
    pad_need = (padded - counts)[:, None]
    pad_key = jnp.where(jnp.arange(MOE_TB, dtype=i32)[None, :] < pad_need, 2 * e_ids[:, None] + 1, 2 * N_EXPERTS)
    keys = jnp.concatenate([2 * flat_e, pad_key.reshape(-1)])
    vals = jnp.concatenate([jnp.arange(n_asg, dtype=i32) // TOP_K, jnp.zeros((N_EXPERTS * MOE_TB,), i32)])
    _, src = lax.sort_key_val(keys, vals, is_stable=True)
    buf = tokens[src]
    y = expert_ffn(buf, blk_e, n_used, *w_pack, layer)
    dest = dest.reshape(n_tok, TOP_K)
    return [y[dest[:, k]] for k in range(TOP_K)], weights


def prep_expert_weights(w_gu_all, b_gu, w_dn_all, b_dn, layer):
    packed = tuple(split_gate_up(w_gu_all, layer)) + (
        b_gu[:, None, 0::2].astype(F32), b_gu[:, None, 1::2].astype(F32),
        w_dn_all, b_dn[:, None, :].astype(F32))
    return packed


def _combine_body(x_ref, y0_ref, y1_ref, y2_ref, y3_ref, w_ref, g5_ref, lg_ref, lb_ref, o_ref):
    w = w_ref[...]
    y = sum(w[:, k:k + 1] * y_ref[...].astype(F32) for k, y_ref in enumerate((y0_ref, y1_ref, y2_ref, y3_ref)))
    o_ref[...] = _ln(DEEPNORM_ALPHA * x_ref[...] + g5_ref[...] * y) * lg_ref[...] + lb_ref[...]


def combine_postnorm(x1, ys, weights, gate5, ln_g, ln_b, tiles_per_sample, batch):
    rows, d = x1.shape
    gmap = _group_map(tiles_per_sample * (ROW_TILE // OUT_TILE), batch)
    row = lambda i: (i, 0)
    const = lambda i: (0, 0)
    return pl.pallas_call(
        _combine_body,
        grid=(rows // OUT_TILE,),
        in_specs=[pl.BlockSpec((OUT_TILE, d), row)] + [pl.BlockSpec((OUT_TILE, d), row)] * TOP_K
                 + [pl.BlockSpec((OUT_TILE, TOP_K), row), pl.BlockSpec((None, 1, d), gmap),
                    pl.BlockSpec((1, d), const), pl.BlockSpec((1, d), const)],
        out_specs=pl.BlockSpec((OUT_TILE, d), row),
        out_shape=jax.ShapeDtypeStruct((rows, d), F32),
        compiler_params=pltpu.CompilerParams(dimension_semantics=("parallel",), vmem_limit_bytes=VMEM_LIMIT),
        name="combine_postnorm",
    )(x1, *ys, weights, gate5, ln_g, ln_b)


def kernel(x, c, ctx, c_ctx, mod_w, mod_b, w_in, gqa_q_gain, gqa_k_gain, conv_w, conv_b,
           filt_w1, filt_b1, filt_freq1, filt_w2, filt_b2, filt_freq2, filt_w3, hyena_bias,
           swa_sinks, w_branch, w_gate, w_out, ln1_g, ln1_b, router_w, router_b,
           exp_w_gate_up, exp_b_gate_up, exp_w_down, exp_b_down, ln2_g, ln2_b):
    batch, n_lat, d = x.shape
    n_ctx = ctx.shape[1]
    rows = n_lat // GRID_W
    rope_gqa = rope_tables(rows, GQA_HEAD_DIM)
    rope_swa = rope_tables(rows, SWA_HEAD_DIM)
    n_l, n_c = batch * n_lat, batch * n_ctx
    assert n_lat % ROW_TILE == 0 and n_c % ROW_TILE == 0
    tps = n_lat // ROW_TILE
    xs = jnp.concatenate([x.reshape(n_l, d), ctx.reshape(n_c, d)], axis=0)
    cond = jnp.concatenate([c, c_ctx[None, :]], axis=0)
    w_down_b = cast_weights_bf16(exp_w_down)
    for l in range(DEPTH):
        ctx_continues = l < DEPTH - 1
        m_rows = n_l + n_c if ctx_continues else n_l
        mods = (jax.nn.silu(cond) @ mod_w[l] + mod_b[l]).reshape(batch + 1, 6, 1, d).swapaxes(0, 1)
        filt = (filt_w1[l], filt_b1[l], filt_freq1[l], filt_w2[l], filt_b2[l], filt_freq2[l], filt_w3[l])
        sinks = swa_sinks[l].reshape(SWA_KV_HEADS, SWA_HEADS // SWA_KV_HEADS)

        h, fa, gq, gk, gv, hy, sq, sk, sv = in_projection(
            xs, mods[0], mods[1], w_in[l].astype(BF16), rope_gqa, rope_swa, gqa_q_gain[l], gqa_k_gain[l], tps, batch)
        lat3 = lambda t: t[:n_l].reshape(batch, n_lat, -1)
        ctx3 = lambda t: t[n_l:].reshape(batch, n_ctx, -1)
        fa_l, fa_c, hy_c = lat3(fa), ctx3(fa), ctx3(hy)
        hy_parts = [p.reshape(batch, n_lat, -1) for p in short_conv_split(hy, 0, conv_w[l], conv_b[l], tps, batch)]
        ga = global_attention(gq[:n_l], gk[:n_l], gv[:n_l], gk[n_l:], gv[n_l:], batch)
        wa = window_attention(sq[:n_l], sk[:n_l], sv[:n_l], sk[n_l:], sv[n_l:], swa_sinks[l], batch)
        branches = [
            fourier_mix_pallas(fa_l).reshape(n_l, -1),
            ga,
            hyena_mix_pallas(*hy_parts, filt, hyena_bias[l]).reshape(n_l, -1),
            wa,
        ]
        if ctx_continues:
            gq_ctx = group_heads(heads(ctx3(gq), GQA_HEADS), GQA_KV_HEADS)
            sq_ctx = group_heads(heads(ctx3(sq), SWA_HEADS), SWA_KV_HEADS)
            ctx_branches = (
                fourier_mix_pallas(fa_c),
                flat_heads(softmax_attend(gq_ctx, heads(ctx3(gk), GQA_KV_HEADS), heads(ctx3(gv), GQA_KV_HEADS))),
                hyena_mix_pallas(*jnp.split(short_conv(hy_c, conv_w[l], conv_b[l]), 3, axis=-1), filt, hyena_bias[l]),
                flat_heads(softmax_attend(sq_ctx, heads(ctx3(sk), SWA_KV_HEADS), heads(ctx3(sv), SWA_KV_HEADS), sinks)),
            )
            branches = [jnp.concatenate([bl, bc.reshape(n_c, -1).astype(bl.dtype)], axis=0)
                        for bl, bc in zip(branches, ctx_branches)]
        merged = merge_gated(h, branches, w_gate[l].astype(BF16), w_branch[l].astype(BF16), m_rows)
        router_wp = jnp.zeros((d, ROUTER_PAD), F32).at[:, :N_EXPERTS].set(router_w[l])
        router_hi = router_wp.astype(BF16)
        router_hl = jnp.concatenate([router_hi, (router_wp - router_hi.astype(F32)).astype(BF16)], axis=1)
        router_bp = jnp.zeros((1, ROUTER_PAD), F32).at[0, :N_EXPERTS].set(router_b[l])
        x1, h_moe, logits = outproj_postnorm(merged, w_out[l].astype(BF16), xs, mods[2], mods[3], mods[4],
                                             ln1_g[l][None, :], ln1_b[l][None, :], router_hl, router_bp, tps, batch)

        w_pack = prep_expert_weights(exp_w_gate_up, exp_b_gate_up[l], w_down_b, exp_b_down[l], l)
        ys, weights = moe_ffn(h_moe, logits[:, :N_EXPERTS], w_pack, l)
        xs = combine_postnorm(x1, ys, weights, mods[5], ln2_g[l][None, :], ln2_b[l][None, :], tps, batch)
    return xs.reshape(batch, n_lat, d)
```

```python
import functools
import math

import jax
import jax.numpy as jnp
import numpy as np
from jax import lax
from jax.experimental import pallas as pl
from jax.experimental.pallas import tpu as pltpu

DEPTH = 2
GRID_W = 64
FNET_GROUPS = 4
FNET_GROUP_DIM = 128
GQA_HEADS = 4
GQA_KV_HEADS = 2
GQA_HEAD_DIM = 128
HYENA_WIDTH = 512
HYENA_ORDER = 2
HYENA_EMB_DIM = 33
HYENA_DECAY_TARGET = 1e-2
HYENA_SHORT_DECAY_PCT = 0.3
HYENA_LONG_DECAY_PCT = 1.5
SWA_HEADS = 8
SWA_KV_HEADS = 2
SWA_HEAD_DIM = 64
WINDOW = 128
ROPE_THETA = 10000.0
N_EXPERTS = 32
TOP_K = 4
D_EXPERT = 1536
SWIGLU_LIMIT = 7.0
SWIGLU_ALPHA = 1.702
LN_EPS = 1e-5
RMS_EPS = 1e-6
DEEPNORM_ALPHA = (2 * DEPTH) ** 0.25
SPLIT_SIZES = (FNET_GROUPS * FNET_GROUP_DIM,
               GQA_HEADS * GQA_HEAD_DIM, GQA_KV_HEADS * GQA_HEAD_DIM, GQA_KV_HEADS * GQA_HEAD_DIM,
               3 * HYENA_WIDTH,
               SWA_HEADS * SWA_HEAD_DIM, SWA_KV_HEADS * SWA_HEAD_DIM, SWA_KV_HEADS * SWA_HEAD_DIM)
F32 = jnp.float32
BF16 = jnp.bfloat16


ROW_TILE = 512


def _group_map(tiles_per_sample, batch):
    return lambda i, *_: (jnp.minimum(i // tiles_per_sample, batch), 0, 0)


def _ln(x):
    xc = x - jnp.mean(x, axis=-1, keepdims=True)
    return xc * lax.rsqrt(jnp.mean(xc * xc, axis=-1, keepdims=True) + LN_EPS)


def _norm_rope(x, c, s, gain, quarter):
    width = x.shape[1]
    reps = width // LANES
    if gain is not None:
        x = jnp.concatenate(
            [xh * lax.rsqrt(jnp.mean(xh * xh, axis=-1, keepdims=True) + RMS_EPS) * gain
             for xh in (x[:, r * LANES:(r + 1) * LANES] for r in range(reps))], axis=1)
    lane = lax.broadcasted_iota(jnp.int32, x.shape, 1)
    first = (lane % (2 * quarter)) < quarter
    partner = jnp.where(first, pltpu.roll(x, width - quarter, 1), pltpu.roll(x, quarter, 1))
    return x * jnp.concatenate([c] * reps, axis=1) + partner * jnp.concatenate([s] * reps, axis=1)


IN_TILE = 256


def _inproj_body(x_ref, sh_ref, sc_ref, w_ref, cg_ref, sg_ref, cs_ref, ss_ref, gq_ref, gk_ref,
                 h_ref, fa_ref, q_ref, k_ref, v_ref, hy_ref, sq_ref, sk_ref, sv_ref):
    h = (_ln(x_ref[...]) * (1.0 + sc_ref[...]) + sh_ref[...]).astype(BF16)
    h_ref[...] = h
    offs = np.cumsum((0,) + SPLIT_SIZES)
    seg = lambda i: jnp.dot(h, w_ref[:, int(offs[i]):int(offs[i + 1])], preferred_element_type=F32)
    gq4, sq4 = GQA_HEAD_DIM // 4, SWA_HEAD_DIM // 4
    fa_ref[...] = seg(0)
    q_ref[...] = _norm_rope(seg(1), cg_ref[...], sg_ref[...], gq_ref[...], gq4).astype(BF16)
    k_ref[...] = _norm_rope(seg(2), cg_ref[...], sg_ref[...], gk_ref[...], gq4).astype(BF16)
    v_ref[...] = seg(3).astype(BF16)
    hy_ref[...] = seg(4)
    sq_ref[...] = _norm_rope(seg(5), cs_ref[...], ss_ref[...], None, sq4).astype(BF16)
    sk_ref[...] = _norm_rope(seg(6), cs_ref[...], ss_ref[...], None, sq4).astype(BF16)
    sv_ref[...] = seg(7).astype(BF16)


def in_projection(x, shift, scale, w_in, rope_g, rope_s, gain_q, gain_k, tiles_per_sample, batch):
    t, d = x.shape
    per = ROW_TILE // IN_TILE
    gmap = _group_map(tiles_per_sample * per, batch)
    lat_tiles = tiles_per_sample * per
    tmap = lambda i: (jnp.where(i < lat_tiles * batch, i % lat_tiles, lat_tiles), 0)
    row = lambda i: (i, 0)
    const = lambda i: (0, 0)
    tab = pl.BlockSpec((IN_TILE, LANES), tmap)
    vec = pl.BlockSpec((1, LANES), const)
    widths = (d,) + SPLIT_SIZES
    dtypes = (BF16, F32, BF16, BF16, BF16, F32, BF16, BF16, BF16)
    return pl.pallas_call(
        _inproj_body,
        grid=(t // IN_TILE,),
        in_specs=[pl.BlockSpec((IN_TILE, d), row), pl.BlockSpec((None, 1, d), gmap), pl.BlockSpec((None, 1, d), gmap),
                  pl.BlockSpec(w_in.shape, const, pipeline_mode=pl.Buffered(1)), tab, tab, tab, tab, vec, vec],
        out_specs=[pl.BlockSpec((IN_TILE, w), row) for w in widths],
        out_shape=[jax.ShapeDtypeStruct((t, w), dt) for w, dt in zip(widths, dtypes)],
        compiler_params=pltpu.CompilerParams(dimension_semantics=("parallel",), vmem_limit_bytes=VMEM_LIMIT),
        name="in_projection",
    )(x, shift, scale, w_in, *rope_g, *rope_s, gain_q.astype(F32)[None, :], gain_k.astype(F32)[None, :])


def rope_tables(rows, head_dim):
    cos, sin = axial_rope(rows, head_dim)
    c = jnp.concatenate([cos[:, 0], cos[:, 0], cos[:, 1], cos[:, 1]], axis=-1)
    s = jnp.concatenate([-sin[:, 0], sin[:, 0], -sin[:, 1], sin[:, 1]], axis=-1)
    reps = LANES // head_dim
    c, s = jnp.tile(c, (1, reps)), jnp.tile(s, (1, reps))
    return (jnp.concatenate([c, jnp.ones((ROW_TILE, LANES), F32)], axis=0),
            jnp.concatenate([s, jnp.zeros((ROW_TILE, LANES), F32)], axis=0))


MERGE_TILE = 256


def _merge_body(h_ref, b0_ref, b1_ref, b2_ref, b3_ref, wg_ref, wb_ref, o_ref):
    h = h_ref[...]
    acc = None
    for j, b_ref in enumerate((b0_ref, b1_ref, b2_ref, b3_ref)):
        gate = jax.nn.sigmoid(jnp.dot(h, wg_ref[j], preferred_element_type=F32))
        term = gate * jnp.dot(b_ref[...].astype(BF16), wb_ref[j], preferred_element_type=F32)
        acc = term if acc is None else acc + term
    o_ref[...] = acc.astype(o_ref.dtype)


def merge_gated(h, branches, w_gate, w_branch, rows):
    d = h.shape[1]
    nb, bw, _ = w_branch.shape
    assert nb == len(branches) == 4
    row = lambda i: (i, 0)
    whole = lambda i: (0, 0, 0)
    return pl.pallas_call(
        _merge_body,
        grid=(rows // MERGE_TILE,),
        in_specs=[pl.BlockSpec((MERGE_TILE, d), row)] + [pl.BlockSpec((MERGE_TILE, bw), row)] * nb
                 + [pl.BlockSpec((nb, d, d), whole, pipeline_mode=pl.Buffered(1)),
                    pl.BlockSpec((nb, bw, d), whole, pipeline_mode=pl.Buffered(1))],
        out_specs=pl.BlockSpec((MERGE_TILE, d), row),
        out_shape=jax.ShapeDtypeStruct((rows, d), BF16),
        compiler_params=pltpu.CompilerParams(dimension_semantics=("parallel",), vmem_limit_bytes=VMEM_LIMIT),
        name="merge_gated",
    )(h, *branches, w_gate, w_branch)


OUT_TILE = 256
ROUTER_PAD = 128


def _outproj_body(m_ref, w_ref, x_ref, g2_ref, sh_ref, sc_ref, lg_ref, lb_ref, rw_ref, rb_ref,
                  x1_ref, hm_ref, lo_ref):
    mix = jnp.dot(m_ref[...], w_ref[...], preferred_element_type=F32)
    x1 = _ln(DEEPNORM_ALPHA * x_ref[...] + g2_ref[...] * mix) * lg_ref[...] + lb_ref[...]
    x1_ref[...] = x1
    hm = _ln(x1) * (1.0 + sc_ref[...]) + sh_ref[...]
    hm_hi = hm.astype(BF16)
    hm_ref[...] = hm_hi
    hm_lo = (hm - hm_hi.astype(F32)).astype(BF16)
    hh = jnp.dot(hm_hi, rw_ref[...], preferred_element_type=F32)
    lh = jnp.dot(hm_lo, rw_ref[:, :ROUTER_PAD], preferred_element_type=F32)
    lo_ref[...] = hh[:, :ROUTER_PAD] + hh[:, ROUTER_PAD:] + lh + rb_ref[...]


def outproj_postnorm(merged, w_out, x, gate2, shift3, scale4, ln_g, ln_b, router_w, router_b,
                     tiles_per_sample, batch):
    rows, d = merged.shape
    gmap = _group_map(tiles_per_sample * (ROW_TILE // OUT_TILE), batch)
    row = lambda i: (i, 0)
    const = lambda i: (0, 0)
    return pl.pallas_call(
        _outproj_body,
        grid=(rows // OUT_TILE,),
        in_specs=[pl.BlockSpec((OUT_TILE, d), row), pl.BlockSpec((d, d), const), pl.BlockSpec((OUT_TILE, d), row),
                  pl.BlockSpec((None, 1, d), gmap), pl.BlockSpec((None, 1, d), gmap), pl.BlockSpec((None, 1, d), gmap),
                  pl.BlockSpec((1, d), const), pl.BlockSpec((1, d), const),
                  pl.BlockSpec((d, 2 * ROUTER_PAD), const), pl.BlockSpec((1, ROUTER_PAD), const)],
        out_specs=[pl.BlockSpec((OUT_TILE, d), row), pl.BlockSpec((OUT_TILE, d), row),
                   pl.BlockSpec((OUT_TILE, ROUTER_PAD), row)],
        out_shape=[jax.ShapeDtypeStruct((rows, d), F32), jax.ShapeDtypeStruct((rows, d), BF16),
                   jax.ShapeDtypeStruct((rows, ROUTER_PAD), F32)],
        compiler_params=pltpu.CompilerParams(dimension_semantics=("parallel",), vmem_limit_bytes=VMEM_LIMIT),
        name="outproj_postnorm",
    )(merged, w_out, x, gate2, shift3, scale4, ln_g, ln_b, router_w, router_b)


GATTN_TQ = 512
GATTN_TK = 256
GATTN_UNROLL = 16
VMEM_LIMIT = 56 * 1024 * 1024
LANES = 128
LOG2E = math.log2(math.e)


def _gattn_body(q_ref, kc_ref, vc_ref, k_ref, v_ref, o_ref, m_ref, acc_ref, *, c2):
    tq = q_ref.shape[0]
    hd = k_ref.shape[1]
    q2 = jnp.concatenate([q_ref[:, :hd], q_ref[:, hd:]], axis=0)
    m_ref[...] = jnp.full(m_ref.shape, -jnp.inf, F32)
    acc_ref[...] = jnp.zeros(acc_ref.shape, F32)

    def step(k, v):
        s = lax.dot_general(q2, k, (((1,), (1,)), ((), ())), preferred_element_type=F32) * c2
        m_old = m_ref[...]
        m_new = jnp.maximum(m_old, jnp.max(s, axis=-1, keepdims=True))
        alpha = jnp.exp2(m_old - m_new)
        p = jnp.concatenate([jnp.exp2(s[:, j * LANES:(j + 1) * LANES] - m_new)
                             for j in range(k.shape[0] // LANES)], axis=1)
        acc_ref[...] = (jnp.concatenate([alpha, alpha], axis=1) * acc_ref[...]
                        + jnp.dot(p.astype(BF16), v, preferred_element_type=F32))
        m_ref[...] = m_new

    step(kc_ref[...], vc_ref[...])

    def loop(c, carry):
        off = pl.multiple_of(c * GATTN_TK, GATTN_TK)
        step(k_ref[pl.ds(off, GATTN_TK), :], v_ref[pl.ds(off, GATTN_TK), :])
        return carry

    n_chunks = k_ref.shape[0] // GATTN_TK
    lax.fori_loop(0, n_chunks, loop, 0, unroll=math.gcd(GATTN_UNROLL, n_chunks))
    acc = acc_ref[...]
    out = acc[:, :hd] / acc[:, hd:]
    o_ref[...] = jnp.concatenate([out[:tq], out[tq:]], axis=1).astype(o_ref.dtype)


def _with_ones(v, n_heads):
    r = v.shape[0]
    v3 = v.reshape(r, n_heads, -1)
    return jnp.concatenate([v3, jnp.ones_like(v3)], axis=-1).reshape(r, -1)


def global_attention(q, k, v, k_ctx, v_ctx, batch):
    hd = GQA_HEAD_DIM
    n = q.shape[0] // batch
    n_c = k_ctx.shape[0] // batch
    g = GQA_HEADS // GQA_KV_HEADS
    tq = min(GATTN_TQ, n)
    assert g == 2 and hd == LANES and n % tq == 0 and n % GATTN_TK == 0 and n_c % LANES == 0
    nq = n // tq
    return pl.pallas_call(
        functools.partial(_gattn_body, c2=hd ** -0.5 * LOG2E),
        grid=(batch, GQA_KV_HEADS, nq),
        in_specs=[pl.BlockSpec((tq, g * hd), lambda b, h, i: (b * nq + i, h)),
                  pl.BlockSpec((n_c, hd), lambda b, h, i: (b, h)),
                  pl.BlockSpec((n_c, 2 * hd), lambda b, h, i: (b, h)),
                  pl.BlockSpec((n, hd), lambda b, h, i: (b, h)),
                  pl.BlockSpec((n, 2 * hd), lambda b, h, i: (b, h))],
        out_specs=pl.BlockSpec((tq, g * hd), lambda b, h, i: (b * nq + i, h)),
        out_shape=jax.ShapeDtypeStruct(q.shape, BF16),
        scratch_shapes=[pltpu.VMEM((g * tq, hd), F32), pltpu.VMEM((g * tq, 2 * hd), F32)],
        compiler_params=pltpu.CompilerParams(dimension_semantics=("parallel", "parallel", "parallel"),
                                             vmem_limit_bytes=VMEM_LIMIT),
        name="gattn",
    )(q, k_ctx, _with_ones(v_ctx, GQA_KV_HEADS), k, _with_ones(v, GQA_KV_HEADS))


WATTN_TQ = 256


def _wattn_body(sink_ref, q_ref, kc_ref, vc_ref, kp_ref, kx_ref, kn_ref, vp_ref, vx_ref, vn_ref, o_ref,
                *, scale, n, nq):
    tq = q_ref.shape[0]
    hd = SWA_HEAD_DIM
    g = SWA_HEADS // SWA_KV_HEADS
    h = pl.program_id(1)
    i = pl.program_id(2)
    k_band = jnp.concatenate([kp_ref[...], kx_ref[...], kn_ref[...]], axis=0)
    v_band = jnp.concatenate([vp_ref[...], vx_ref[...], vn_ref[...]], axis=0)
    qpos = i * tq + lax.broadcasted_iota(jnp.int32, (tq, tq + 2 * WINDOW), 0)
    kpos = i * tq - WINDOW + lax.broadcasted_iota(jnp.int32, (tq, tq + 2 * WINDOW), 1)
    valid = (jnp.abs(kpos - qpos) <= WINDOW) & (kpos >= 0) & (kpos < n)
    outs = []
    for j in range(g):
        q = q_ref[:, j * hd:(j + 1) * hd]
        s_c = lax.dot_general(q, kc_ref[...], (((1,), (1,)), ((), ())), preferred_element_type=F32) * scale
        s_b = lax.dot_general(q, k_band, (((1,), (1,)), ((), ())), preferred_element_type=F32) * scale
        s_b = jnp.where(valid, s_b, -jnp.inf)
        sink = sink_ref[h * g + j]
        m = jnp.maximum(jnp.maximum(jnp.max(s_c, -1, keepdims=True), jnp.max(s_b, -1, keepdims=True)), sink)
        p_c = jnp.exp(s_c - m)
        p_b = jnp.exp(s_b - m)
        l = jnp.sum(p_c, -1, keepdims=True) + jnp.sum(p_b, -1, keepdims=True) + jnp.exp(sink - m)
        o = (jnp.dot(p_c.astype(BF16), vc_ref[...], preferred_element_type=F32)
             + jnp.dot(p_b.astype(BF16), v_band, preferred_element_type=F32))
        outs.append(o / l)
    o_ref[...] = jnp.concatenate(outs, axis=1).astype(o_ref.dtype)


def window_attention(q, k, v, k_ctx, v_ctx, sinks, batch):
    hd = SWA_HEAD_DIM
    n = q.shape[0] // batch
    n_c = k_ctx.shape[0] // batch
    g = SWA_HEADS // SWA_KV_HEADS
    tq = WATTN_TQ
    assert n % tq == 0 and tq % WINDOW == 0
    nq = n // tq
    r = tq // WINDOW
    nw = n // WINDOW
    k_h = k.reshape(batch * n, SWA_KV_HEADS, hd).transpose(1, 0, 2)
    v_h = v.reshape(batch * n, SWA_KV_HEADS, hd).transpose(1, 0, 2)
    kc_h = k_ctx.reshape(batch * n_c, SWA_KV_HEADS, hd).transpose(1, 0, 2)
    vc_h = v_ctx.reshape(batch * n_c, SWA_KV_HEADS, hd).transpose(1, 0, 2)
    prev_map = lambda b, h, i, s: (h, b * nw + jnp.maximum(i * r - 1, 0), 0)
    cur_map = lambda b, h, i, s: (h, b * nq + i, 0)
    next_map = lambda b, h, i, s: (h, b * nw + jnp.minimum((i + 1) * r, nw - 1), 0)
    ctx_map = lambda b, h, i, s: (h, b, 0)
    grid_spec = pltpu.PrefetchScalarGridSpec(
        num_scalar_prefetch=1,
        grid=(batch, SWA_KV_HEADS, nq),
        in_specs=[pl.BlockSpec((tq, g * hd), lambda b, h, i, s: (b * nq + i, h)),
                  pl.BlockSpec((None, n_c, hd), ctx_map),
                  pl.BlockSpec((None, n_c, hd), ctx_map),
                  pl.BlockSpec((None, WINDOW, hd), prev_map),
                  pl.BlockSpec((None, tq, hd), cur_map),
                  pl.BlockSpec((None, WINDOW, hd), next_map),
                  pl.BlockSpec((None, WINDOW, hd), prev_map),
                  pl.BlockSpec((None, tq, hd), cur_map),
                  pl.BlockSpec((None, WINDOW, hd), next_map)],
        out_specs=pl.BlockSpec((tq, g * hd), lambda b, h, i, s: (b * nq + i, h)),
    )
    return pl.pallas_call(
        functools.partial(_wattn_body, scale=hd ** -0.5, n=n, nq=nq),
        grid_spec=grid_spec,
        out_shape=jax.ShapeDtypeStruct(q.shape, BF16),
        compiler_params=pltpu.CompilerParams(dimension_semantics=("parallel", "parallel", "parallel"),
                                             vmem_limit_bytes=VMEM_LIMIT),
        name="wattn",
    )(sinks.astype(F32), q, kc_h, vc_h, k_h, k_h, k_h, v_h, v_h, v_h)


MOE_TB = 512
MOE_FC = 512


def _ffn_body(blk_e_ref, nused_ref, x_ref, wg_ref, wu_ref, bg_ref, bu_ref, wd_ref, bd_ref, o_ref):
    i = pl.program_id(0)

    @pl.when(i < nused_ref[0])
    def _():
        x = x_ref[...]
        acc = jnp.zeros(o_ref.shape, F32) + bd_ref[...]
        for c in range(D_EXPERT // MOE_FC):
            sl = slice(c * MOE_FC, (c + 1) * MOE_FC)
            gate = jnp.dot(x, wg_ref[:, sl], preferred_element_type=F32) + bg_ref[:, sl]
            up = jnp.dot(x, wu_ref[:, sl], preferred_element_type=F32) + bu_ref[:, sl]
            gate = jnp.minimum(gate, SWIGLU_LIMIT)
            up = jnp.clip(up, -SWIGLU_LIMIT, SWIGLU_LIMIT)
            glu = gate * jax.nn.sigmoid(gate * SWIGLU_ALPHA)
            act = ((up + 1.0) * glu).astype(BF16)
            acc = acc + jnp.dot(act, wd_ref[sl, :], preferred_element_type=F32)
        o_ref[...] = acc.astype(o_ref.dtype)

    @pl.when(i >= nused_ref[0])
    def _():
        o_ref[...] = jnp.zeros(o_ref.shape, o_ref.dtype)


def expert_ffn(buf, blk_e, n_used, w_gate, w_up, b_gate, b_up, w_dn_all, b_dn, layer):
    n_rows, d = buf.shape
    n_blk = n_rows // MOE_TB
    f = w_gate.shape[2]
    wmap = lambda i, e, u: (e[i], 0, 0)
    grid_spec = pltpu.PrefetchScalarGridSpec(
        num_scalar_prefetch=2,
        grid=(n_blk,),
        in_specs=[pl.BlockSpec((MOE_TB, d), lambda i, e, u: (i, 0)),
                  pl.BlockSpec((None, d, f), wmap),
                  pl.BlockSpec((None, d, f), wmap),
                  pl.BlockSpec((None, 1, f), wmap),
                  pl.BlockSpec((None, 1, f), wmap),
                  pl.BlockSpec((None, None, f, d), lambda i, e, u: (layer, e[i], 0, 0)),
                  pl.BlockSpec((None, 1, d), wmap)],
        out_specs=pl.BlockSpec((MOE_TB, d), lambda i, e, u: (i, 0)),
    )
    return pl.pallas_call(
        _ffn_body,
        grid_spec=grid_spec,
        out_shape=jax.ShapeDtypeStruct((n_rows, d), BF16),
        compiler_params=pltpu.CompilerParams(dimension_semantics=("arbitrary",),
                                             vmem_limit_bytes=VMEM_LIMIT),
        name="expert_ffn",
    )(blk_e, n_used, buf, w_gate, w_up, b_gate, b_up, w_dn_all, b_dn)


SPLIT_ROWS = 512
WEIGHT_STREAMS = 4


def _deint_body(*refs):
    w_refs, (p_ref, g_ref, u_ref) = refs[:WEIGHT_STREAMS], refs[WEIGHT_STREAMS:]
    blk = p_ref.shape[0]
    rows = w_refs[0].shape[0]
    for q, w_ref in enumerate(w_refs):
        rs = slice(q * rows, (q + 1) * rows)
        for c in range(w_ref.shape[1] // blk):
            y = jnp.dot(w_ref[:, c * blk:(c + 1) * blk].astype(BF16), p_ref[...], preferred_element_type=F32)
            g_ref[rs, c * LANES:(c + 1) * LANES] = y[:, :LANES].astype(BF16)
            u_ref[rs, c * LANES:(c + 1) * LANES] = y[:, LANES:].astype(BF16)


def split_gate_up(w_gu_all, layer):
    _, e, d, f2 = w_gu_all.shape
    blk = 2 * LANES
    src = jnp.arange(blk)[:, None]
    dst = jnp.arange(blk)[None, :]
    perm = (src == jnp.where(dst < LANES, 2 * dst, 2 * (dst - LANES) + 1)).astype(BF16)
    out = jax.ShapeDtypeStruct((e, d, f2 // 2), BF16)
    sub = SPLIT_ROWS // WEIGHT_STREAMS
    stream = lambda q: pl.BlockSpec((None, None, sub, f2), lambda i, j: (layer, i, WEIGHT_STREAMS * j + q, 0))
    return pl.pallas_call(
        _deint_body,
        grid=(e, d // SPLIT_ROWS),
        in_specs=[stream(q) for q in range(WEIGHT_STREAMS)] + [pl.BlockSpec((blk, blk), lambda i, j: (0, 0))],
        out_specs=[pl.BlockSpec((None, SPLIT_ROWS, f2 // 2), lambda i, j: (i, j, 0)),
                   pl.BlockSpec((None, SPLIT_ROWS, f2 // 2), lambda i, j: (i, j, 0))],
        out_shape=[out, out],
        compiler_params=pltpu.CompilerParams(dimension_semantics=("parallel", "parallel"),
                                             vmem_limit_bytes=VMEM_LIMIT),
        name="split_gate_up",
    )(*([w_gu_all] * WEIGHT_STREAMS), perm)


def _cast_body(*refs):
    w_refs, o_ref = refs[:WEIGHT_STREAMS], refs[WEIGHT_STREAMS]
    rows = w_refs[0].shape[0]
    for q, w_ref in enumerate(w_refs):
        o_ref[q * rows:(q + 1) * rows, :] = w_ref[...].astype(o_ref.dtype)


def cast_weights_bf16(w):
    nl, e, r, c = w.shape
    sub = r // WEIGHT_STREAMS
    assert r % (WEIGHT_STREAMS * 16) == 0
    stream = lambda q: pl.BlockSpec((None, None, sub, c), lambda l, i: (l, i, q, 0))
    return pl.pallas_call(
        _cast_body,
        grid=(nl, e),
        in_specs=[stream(q) for q in range(WEIGHT_STREAMS)],
        out_specs=pl.BlockSpec((None, None, r, c), lambda l, i: (l, i, 0, 0)),
        out_shape=jax.ShapeDtypeStruct(w.shape, BF16),
        compiler_params=pltpu.CompilerParams(dimension_semantics=("parallel", "parallel"),
                                             vmem_limit_bytes=VMEM_LIMIT),
        name="cast_weights_bf16",
    )(*([w] * WEIGHT_STREAMS))


FFT_N2 = 128
FFT_SPLIT_MIN = 1024


def _angles(num, den):
    return (2.0 * math.pi / den) * (num % den).astype(F32)


def dft_stage_a(n1, k_in, real_input):
    ang = _angles(jnp.arange(n1)[:, None] * jnp.arange(k_in)[None, :], n1)
    c, s = jnp.cos(ang), jnp.sin(ang)
    if real_input:
        return jnp.concatenate([c, -s], axis=0).astype(BF16)
    return jnp.block([[c, s], [-s, c]]).astype(BF16)


def dft_stage_a_inv(n1, k_out):
    ang = _angles(jnp.arange(k_out)[:, None] * jnp.arange(n1)[None, :], n1)
    c, s = jnp.cos(ang), jnp.sin(ang)
    return jnp.block([[c, -s], [s, c]]).astype(BF16)


def dft_stage_b(n1, n2):
    k1 = jnp.arange(n1)[:, None, None]
    k2 = jnp.arange(n2)[None, :, None]
    m = jnp.arange(n2)[None, None, :]
    ang = _angles(m * (k1 + n1 * k2), n1 * n2)
    c, s = jnp.cos(ang), jnp.sin(ang)
    top = jnp.concatenate([c, s], axis=2)
    bot = jnp.concatenate([-s, c], axis=2)
    return jnp.concatenate([top, bot], axis=1).astype(BF16)


SUBLANES = 8
U32 = jnp.uint32
HIGH_HALF = 0xFFFF0000


def pack_complex(re, im):
    ur = lax.bitcast_convert_type(re.astype(BF16).astype(F32), U32)
    ui = lax.bitcast_convert_type(im.astype(BF16).astype(F32), U32)
    return (ur >> 16) | (ui & U32(HIGH_HALF))


def unpack_complex(w):
    re = lax.bitcast_convert_type(w << 16, F32).astype(BF16)
    im = lax.bitcast_convert_type(w & U32(HIGH_HALF), F32).astype(BF16)
    return re, im


def _lmm_body(m_ref, x_ref, o_ref):
    m = m_ref[...]
    k, sub, lanes = x_ref.shape
    r = o_ref.shape[0]
    x2 = x_ref.reshape(k * sub, lanes)
    o2 = o_ref.reshape(r * sub, lanes)
    for s in range(sub):
        xs = x2[pl.ds(s, k, stride=sub), :].astype(BF16)
        y = jnp.dot(m, xs, preferred_element_type=F32)
        o2[pl.ds(s, r, stride=sub), :] = pack_complex(y[:r], y[r:])


def left_matmul(m, x, n2):
    g, rows, c = x.shape
    k = rows // n2
    r = m.shape[0] // 2
    assert m.shape[1] == k and n2 % SUBLANES == 0 and c % LANES == 0
    x5 = x.reshape(g, k, n2 // SUBLANES, SUBLANES, c)
    out = pl.pallas_call(
        _lmm_body,
        grid=(g, n2 // SUBLANES, c // LANES),
        in_specs=[pl.BlockSpec((2 * r, k), lambda i, j, l: (0, 0)),
                  pl.BlockSpec((None, k, None, SUBLANES, LANES), lambda i, j, l: (i, 0, j, 0, l))],
        out_specs=pl.BlockSpec((None, r, None, SUBLANES, LANES), lambda i, j, l: (i, 0, j, 0, l)),
        out_shape=jax.ShapeDtypeStruct((g, r, n2 // SUBLANES, SUBLANES, c), U32),
        compiler_params=pltpu.CompilerParams(dimension_semantics=("parallel", "parallel", "parallel"),
                                             vmem_limit_bytes=VMEM_LIMIT),
        name="left_matmul",
    )(m, x5)
    return out.reshape(g, r * n2, c)


def _lmm_gate_body(m_ref, x_ref, a_ref, z_ref, b_ref, o_ref, *, scale):
    m = m_ref[...]
    k, sub, lanes = x_ref.shape
    r = o_ref.shape[0]
    x2 = x_ref.reshape(k * sub, lanes)
    a2 = a_ref.reshape(r * sub, lanes)
    z2 = z_ref.reshape(r * sub, lanes)
    o2 = o_ref.reshape(r * sub, lanes)
    for s in range(sub):
        rows = pl.ds(s, r, stride=sub)
        xs = jnp.concatenate(unpack_complex(x2[pl.ds(s, k, stride=sub), :]), axis=0)
        y = jnp.dot(m, xs, preferred_element_type=F32)
        o2[rows, :] = a2[rows, :] * (y * scale + z2[rows, :] * b_ref[...])


def left_matmul_gated(m, x, n2, gate, z, bias, scale):
    g, rows, c = x.shape
    k = rows // n2
    r = m.shape[0]
    assert m.shape[1] == 2 * k and gate.shape == (g, r * n2, c) and z.shape == gate.shape
    view = lambda t, lead: t.reshape(g, lead, n2 // SUBLANES, SUBLANES, c)
    blk = lambda lead: pl.BlockSpec((None, lead, None, SUBLANES, LANES), lambda i, j, l: (i, 0, j, 0, l))
    out = pl.pallas_call(
        functools.partial(_lmm_gate_body, scale=scale),
        grid=(g, n2 // SUBLANES, c // LANES),
        in_specs=[pl.BlockSpec((r, 2 * k), lambda i, j, l: (0, 0)), blk(k), blk(r), blk(r),
                  pl.BlockSpec((1, LANES), lambda i, j, l: (0, l))],
        out_specs=blk(r),
        out_shape=jax.ShapeDtypeStruct((g, r, n2 // SUBLANES, SUBLANES, c), F32),
        compiler_params=pltpu.CompilerParams(dimension_semantics=("parallel", "parallel", "parallel"),
                                             vmem_limit_bytes=VMEM_LIMIT),
        name="left_matmul_gated",
    )(m, view(x, k), view(gate, r), view(z, r), bias.astype(F32)[None, :])
    return out.reshape(g, r * n2, c)


def _sconv_body(p_ref, x_ref, n_ref, w_ref, b_ref, o1_ref, o2_ref, o3_ref, *, tiles_per_sample):
    i = pl.program_id(0)
    pos = i % tiles_per_sample
    x = x_ref[...]
    rows = x.shape[0]
    before = jnp.where(pos == 0, 0.0, p_ref[SUBLANES - 1:SUBLANES, :])
    after = jnp.where(pos == tiles_per_sample - 1, 0.0, n_ref[0:1, :])
    ridx = lax.broadcasted_iota(jnp.int32, x.shape, 0)
    up = jnp.where(ridx == 0, before, pltpu.roll(x, 1, 0))
    dn = jnp.where(ridx == rows - 1, after, pltpu.roll(x, rows - 1, 0))
    y = up * w_ref[0:1, :] + x * w_ref[1:2, :] + dn * w_ref[2:3, :] + b_ref[...]
    w = o1_ref.shape[1]
    o1_ref[...] = y[:, :w]
    o2_ref[...] = y[:, w:2 * w]
    o3_ref[...] = y[:, 2 * w:]


def short_conv_split(z, col_blocks, conv_w, conv_b, tiles_per_sample, batch):
    t, _ = z.shape
    wid = conv_w.shape[1]
    n_rows = batch * tiles_per_sample * ROW_TILE
    per = ROW_TILE // SUBLANES
    last = t // SUBLANES - 1
    out = jax.ShapeDtypeStruct((n_rows, wid // 3), F32)
    return pl.pallas_call(
        functools.partial(_sconv_body, tiles_per_sample=tiles_per_sample),
        grid=(batch * tiles_per_sample,),
        in_specs=[pl.BlockSpec((SUBLANES, wid), lambda i: (jnp.maximum(i * per - 1, 0), col_blocks)),
                  pl.BlockSpec((ROW_TILE, wid), lambda i: (i, col_blocks)),
                  pl.BlockSpec((SUBLANES, wid), lambda i: (jnp.minimum((i + 1) * per, last), col_blocks)),
                  pl.BlockSpec((3, wid), lambda i: (0, 0)), pl.BlockSpec((1, wid), lambda i: (0, 0))],
        out_specs=[pl.BlockSpec((ROW_TILE, wid // 3), lambda i: (i, 0))] * 3,
        out_shape=[out, out, out],
        compiler_params=pltpu.CompilerParams(dimension_semantics=("parallel",), vmem_limit_bytes=VMEM_LIMIT),
        name="short_conv_split",
    )(z, z, z, conv_w.astype(F32), conv_b.astype(F32)[None, :])


def _stacked(t_ref):
    return jnp.concatenate(unpack_complex(t_ref[...]), axis=0)


def _conv_b_body(t_ref, gs_ref, h_ref, u_ref):
    n2 = t_ref.shape[0]
    gs = gs_ref[...]
    y = jnp.dot(gs, _stacked(t_ref), preferred_element_type=F32)
    yr, yi = y[:n2], y[n2:]
    hr, hi = h_ref[0], h_ref[1]
    z = jnp.concatenate([yr * hr - yi * hi, yr * hi + yi * hr], axis=0).astype(BF16)
    u = lax.dot_general(gs, z, (((0,), (0,)), ((), ())), preferred_element_type=F32)
    u_ref[...] = pack_complex(u[:n2], u[n2:])


def conv_stage_b(t, gs, h):
    n1, n2, c = t.shape
    spec = pl.BlockSpec((None, n2, c), lambda i: (i, 0, 0))
    return pl.pallas_call(
        _conv_b_body,
        grid=(n1,),
        in_specs=[spec, pl.BlockSpec((None, 2 * n2, 2 * n2), lambda i: (i, 0, 0)),
                  pl.BlockSpec((2, None, n2, c), lambda i: (0, i, 0, 0))],
        out_specs=spec,
        out_shape=jax.ShapeDtypeStruct(t.shape, U32),
        compiler_params=pltpu.CompilerParams(dimension_semantics=("parallel",), vmem_limit_bytes=VMEM_LIMIT),
        name="conv_stage_b",
    )(t, gs, h)


def _filt_b_body(t_ref, gs_ref, ss_ref, h_ref):
    n2 = t_ref.shape[0]
    w = HYENA_WIDTH
    y = jnp.dot(gs_ref[...], _stacked(t_ref), preferred_element_type=F32)
    yr, yi = y[:n2], y[n2:]
    ss = ss_ref[...]
    for o in range(HYENA_ORDER):
        f0, b0 = (2 * o) * w, (2 * o + 1) * w
        scale = lax.rsqrt(ss[:, f0:f0 + w] + ss[:, b0:b0 + w] + 1e-6)
        h_ref[o, 0] = (yr[:, f0:f0 + w] + yr[:, b0:b0 + w]) * scale
        h_ref[o, 1] = (yi[:, f0:f0 + w] - yi[:, b0:b0 + w]) * scale


def filter_stage_b(t, gs, sumsq):
    n1, n2, c = t.shape
    w = HYENA_WIDTH
    return pl.pallas_call(
        _filt_b_body,
        grid=(n1,),
        in_specs=[pl.BlockSpec((None, n2, c), lambda i: (i, 0, 0)),
                  pl.BlockSpec((None, 2 * n2, 2 * n2), lambda i: (i, 0, 0)),
                  pl.BlockSpec((1, c), lambda i: (0, 0))],
        out_specs=pl.BlockSpec((HYENA_ORDER, 2, None, n2, w), lambda i: (0, 0, i, 0, 0)),
        out_shape=jax.ShapeDtypeStruct((HYENA_ORDER, 2, n1, n2, w), F32),
        compiler_params=pltpu.CompilerParams(dimension_semantics=("parallel",), vmem_limit_bytes=VMEM_LIMIT),
        name="filter_stage_b",
    )(t, gs, sumsq)


def _fnet_b_body(t_ref, gs_ref, cs_ref, o_ref):
    n2, c = t_ref.shape
    y = jnp.dot(gs_ref[...], _stacked(t_ref), preferred_element_type=F32)
    yr, yi = y[:n2].astype(BF16), y[n2:].astype(BF16)
    gd = FNET_GROUP_DIM
    outs = []
    for g in range(c // gd):
        v = jnp.concatenate([yr[:, g * gd:(g + 1) * gd], yi[:, g * gd:(g + 1) * gd]], axis=1)
        outs.append(jnp.dot(v, cs_ref[...], preferred_element_type=F32))
    o_ref[...] = jnp.concatenate(outs, axis=1).astype(o_ref.dtype)


def fnet_stage_b(t, gs, cs, out_dtype):
    b, n1, n2, c = t.shape
    return pl.pallas_call(
        _fnet_b_body,
        grid=(b, n1),
        in_specs=[pl.BlockSpec((None, None, n2, c), lambda i, j: (i, j, 0, 0)),
                  pl.BlockSpec((None, 2 * n2, 2 * n2), lambda i, j: (j, 0, 0)),
                  pl.BlockSpec(cs.shape, lambda i, j: (0, 0))],
        out_specs=pl.BlockSpec((None, n2, c), lambda i, j: (i, 0, j)),
        out_shape=jax.ShapeDtypeStruct((b, n2, n1 * c), out_dtype),
        compiler_params=pltpu.CompilerParams(dimension_semantics=("parallel", "parallel"),
                                             vmem_limit_bytes=VMEM_LIMIT),
        name="fnet_stage_b",
    )(t, gs, cs)


def fourier_mix_pallas(u):
    b, n, c = u.shape
    gd = FNET_GROUP_DIM
    n2 = FFT_N2 if n >= FFT_SPLIT_MIN else n
    n1 = n // n2
    gs = dft_stage_b(n1, n2)
    if n1 > 1:
        t = left_matmul(dft_stage_a(n1, n1, True), u.astype(F32), n2).reshape(b, n1, n2, c)
    else:
        t = pack_complex(u.astype(F32), jnp.zeros(u.shape, F32)).reshape(b, 1, n2, c)
    ang = _angles(jnp.arange(gd)[:, None] * jnp.arange(gd)[None, :], gd)
    cs = (jnp.concatenate([jnp.cos(ang), jnp.sin(ang)], axis=0) * (n * gd) ** -0.5).astype(BF16)
    out = fnet_stage_b(t, gs, cs, u.dtype)
    return out.reshape(b, n2, n1, c).reshape(b, n, c)


FILT_TM = 512
FILT_LANES = 128


def _filter_body(ff_ref, w1_ref, b1_ref, fr1_ref, w2_ref, b2_ref, fr2_ref, w3_ref, dl_ref, taps_ref, ss_ref, *, n):
    i = pl.program_id(0)
    tm = taps_ref.shape[0]
    hi = lax.Precision.HIGHEST
    pos = (i * tm + lax.broadcasted_iota(jnp.int32, (tm, FILT_LANES), 0)).astype(F32)
    lane = lax.broadcasted_iota(jnp.int32, (tm, FILT_LANES), 1)
    t = pos / (n - 1.0) if n > 1 else pos * 0.0
    omega = (2.0 * math.pi) * pos / n
    arg = ff_ref[...] * omega
    bands = (HYENA_EMB_DIM - 1) // 2
    z = jnp.where(lane == 0, t, jnp.where(lane <= bands, jnp.cos(arg),
                                          jnp.where(lane <= 2 * bands, -jnp.sin(arg), 0.0)))
    h = jnp.sin(fr1_ref[...] * (jnp.dot(z, w1_ref[...], precision=hi, preferred_element_type=F32) + b1_ref[...]))
    h = jnp.sin(fr2_ref[...] * (jnp.dot(h, w2_ref[...], precision=hi, preferred_element_type=F32) + b2_ref[...]))
    h = jnp.dot(h, w3_ref[...], precision=hi, preferred_element_type=F32)
    width = h.shape[1]
    h = h * jnp.exp(-t[:, :1] * dl_ref[...])
    row = i * tm + lax.broadcasted_iota(jnp.int32, (tm, width), 0)
    col = lax.broadcasted_iota(jnp.int32, (tm, width), 1)
    h = jnp.where((row == 0) & ((col // HYENA_WIDTH) % 2 == 1), 0.0, h)
    taps_ref[...] = h

    @pl.when(i == 0)
    def _():
        ss_ref[...] = jnp.zeros(ss_ref.shape, F32)

    ss_ref[...] += jnp.sum(h * h, axis=0, keepdims=True)


def hyena_taps(n, w1, b1, fr1, w2, b2, fr2, w3):
    bands = (HYENA_EMB_DIM - 1) // 2
    fd = w1.shape[1]
    width = w3.shape[1]
    freqs = jnp.linspace(1e-4, bands - 1, bands, dtype=F32)
    featfreq = jnp.zeros((1, FILT_LANES), F32).at[0, 1:1 + bands].set(freqs).at[0, 1 + bands:1 + 2 * bands].set(freqs)
    w1p = jnp.zeros((FILT_LANES, fd), F32).at[:HYENA_EMB_DIM].set(w1.astype(F32))
    max_decay = math.log(HYENA_DECAY_TARGET) / HYENA_SHORT_DECAY_PCT
    min_decay = math.log(HYENA_DECAY_TARGET) / HYENA_LONG_DECAY_PCT
    deltas = jnp.abs(jnp.linspace(min_decay, max_decay, HYENA_WIDTH, dtype=F32))
    dl = jnp.tile(deltas, width // HYENA_WIDTH)[None, :]
    tm = min(FILT_TM, n)
    full = lambda a: pl.BlockSpec(a.shape, lambda i: (0,) * a.ndim)
    ops = (featfreq, w1p, b1.astype(F32)[None, :], fr1.astype(F32)[None, :], w2.astype(F32),
           b2.astype(F32)[None, :], fr2.astype(F32)[None, :], w3.astype(F32), dl)
    return pl.pallas_call(
        functools.partial(_filter_body, n=n),
        grid=(n // tm,),
        in_specs=[full(a) for a in ops],
        out_specs=[pl.BlockSpec((tm, width), lambda i: (i, 0)), pl.BlockSpec((1, width), lambda i: (0, 0))],
        out_shape=[jax.ShapeDtypeStruct((n, width), F32), jax.ShapeDtypeStruct((1, width), F32)],
        compiler_params=pltpu.CompilerParams(dimension_semantics=("arbitrary",), vmem_limit_bytes=VMEM_LIMIT),
        name="hyena_taps",
    )(*ops)


def _fft_split(n):
    if n >= FFT_SPLIT_MIN:
        return 2 * n // FFT_N2, FFT_N2, n // FFT_N2
    return 1, 2 * n, 1


def hyena_mix_pallas(x1, x2, v, filt, bias):
    batch, n, w = v.shape
    assert batch == 2
    n1, n2, k1 = _fft_split(n)
    gs = dft_stage_b(n1, n2)
    taps, sumsq = hyena_taps(n, *filt)
    if n1 > 1:
        t = left_matmul(dft_stage_a(n1, k1, True), taps[None], n2).reshape(n1, n2, taps.shape[1])
    else:
        padded = jnp.pad(taps, ((0, n), (0, 0)))
        t = pack_complex(padded, jnp.zeros_like(padded))[None]
    spec = filter_stage_b(t, gs, sumsq)
    fwd = dft_stage_a(n1, k1, False)
    inv = dft_stage_a_inv(n1, k1)

    def gated_long_conv(gate, z, order):
        scale = 1.0 / (2 * n)
        if n1 > 1:
            t = left_matmul(fwd, z.reshape(1, 2 * n, w), n2).reshape(n1, n2, w)
            uu = conv_stage_b(t, gs, spec[order])
            return left_matmul_gated(inv, uu.reshape(1, n1 * n2, w), n2, gate.reshape(1, 2 * n, w),
                                     z.reshape(1, 2 * n, w), bias[order], scale).reshape(2, n, w)
        zp = jnp.pad(z, ((0, 0), (0, n), (0, 0)))
        uu = conv_stage_b(pack_complex(zp[0], zp[1])[None], gs, spec[order])[0, :n]
        y = jnp.stack(unpack_complex(uu)).astype(F32)
        return gate * (y * scale + z * bias[order].astype(F32))

    return gated_long_conv(x2, gated_long_conv(x1, v, 0), 1)


def heads(z, n_heads):
    b, n, w = z.shape
    return z.reshape(b, n, n_heads, w // n_heads)


def group_heads(q, n_kv):
    b, n, h, hd = q.shape
    return q.reshape(b, n, n_kv, h // n_kv, hd)


def flat_heads(o):
    return o.reshape(o.shape[0], o.shape[1], -1)


def axial_rope(rows, head_dim):
    n_freq = head_dim // 4
    inv_freq = jnp.power(ROPE_THETA, -jnp.arange(n_freq, dtype=F32) / n_freq)
    t = jnp.arange(rows * GRID_W)
    row = (t // GRID_W).astype(F32)
    col = (t % GRID_W).astype(F32)
    ang = jnp.stack([row[:, None] * inv_freq, col[:, None] * inv_freq], axis=1)
    return jnp.cos(ang), jnp.sin(ang)


def softmax_attend(q, k, v, sinks=None):
    scale = q.shape[-1] ** -0.5
    s = jnp.einsum("bqkgd,bskd->bkgqs", q, k, preferred_element_type=F32) * scale
    if sinks is None:
        p = jax.nn.softmax(s, axis=-1)
    else:
        sink_col = jnp.broadcast_to(sinks.astype(F32)[None, :, :, None, None], s.shape[:-1] + (1,))
        p = jax.nn.softmax(jnp.concatenate([sink_col, s], axis=-1), axis=-1)[..., 1:]
    return jnp.einsum("bkgqs,bskd->bqkgd", p.astype(v.dtype), v)


def short_conv(u, w, bias):
    n = u.shape[1]
    up = jnp.pad(u, ((0, 0), (1, 1), (0, 0)))
    return up[:, :n] * w[0] + up[:, 1:n + 1] * w[1] + up[:, 2:] * w[2] + bias


def moe_ffn(tokens, logits, prep_weights, layer):
    n_tok, d = tokens.shape
    i32 = jnp.int32
    top_v, top_i = lax.top_k(logits, TOP_K)
    weights = jax.nn.softmax(top_v, axis=-1)
    flat_e = top_i.reshape(-1).astype(i32)
    n_asg = n_tok * TOP_K
    e_ids = jnp.arange(N_EXPERTS, dtype=i32)
    onehot = (flat_e[:, None] == e_ids[None, :]).astype(i32)
    csum = jnp.cumsum(onehot, axis=0)
    rank = jnp.sum(onehot * csum, axis=1) - 1
    counts = csum[-1]
    padded = (counts + MOE_TB - 1) // MOE_TB * MOE_TB
    padded_end = jnp.cumsum(padded)
    padded_start = padded_end - padded
    dest = jnp.sum(onehot * padded_start[None, :], axis=1) + rank
    n_rows = n_asg + N_EXPERTS * MOE_TB
    n_blk = n_rows // MOE_TB
    blk_start = jnp.arange(n_blk, dtype=i32) * MOE_TB
    blk_e = jnp.minimum(jnp.sum((blk_start[:, None] >= padded_end[None, :]).astype(i32), axis=1), N_EXPERTS - 1)
    n_used = (padded_end[-1:] // MOE_TB).astype(i32)
    pad_need = (padded - counts)[:, None]
    pad_key = jnp.where(jnp.arange(MOE_TB, dtype=i32)[None, :] < pad_need, 2 * e_ids[:, None] + 1, 2 * N_EXPERTS)
    keys = jnp.concatenate([2 * flat_e, pad_key.reshape(-1)])
    vals = jnp.concatenate([jnp.arange(n_asg, dtype=i32) // TOP_K, jnp.zeros((N_EXPERTS * MOE_TB,), i32)])
    _, src = lax.sort_key_val(keys, vals, is_stable=True)
    buf = tokens[src]
    y = expert_ffn(buf, blk_e, n_used, *prep_weights(), layer)
    dest = dest.reshape(n_tok, TOP_K)
    return [y[dest[:, k]] for k in range(TOP_K)], weights


def prep_expert_weights(w_gu_all, b_gu, w_dn_all, b_dn, layer):
    packed = tuple(split_gate_up(w_gu_all, layer)) + (
        b_gu[:, None, 0::2].astype(F32), b_gu[:, None, 1::2].astype(F32),
        w_dn_all, b_dn[:, None, :].astype(F32))
    return packed


def _combine_body(x_ref, y0_ref, y1_ref, y2_ref, y3_ref, w_ref, g5_ref, lg_ref, lb_ref, o_ref):
    w = w_ref[...]
    y = sum(w[:, k:k + 1] * y_ref[...].astype(F32) for k, y_ref in enumerate((y0_ref, y1_ref, y2_ref, y3_ref)))
    o_ref[...] = _ln(DEEPNORM_ALPHA * x_ref[...] + g5_ref[...] * y) * lg_ref[...] + lb_ref[...]


def combine_postnorm(x1, ys, weights, gate5, ln_g, ln_b, tiles_per_sample, batch):
    rows, d = x1.shape
    gmap = _group_map(tiles_per_sample * (ROW_TILE // OUT_TILE), batch)
    row = lambda i: (i, 0)
    const = lambda i: (0, 0)
    return pl.pallas_call(
        _combine_body,
        grid=(rows // OUT_TILE,),
        in_specs=[pl.BlockSpec((OUT_TILE, d), row)] + [pl.BlockSpec((OUT_TILE, d), row)] * TOP_K
                 + [pl.BlockSpec((OUT_TILE, TOP_K), row), pl.BlockSpec((None, 1, d), gmap),
                    pl.BlockSpec((1, d), const), pl.BlockSpec((1, d), const)],
        out_specs=pl.BlockSpec((OUT_TILE, d), row),
        out_shape=jax.ShapeDtypeStruct((rows, d), F32),
        compiler_params=pltpu.CompilerParams(dimension_semantics=("parallel",), vmem_limit_bytes=VMEM_LIMIT),
        name="combine_postnorm",
    )(x1, *ys, weights, gate5, ln_g, ln_b)


def kernel(x, c, ctx, c_ctx, mod_w, mod_b, w_in, gqa_q_gain, gqa_k_gain, conv_w, conv_b,
           filt_w1, filt_b1, filt_freq1, filt_w2, filt_b2, filt_freq2, filt_w3, hyena_bias,
           swa_sinks, w_branch, w_gate, w_out, ln1_g, ln1_b, router_w, router_b,
           exp_w_gate_up, exp_b_gate_up, exp_w_down, exp_b_down, ln2_g, ln2_b):
    batch, n_lat, d = x.shape
    n_ctx = ctx.shape[1]
    rows = n_lat // GRID_W
    rope_gqa = rope_tables(rows, GQA_HEAD_DIM)
    rope_swa = rope_tables(rows, SWA_HEAD_DIM)
    n_l, n_c = batch * n_lat, batch * n_ctx
    assert n_lat % ROW_TILE == 0 and n_c % ROW_TILE == 0
    tps = n_lat // ROW_TILE
    xs = jnp.concatenate([x.reshape(n_l, d), ctx.reshape(n_c, d)], axis=0)
    cond = jnp.concatenate([c, c_ctx[None, :]], axis=0)
    w_down_b = cast_weights_bf16(exp_w_down)
    for l in range(DEPTH):
        ctx_continues = l < DEPTH - 1
        m_rows = n_l + n_c if ctx_continues else n_l
        mods = (jax.nn.silu(cond) @ mod_w[l] + mod_b[l]).reshape(batch + 1, 6, 1, d).swapaxes(0, 1)
        filt = (filt_w1[l], filt_b1[l], filt_freq1[l], filt_w2[l], filt_b2[l], filt_freq2[l], filt_w3[l])
        sinks = swa_sinks[l].reshape(SWA_KV_HEADS, SWA_HEADS // SWA_KV_HEADS)

        h, fa, gq, gk, gv, hy, sq, sk, sv = in_projection(
            xs, mods[0], mods[1], w_in[l].astype(BF16), rope_gqa, rope_swa, gqa_q_gain[l], gqa_k_gain[l], tps, batch)
        lat3 = lambda t: t[:n_l].reshape(batch, n_lat, -1)
        ctx3 = lambda t: t[n_l:].reshape(batch, n_ctx, -1)
        fa_l, fa_c, hy_c = lat3(fa), ctx3(fa), ctx3(hy)
        hy_parts = [p.reshape(batch, n_lat, -1) for p in short_conv_split(hy, 0, conv_w[l], conv_b[l], tps, batch)]
        ga = global_attention(gq[:n_l], gk[:n_l], gv[:n_l], gk[n_l:], gv[n_l:], batch)
        wa = window_attention(sq[:n_l], sk[:n_l], sv[:n_l], sk[n_l:], sv[n_l:], swa_sinks[l], batch)
        branches = [
            fourier_mix_pallas(fa_l).reshape(n_l, -1),
            ga,
            hyena_mix_pallas(*hy_parts, filt, hyena_bias[l]).reshape(n_l, -1),
            wa,
        ]
        if ctx_continues:
            gq_ctx = group_heads(heads(ctx3(gq), GQA_HEADS), GQA_KV_HEADS)
            sq_ctx = group_heads(heads(ctx3(sq), SWA_HEADS), SWA_KV_HEADS)
            ctx_branches = (
                fourier_mix_pallas(fa_c),
                flat_heads(softmax_attend(gq_ctx, heads(ctx3(gk), GQA_KV_HEADS), heads(ctx3(gv), GQA_KV_HEADS))),
                hyena_mix_pallas(*jnp.split(short_conv(hy_c, conv_w[l], conv_b[l]), 3, axis=-1), filt, hyena_bias[l]),
                flat_heads(softmax_attend(sq_ctx, heads(ctx3(sk), SWA_KV_HEADS), heads(ctx3(sv), SWA_KV_HEADS), sinks)),
            )
            branches = [jnp.concatenate([bl, bc.reshape(n_c, -1).astype(bl.dtype)], axis=0)
                        for bl, bc in zip(branches, ctx_branches)]
        merged = merge_gated(h, branches, w_gate[l].astype(BF16), w_branch[l].astype(BF16), m_rows)
        router_wp = jnp.zeros((d, ROUTER_PAD), F32).at[:, :N_EXPERTS].set(router_w[l])
        router_hi = router_wp.astype(BF16)
        router_hl = jnp.concatenate([router_hi, (router_wp - router_hi.astype(F32)).astype(BF16)], axis=1)
        router_bp = jnp.zeros((1, ROUTER_PAD), F32).at[0, :N_EXPERTS].set(router_b[l])
        x1, h_moe, logits = outproj_postnorm(merged, w_out[l].astype(BF16), xs, mods[2], mods[3], mods[4],
                                             ln1_g[l][None, :], ln1_b[l][None, :], router_hl, router_bp, tps, batch)

        prep = functools.partial(prep_expert_weights, exp_w_gate_up, exp_b_gate_up[l], w_down_b, exp_b_down[l], l)
        ys, weights = moe_ffn(h_moe, logits[:, :N_EXPERTS], prep, l)
        xs = combine_postnorm(x1, ys, weights, mods[5], ln2_g[l][None, :], ln2_b[l][None, :], tps, batch)
    return xs.reshape(batch, n_lat, d)
```
